```python
import jax, jax.numpy as jnp
from jax import lax
import numpy as np

D_MODEL = 1024
BATCH = 8
SEQ = 4096
DEPTH = 1

HEAD_DIM = 64
SB_HEADS = 8
SWA_HEADS = 8
SWA_KV_HEADS = 2
SWA_WINDOW = 128
MEM_HEADS = 4
MEM_HEAD_DIM = 128
MEM_LEN = 256
Q_BLOCK = 128
N_BRANCH = 3
N_EXPERTS = 256
TOP_K = 8
N_GROUPS = 8
TOPK_GROUPS = 4
EXPERT_FF = 256
SHARED_FF = 256
ROUTED_SCALE = 2.5
MOE_BLOCK = 512
LN_EPS = 1e-5
DEEPNORM_ALPHA = (2 * DEPTH) ** 0.25
DEEPNORM_BETA = (8 * DEPTH) ** -0.25

SB_W = SB_HEADS * HEAD_DIM
SWA_Q_W = SWA_HEADS * HEAD_DIM
SWA_KV_W = SWA_KV_HEADS * HEAD_DIM
MEM_W = MEM_HEADS * MEM_HEAD_DIM
IN_SIZES = (SB_W, SB_W, SB_W, SWA_Q_W, SWA_KV_W, SWA_KV_W, MEM_W, N_BRANCH * D_MODEL)
IN_COLS = sum(IN_SIZES)

kernel_name = "hybrid_sb_swa_mem_moe_deepnorm"


def layer_norm(h, g, b):
    hf = h.astype(jnp.float32)
    mu = hf.mean(-1, keepdims=True)
    var = jnp.square(hf - mu).mean(-1, keepdims=True)
    y = (hf - mu) * lax.rsqrt(var + LN_EPS)
    return (y * g.astype(jnp.float32) + b.astype(jnp.float32)).astype(h.dtype)


def stick_breaking_attention(q, k, v):
    B, S, H, d = q.shape
    nblk = S // Q_BLOCK
    scale = d ** -0.5
    r = jnp.arange(Q_BLOCK)
    incl = (r[:, None] >= r[None, :]).astype(jnp.float32)
    outs = []
    for i in range(nblk):
        L = (i + 1) * Q_BLOCK
        z = jnp.einsum('bqhd,bshd->bhqs', q[:, i * Q_BLOCK:L], k[:, :L],
                       preferred_element_type=jnp.float32) * scale
        causal = jnp.arange(L)[None, :] < (i * Q_BLOCK + r)[:, None]
        log_1m = jnp.where(causal, jax.nn.log_sigmoid(-z), 0.0)
        lb = log_1m.reshape(B, H, Q_BLOCK, i + 1, Q_BLOCK)
        within = jnp.einsum('bhqkj,js->bhqks', lb, incl, precision=lax.Precision.HIGHEST)
        m = jnp.arange(i + 1)
        strict = (m[:, None] > m[None, :]).astype(jnp.float32)
        later = jnp.einsum('bhqm,mk->bhqk', lb.sum(-1), strict,
                           precision=lax.Precision.HIGHEST)
        suffix = (within + later[..., None]).reshape(B, H, Q_BLOCK, L) - log_1m
        a = jnp.where(causal, jnp.exp(jax.nn.log_sigmoid(z) + suffix), 0.0)
        outs.append(jnp.einsum('bhqs,bshd->bqhd', a.astype(v.dtype), v[:, :L]))
    return jnp.concatenate(outs, axis=1).reshape(B, S, H * d)


def sliding_window_attention(q, k, v, sinks):
    B, S, Hq, d = q.shape
    Hkv = k.shape[2]
    G = Hq // Hkv
    nblk = S // Q_BLOCK
    scale = d ** -0.5
    qb = q.reshape(B, nblk, Q_BLOCK, Hkv, G, d)

    def band(t):
        tb = t.reshape(B, nblk, Q_BLOCK, Hkv, d)
        prev = jnp.pad(tb[:, :-1], ((0, 0), (1, 0), (0, 0), (0, 0), (0, 0)))
        return jnp.concatenate([prev, tb], axis=2)

    kb, vb = band(k), band(v)
    z = jnp.einsum('bnqhgd,bnkhd->bnhgqk', qb, kb,
                   preferred_element_type=jnp.float32) * scale
    q_rel = jnp.arange(Q_BLOCK)[:, None] + Q_BLOCK
    k_rel = jnp.arange(2 * Q_BLOCK)[None, :]
    dist = (q_rel - k_rel).astype(jnp.float32)
    in_window = (dist >= 0) & (dist < SWA_WINDOW)
    real_key = (jnp.arange(nblk)[:, None, None] > 0) | (k_rel[None] >= Q_BLOCK)
    mask = in_window[None] & real_key
    slopes = jnp.exp2(-8.0 * jnp.arange(1, Hq + 1, dtype=jnp.float32) / Hq)
    alibi = -slopes.reshape(Hkv, G)[:, :, None, None] * dist
    z = jnp.where(mask[None, :, None, None], z + alibi, -jnp.inf)
    sink = jnp.broadcast_to(sinks.astype(jnp.float32).reshape(Hkv, G)[:, :, None, None],
                            z.shape[:-1] + (1,))
    p = jax.nn.softmax(jnp.concatenate([z, sink], axis=-1), axis=-1)[..., :-1]
    o = jnp.einsum('bnhgqk,bnkhd->bnqhgd', p.astype(vb.dtype), vb)
    return o.reshape(B, S, Hq * d)


def memory_attention(q, mem_k, mem_v):
    B, S, H, d = q.shape
    z = jnp.einsum('bshd,bmhd->bhsm', q, mem_k,
                   preferred_element_type=jnp.float32) * d ** -0.5
    p = jax.nn.softmax(z, axis=-1)
    o = jnp.einsum('bhsm,bmhd->bshd', p.astype(mem_v.dtype), mem_v)
    return o.reshape(B, S, H * d)


def swiglu(h, w_gu, w_down):
    gate, up = jnp.split(h @ w_gu, 2, axis=-1)
    return (jax.nn.silu(gate) * up) @ w_down


def moe(u, w_router, router_bias, w_e_gu, w_e_down, w_s_gu, w_s_down):
    B, S, D = u.shape
    t = u.reshape(B * S, D)
    T = t.shape[0]
    scores = jax.nn.sigmoid((t @ w_router).astype(jnp.float32))
    biased = scores + router_bias.astype(jnp.float32)
    grp = biased.reshape(T, N_GROUPS, N_EXPERTS // N_GROUPS)
    grp_score = lax.top_k(grp, 2)[0].sum(-1)
    _, top_grp = lax.top_k(grp_score, TOPK_GROUPS)
    grp_mask = jax.nn.one_hot(top_grp, N_GROUPS, dtype=jnp.float32).sum(1) > 0
    expert_mask = jnp.repeat(grp_mask, N_EXPERTS // N_GROUPS, axis=1)
    _, idx = lax.top_k(jnp.where(expert_mask, biased, -jnp.inf), TOP_K)
    w = jnp.take_along_axis(scores, idx, axis=-1)
    w = w / w.sum(-1, keepdims=True) * ROUTED_SCALE
    n_slots = T * TOP_K
    flat_e = idx.reshape(-1)
    order = jnp.argsort(flat_e)
    sorted_e = flat_e[order]
    tok = order // TOP_K
    sizes = jnp.bincount(flat_e, length=N_EXPERTS).astype(jnp.int32)
    padded = (sizes + MOE_BLOCK - 1) // MOE_BLOCK * MOE_BLOCK
    start = jnp.cumsum(sizes) - sizes
    pad_start = jnp.cumsum(padded) - padded
    dest = pad_start[sorted_e] + jnp.arange(n_slots, dtype=jnp.int32) - start[sorted_e]
    n_blocks = -(-n_slots // MOE_BLOCK) + N_EXPERTS
    xs = jnp.zeros((n_blocks * MOE_BLOCK, D), t.dtype).at[dest].set(t[tok])
    blk_start = jnp.arange(n_blocks, dtype=jnp.int32) * MOE_BLOCK
    blk_e = jnp.minimum(((pad_start + padded)[None, :] <= blk_start[:, None]).sum(-1),
                        N_EXPERTS - 1)
    xb = xs.reshape(n_blocks, MOE_BLOCK, D)
    gate, up = jnp.split(jnp.einsum('nbd,ndf->nbf', xb, w_e_gu[blk_e]), 2, axis=-1)
    yb = jnp.einsum('nbf,nfd->nbd', jax.nn.silu(gate) * up, w_e_down[blk_e])
    ys = yb.reshape(n_blocks * MOE_BLOCK, D)[dest]
    ys = ys * w.reshape(-1)[order][:, None].astype(ys.dtype)
    routed = jnp.zeros_like(t).at[tok].add(ys)
    shared = swiglu(t, w_s_gu, w_s_down)
    return (routed + shared).reshape(B, S, D)


def setup_inputs(seed: int = 0) -> dict:
    key = jax.random.key(seed)
    ks = jax.random.split(key, 24)
    nrm = lambda k, shape, fan_in, s=1.0: jax.random.normal(k, shape, jnp.float32) * (fan_in ** -0.5) * s
    col_scale = jnp.concatenate([
        jnp.ones(2 * SB_W), jnp.full((SB_W,), DEEPNORM_BETA),
        jnp.ones(SWA_Q_W + SWA_KV_W), jnp.full((SWA_KV_W,), DEEPNORM_BETA),
        jnp.ones(MEM_W + N_BRANCH * D_MODEL)])
    mem_scale = jnp.concatenate([jnp.ones(MEM_W), jnp.full((MEM_W,), DEEPNORM_BETA)])
    return {
        "x": jax.random.normal(ks[0], (BATCH, SEQ, D_MODEL), jnp.float32),
        "mem": jax.random.normal(ks[1], (BATCH, MEM_LEN, D_MODEL), jnp.float32),
        "w_in": nrm(ks[2], (D_MODEL, IN_COLS), D_MODEL) * col_scale,
        "b_gate": 0.02 * jax.random.normal(ks[3], (N_BRANCH * D_MODEL,), jnp.float32),
        "w_mem_kv": nrm(ks[4], (D_MODEL, 2 * MEM_W), D_MODEL) * mem_scale,
        "sinks": 0.5 * jax.random.normal(ks[5], (SWA_HEADS,), jnp.float32),
        "w_o_sb": nrm(ks[6], (SB_W, D_MODEL), SB_W),
        "w_o_swa": nrm(ks[7], (SWA_Q_W, D_MODEL), SWA_Q_W),
        "w_o_mem": nrm(ks[8], (MEM_W, D_MODEL), MEM_W),
        "w_out": nrm(ks[9], (D_MODEL, D_MODEL), D_MODEL, DEEPNORM_BETA),
        "ln1_g": 1.0 + 0.02 * jax.random.normal(ks[10], (D_MODEL,), jnp.float32),
        "ln1_b": 0.02 * jax.random.normal(ks[11], (D_MODEL,), jnp.float32),
        "w_router": nrm(ks[12], (D_MODEL, N_EXPERTS), D_MODEL),
        "router_bias": 0.01 * jax.random.normal(ks[13], (N_EXPERTS,), jnp.float32),
        "w_e_gu": nrm(ks[14], (N_EXPERTS, D_MODEL, 2 * EXPERT_FF), D_MODEL),
        "w_e_down": nrm(ks[15], (N_EXPERTS, EXPERT_FF, D_MODEL), EXPERT_FF, DEEPNORM_BETA),
        "w_s_gu": nrm(ks[16], (D_MODEL, 2 * SHARED_FF), D_MODEL),
        "w_s_down": nrm(ks[17], (SHARED_FF, D_MODEL), SHARED_FF, DEEPNORM_BETA),
        "ln2_g": 1.0 + 0.02 * jax.random.normal(ks[18], (D_MODEL,), jnp.float32),
        "ln2_b": 0.02 * jax.random.normal(ks[19], (D_MODEL,), jnp.float32),
    }


def reference(x, mem, w_in, b_gate, w_mem_kv, sinks, w_o_sb, w_o_swa, w_o_mem, w_out,
              ln1_g, ln1_b, w_router, router_bias, w_e_gu, w_e_down, w_s_gu, w_s_down,
              ln2_g, ln2_b):
    B, S, D = x.shape
    offsets = np.cumsum(IN_SIZES)[:-1].tolist()
    for _ in range(DEPTH):
        proj = x @ w_in
        q_sb, k_sb, v_sb, q_sw, k_sw, v_sw, q_m, gate_logits = jnp.split(proj, offsets, axis=-1)
        o_sb = stick_breaking_attention(q_sb.reshape(B, S, SB_HEADS, HEAD_DIM),
                                        k_sb.reshape(B, S, SB_HEADS, HEAD_DIM),
                                        v_sb.reshape(B, S, SB_HEADS, HEAD_DIM))
        o_sw = sliding_window_attention(q_sw.reshape(B, S, SWA_HEADS, HEAD_DIM),
                                        k_sw.reshape(B, S, SWA_KV_HEADS, HEAD_DIM),
                                        v_sw.reshape(B, S, SWA_KV_HEADS, HEAD_DIM),
                                        sinks)
        mk, mv = jnp.split(mem @ w_mem_kv, 2, axis=-1)
        M = mem.shape[1]
        o_m = memory_attention(q_m.reshape(B, S, MEM_HEADS, MEM_HEAD_DIM),
                               mk.reshape(B, M, MEM_HEADS, MEM_HEAD_DIM),
                               mv.reshape(B, M, MEM_HEADS, MEM_HEAD_DIM))
        g_sb, g_sw, g_m = jnp.split(jax.nn.sigmoid(gate_logits + b_gate), N_BRANCH, axis=-1)
        merged = g_sb * (o_sb @ w_o_sb) + g_sw * (o_sw @ w_o_swa) + g_m * (o_m @ w_o_mem)
        x = layer_norm(DEEPNORM_ALPHA * x + merged @ w_out, ln1_g, ln1_b)
        x = layer_norm(DEEPNORM_ALPHA * x + moe(x, w_router, router_bias, w_e_gu, w_e_down,
                                                 w_s_gu, w_s_down), ln2_g, ln2_b)
    return x
```

```python
import functools

import jax
import jax.numpy as jnp
from jax import lax
from jax.experimental import pallas as pl
from jax.experimental.pallas import tpu as pltpu

F32 = jnp.float32
BF16 = jnp.bfloat16
I32 = jnp.int32

HEAD_DIM = 64
SB_HEADS = 8
SWA_HEADS = 8
SWA_KV_HEADS = 2
SWA_GROUP = SWA_HEADS // SWA_KV_HEADS
SWA_WINDOW = 128
MEM_HEADS = 4
MEM_HEAD_DIM = 128
N_BRANCH = 3
N_EXPERTS = 256
TOP_K = 8
N_GROUPS = 8
GROUP_SIZE = N_EXPERTS // N_GROUPS
TOPK_GROUPS = 4
EXPERT_FF = 256
SHARED_FF = 256
ROUTED_SCALE = 2.5
LN_EPS = 1e-5
DEPTH = 1
DEEPNORM_ALPHA = (2 * DEPTH) ** 0.25

LANES = 128
VMEM_LIMIT = 56 * 1024 * 1024

D_GATE = 0
W_GATE = 3072
C_QSB = 3072
C_KSB = 3584
C_VSB = 4096
C_QSW = 4608
C_QM = 5632
C_KSW = 6144
C_VSW = 6272
PROJ_COLS = 6656
PROJ_CHUNK = 512

SB_SKIP = 110.0

EXPERT_ROWS = 512


def _nt_dot(a, b):
    return lax.dot_general(a, b, (((1,), (1,)), ((), ())), preferred_element_type=F32)


def _sigmoid(x):
    return 1.0 / (1.0 + jnp.exp(-x))


def _layer_norm(h, g, b):
    mu = jnp.mean(h, axis=-1, keepdims=True)
    d = h - mu
    var = jnp.mean(d * d, axis=-1, keepdims=True)
    return d * lax.rsqrt(var + LN_EPS) * g + b


def _resident(shape):
    nd = len(shape)
    return pl.BlockSpec(shape, lambda *_: (0,) * nd, pipeline_mode=pl.Buffered(1))


def _proj_kernel(x_ref, w_ref, b_ref, o_ref, *, gate_cols):
    xb = x_ref[...].astype(BF16)
    for j in range(o_ref.shape[1] // PROJ_CHUNK):
        cols = slice(j * PROJ_CHUNK, (j + 1) * PROJ_CHUNK)
        acc = jnp.dot(xb, w_ref[:, cols], preferred_element_type=F32)
        if (j + 1) * PROJ_CHUNK <= gate_cols:
            acc = _sigmoid(acc + b_ref[:, cols])
        o_ref[:, cols] = acc.astype(o_ref.dtype)


def _proj(x2, w_all, b_gate, tm):
    t, d = x2.shape
    n = w_all.shape[1]
    return pl.pallas_call(
        functools.partial(_proj_kernel, gate_cols=b_gate.shape[1]),
        grid=(t // tm,),
        in_specs=[pl.BlockSpec((tm, d), lambda i: (i, 0)),
                  _resident((d, n)),
                  _resident(b_gate.shape)],
        out_specs=pl.BlockSpec((tm, n), lambda i: (i, 0)),
        out_shape=jax.ShapeDtypeStruct((t, n), BF16),
        compiler_params=pltpu.CompilerParams(
            dimension_semantics=("parallel",), vmem_limit_bytes=VMEM_LIMIT),
        name="in_proj",
    )(x2, w_all, b_gate)


def _mm_kernel(x_ref, w_ref, o_ref):
    o_ref[...] = jnp.dot(x_ref[...].astype(BF16), w_ref[...],
                         preferred_element_type=F32).astype(o_ref.dtype)


def _matmul_bf16(x2, w, tm):
    t, d = x2.shape
    n = w.shape[1]
    return pl.pallas_call(
        _mm_kernel,
        grid=(t // tm,),
        in_specs=[pl.BlockSpec((tm, d), lambda i: (i, 0)), _resident((d, n))],
        out_specs=pl.BlockSpec((tm, n), lambda i: (i, 0)),
        out_shape=jax.ShapeDtypeStruct((t, n), BF16),
        compiler_params=pltpu.CompilerParams(
            dimension_semantics=("parallel",), vmem_limit_bytes=VMEM_LIMIT),
        name="mem_kv_proj",
    )(x2, w)


def _sb_kernel(q_ref, k_ref, v_ref, o_ref, *, tq):
    i = pl.program_id(2)
    q = q_ref[...]
    lane = lax.broadcasted_iota(I32, (1, LANES), 1)
    r = lax.broadcasted_iota(I32, (tq, tq), 0)
    c = lax.broadcasted_iota(I32, (tq, tq), 1)
    tri = (r >= c).astype(BF16)
    causal = c < r

    def block(qh, kb, carry, acc, diag):
        rows = pl.ds(pl.multiple_of(kb * tq, tq), tq)
        k = k_ref[rows, :]
        v = v_ref[rows, :]
        z = _nt_dot(qh, k)
        sp = jnp.maximum(z, 0.0) + jnp.log1p(jnp.exp(-jnp.abs(z)))
        log_b = z - sp
        log_1m = -sp
        if diag:
            log_1m = jnp.where(causal, log_1m, 0.0)
        hi = log_1m.astype(BF16)
        lo = (log_1m - hi.astype(F32)).astype(BF16)
        suffix = (jnp.dot(hi, tri, preferred_element_type=F32)
                  + jnp.dot(lo, tri, preferred_element_type=F32))
        a = jnp.exp(log_b + (suffix - log_1m) + carry)
        if diag:
            a = jnp.where(causal, a, 0.0)
        acc = acc + jnp.dot(a.astype(BF16), v, preferred_element_type=F32)
        carry = carry + suffix[:, 0:1]
        return carry, acc

    out = jnp.zeros((tq, LANES), F32)
    for h in range(LANES // HEAD_DIM):
        hmask = (lane >= h * HEAD_DIM) & (lane < (h + 1) * HEAD_DIM)
        qh = jnp.where(hmask, q, jnp.zeros_like(q))
        carry, acc = block(qh, i, jnp.zeros((tq, 1), F32),
                           jnp.zeros((tq, LANES), F32), True)

        def cond(s):
            kb, carry, _ = s
            return (kb >= 0) & (jnp.max(carry) > -SB_SKIP)

        def body(s, qh=qh):
            kb, carry, acc = s
            carry, acc = block(qh, kb, carry, acc, False)
            return kb - 1, carry, acc

        _, _, acc = lax.while_loop(cond, body, (i - 1, carry, acc))
        out = jnp.where(hmask, acc, out)
    o_ref[...] = out.astype(o_ref.dtype)


def _sb_attention(p, batch, seq, tq):
    t = batch * seq
    nq = seq // tq
    npair = SB_HEADS * HEAD_DIM // LANES
    qc, kc, vc = C_QSB // LANES, C_KSB // LANES, C_VSB // LANES
    return pl.pallas_call(
        functools.partial(_sb_kernel, tq=tq),
        grid=(batch, npair, nq),
        in_specs=[pl.BlockSpec((tq, LANES), lambda b, h, i: (b * nq + i, qc + h)),
                  pl.BlockSpec((seq, LANES), lambda b, h, i: (b, kc + h)),
                  pl.BlockSpec((seq, LANES), lambda b, h, i: (b, vc + h))],
        out_specs=pl.BlockSpec((tq, LANES), lambda b, h, i: (b * nq + i, h)),
        out_shape=jax.ShapeDtypeStruct((t, SB_HEADS * HEAD_DIM), BF16),
        compiler_params=pltpu.CompilerParams(
            dimension_semantics=("parallel", "parallel", "arbitrary"),
            vmem_limit_bytes=VMEM_LIMIT),
        name="sb_attention",
    )(p, p, p)


def _swa_kernel(hp_ref, q_ref, kp_ref, kc_ref, vp_ref, vc_ref, o_ref):
    h = pl.program_id(1)
    n = pl.program_id(2)
    blk = q_ref.shape[0]
    r = lax.broadcasted_iota(I32, (blk, blk), 0)
    c = lax.broadcasted_iota(I32, (blk, blk), 1)
    dist_c = (r - c).astype(F32)
    dist_p = dist_c + float(blk)
    mask_c = c <= r
    mask_p = (c > r) & (n > 0)
    lane = lax.broadcasted_iota(I32, (1, LANES), 1)
    kvmask = (lane >= h * HEAD_DIM) & (lane < (h + 1) * HEAD_DIM)
    kp, kc, vp, vc = kp_ref[...], kc_ref[...], vp_ref[...], vc_ref[...]
    neg = jnp.float32(-jnp.inf)
    for g in range(SWA_GROUP):
        hq = h * SWA_GROUP + g
        slope = hp_ref[0, hq]
        sink = hp_ref[1, hq]
        qg = q_ref[:, g * LANES:(g + 1) * LANES]
        zp = jnp.where(mask_p, _nt_dot(qg, kp) - slope * dist_p, neg)
        zc = jnp.where(mask_c, _nt_dot(qg, kc) - slope * dist_c, neg)
        m = jnp.maximum(jnp.max(zp, axis=1, keepdims=True),
                        jnp.max(zc, axis=1, keepdims=True))
        m = jnp.maximum(m, sink)
        pp = jnp.exp(zp - m)
        pc = jnp.exp(zc - m)
        den = (jnp.sum(pp, axis=1, keepdims=True) + jnp.sum(pc, axis=1, keepdims=True)
               + jnp.exp(sink - m))
        o = (jnp.dot(pp.astype(BF16), vp, preferred_element_type=F32)
             + jnp.dot(pc.astype(BF16), vc, preferred_element_type=F32)) / den
        o_ref[:, g * LANES:(g + 1) * LANES] = jnp.where(kvmask, o, 0.0).astype(o_ref.dtype)


def _swa_attention(p, head_params, batch, seq):
    blk = SWA_WINDOW
    t = batch * seq
    nb = seq // blk
    qw = SWA_GROUP * LANES
    qc, kc, vc = C_QSW // qw, C_KSW // LANES, C_VSW // LANES
    cur = lambda col: (lambda b, h, n, hp: (b * nb + n, col))
    prev = lambda col: (lambda b, h, n, hp: (b * nb + jnp.maximum(n - 1, 0), col))
    grid_spec = pltpu.PrefetchScalarGridSpec(
        num_scalar_prefetch=1,
        grid=(batch, SWA_KV_HEADS, nb),
        in_specs=[pl.BlockSpec((blk, qw), lambda b, h, n, hp: (b * nb + n, qc + h)),
                  pl.BlockSpec((blk, LANES), prev(kc)),
                  pl.BlockSpec((blk, LANES), cur(kc)),
                  pl.BlockSpec((blk, LANES), prev(vc)),
                  pl.BlockSpec((blk, LANES), cur(vc))],
        out_specs=pl.BlockSpec((blk, qw), lambda b, h, n, hp: (b * nb + n, h)),
    )
    return pl.pallas_call(
        _swa_kernel,
        grid_spec=grid_spec,
        out_shape=jax.ShapeDtypeStruct((t, SWA_HEADS * LANES), BF16),
        compiler_params=pltpu.CompilerParams(
            dimension_semantics=("parallel", "parallel", "arbitrary"),
            vmem_limit_bytes=VMEM_LIMIT),
        name="swa_attention",
    )(head_params, p, p, p, p, p)


def _mem_kernel(q_ref, mk_ref, mv_ref, o_ref):
    scale = MEM_HEAD_DIM ** -0.5
    for h in range(MEM_HEADS):
        cols = slice(h * MEM_HEAD_DIM, (h + 1) * MEM_HEAD_DIM)
        z = _nt_dot(q_ref[:, cols], mk_ref[:, cols]) * scale
        m = jnp.max(z, axis=1, keepdims=True)
        p = jnp.exp(z - m)
        den = jnp.sum(p, axis=1, keepdims=True)
        o = jnp.dot(p.astype(BF16), mv_ref[:, cols], preferred_element_type=F32) / den
        o_ref[:, cols] = o.astype(o_ref.dtype)


def _mem_attention(p, mkv, batch, seq, mem_len, tq):
    t = batch * seq
    nq = seq // tq
    w = MEM_HEADS * MEM_HEAD_DIM
    return pl.pallas_call(
        _mem_kernel,
        grid=(batch, nq),
        in_specs=[pl.BlockSpec((tq, w), lambda b, i: (b * nq + i, C_QM // w)),
                  pl.BlockSpec((mem_len, w), lambda b, i: (b, 0)),
                  pl.BlockSpec((mem_len, w), lambda b, i: (b, 1))],
        out_specs=pl.BlockSpec((tq, w), lambda b, i: (b * nq + i, 0)),
        out_shape=jax.ShapeDtypeStruct((t, w), BF16),
        compiler_params=pltpu.CompilerParams(
            dimension_semantics=("parallel", "arbitrary"), vmem_limit_bytes=VMEM_LIMIT),
        name="mem_attention",
    )(p, mkv, mkv)


def _merge_kernel(osb_ref, osw_ref, om_ref, g_ref, x_ref, wsb_ref, wsw_ref, wm_ref,
                  wout_ref, lng_ref, lnb_ref, x1_ref):
    d = x_ref.shape[1]
    merged = g_ref[:, 0:d].astype(F32) * jnp.dot(
        osb_ref[...], wsb_ref[...], preferred_element_type=F32)
    merged += g_ref[:, d:2 * d].astype(F32) * jnp.dot(
        osw_ref[...], wsw_ref[...], preferred_element_type=F32)
    merged += g_ref[:, 2 * d:3 * d].astype(F32) * jnp.dot(
        om_ref[...], wm_ref[...], preferred_element_type=F32)
    y = jnp.dot(merged.astype(BF16), wout_ref[...], preferred_element_type=F32)
    x1_ref[...] = _layer_norm(DEEPNORM_ALPHA * x_ref[...] + y, lng_ref[...], lnb_ref[...])


def _merge(o_sb, o_sw, o_m, p, x2, w_sb, w_sw, w_m, w_out, ln_g, ln_b, tm):
    t, d = x2.shape
    row = lambda w: pl.BlockSpec((tm, w), lambda i: (i, 0))
    return pl.pallas_call(
        _merge_kernel,
        grid=(t // tm,),
        in_specs=[row(o_sb.shape[1]), row(o_sw.shape[1]), row(o_m.shape[1]),
                  pl.BlockSpec((tm, N_BRANCH * d), lambda i: (i, 0)),
                  row(d),
                  _resident(w_sb.shape), _resident(w_sw.shape), _resident(w_m.shape),
                  _resident(w_out.shape), _resident(ln_g.shape), _resident(ln_b.shape)],
        out_specs=row(d),
        out_shape=jax.ShapeDtypeStruct((t, d), F32),
        compiler_params=pltpu.CompilerParams(
            dimension_semantics=("parallel",), vmem_limit_bytes=VMEM_LIMIT),
        name="merge_ln1",
    )(o_sb, o_sw, o_m, p, x2, w_sb, w_sw, w_m, w_out, ln_g, ln_b)


def _router_kernel(x_ref, wh_ref, wl_ref, bias_ref, idx_ref, rank_ref, wgt_ref, cnt_ref,
                   carry_ref):
    step = pl.program_id(0)
    tr = x_ref.shape[0]

    @pl.when(step == 0)
    def _():
        carry_ref[...] = jnp.zeros_like(carry_ref)

    x = x_ref[...]
    xh = x.astype(BF16)
    xl = (x - xh.astype(F32)).astype(BF16)
    logits = _nt_dot(wh_ref[...], xh) + _nt_dot(wh_ref[...], xl) + _nt_dot(wl_ref[...], xh)
    scores = _sigmoid(logits)
    biased = scores + bias_ref[...]
    neg = jnp.float32(-jnp.inf)

    sub = lax.broadcasted_iota(I32, (GROUP_SIZE, tr), 0)
    gscore = []
    for g in range(N_GROUPS):
        blk = biased[g * GROUP_SIZE:(g + 1) * GROUP_SIZE, :]
        m1 = jnp.max(blk, axis=0, keepdims=True)
        i1 = jnp.min(jnp.where(blk == m1, sub, GROUP_SIZE), axis=0, keepdims=True)
        m2 = jnp.max(jnp.where(sub == i1, neg, blk), axis=0, keepdims=True)
        gscore.append(m1 + m2)
    gs = jnp.concatenate(gscore, axis=0)

    giota = lax.broadcasted_iota(I32, (N_GROUPS, tr), 0)
    gsel = jnp.zeros((N_GROUPS, tr), F32)
    for _ in range(TOPK_GROUPS):
        m = jnp.max(gs, axis=0, keepdims=True)
        gi = jnp.min(jnp.where(gs == m, giota, N_GROUPS), axis=0, keepdims=True)
        hit = giota == gi
        gsel = jnp.where(hit, 1.0, gsel)
        gs = jnp.where(hit, neg, gs)

    masked = jnp.concatenate(
        [jnp.where(gsel[g:g + 1, :] > 0.0, biased[g * GROUP_SIZE:(g + 1) * GROUP_SIZE, :], neg)
         for g in range(N_GROUPS)], axis=0)

    eiota = lax.broadcasted_iota(I32, (N_EXPERTS, tr), 0)
    sel = jnp.zeros((N_EXPERTS, tr), F32)
    idx_rows, w_rows = [], []
    for _ in range(TOP_K):
        m = jnp.max(masked, axis=0, keepdims=True)
        ei = jnp.min(jnp.where(masked == m, eiota, N_EXPERTS), axis=0, keepdims=True)
        hit = eiota == ei
        idx_rows.append(ei)
        w_rows.append(jnp.sum(jnp.where(hit, scores, 0.0), axis=0, keepdims=True))
        sel = jnp.where(hit, 1.0, sel)
        masked = jnp.where(hit, neg, masked)

    wsum = w_rows[0]
    for wk in w_rows[1:]:
        wsum = wsum + wk
    wgt_ref[...] = jnp.concatenate(w_rows, axis=0) / wsum * ROUTED_SCALE
    idx_ref[...] = jnp.concatenate(idx_rows, axis=0)

    a = lax.broadcasted_iota(I32, (tr, tr), 0)
    b = lax.broadcasted_iota(I32, (tr, tr), 1)
    before = (a < b).astype(BF16)
    rank = jnp.dot(sel.astype(BF16), before, preferred_element_type=F32) + carry_ref[...]
    rank_rows = [jnp.sum(jnp.where(eiota == ei, rank, 0.0), axis=0, keepdims=True)
                 for ei in idx_rows]
    rank_ref[...] = jnp.concatenate(rank_rows, axis=0).astype(I32)
    carry_ref[...] = carry_ref[...] + jnp.sum(sel, axis=1, keepdims=True)
    cnt_ref[...] = carry_ref[...]


def _router(x1, wr_hi, wr_lo, bias_col, tr):
    t, d = x1.shape
    slot = pl.BlockSpec((TOP_K, tr), lambda i: (0, i))
    return pl.pallas_call(
        _router_kernel,
        grid=(t // tr,),
        in_specs=[pl.BlockSpec((tr, d), lambda i: (i, 0)),
                  _resident(wr_hi.shape), _resident(wr_lo.shape), _resident(bias_col.shape)],
        out_specs=[slot, slot, slot, pl.BlockSpec((N_EXPERTS, 1), lambda i: (0, 0))],
        out_shape=[jax.ShapeDtypeStruct((TOP_K, t), I32),
                   jax.ShapeDtypeStruct((TOP_K, t), I32),
                   jax.ShapeDtypeStruct((TOP_K, t), F32),
                   jax.ShapeDtypeStruct((N_EXPERTS, 1), F32)],
        scratch_shapes=[pltpu.VMEM((N_EXPERTS, 1), F32)],
        compiler_params=pltpu.CompilerParams(
            dimension_semantics=("arbitrary",), vmem_limit_bytes=VMEM_LIMIT),
        name="router",
    )(x1, wr_hi, wr_lo, bias_col)


def _dest_kernel(idx_ref, rank_ref, start_ref, dest_ref):
    tr = idx_ref.shape[1]
    eiota = lax.broadcasted_iota(I32, (N_EXPERTS, tr), 0)
    rows = []
    for k in range(TOP_K):
        hit = eiota == idx_ref[k:k + 1, :]
        rows.append(jnp.sum(jnp.where(hit, start_ref[...], 0), axis=0, keepdims=True))
    dest_ref[...] = jnp.concatenate(rows, axis=0) + rank_ref[...]


def _dest(idx, rank, start_col, tr):
    t = idx.shape[1]
    slot = pl.BlockSpec((TOP_K, tr), lambda i: (0, i))
    return pl.pallas_call(
        _dest_kernel,
        grid=(t // tr,),
        in_specs=[slot, slot, _resident(start_col.shape)],
        out_specs=slot,
        out_shape=jax.ShapeDtypeStruct((TOP_K, t), I32),
        compiler_params=pltpu.CompilerParams(dimension_semantics=("parallel",)),
        name="slot_dest",
    )(idx, rank, start_col)


def _dispatch_kernel(dest_ref, x_ref, xs_ref, sem):
    td = x_ref.shape[0]

    def row_copy(t, d):
        return pltpu.make_async_copy(x_ref.at[pl.ds(t, 1), :], xs_ref.at[pl.ds(d, 1), :], sem)

    def issue(t, carry):
        for k in range(TOP_K):
            row_copy(t, dest_ref[k, t]).start()
        return carry

    lax.fori_loop(0, td, issue, 0)

    def drain(t, carry):
        for k in range(TOP_K):
            row_copy(t, dest_ref[k, t]).wait()
        return carry

    lax.fori_loop(0, td, drain, 0)


def _dispatch(x1, dest, n_rows, td):
    t, d = x1.shape
    return pl.pallas_call(
        _dispatch_kernel,
        grid=(t // td,),
        in_specs=[pl.BlockSpec((TOP_K, td), lambda i: (0, i), memory_space=pltpu.SMEM),
                  pl.BlockSpec((td, d), lambda i: (i, 0))],
        out_specs=pl.BlockSpec(memory_space=pl.ANY),
        out_shape=jax.ShapeDtypeStruct((n_rows, d), F32),
        scratch_shapes=[pltpu.SemaphoreType.DMA(())],
        compiler_params=pltpu.CompilerParams(
            dimension_semantics=("arbitrary",), has_side_effects=True),
        name="dispatch",
    )(dest, x1)


def _expert_kernel(blk_e_ref, nv_ref, xs_ref, wgu_ref, wd_ref, ys_ref, wgu_s, wd_s):
    n = pl.program_id(0)
    valid = n < nv_ref[0]
    e = blk_e_ref[n]
    e_prev = blk_e_ref[jnp.maximum(n - 1, 0)]

    @pl.when(valid & ((n == 0) | (e != e_prev)))
    def _():
        wgu_s[...] = wgu_ref[0].astype(BF16)
        wd_s[...] = wd_ref[0].astype(BF16)

    @pl.when(valid)
    def _():
        ff = wd_s.shape[0]
        h = jnp.dot(xs_ref[...].astype(BF16), wgu_s[...], preferred_element_type=F32)
        gate, up = h[:, :ff], h[:, ff:]
        act = gate * _sigmoid(gate) * up
        ys_ref[...] = jnp.dot(act.astype(BF16), wd_s[...], preferred_element_type=F32)


def _experts(xs, blk_e, n_valid, w_gu, w_down):
    n_rows, d = xs.shape
    nb = n_rows // EXPERT_ROWS
    ff2 = w_gu.shape[2]
    ff = w_down.shape[1]
    last = lambda n, nv: jnp.minimum(n, nv[0] - 1)
    grid_spec = pltpu.PrefetchScalarGridSpec(
        num_scalar_prefetch=2,
        grid=(nb,),
        in_specs=[pl.BlockSpec((EXPERT_ROWS, d), lambda n, be, nv: (last(n, nv), 0)),
                  pl.BlockSpec((1, d, ff2), lambda n, be, nv: (be[n], 0, 0)),
                  pl.BlockSpec((1, ff, d), lambda n, be, nv: (be[n], 0, 0))],
        out_specs=pl.BlockSpec((EXPERT_ROWS, d), lambda n, be, nv: (last(n, nv), 0)),
        scratch_shapes=[pltpu.VMEM((d, ff2), BF16), pltpu.VMEM((ff, d), BF16)],
    )
    return pl.pallas_call(
        _expert_kernel,
        grid_spec=grid_spec,
        out_shape=jax.ShapeDtypeStruct((n_rows, d), F32),
        compiler_params=pltpu.CompilerParams(
            dimension_semantics=("arbitrary",), vmem_limit_bytes=VMEM_LIMIT),
        name="experts",
    )(blk_e, n_valid, xs, w_gu, w_down)


def _combine_kernel(dest_ref, x1_ref, wt_ref, ys_ref, wsgu_ref, wsd_ref, lng_ref, lnb_ref,
                    o_ref, buf, sem):
    tc = x1_ref.shape[0]

    def row_copy(t, k, d):
        return pltpu.make_async_copy(ys_ref.at[pl.ds(d, 1), :], buf.at[k, pl.ds(t, 1), :], sem)

    def issue(t, carry):
        for k in range(TOP_K):
            row_copy(t, k, dest_ref[k, t]).start()
        return carry

    lax.fori_loop(0, tc, issue, 0)

    x1 = x1_ref[...]
    ff = wsd_ref.shape[0]
    h = jnp.dot(x1.astype(BF16), wsgu_ref[...], preferred_element_type=F32)
    gate, up = h[:, :ff], h[:, ff:]
    act = gate * _sigmoid(gate) * up
    moe = jnp.dot(act.astype(BF16), wsd_ref[...], preferred_element_type=F32)

    def drain(t, carry):
        for k in range(TOP_K):
            row_copy(t, k, dest_ref[k, t]).wait()
        return carry

    lax.fori_loop(0, tc, drain, 0)

    for k in range(TOP_K):
        moe = moe + wt_ref[:, k:k + 1] * buf[k]
    o_ref[...] = _layer_norm(DEEPNORM_ALPHA * x1 + moe, lng_ref[...], lnb_ref[...])


def _combine(x1, dest, wgt_t, ys, ws_gu, ws_down, ln_g, ln_b, tc):
    t, d = x1.shape
    return pl.pallas_call(
        _combine_kernel,
        grid=(t // tc,),
        in_specs=[pl.BlockSpec((TOP_K, tc), lambda i: (0, i), memory_space=pltpu.SMEM),
                  pl.BlockSpec((tc, d), lambda i: (i, 0)),
                  pl.BlockSpec((tc, TOP_K), lambda i: (i, 0)),
                  pl.BlockSpec(memory_space=pl.ANY),
                  _resident(ws_gu.shape), _resident(ws_down.shape),
                  _resident(ln_g.shape), _resident(ln_b.shape)],
        out_specs=pl.BlockSpec((tc, d), lambda i: (i, 0)),
        out_shape=jax.ShapeDtypeStruct((t, d), F32),
        scratch_shapes=[pltpu.VMEM((TOP_K, tc, d), F32), pltpu.SemaphoreType.DMA(())],
        compiler_params=pltpu.CompilerParams(
            dimension_semantics=("arbitrary",), vmem_limit_bytes=VMEM_LIMIT),
        name="combine_ln2",
    )(dest, x1, wgt_t, ys, ws_gu, ws_down, ln_g, ln_b)


def _fused_in_weights(w_in):
    d = w_in.shape[0]
    sizes = (SB_HEADS * HEAD_DIM,) * 3 + (SWA_HEADS * HEAD_DIM, SWA_KV_HEADS * HEAD_DIM,
                                          SWA_KV_HEADS * HEAD_DIM, MEM_HEADS * MEM_HEAD_DIM)
    parts, off = [], 0
    for s in sizes:
        parts.append(w_in[:, off:off + s])
        off += s
    q_sb, k_sb, v_sb, q_sw, k_sw, v_sw, q_m = parts
    gates = w_in[:, off:]
    scale = HEAD_DIM ** -0.5
    q_sw = (q_sw * scale).reshape(d, SWA_KV_HEADS, SWA_GROUP, HEAD_DIM)
    zeros = jnp.zeros((d, SWA_GROUP, HEAD_DIM), w_in.dtype)
    q_sw = jnp.stack([jnp.concatenate([q_sw[:, 0], zeros], axis=-1),
                      jnp.concatenate([zeros, q_sw[:, 1]], axis=-1)], axis=1)
    q_sw = q_sw.reshape(d, SWA_HEADS * LANES)
    used = W_GATE + 3 * SB_HEADS * HEAD_DIM + SWA_HEADS * LANES + q_m.shape[1] + 2 * k_sw.shape[1]
    pad = jnp.zeros((d, PROJ_COLS - used), w_in.dtype)
    return jnp.concatenate([gates, q_sb * scale, k_sb, v_sb, q_sw, q_m, k_sw, v_sw, pad],
                           axis=1).astype(BF16)


def _padded_swa_out_weights(w_o_swa):
    d = w_o_swa.shape[1]
    w = w_o_swa.reshape(SWA_KV_HEADS, SWA_GROUP, HEAD_DIM, d)
    zeros = jnp.zeros((SWA_GROUP, HEAD_DIM, d), w_o_swa.dtype)
    w = jnp.stack([jnp.concatenate([w[0], zeros], axis=1),
                   jnp.concatenate([zeros, w[1]], axis=1)], axis=0)
    return w.reshape(SWA_HEADS * LANES, d).astype(BF16)


def kernel(x, mem, w_in, b_gate, w_mem_kv, sinks, w_o_sb, w_o_swa, w_o_mem, w_out,
           ln1_g, ln1_b, w_router, router_bias, w_e_gu, w_e_down, w_s_gu, w_s_down,
           ln2_g, ln2_b):
    batch, seq, d = x.shape
    mem_len = mem.shape[1]
    t = batch * seq
    x2 = x.reshape(t, d)
    row_tile = min(512, t)

    p = _proj(x2, _fused_in_weights(w_in), b_gate.reshape(1, -1), row_tile)
    mkv = _matmul_bf16(mem.reshape(batch * mem_len, d), w_mem_kv.astype(BF16), mem_len)
    o_sb = _sb_attention(p, batch, seq, min(256, seq))
    slopes = jnp.exp2(-8.0 * jnp.arange(1, SWA_HEADS + 1, dtype=F32) / SWA_HEADS)
    o_sw = _swa_attention(p, jnp.stack([slopes, sinks.astype(F32)]), batch, seq)
    o_m = _mem_attention(p, mkv, batch, seq, mem_len, min(512, seq))
    x1 = _merge(o_sb, o_sw, o_m, p, x2, w_o_sb.astype(BF16), _padded_swa_out_weights(w_o_swa),
                w_o_mem.astype(BF16), w_out.astype(BF16),
                ln1_g.reshape(1, d), ln1_b.reshape(1, d), row_tile)

    out = _moe_ln(x1, w_router, router_bias, w_e_gu, w_e_down, w_s_gu, w_s_down, ln2_g, ln2_b)
    return out.reshape(batch, seq, d)


def _moe_ln(x1, w_router, router_bias, w_e_gu, w_e_down, w_s_gu, w_s_down, ln2_g, ln2_b):
    t, d = x1.shape
    wr_t = w_router.T
    wr_hi = wr_t.astype(BF16)
    wr_lo = (wr_t - wr_hi.astype(F32)).astype(BF16)
    route_tile = min(256, t)
    idx, rank, wgt, cnt = _router(x1, wr_hi, wr_lo, router_bias.reshape(-1, 1).astype(F32),
                                  route_tile)
    counts = cnt[:, 0].astype(I32)
    padded = (counts + EXPERT_ROWS - 1) // EXPERT_ROWS * EXPERT_ROWS
    pad_end = jnp.cumsum(padded)
    pad_start = pad_end - padded
    n_blocks = t * TOP_K // EXPERT_ROWS + N_EXPERTS
    n_valid = pad_end[-1] // EXPERT_ROWS
    blk_start = jnp.arange(n_blocks, dtype=I32) * EXPERT_ROWS
    blk_e = jnp.minimum((pad_end[None, :] <= blk_start[:, None]).sum(-1), N_EXPERTS - 1)
    blk_e = jnp.where(jnp.arange(n_blocks) < n_valid, blk_e, blk_e[n_valid - 1]).astype(I32)
    dest = _dest(idx, rank, pad_start.reshape(-1, 1), route_tile)

    xs = _dispatch(x1, dest, n_blocks * EXPERT_ROWS, min(256, t))
    ys = _experts(xs, blk_e, n_valid.reshape(1).astype(I32), w_e_gu, w_e_down)
    return _combine(x1, dest, wgt.T, ys, w_s_gu.astype(BF16), w_s_down.astype(BF16),
                    ln2_g.reshape(1, d), ln2_b.reshape(1, d), min(256, t))
```

```python
import functools

import jax
import jax.numpy as jnp
from jax import lax
from jax.experimental import pallas as pl
from jax.experimental.pallas import tpu as pltpu

F32 = jnp.float32
BF16 = jnp.bfloat16
I32 = jnp.int32
U32 = jnp.uint32

HEAD_DIM = 64
SB_HEADS = 8
SWA_HEADS = 8
SWA_KV_HEADS = 2
SWA_GROUP = SWA_HEADS // SWA_KV_HEADS
SWA_WINDOW = 128
MEM_HEADS = 4
MEM_HEAD_DIM = 128
N_BRANCH = 3
N_EXPERTS = 256
TOP_K = 8
N_GROUPS = 8
GROUP_SIZE = N_EXPERTS // N_GROUPS
TOPK_GROUPS = 4
EXPERT_FF = 256
SHARED_FF = 256
ROUTED_SCALE = 2.5
LN_EPS = 1e-5
DEPTH = 1
DEEPNORM_ALPHA = (2 * DEPTH) ** 0.25

LANES = 128
VMEM_LIMIT = 56 * 1024 * 1024

D_GATE = 0
W_GATE = 3072
C_QSB = 3072
C_KSB = 3584
C_VSB = 4096
C_QSW = 4608
C_QM = 5632
C_KSW = 6144
C_VSW = 6272
PROJ_COLS = 6656
PROJ_CHUNK = 512

SB_SKIP = 110.0

EXPERT_ROWS = 512


def _nt_dot(a, b):
    return lax.dot_general(a, b, (((1,), (1,)), ((), ())), preferred_element_type=F32)


def _sigmoid(x):
    return 1.0 / (1.0 + jnp.exp(-x))


def _layer_norm(h, g, b):
    mu = jnp.mean(h, axis=-1, keepdims=True)
    d = h - mu
    var = jnp.mean(d * d, axis=-1, keepdims=True)
    return d * lax.rsqrt(var + LN_EPS) * g + b


def _pack_bf16_pair(a, b):
    a_bits = lax.bitcast_convert_type(a.astype(BF16).astype(F32), U32)
    b_bits = lax.bitcast_convert_type(b.astype(BF16).astype(F32), U32)
    return (a_bits >> 16) | b_bits


def _unpack_bf16_pair(w):
    a = lax.bitcast_convert_type(w << 16, F32)
    b = lax.bitcast_convert_type(w & jnp.uint32(0xFFFF0000), F32)
    return a, b


def _resident(shape):
    nd = len(shape)
    return pl.BlockSpec(shape, lambda *_: (0,) * nd, pipeline_mode=pl.Buffered(1))


def _proj_kernel(x_ref, w_ref, b_ref, o_ref, *, gate_cols):
    xb = x_ref[...].astype(BF16)
    for j in range(o_ref.shape[1] // PROJ_CHUNK):
        cols = slice(j * PROJ_CHUNK, (j + 1) * PROJ_CHUNK)
        acc = jnp.dot(xb, w_ref[:, cols], preferred_element_type=F32)
        if (j + 1) * PROJ_CHUNK <= gate_cols:
            acc = _sigmoid(acc + b_ref[:, cols])
        o_ref[:, cols] = acc.astype(o_ref.dtype)


def _proj(x2, w_all, b_gate, tm):
    t, d = x2.shape
    n = w_all.shape[1]
    return pl.pallas_call(
        functools.partial(_proj_kernel, gate_cols=b_gate.shape[1]),
        grid=(t // tm,),
        in_specs=[pl.BlockSpec((tm, d), lambda i: (i, 0)),
                  _resident((d, n)),
                  _resident(b_gate.shape)],
        out_specs=pl.BlockSpec((tm, n), lambda i: (i, 0)),
        out_shape=jax.ShapeDtypeStruct((t, n), BF16),
        compiler_params=pltpu.CompilerParams(
            dimension_semantics=("parallel",), vmem_limit_bytes=VMEM_LIMIT),
        name="in_proj",
    )(x2, w_all, b_gate)


def _mm_kernel(x_ref, w_ref, o_ref):
    o_ref[...] = jnp.dot(x_ref[...].astype(BF16), w_ref[...],
                         preferred_element_type=F32).astype(o_ref.dtype)


def _matmul_bf16(x2, w, tm):
    t, d = x2.shape
    n = w.shape[1]
    return pl.pallas_call(
        _mm_kernel,
        grid=(t // tm,),
        in_specs=[pl.BlockSpec((tm, d), lambda i: (i, 0)), _resident((d, n))],
        out_specs=pl.BlockSpec((tm, n), lambda i: (i, 0)),
        out_shape=jax.ShapeDtypeStruct((t, n), BF16),
        compiler_params=pltpu.CompilerParams(
            dimension_semantics=("parallel",), vmem_limit_bytes=VMEM_LIMIT),
        name="mem_kv_proj",
    )(x2, w)


def _sb_kernel(q_ref, k_ref, v_ref, o_ref, *, tq):
    i = pl.program_id(2)
    q = q_ref[...]
    lane = lax.broadcasted_iota(I32, (1, LANES), 1)
    r = lax.broadcasted_iota(I32, (tq, tq), 0)
    c = lax.broadcasted_iota(I32, (tq, tq), 1)
    tri = (r >= c).astype(BF16)
    causal = c < r
    nh = LANES // HEAD_DIM
    hmasks = [(lane >= h * HEAD_DIM) & (lane < (h + 1) * HEAD_DIM) for h in range(nh)]
    qs = jnp.concatenate([jnp.where(m, q, jnp.zeros_like(q)) for m in hmasks], axis=0)
    causal2 = jnp.concatenate([causal] * nh, axis=0)
    m = nh * tq

    def block(kb, carry, acc, diag):
        rows = pl.ds(pl.multiple_of(kb * tq, tq), tq)
        k = k_ref[rows, :]
        v = v_ref[rows, :]
        z = _nt_dot(qs, k)
        sp = jnp.maximum(z, 0.0) + jnp.log(1.0 + jnp.exp(-jnp.abs(z)))
        log_b = z - sp
        log_1m = -sp
        if diag:
            log_1m = jnp.where(causal2, log_1m, 0.0)
        hi = log_1m.astype(BF16)
        lo = (log_1m - hi.astype(F32)).astype(BF16)
        s2 = jnp.dot(jnp.concatenate([hi, lo], axis=0), tri, preferred_element_type=F32)
        suffix = s2[:m] + s2[m:]
        a = jnp.exp(log_b + (suffix - log_1m) + carry)
        if diag:
            a = jnp.where(causal2, a, 0.0)
        acc = acc + jnp.dot(a.astype(BF16), v, preferred_element_type=F32)
        return carry + suffix[:, 0:1], acc

    carry, acc = block(i, jnp.zeros((m, 1), F32), jnp.zeros((m, LANES), F32), True)

    def cond(s):
        kb, carry, _ = s
        return (kb >= 0) & (jnp.max(carry) > -SB_SKIP)

    def body(s):
        kb, carry, acc = s
        carry, acc = block(kb, carry, acc, False)
        return kb - 1, carry, acc

    _, _, acc = lax.while_loop(cond, body, (i - 1, carry, acc))
    o_ref[...] = jnp.where(hmasks[0], acc[:tq], acc[tq:]).astype(o_ref.dtype)


def _sb_attention(p, batch, seq, tq):
    t = batch * seq
    nq = seq // tq
    npair = SB_HEADS * HEAD_DIM // LANES
    qc, kc, vc = C_QSB // LANES, C_KSB // LANES, C_VSB // LANES
    return pl.pallas_call(
        functools.partial(_sb_kernel, tq=tq),
        grid=(batch, npair, nq),
        in_specs=[pl.BlockSpec((tq, LANES), lambda b, h, i: (b * nq + i, qc + h)),
                  pl.BlockSpec((seq, LANES), lambda b, h, i: (b, kc + h)),
                  pl.BlockSpec((seq, LANES), lambda b, h, i: (b, vc + h))],
        out_specs=pl.BlockSpec((tq, LANES), lambda b, h, i: (b * nq + i, h)),
        out_shape=jax.ShapeDtypeStruct((t, SB_HEADS * HEAD_DIM), BF16),
        compiler_params=pltpu.CompilerParams(
            dimension_semantics=("parallel", "parallel", "arbitrary"),
            vmem_limit_bytes=VMEM_LIMIT),
        name="sb_attention",
    )(p, p, p)


def _swa_kernel(hp_ref, q_ref, kp_ref, kc_ref, vp_ref, vc_ref, o_ref):
    h = pl.program_id(1)
    n = pl.program_id(2)
    blk = q_ref.shape[0]
    r = lax.broadcasted_iota(I32, (blk, blk), 0)
    c = lax.broadcasted_iota(I32, (blk, blk), 1)
    dist_c = (r - c).astype(F32)
    dist_p = dist_c + float(blk)
    mask_c = c <= r
    mask_p = (c > r) & (n > 0)
    lane = lax.broadcasted_iota(I32, (1, LANES), 1)
    kvmask = (lane >= h * HEAD_DIM) & (lane < (h + 1) * HEAD_DIM)
    kp, kc, vp, vc = kp_ref[...], kc_ref[...], vp_ref[...], vc_ref[...]
    neg = jnp.float32(-jnp.inf)
    for g in range(SWA_GROUP):
        hq = h * SWA_GROUP + g
        slope = hp_ref[0, hq]
        sink = hp_ref[1, hq]
        qg = q_ref[:, g * LANES:(g + 1) * LANES]
        zp = jnp.where(mask_p, _nt_dot(qg, kp) - slope * dist_p, neg)
        zc = jnp.where(mask_c, _nt_dot(qg, kc) - slope * dist_c, neg)
        m = jnp.maximum(jnp.max(zp, axis=1, keepdims=True),
                        jnp.max(zc, axis=1, keepdims=True))
        m = jnp.maximum(m, sink)
        pp = jnp.exp(zp - m)
        pc = jnp.exp(zc - m)
        den = (jnp.sum(pp, axis=1, keepdims=True) + jnp.sum(pc, axis=1, keepdims=True)
               + jnp.exp(sink - m))
        o = (jnp.dot(pp.astype(BF16), vp, preferred_element_type=F32)
             + jnp.dot(pc.astype(BF16), vc, preferred_element_type=F32)) / den
        o_ref[:, g * LANES:(g + 1) * LANES] = jnp.where(kvmask, o, 0.0).astype(o_ref.dtype)


def _swa_attention(p, head_params, batch, seq):
    blk = SWA_WINDOW
    t = batch * seq
    nb = seq // blk
    qw = SWA_GROUP * LANES
    qc, kc, vc = C_QSW // qw, C_KSW // LANES, C_VSW // LANES
    cur = lambda col: (lambda b, h, n, hp: (b * nb + n, col))
    prev = lambda col: (lambda b, h, n, hp: (b * nb + jnp.maximum(n - 1, 0), col))
    grid_spec = pltpu.PrefetchScalarGridSpec(
        num_scalar_prefetch=1,
        grid=(batch, SWA_KV_HEADS, nb),
        in_specs=[pl.BlockSpec((blk, qw), lambda b, h, n, hp: (b * nb + n, qc + h)),
                  pl.BlockSpec((blk, LANES), prev(kc)),
                  pl.BlockSpec((blk, LANES), cur(kc)),
                  pl.BlockSpec((blk, LANES), prev(vc)),
                  pl.BlockSpec((blk, LANES), cur(vc))],
        out_specs=pl.BlockSpec((blk, qw), lambda b, h, n, hp: (b * nb + n, h)),
    )
    return pl.pallas_call(
        _swa_kernel,
        grid_spec=grid_spec,
        out_shape=jax.ShapeDtypeStruct((t, SWA_HEADS * LANES), BF16),
        compiler_params=pltpu.CompilerParams(
            dimension_semantics=("parallel", "parallel", "arbitrary"),
            vmem_limit_bytes=VMEM_LIMIT),
        name="swa_attention",
    )(head_params, p, p, p, p, p)


def _mem_kernel(q_ref, mk_ref, mv_ref, o_ref):
    scale = MEM_HEAD_DIM ** -0.5
    for h in range(MEM_HEADS):
        cols = slice(h * MEM_HEAD_DIM, (h + 1) * MEM_HEAD_DIM)
        z = _nt_dot(q_ref[:, cols], mk_ref[:, cols]) * scale
        m = jnp.max(z, axis=1, keepdims=True)
        p = jnp.exp(z - m)
        den = jnp.sum(p, axis=1, keepdims=True)
        o = jnp.dot(p.astype(BF16), mv_ref[:, cols], preferred_element_type=F32) / den
        o_ref[:, cols] = o.astype(o_ref.dtype)


def _mem_attention(p, mkv, batch, seq, mem_len, tq):
    t = batch * seq
    nq = seq // tq
    w = MEM_HEADS * MEM_HEAD_DIM
    return pl.pallas_call(
        _mem_kernel,
        grid=(batch, nq),
        in_specs=[pl.BlockSpec((tq, w), lambda b, i: (b * nq + i, C_QM // w)),
                  pl.BlockSpec((mem_len, w), lambda b, i: (b, 0)),
                  pl.BlockSpec((mem_len, w), lambda b, i: (b, 1))],
        out_specs=pl.BlockSpec((tq, w), lambda b, i: (b * nq + i, 0)),
        out_shape=jax.ShapeDtypeStruct((t, w), BF16),
        compiler_params=pltpu.CompilerParams(
            dimension_semantics=("parallel", "arbitrary"), vmem_limit_bytes=VMEM_LIMIT),
        name="mem_attention",
    )(p, mkv, mkv)


def _merge_kernel(osb_ref, osw_ref, om_ref, g_ref, x_ref, wsb_ref, wsw_ref, wm_ref,
                  wout_ref, lng_ref, lnb_ref, x1_ref, x1p_ref):
    d = x_ref.shape[1]
    merged = g_ref[:, 0:d].astype(F32) * jnp.dot(
        osb_ref[...], wsb_ref[...], preferred_element_type=F32)
    merged += g_ref[:, d:2 * d].astype(F32) * jnp.dot(
        osw_ref[...], wsw_ref[...], preferred_element_type=F32)
    merged += g_ref[:, 2 * d:3 * d].astype(F32) * jnp.dot(
        om_ref[...], wm_ref[...], preferred_element_type=F32)
    y = jnp.dot(merged.astype(BF16), wout_ref[...], preferred_element_type=F32)
    x1 = _layer_norm(DEEPNORM_ALPHA * x_ref[...] + y, lng_ref[...], lnb_ref[...])
    x1_ref[...] = x1
    x1p_ref[...] = _pack_bf16_pair(x1[:, :d // 2], x1[:, d // 2:])


def _merge(o_sb, o_sw, o_m, p, x2, w_sb, w_sw, w_m, w_out, ln_g, ln_b, tm):
    t, d = x2.shape
    row = lambda w: pl.BlockSpec((tm, w), lambda i: (i, 0))
    return pl.pallas_call(
        _merge_kernel,
        grid=(t // tm,),
        in_specs=[row(o_sb.shape[1]), row(o_sw.shape[1]), row(o_m.shape[1]),
                  pl.BlockSpec((tm, N_BRANCH * d), lambda i: (i, 0)),
                  row(d),
                  _resident(w_sb.shape), _resident(w_sw.shape), _resident(w_m.shape),
                  _resident(w_out.shape), _resident(ln_g.shape), _resident(ln_b.shape)],
        out_specs=[row(d), row(d // 2)],
        out_shape=[jax.ShapeDtypeStruct((t, d), F32), jax.ShapeDtypeStruct((t, d // 2), U32)],
        compiler_params=pltpu.CompilerParams(
            dimension_semantics=("parallel",), vmem_limit_bytes=VMEM_LIMIT),
        name="merge_ln1",
    )(o_sb, o_sw, o_m, p, x2, w_sb, w_sw, w_m, w_out, ln_g, ln_b)


def _router_kernel(x_ref, wh_ref, wl_ref, bias_ref, idx_ref, rank_ref, wgt_ref, cnt_ref,
                   carry_ref):
    step = pl.program_id(0)
    tr = x_ref.shape[0]

    @pl.when(step == 0)
    def _():
        carry_ref[...] = jnp.zeros_like(carry_ref)

    x = x_ref[...]
    xh = x.astype(BF16)
    xl = (x - xh.astype(F32)).astype(BF16)
    logits = _nt_dot(wh_ref[...], xh) + _nt_dot(wh_ref[...], xl) + _nt_dot(wl_ref[...], xh)
    scores = _sigmoid(logits)
    biased = scores + bias_ref[...]
    neg = jnp.float32(-jnp.inf)

    sub = lax.broadcasted_iota(I32, (GROUP_SIZE, tr), 0)
    gscore = []
    for g in range(N_GROUPS):
        blk = biased[g * GROUP_SIZE:(g + 1) * GROUP_SIZE, :]
        m1 = jnp.max(blk, axis=0, keepdims=True)
        i1 = jnp.min(jnp.where(blk == m1, sub, GROUP_SIZE), axis=0, keepdims=True)
        m2 = jnp.max(jnp.where(sub == i1, neg, blk), axis=0, keepdims=True)
        gscore.append(m1 + m2)
    gs = jnp.concatenate(gscore, axis=0)

    giota = lax.broadcasted_iota(I32, (N_GROUPS, tr), 0)
    gsel = jnp.zeros((N_GROUPS, tr), F32)
    for _ in range(TOPK_GROUPS):
        m = jnp.max(gs, axis=0, keepdims=True)
        gi = jnp.min(jnp.where(gs == m, giota, N_GROUPS), axis=0, keepdims=True)
        hit = giota == gi
        gsel = jnp.where(hit, 1.0, gsel)
        gs = jnp.where(hit, neg, gs)

    masked = jnp.concatenate(
        [jnp.where(gsel[g:g + 1, :] > 0.0, biased[g * GROUP_SIZE:(g + 1) * GROUP_SIZE, :], neg)
         for g in range(N_GROUPS)], axis=0)

    eiota = lax.broadcasted_iota(I32, (N_EXPERTS, tr), 0)
    sel = jnp.zeros((N_EXPERTS, tr), F32)
    idx_rows, w_rows = [], []
    for _ in range(TOP_K):
        m = jnp.max(masked, axis=0, keepdims=True)
        ei = jnp.min(jnp.where(masked == m, eiota, N_EXPERTS), axis=0, keepdims=True)
        hit = eiota == ei
        idx_rows.append(ei)
        w_rows.append(jnp.sum(jnp.where(hit, scores, 0.0), axis=0, keepdims=True))
        sel = jnp.where(hit, 1.0, sel)
        masked = jnp.where(hit, neg, masked)

    wsum = w_rows[0]
    for wk in w_rows[1:]:
        wsum = wsum + wk
    wgt_ref[...] = jnp.concatenate(w_rows, axis=0) / wsum * ROUTED_SCALE
    idx_ref[...] = jnp.concatenate(idx_rows, axis=0)

    a = lax.broadcasted_iota(I32, (tr, tr), 0)
    b = lax.broadcasted_iota(I32, (tr, tr), 1)
    before = (a < b).astype(BF16)
    rank = jnp.dot(sel.astype(BF16), before, preferred_element_type=F32) + carry_ref[...]
    rank_rows = [jnp.sum(jnp.where(eiota == ei, rank, 0.0), axis=0, keepdims=True)
                 for ei in idx_rows]
    rank_ref[...] = jnp.concatenate(rank_rows, axis=0).astype(I32)
    carry_ref[...] = carry_ref[...] + jnp.sum(sel, axis=1, keepdims=True)
    cnt_ref[...] = carry_ref[...]


def _router(x1, wr_hi, wr_lo, bias_col, tr):
    t, d = x1.shape
    slot = pl.BlockSpec((TOP_K, tr), lambda i: (0, i))
    return pl.pallas_call(
        _router_kernel,
        grid=(t // tr,),
        in_specs=[pl.BlockSpec((tr, d), lambda i: (i, 0)),
                  _resident(wr_hi.shape), _resident(wr_lo.shape), _resident(bias_col.shape)],
        out_specs=[slot, slot, slot, pl.BlockSpec((N_EXPERTS, 1), lambda i: (0, 0))],
        out_shape=[jax.ShapeDtypeStruct((TOP_K, t), I32),
                   jax.ShapeDtypeStruct((TOP_K, t), I32),
                   jax.ShapeDtypeStruct((TOP_K, t), F32),
                   jax.ShapeDtypeStruct((N_EXPERTS, 1), F32)],
        scratch_shapes=[pltpu.VMEM((N_EXPERTS, 1), F32)],
        compiler_params=pltpu.CompilerParams(
            dimension_semantics=("arbitrary",), vmem_limit_bytes=VMEM_LIMIT),
        name="router",
    )(x1, wr_hi, wr_lo, bias_col)


def _dest_kernel(idx_ref, rank_ref, start_ref, dest_ref):
    tr = idx_ref.shape[1]
    eiota = lax.broadcasted_iota(I32, (N_EXPERTS, tr), 0)
    rows = []
    for k in range(TOP_K):
        hit = eiota == idx_ref[k:k + 1, :]
        rows.append(jnp.sum(jnp.where(hit, start_ref[...], 0), axis=0, keepdims=True))
    dest_ref[...] = jnp.concatenate(rows, axis=0) + rank_ref[...]


def _dest(idx, rank, start_col, tr):
    t = idx.shape[1]
    slot = pl.BlockSpec((TOP_K, tr), lambda i: (0, i))
    return pl.pallas_call(
        _dest_kernel,
        grid=(t // tr,),
        in_specs=[slot, slot, _resident(start_col.shape)],
        out_specs=slot,
        out_shape=jax.ShapeDtypeStruct((TOP_K, t), I32),
        compiler_params=pltpu.CompilerParams(dimension_semantics=("parallel",)),
        name="slot_dest",
    )(idx, rank, start_col)


def _dispatch_kernel(dest_ref, x_ref, xs_ref, sem):
    td = x_ref.shape[0]

    def row_copy(t, d):
        return pltpu.make_async_copy(x_ref.at[pl.ds(t, 1), :], xs_ref.at[pl.ds(d, 1), :], sem)

    def issue(t, carry):
        for k in range(TOP_K):
            row_copy(t, dest_ref[k, t]).start()
        return carry

    lax.fori_loop(0, td, issue, 0)

    def drain(t, carry):
        for k in range(TOP_K):
            row_copy(t, dest_ref[k, t]).wait()
        return carry

    lax.fori_loop(0, td, drain, 0)


def _dispatch(x1p, dest, n_rows, td):
    t, d = x1p.shape
    return pl.pallas_call(
        _dispatch_kernel,
        grid=(t // td,),
        in_specs=[pl.BlockSpec((TOP_K, td), lambda i: (0, i), memory_space=pltpu.SMEM),
                  pl.BlockSpec((td, d), lambda i: (i, 0))],
        out_specs=pl.BlockSpec(memory_space=pl.ANY),
        out_shape=jax.ShapeDtypeStruct((n_rows, d), x1p.dtype),
        scratch_shapes=[pltpu.SemaphoreType.DMA(())],
        compiler_params=pltpu.CompilerParams(
            dimension_semantics=("arbitrary",), has_side_effects=True),
        name="dispatch",
    )(dest, x1p)


def _expert_kernel(blk_e_ref, nv_ref, xs_ref, wgu_ref, wd_ref, ys_ref, wgu_s, wd_s):
    n = pl.program_id(0)
    valid = n < nv_ref[0]
    e = blk_e_ref[n]
    e_prev = blk_e_ref[jnp.maximum(n - 1, 0)]

    @pl.when(valid & ((n == 0) | (e != e_prev)))
    def _():
        wgu_s[...] = wgu_ref[0].astype(BF16)
        wd_s[...] = wd_ref[0].astype(BF16)

    @pl.when(valid)
    def _():
        ff = wd_s.shape[0]
        half = xs_ref.shape[1]
        x_lo, x_hi = _unpack_bf16_pair(xs_ref[...])
        h = (jnp.dot(x_lo.astype(BF16), wgu_s[:half, :], preferred_element_type=F32)
             + jnp.dot(x_hi.astype(BF16), wgu_s[half:, :], preferred_element_type=F32))
        gate, up = h[:, :ff], h[:, ff:]
        act = gate * _sigmoid(gate) * up
        y = jnp.dot(act.astype(BF16), wd_s[...], preferred_element_type=F32)
        ys_ref[...] = _pack_bf16_pair(y[:, :half], y[:, half:])


def _experts(xs, blk_e, n_valid, w_gu, w_down):
    n_rows, half = xs.shape
    d = 2 * half
    nb = n_rows // EXPERT_ROWS
    ff2 = w_gu.shape[2]
    ff = w_down.shape[1]
    last = lambda n, nv: jnp.minimum(n, nv[0] - 1)
    grid_spec = pltpu.PrefetchScalarGridSpec(
        num_scalar_prefetch=2,
        grid=(nb,),
        in_specs=[pl.BlockSpec((EXPERT_ROWS, half), lambda n, be, nv: (last(n, nv), 0)),
                  pl.BlockSpec((1, d, ff2), lambda n, be, nv: (be[n], 0, 0)),
                  pl.BlockSpec((1, ff, d), lambda n, be, nv: (be[n], 0, 0))],
        out_specs=pl.BlockSpec((EXPERT_ROWS, half), lambda n, be, nv: (last(n, nv), 0)),
        scratch_shapes=[pltpu.VMEM((d, ff2), BF16), pltpu.VMEM((ff, d), BF16)],
    )
    return pl.pallas_call(
        _expert_kernel,
        grid_spec=grid_spec,
        out_shape=jax.ShapeDtypeStruct((n_rows, half), U32),
        compiler_params=pltpu.CompilerParams(
            dimension_semantics=("arbitrary",), vmem_limit_bytes=VMEM_LIMIT),
        name="experts",
    )(blk_e, n_valid, xs, w_gu, w_down)


def _combine_kernel(dest_ref, x1_ref, wt_ref, ys_ref, wsgu_ref, wsd_ref, lng_ref, lnb_ref,
                    o_ref, buf, sem):
    tc = x1_ref.shape[0]

    def row_copy(t, k, d):
        return pltpu.make_async_copy(ys_ref.at[pl.ds(d, 1), :], buf.at[k, pl.ds(t, 1), :], sem)

    def issue(t, carry):
        for k in range(TOP_K):
            row_copy(t, k, dest_ref[k, t]).start()
        return carry

    lax.fori_loop(0, tc, issue, 0)

    x1 = x1_ref[...]
    ff = wsd_ref.shape[0]
    h = jnp.dot(x1.astype(BF16), wsgu_ref[...], preferred_element_type=F32)
    gate, up = h[:, :ff], h[:, ff:]
    act = gate * _sigmoid(gate) * up
    moe = jnp.dot(act.astype(BF16), wsd_ref[...], preferred_element_type=F32)

    def drain(t, carry):
        for k in range(TOP_K):
            row_copy(t, k, dest_ref[k, t]).wait()
        return carry

    lax.fori_loop(0, tc, drain, 0)

    half = buf.shape[2]
    r_lo = jnp.zeros((tc, half), F32)
    r_hi = jnp.zeros((tc, half), F32)
    for k in range(TOP_K):
        y_lo, y_hi = _unpack_bf16_pair(buf[k])
        w = wt_ref[:, k:k + 1]
        r_lo = r_lo + w * y_lo
        r_hi = r_hi + w * y_hi
    moe = moe + jnp.concatenate([r_lo, r_hi], axis=1)
    o_ref[...] = _layer_norm(DEEPNORM_ALPHA * x1 + moe, lng_ref[...], lnb_ref[...])


def _combine(x1, dest, wgt_t, ys, ws_gu, ws_down, ln_g, ln_b, tc):
    t, d = x1.shape
    return pl.pallas_call(
        _combine_kernel,
        grid=(t // tc,),
        in_specs=[pl.BlockSpec((TOP_K, tc), lambda i: (0, i), memory_space=pltpu.SMEM),
                  pl.BlockSpec((tc, d), lambda i: (i, 0)),
                  pl.BlockSpec((tc, TOP_K), lambda i: (i, 0)),
                  pl.BlockSpec(memory_space=pl.ANY),
                  _resident(ws_gu.shape), _resident(ws_down.shape),
                  _resident(ln_g.shape), _resident(ln_b.shape)],
        out_specs=pl.BlockSpec((tc, d), lambda i: (i, 0)),
        out_shape=jax.ShapeDtypeStruct((t, d), F32),
        scratch_shapes=[pltpu.VMEM((TOP_K, tc, ys.shape[1]), ys.dtype),
                        pltpu.SemaphoreType.DMA(())],
        compiler_params=pltpu.CompilerParams(
            dimension_semantics=("arbitrary",), vmem_limit_bytes=VMEM_LIMIT),
        name="combine_ln2",
    )(dest, x1, wgt_t, ys, ws_gu, ws_down, ln_g, ln_b)


def _fused_in_weights(w_in):
    d = w_in.shape[0]
    sizes = (SB_HEADS * HEAD_DIM,) * 3 + (SWA_HEADS * HEAD_DIM, SWA_KV_HEADS * HEAD_DIM,
                                          SWA_KV_HEADS * HEAD_DIM, MEM_HEADS * MEM_HEAD_DIM)
    parts, off = [], 0
    for s in sizes:
        parts.append(w_in[:, off:off + s])
        off += s
    q_sb, k_sb, v_sb, q_sw, k_sw, v_sw, q_m = parts
    gates = w_in[:, off:]
    scale = HEAD_DIM ** -0.5
    q_sw = (q_sw * scale).reshape(d, SWA_KV_HEADS, SWA_GROUP, HEAD_DIM)
    zeros = jnp.zeros((d, SWA_GROUP, HEAD_DIM), w_in.dtype)
    q_sw = jnp.stack([jnp.concatenate([q_sw[:, 0], zeros], axis=-1),
                      jnp.concatenate([zeros, q_sw[:, 1]], axis=-1)], axis=1)
    q_sw = q_sw.reshape(d, SWA_HEADS * LANES)
    used = W_GATE + 3 * SB_HEADS * HEAD_DIM + SWA_HEADS * LANES + q_m.shape[1] + 2 * k_sw.shape[1]
    pad = jnp.zeros((d, PROJ_COLS - used), w_in.dtype)
    return jnp.concatenate([gates, q_sb * scale, k_sb, v_sb, q_sw, q_m, k_sw, v_sw, pad],
                           axis=1).astype(BF16)


def _padded_swa_out_weights(w_o_swa):
    d = w_o_swa.shape[1]
    w = w_o_swa.reshape(SWA_KV_HEADS, SWA_GROUP, HEAD_DIM, d)
    zeros = jnp.zeros((SWA_GROUP, HEAD_DIM, d), w_o_swa.dtype)
    w = jnp.stack([jnp.concatenate([w[0], zeros], axis=1),
                   jnp.concatenate([zeros, w[1]], axis=1)], axis=0)
    return w.reshape(SWA_HEADS * LANES, d).astype(BF16)


def kernel(x, mem, w_in, b_gate, w_mem_kv, sinks, w_o_sb, w_o_swa, w_o_mem, w_out,
           ln1_g, ln1_b, w_router, router_bias, w_e_gu, w_e_down, w_s_gu, w_s_down,
           ln2_g, ln2_b):
    batch, seq, d = x.shape
    mem_len = mem.shape[1]
    t = batch * seq
    x2 = x.reshape(t, d)
    row_tile = min(512, t)

    p = _proj(x2, _fused_in_weights(w_in), b_gate.reshape(1, -1), row_tile)
    mkv = _matmul_bf16(mem.reshape(batch * mem_len, d), w_mem_kv.astype(BF16), mem_len)
    o_sb = _sb_attention(p, batch, seq, min(256, seq))
    slopes = jnp.exp2(-8.0 * jnp.arange(1, SWA_HEADS + 1, dtype=F32) / SWA_HEADS)
    o_sw = _swa_attention(p, jnp.stack([slopes, sinks.astype(F32)]), batch, seq)
    o_m = _mem_attention(p, mkv, batch, seq, mem_len, min(512, seq))
    x1, x1p = _merge(o_sb, o_sw, o_m, p, x2, w_o_sb.astype(BF16), _padded_swa_out_weights(w_o_swa),
                w_o_mem.astype(BF16), w_out.astype(BF16),
                ln1_g.reshape(1, d), ln1_b.reshape(1, d), row_tile)

    out = _moe_ln(x1, x1p, w_router, router_bias, w_e_gu, w_e_down, w_s_gu, w_s_down,
                  ln2_g, ln2_b)
    return out.reshape(batch, seq, d)


def _moe_ln(x1, x1p, w_router, router_bias, w_e_gu, w_e_down, w_s_gu, w_s_down, ln2_g, ln2_b):
    t, d = x1.shape
    wr_t = w_router.T
    wr_hi = wr_t.astype(BF16)
    wr_lo = (wr_t - wr_hi.astype(F32)).astype(BF16)
    route_tile = min(256, t)
    idx, rank, wgt, cnt = _router(x1, wr_hi, wr_lo, router_bias.reshape(-1, 1).astype(F32),
                                  route_tile)
    counts = cnt[:, 0].astype(I32)
    padded = (counts + EXPERT_ROWS - 1) // EXPERT_ROWS * EXPERT_ROWS
    pad_end = jnp.cumsum(padded)
    pad_start = pad_end - padded
    n_blocks = t * TOP_K // EXPERT_ROWS + N_EXPERTS
    n_valid = pad_end[-1] // EXPERT_ROWS
    blk_start = jnp.arange(n_blocks, dtype=I32) * EXPERT_ROWS
    blk_e = jnp.minimum((pad_end[None, :] <= blk_start[:, None]).sum(-1), N_EXPERTS - 1)
    blk_e = jnp.where(jnp.arange(n_blocks) < n_valid, blk_e, blk_e[n_valid - 1]).astype(I32)
    dest = _dest(idx, rank, pad_start.reshape(-1, 1), route_tile)

    xs = _dispatch(x1p, dest, n_blocks * EXPERT_ROWS, min(256, t))
    ys = _experts(xs, blk_e, n_valid.reshape(1).astype(I32), w_e_gu, w_e_down)
    return _combine(x1, dest, wgt.T, ys, w_s_gu.astype(BF16), w_s_down.astype(BF16),
                    ln2_g.reshape(1, d), ln2_b.reshape(1, d), min(256, t))
```

```python
import functools

import jax
import jax.numpy as jnp
from jax import lax
from jax.experimental import pallas as pl
from jax.experimental.pallas import tpu as pltpu
from jax.experimental.pallas import tpu_sc as plsc

F32 = jnp.float32
BF16 = jnp.bfloat16
I32 = jnp.int32
U32 = jnp.uint32

HEAD_DIM = 64
SB_HEADS = 8
SWA_HEADS = 8
SWA_KV_HEADS = 2
SWA_GROUP = SWA_HEADS // SWA_KV_HEADS
SWA_WINDOW = 128
MEM_HEADS = 4
MEM_HEAD_DIM = 128
N_BRANCH = 3
N_EXPERTS = 256
TOP_K = 8
N_GROUPS = 8
GROUP_SIZE = N_EXPERTS // N_GROUPS
TOPK_GROUPS = 4
EXPERT_FF = 256
SHARED_FF = 256
ROUTED_SCALE = 2.5
LN_EPS = 1e-5
DEPTH = 1
DEEPNORM_ALPHA = (2 * DEPTH) ** 0.25

LANES = 128
SC_CORES = 2
SC_SUBCORES = 16
SC_WORKERS = SC_CORES * SC_SUBCORES
SC_INDEX_WINDOW = 128
SC_GATHER_ROWS = 64
VMEM_LIMIT = 56 * 1024 * 1024

D_GATE = 0
W_GATE = 3072
C_QSB = 3072
C_KSB = 3584
C_VSB = 4096
C_QSW = 4608
C_QM = 5632
C_KSW = 6144
C_VSW = 6272
PROJ_COLS = 6656
PROJ_CHUNK = 512

SB_SKIP = 110.0

EXPERT_ROWS = 512


def _nt_dot(a, b):
    return lax.dot_general(a, b, (((1,), (1,)), ((), ())), preferred_element_type=F32)


def _sigmoid(x):
    return 1.0 / (1.0 + jnp.exp(-x))


def _layer_norm(h, g, b):
    mu = jnp.mean(h, axis=-1, keepdims=True)
    d = h - mu
    var = jnp.mean(d * d, axis=-1, keepdims=True)
    return d * lax.rsqrt(var + LN_EPS) * g + b


def _pack_bf16_pair(a, b):
    a_bits = lax.bitcast_convert_type(a.astype(BF16).astype(F32), U32)
    b_bits = lax.bitcast_convert_type(b.astype(BF16).astype(F32), U32)
    return (a_bits >> 16) | b_bits


def _unpack_bf16_pair(w):
    a = lax.bitcast_convert_type(w << 16, F32)
    b = lax.bitcast_convert_type(w & jnp.uint32(0xFFFF0000), F32)
    return a, b


def _resident(shape):
    nd = len(shape)
    return pl.BlockSpec(shape, lambda *_: (0,) * nd, pipeline_mode=pl.Buffered(1))


def _proj_kernel(x_ref, w_ref, b_ref, o_ref, *, gate_cols):
    xb = x_ref[...].astype(BF16)
    for j in range(o_ref.shape[1] // PROJ_CHUNK):
        cols = slice(j * PROJ_CHUNK, (j + 1) * PROJ_CHUNK)
        acc = jnp.dot(xb, w_ref[:, cols], preferred_element_type=F32)
        if (j + 1) * PROJ_CHUNK <= gate_cols:
            acc = _sigmoid(acc + b_ref[:, cols])
        o_ref[:, cols] = acc.astype(o_ref.dtype)


def _proj(x2, w_all, b_gate, tm):
    t, d = x2.shape
    n = w_all.shape[1]
    return pl.pallas_call(
        functools.partial(_proj_kernel, gate_cols=b_gate.shape[1]),
        grid=(t // tm,),
        in_specs=[pl.BlockSpec((tm, d), lambda i: (i, 0)),
                  _resident((d, n)),
                  _resident(b_gate.shape)],
        out_specs=pl.BlockSpec((tm, n), lambda i: (i, 0)),
        out_shape=jax.ShapeDtypeStruct((t, n), BF16),
        compiler_params=pltpu.CompilerParams(
            dimension_semantics=("parallel",), vmem_limit_bytes=VMEM_LIMIT),
        name="in_proj",
    )(x2, w_all, b_gate)


def _mm_kernel(x_ref, w_ref, o_ref):
    o_ref[...] = jnp.dot(x_ref[...].astype(BF16), w_ref[...],
                         preferred_element_type=F32).astype(o_ref.dtype)


def _matmul_bf16(x2, w, tm):
    t, d = x2.shape
    n = w.shape[1]
    return pl.pallas_call(
        _mm_kernel,
        grid=(t // tm,),
        in_specs=[pl.BlockSpec((tm, d), lambda i: (i, 0)), _resident((d, n))],
        out_specs=pl.BlockSpec((tm, n), lambda i: (i, 0)),
        out_shape=jax.ShapeDtypeStruct((t, n), BF16),
        compiler_params=pltpu.CompilerParams(
            dimension_semantics=("parallel",), vmem_limit_bytes=VMEM_LIMIT),
        name="mem_kv_proj",
    )(x2, w)


def _sb_kernel(q_ref, k_ref, v_ref, o_ref, *, tq):
    i = pl.program_id(2)
    q = q_ref[...]
    lane = lax.broadcasted_iota(I32, (1, LANES), 1)
    r = lax.broadcasted_iota(I32, (tq, tq), 0)
    c = lax.broadcasted_iota(I32, (tq, tq), 1)
    tri = (r >= c).astype(BF16)
    causal = c < r
    nh = LANES // HEAD_DIM
    hmasks = [(lane >= h * HEAD_DIM) & (lane < (h + 1) * HEAD_DIM) for h in range(nh)]
    qs = jnp.concatenate([jnp.where(m, q, jnp.zeros_like(q)) for m in hmasks], axis=0)
    causal2 = jnp.concatenate([causal] * nh, axis=0)
    m = nh * tq

    def block(kb, carry, acc, diag):
        rows = pl.ds(pl.multiple_of(kb * tq, tq), tq)
        k = k_ref[rows, :]
        v = v_ref[rows, :]
        z = _nt_dot(qs, k)
        sp = jnp.maximum(z, 0.0) + jnp.log(1.0 + jnp.exp(-jnp.abs(z)))
        log_b = z - sp
        log_1m = -sp
        if diag:
            log_1m = jnp.where(causal2, log_1m, 0.0)
        hi = log_1m.astype(BF16)
        lo = (log_1m - hi.astype(F32)).astype(BF16)
        s2 = jnp.dot(jnp.concatenate([hi, lo], axis=0), tri, preferred_element_type=F32)
        suffix = s2[:m] + s2[m:]
        a = jnp.exp(log_b + (suffix - log_1m) + carry)
        if diag:
            a = jnp.where(causal2, a, 0.0)
        acc = acc + jnp.dot(a.astype(BF16), v, preferred_element_type=F32)
        return carry + suffix[:, 0:1], acc

    carry, acc = block(i, jnp.zeros((m, 1), F32), jnp.zeros((m, LANES), F32), True)

    def cond(s):
        kb, carry, _ = s
        return (kb >= 0) & (jnp.max(carry) > -SB_SKIP)

    def body(s):
        kb, carry, acc = s
        carry, acc = block(kb, carry, acc, False)
        return kb - 1, carry, acc

    _, _, acc = lax.while_loop(cond, body, (i - 1, carry, acc))
    o_ref[...] = jnp.where(hmasks[0], acc[:tq], acc[tq:]).astype(o_ref.dtype)


def _sb_attention(p, batch, seq, tq):
    t = batch * seq
    nq = seq // tq
    npair = SB_HEADS * HEAD_DIM // LANES
    qc, kc, vc = C_QSB // LANES, C_KSB // LANES, C_VSB // LANES
    return pl.pallas_call(
        functools.partial(_sb_kernel, tq=tq),
        grid=(batch, npair, nq),
        in_specs=[pl.BlockSpec((tq, LANES), lambda b, h, i: (b * nq + i, qc + h)),
                  pl.BlockSpec((seq, LANES), lambda b, h, i: (b, kc + h)),
                  pl.BlockSpec((seq, LANES), lambda b, h, i: (b, vc + h))],
        out_specs=pl.BlockSpec((tq, LANES), lambda b, h, i: (b * nq + i, h)),
        out_shape=jax.ShapeDtypeStruct((t, SB_HEADS * HEAD_DIM), BF16),
        compiler_params=pltpu.CompilerParams(
            dimension_semantics=("parallel", "parallel", "arbitrary"),
            vmem_limit_bytes=VMEM_LIMIT),
        name="sb_attention",
    )(p, p, p)


def _swa_kernel(hp_ref, q_ref, kp_ref, kc_ref, vp_ref, vc_ref, o_ref):
    h = pl.program_id(1)
    n = pl.program_id(2)
    blk = q_ref.shape[0]
    r = lax.broadcasted_iota(I32, (blk, blk), 0)
    c = lax.broadcasted_iota(I32, (blk, blk), 1)
    dist_c = (r - c).astype(F32)
    dist_p = dist_c + float(blk)
    mask_c = c <= r
    mask_p = (c > r) & (n > 0)
    lane = lax.broadcasted_iota(I32, (1, LANES), 1)
    kvmask = (lane >= h * HEAD_DIM) & (lane < (h + 1) * HEAD_DIM)
    kp, kc, vp, vc = kp_ref[...], kc_ref[...], vp_ref[...], vc_ref[...]
    neg = jnp.float32(-jnp.inf)
    for g in range(SWA_GROUP):
        hq = h * SWA_GROUP + g
        slope = hp_ref[0, hq]
        sink = hp_ref[1, hq]
        qg = q_ref[:, g * LANES:(g + 1) * LANES]
        zp = jnp.where(mask_p, _nt_dot(qg, kp) - slope * dist_p, neg)
        zc = jnp.where(mask_c, _nt_dot(qg, kc) - slope * dist_c, neg)
        m = jnp.maximum(jnp.max(zp, axis=1, keepdims=True),
                        jnp.max(zc, axis=1, keepdims=True))
        m = jnp.maximum(m, sink)
        pp = jnp.exp(zp - m)
        pc = jnp.exp(zc - m)
        den = (jnp.sum(pp, axis=1, keepdims=True) + jnp.sum(pc, axis=1, keepdims=True)
               + jnp.exp(sink - m))
        o = (jnp.dot(pp.astype(BF16), vp, preferred_element_type=F32)
             + jnp.dot(pc.astype(BF16), vc, preferred_element_type=F32)) / den
        o_ref[:, g * LANES:(g + 1) * LANES] = jnp.where(kvmask, o, 0.0).astype(o_ref.dtype)


def _swa_attention(p, head_params, batch, seq):
    blk = SWA_WINDOW
    t = batch * seq
    nb = seq // blk
    qw = SWA_GROUP * LANES
    qc, kc, vc = C_QSW // qw, C_KSW // LANES, C_VSW // LANES
    cur = lambda col: (lambda b, h, n, hp: (b * nb + n, col))
    prev = lambda col: (lambda b, h, n, hp: (b * nb + jnp.maximum(n - 1, 0), col))
    grid_spec = pltpu.PrefetchScalarGridSpec(
        num_scalar_prefetch=1,
        grid=(batch, SWA_KV_HEADS, nb),
        in_specs=[pl.BlockSpec((blk, qw), lambda b, h, n, hp: (b * nb + n, qc + h)),
                  pl.BlockSpec((blk, LANES), prev(kc)),
                  pl.BlockSpec((blk, LANES), cur(kc)),
                  pl.BlockSpec((blk, LANES), prev(vc)),
                  pl.BlockSpec((blk, LANES), cur(vc))],
        out_specs=pl.BlockSpec((blk, qw), lambda b, h, n, hp: (b * nb + n, h)),
    )
    return pl.pallas_call(
        _swa_kernel,
        grid_spec=grid_spec,
        out_shape=jax.ShapeDtypeStruct((t, SWA_HEADS * LANES), BF16),
        compiler_params=pltpu.CompilerParams(
            dimension_semantics=("parallel", "parallel", "arbitrary"),
            vmem_limit_bytes=VMEM_LIMIT),
        name="swa_attention",
    )(head_params, p, p, p, p, p)


def _mem_kernel(q_ref, mk_ref, mv_ref, o_ref):
    scale = MEM_HEAD_DIM ** -0.5
    for h in range(MEM_HEADS):
        cols = slice(h * MEM_HEAD_DIM, (h + 1) * MEM_HEAD_DIM)
        z = _nt_dot(q_ref[:, cols], mk_ref[:, cols]) * scale
        m = jnp.max(z, axis=1, keepdims=True)
        p = jnp.exp(z - m)
        den = jnp.sum(p, axis=1, keepdims=True)
        o = jnp.dot(p.astype(BF16), mv_ref[:, cols], preferred_element_type=F32) / den
        o_ref[:, cols] = o.astype(o_ref.dtype)


def _mem_attention(p, mkv, batch, seq, mem_len, tq):
    t = batch * seq
    nq = seq // tq
    w = MEM_HEADS * MEM_HEAD_DIM
    return pl.pallas_call(
        _mem_kernel,
        grid=(batch, nq),
        in_specs=[pl.BlockSpec((tq, w), lambda b, i: (b * nq + i, C_QM // w)),
                  pl.BlockSpec((mem_len, w), lambda b, i: (b, 0)),
                  pl.BlockSpec((mem_len, w), lambda b, i: (b, 1))],
        out_specs=pl.BlockSpec((tq, w), lambda b, i: (b * nq + i, 0)),
        out_shape=jax.ShapeDtypeStruct((t, w), BF16),
        compiler_params=pltpu.CompilerParams(
            dimension_semantics=("parallel", "arbitrary"), vmem_limit_bytes=VMEM_LIMIT),
        name="mem_attention",
    )(p, mkv, mkv)


def _merge_kernel(osb_ref, osw_ref, om_ref, g_ref, x_ref, wsb_ref, wsw_ref, wm_ref,
                  wout_ref, lng_ref, lnb_ref, x1_ref, x1p_ref):
    d = x_ref.shape[1]
    merged = g_ref[:, 0:d].astype(F32) * jnp.dot(
        osb_ref[...], wsb_ref[...], preferred_element_type=F32)
    merged += g_ref[:, d:2 * d].astype(F32) * jnp.dot(
        osw_ref[...], wsw_ref[...], preferred_element_type=F32)
    merged += g_ref[:, 2 * d:3 * d].astype(F32) * jnp.dot(
        om_ref[...], wm_ref[...], preferred_element_type=F32)
    y = jnp.dot(merged.astype(BF16), wout_ref[...], preferred_element_type=F32)
    x1 = _layer_norm(DEEPNORM_ALPHA * x_ref[...] + y, lng_ref[...], lnb_ref[...])
    x1_ref[...] = x1
    x1p_ref[...] = _pack_bf16_pair(x1[:, :d // 2], x1[:, d // 2:])


def _merge(o_sb, o_sw, o_m, p, x2, w_sb, w_sw, w_m, w_out, ln_g, ln_b, tm):
    t, d = x2.shape
    row = lambda w: pl.BlockSpec((tm, w), lambda i: (i, 0))
    return pl.pallas_call(
        _merge_kernel,
        grid=(t // tm,),
        in_specs=[row(o_sb.shape[1]), row(o_sw.shape[1]), row(o_m.shape[1]),
                  pl.BlockSpec((tm, N_BRANCH * d), lambda i: (i, 0)),
                  row(d),
                  _resident(w_sb.shape), _resident(w_sw.shape), _resident(w_m.shape),
                  _resident(w_out.shape), _resident(ln_g.shape), _resident(ln_b.shape)],
        out_specs=[row(d), row(d // 2)],
        out_shape=[jax.ShapeDtypeStruct((t, d), F32), jax.ShapeDtypeStruct((t, d // 2), U32)],
        compiler_params=pltpu.CompilerParams(
            dimension_semantics=("parallel",), vmem_limit_bytes=VMEM_LIMIT),
        name="merge_ln1",
    )(o_sb, o_sw, o_m, p, x2, w_sb, w_sw, w_m, w_out, ln_g, ln_b)


def _router_kernel(x_ref, wh_ref, wl_ref, bias_ref, idx_ref, rank_ref, wgt_ref, cnt_ref,
                   carry_ref):
    step = pl.program_id(0)
    tr = x_ref.shape[0]

    @pl.when(step == 0)
    def _():
        carry_ref[...] = jnp.zeros_like(carry_ref)

    x = x_ref[...]
    xh = x.astype(BF16)
    xl = (x - xh.astype(F32)).astype(BF16)
    logits = _nt_dot(wh_ref[...], xh) + _nt_dot(wh_ref[...], xl) + _nt_dot(wl_ref[...], xh)
    scores = _sigmoid(logits)
    biased = scores + bias_ref[...]
    neg = jnp.float32(-jnp.inf)

    sub = lax.broadcasted_iota(I32, (GROUP_SIZE, tr), 0)
    gscore = []
    for g in range(N_GROUPS):
        blk = biased[g * GROUP_SIZE:(g + 1) * GROUP_SIZE, :]
        m1 = jnp.max(blk, axis=0, keepdims=True)
        i1 = jnp.min(jnp.where(blk == m1, sub, GROUP_SIZE), axis=0, keepdims=True)
        m2 = jnp.max(jnp.where(sub == i1, neg, blk), axis=0, keepdims=True)
        gscore.append(m1 + m2)
    gs = jnp.concatenate(gscore, axis=0)

    giota = lax.broadcasted_iota(I32, (N_GROUPS, tr), 0)
    gsel = jnp.zeros((N_GROUPS, tr), F32)
    for _ in range(TOPK_GROUPS):
        m = jnp.max(gs, axis=0, keepdims=True)
        gi = jnp.min(jnp.where(gs == m, giota, N_GROUPS), axis=0, keepdims=True)
        hit = giota == gi
        gsel = jnp.where(hit, 1.0, gsel)
        gs = jnp.where(hit, neg, gs)

    masked = jnp.concatenate(
        [jnp.where(gsel[g:g + 1, :] > 0.0, biased[g * GROUP_SIZE:(g + 1) * GROUP_SIZE, :], neg)
         for g in range(N_GROUPS)], axis=0)

    eiota = lax.broadcasted_iota(I32, (N_EXPERTS, tr), 0)
    sel = jnp.zeros((N_EXPERTS, tr), F32)
    idx_rows, w_rows = [], []
    for _ in range(TOP_K):
        m = jnp.max(masked, axis=0, keepdims=True)
        ei = jnp.min(jnp.where(masked == m, eiota, N_EXPERTS), axis=0, keepdims=True)
        hit = eiota == ei
        idx_rows.append(ei)
        w_rows.append(jnp.sum(jnp.where(hit, scores, 0.0), axis=0, keepdims=True))
        sel = jnp.where(hit, 1.0, sel)
        masked = jnp.where(hit, neg, masked)

    wsum = w_rows[0]
    for wk in w_rows[1:]:
        wsum = wsum + wk
    wgt_ref[...] = jnp.concatenate(w_rows, axis=0) / wsum * ROUTED_SCALE
    idx_ref[...] = jnp.concatenate(idx_rows, axis=0)

    a = lax.broadcasted_iota(I32, (tr, tr), 0)
    b = lax.broadcasted_iota(I32, (tr, tr), 1)
    before = (a < b).astype(BF16)
    rank = jnp.dot(sel.astype(BF16), before, preferred_element_type=F32) + carry_ref[...]
    rank_rows = [jnp.sum(jnp.where(eiota == ei, rank, 0.0), axis=0, keepdims=True)
                 for ei in idx_rows]
    rank_ref[...] = jnp.concatenate(rank_rows, axis=0).astype(I32)
    carry_ref[...] = carry_ref[...] + jnp.sum(sel, axis=1, keepdims=True)
    cnt_ref[...] = carry_ref[...]


def _router(x1, wr_hi, wr_lo, bias_col, tr):
    t, d = x1.shape
    slot = pl.BlockSpec((TOP_K, tr), lambda i: (0, i))
    return pl.pallas_call(
        _router_kernel,
        grid=(t // tr,),
        in_specs=[pl.BlockSpec((tr, d), lambda i: (i, 0)),
                  _resident(wr_hi.shape), _resident(wr_lo.shape), _resident(bias_col.shape)],
        out_specs=[slot, slot, slot, pl.BlockSpec((N_EXPERTS, 1), lambda i: (0, 0))],
        out_shape=[jax.ShapeDtypeStruct((TOP_K, t), I32),
                   jax.ShapeDtypeStruct((TOP_K, t), I32),
                   jax.ShapeDtypeStruct((TOP_K, t), F32),
                   jax.ShapeDtypeStruct((N_EXPERTS, 1), F32)],
        scratch_shapes=[pltpu.VMEM((N_EXPERTS, 1), F32)],
        compiler_params=pltpu.CompilerParams(
            dimension_semantics=("arbitrary",), vmem_limit_bytes=VMEM_LIMIT),
        name="router",
    )(x1, wr_hi, wr_lo, bias_col)


def _dest_kernel(idx_ref, rank_ref, start_ref, dest_ref):
    tr = idx_ref.shape[1]
    eiota = lax.broadcasted_iota(I32, (N_EXPERTS, tr), 0)
    rows = []
    for k in range(TOP_K):
        hit = eiota == idx_ref[k:k + 1, :]
        rows.append(jnp.sum(jnp.where(hit, start_ref[...], 0), axis=0, keepdims=True))
    dest_ref[...] = jnp.concatenate(rows, axis=0) + rank_ref[...]


def _dest(idx, rank, start_col, tr):
    t = idx.shape[1]
    slot = pl.BlockSpec((TOP_K, tr), lambda i: (0, i))
    return pl.pallas_call(
        _dest_kernel,
        grid=(t // tr,),
        in_specs=[slot, slot, _resident(start_col.shape)],
        out_specs=slot,
        out_shape=jax.ShapeDtypeStruct((TOP_K, t), I32),
        compiler_params=pltpu.CompilerParams(dimension_semantics=("parallel",)),
        name="slot_dest",
    )(idx, rank, start_col)


def _sc_worker_id():
    return lax.axis_index("s") * SC_CORES + lax.axis_index("c")


def _dispatch(x1p, dest, n_rows):
    t, w = x1p.shape
    per = t // SC_WORKERS
    win = min(SC_INDEX_WINDOW, per)
    mesh = plsc.VectorSubcoreMesh(core_axis_name="c", subcore_axis_name="s")

    @functools.partial(
        pl.kernel, mesh=mesh,
        out_type=jax.ShapeDtypeStruct((n_rows, w), x1p.dtype),
        scratch_types=[pltpu.VMEM((TOP_K, win), I32),
                       pltpu.VMEM((win, w), x1p.dtype),
                       pltpu.SemaphoreType.DMA],
        name="sc_dispatch",
    )
    def scatter_rows(x_hbm, dest_hbm, xs_hbm, idx_v, rows_v, sem):
        base = _sc_worker_id() * per

        @pl.loop(0, per // win)
        def _(j):
            t0 = pl.multiple_of(base + j * win, win)
            pltpu.sync_copy(dest_hbm.at[:, pl.ds(t0, win)], idx_v)
            pltpu.sync_copy(x_hbm.at[pl.ds(t0, win)], rows_v)
            copies = [pltpu.async_copy(rows_v, xs_hbm.at[idx_v.at[k]], sem) for k in range(TOP_K)]
            for c in copies:
                c.wait()

    return scatter_rows(x1p, dest)


def _gather_rows(table, idx):
    n = idx.shape[0]
    w = table.shape[1]
    per = n // SC_WORKERS
    chunk = min(SC_GATHER_ROWS, per)
    mesh = plsc.VectorSubcoreMesh(core_axis_name="c", subcore_axis_name="s")

    @functools.partial(
        pl.kernel, mesh=mesh,
        out_type=jax.ShapeDtypeStruct((n, w), table.dtype),
        scratch_types=[pltpu.VMEM((per,), I32),
                       pltpu.VMEM((chunk, w), table.dtype),
                       pltpu.SemaphoreType.DMA],
        name="sc_gather",
    )
    def gather(table_hbm, idx_hbm, out_hbm, idx_v, rows_v, sem):
        base = _sc_worker_id() * per
        pltpu.sync_copy(idx_hbm.at[pl.ds(base, per)], idx_v)

        @pl.loop(0, per // chunk)
        def _(j):
            off = pl.multiple_of(j * chunk, chunk)
            pltpu.async_copy(table_hbm.at[idx_v.at[pl.ds(off, chunk)]], rows_v, sem).wait()
            pltpu.sync_copy(rows_v, out_hbm.at[pl.ds(base + off, chunk)])

    return gather(table, idx)


def _expert_kernel(blk_e_ref, nv_ref, xs_ref, wgu_ref, wd_ref, ys_ref, wgu_s, wd_s):
    n = pl.program_id(0)
    valid = n < nv_ref[0]
    e = blk_e_ref[n]
    e_prev = blk_e_ref[jnp.maximum(n - 1, 0)]

    @pl.when(valid & ((n == 0) | (e != e_prev)))
    def _():
        wgu_s[...] = wgu_ref[0].astype(BF16)
        wd_s[...] = wd_ref[0].astype(BF16)

    @pl.when(valid)
    def _():
        ff = wd_s.shape[0]
        half = xs_ref.shape[1]
        x_lo, x_hi = _unpack_bf16_pair(xs_ref[...])
        h = (jnp.dot(x_lo.astype(BF16), wgu_s[:half, :], preferred_element_type=F32)
             + jnp.dot(x_hi.astype(BF16), wgu_s[half:, :], preferred_element_type=F32))
        gate, up = h[:, :ff], h[:, ff:]
        act = gate * _sigmoid(gate) * up
        y = jnp.dot(act.astype(BF16), wd_s[...], preferred_element_type=F32)
        ys_ref[...] = _pack_bf16_pair(y[:, :half], y[:, half:])


def _experts(xs, blk_e, n_valid, w_gu, w_down):
    n_rows, half = xs.shape
    d = 2 * half
    nb = n_rows // EXPERT_ROWS
    ff2 = w_gu.shape[2]
    ff = w_down.shape[1]
    last = lambda n, nv: jnp.minimum(n, nv[0] - 1)
    grid_spec = pltpu.PrefetchScalarGridSpec(
        num_scalar_prefetch=2,
        grid=(nb,),
        in_specs=[pl.BlockSpec((EXPERT_ROWS, half), lambda n, be, nv: (last(n, nv), 0)),
                  pl.BlockSpec((1, d, ff2), lambda n, be, nv: (be[n], 0, 0)),
                  pl.BlockSpec((1, ff, d), lambda n, be, nv: (be[n], 0, 0))],
        out_specs=pl.BlockSpec((EXPERT_ROWS, half), lambda n, be, nv: (last(n, nv), 0)),
        scratch_shapes=[pltpu.VMEM((d, ff2), BF16), pltpu.VMEM((ff, d), BF16)],
    )
    return pl.pallas_call(
        _expert_kernel,
        grid_spec=grid_spec,
        out_shape=jax.ShapeDtypeStruct((n_rows, half), U32),
        compiler_params=pltpu.CompilerParams(
            dimension_semantics=("arbitrary",), vmem_limit_bytes=VMEM_LIMIT),
        name="experts",
    )(blk_e, n_valid, xs, w_gu, w_down)


def _combine_kernel(x1_ref, wt_ref, yg_ref, wsgu_ref, wsd_ref, lng_ref, lnb_ref, o_ref):
    tc = x1_ref.shape[0]
    x1 = x1_ref[...]
    ff = wsd_ref.shape[0]
    h = jnp.dot(x1.astype(BF16), wsgu_ref[...], preferred_element_type=F32)
    gate, up = h[:, :ff], h[:, ff:]
    act = gate * _sigmoid(gate) * up
    moe = jnp.dot(act.astype(BF16), wsd_ref[...], preferred_element_type=F32)

    half = yg_ref.shape[2]
    r_lo = jnp.zeros((tc, half), F32)
    r_hi = jnp.zeros((tc, half), F32)
    for k in range(TOP_K):
        y_lo, y_hi = _unpack_bf16_pair(yg_ref[k])
        w = wt_ref[:, k:k + 1]
        r_lo = r_lo + w * y_lo
        r_hi = r_hi + w * y_hi
    moe = moe + jnp.concatenate([r_lo, r_hi], axis=1)
    o_ref[...] = _layer_norm(DEEPNORM_ALPHA * x1 + moe, lng_ref[...], lnb_ref[...])


def _combine(x1, wgt_t, yg, ws_gu, ws_down, ln_g, ln_b, tc):
    t, d = x1.shape
    return pl.pallas_call(
        _combine_kernel,
        grid=(t // tc,),
        in_specs=[pl.BlockSpec((tc, d), lambda i: (i, 0)),
                  pl.BlockSpec((tc, TOP_K), lambda i: (i, 0)),
                  pl.BlockSpec((TOP_K, tc, yg.shape[2]), lambda i: (0, i, 0)),
                  _resident(ws_gu.shape), _resident(ws_down.shape),
                  _resident(ln_g.shape), _resident(ln_b.shape)],
        out_specs=pl.BlockSpec((tc, d), lambda i: (i, 0)),
        out_shape=jax.ShapeDtypeStruct((t, d), F32),
        compiler_params=pltpu.CompilerParams(
            dimension_semantics=("parallel",), vmem_limit_bytes=VMEM_LIMIT),
        name="combine_ln2",
    )(x1, wgt_t, yg, ws_gu, ws_down, ln_g, ln_b)


def _fused_in_weights(w_in):
    d = w_in.shape[0]
    sizes = (SB_HEADS * HEAD_DIM,) * 3 + (SWA_HEADS * HEAD_DIM, SWA_KV_HEADS * HEAD_DIM,
                                          SWA_KV_HEADS * HEAD_DIM, MEM_HEADS * MEM_HEAD_DIM)
    parts, off = [], 0
    for s in sizes:
        parts.append(w_in[:, off:off + s])
        off += s
    q_sb, k_sb, v_sb, q_sw, k_sw, v_sw, q_m = parts
    gates = w_in[:, off:]
    scale = HEAD_DIM ** -0.5
    q_sw = (q_sw * scale).reshape(d, SWA_KV_HEADS, SWA_GROUP, HEAD_DIM)
    zeros = jnp.zeros((d, SWA_GROUP, HEAD_DIM), w_in.dtype)
    q_sw = jnp.stack([jnp.concatenate([q_sw[:, 0], zeros], axis=-1),
                      jnp.concatenate([zeros, q_sw[:, 1]], axis=-1)], axis=1)
    q_sw = q_sw.reshape(d, SWA_HEADS * LANES)
    used = W_GATE + 3 * SB_HEADS * HEAD_DIM + SWA_HEADS * LANES + q_m.shape[1] + 2 * k_sw.shape[1]
    pad = jnp.zeros((d, PROJ_COLS - used), w_in.dtype)
    return jnp.concatenate([gates, q_sb * scale, k_sb, v_sb, q_sw, q_m, k_sw, v_sw, pad],
                           axis=1).astype(BF16)


def _padded_swa_out_weights(w_o_swa):
    d = w_o_swa.shape[1]
    w = w_o_swa.reshape(SWA_KV_HEADS, SWA_GROUP, HEAD_DIM, d)
    zeros = jnp.zeros((SWA_GROUP, HEAD_DIM, d), w_o_swa.dtype)
    w = jnp.stack([jnp.concatenate([w[0], zeros], axis=1),
                   jnp.concatenate([zeros, w[1]], axis=1)], axis=0)
    return w.reshape(SWA_HEADS * LANES, d).astype(BF16)


def kernel(x, mem, w_in, b_gate, w_mem_kv, sinks, w_o_sb, w_o_swa, w_o_mem, w_out,
           ln1_g, ln1_b, w_router, router_bias, w_e_gu, w_e_down, w_s_gu, w_s_down,
           ln2_g, ln2_b):
    batch, seq, d = x.shape
    mem_len = mem.shape[1]
    t = batch * seq
    x2 = x.reshape(t, d)
    row_tile = min(512, t)

    p = _proj(x2, _fused_in_weights(w_in), b_gate.reshape(1, -1), row_tile)
    mkv = _matmul_bf16(mem.reshape(batch * mem_len, d), w_mem_kv.astype(BF16), mem_len)
    o_sb = _sb_attention(p, batch, seq, min(256, seq))
    slopes = jnp.exp2(-8.0 * jnp.arange(1, SWA_HEADS + 1, dtype=F32) / SWA_HEADS)
    o_sw = _swa_attention(p, jnp.stack([slopes, sinks.astype(F32)]), batch, seq)
    o_m = _mem_attention(p, mkv, batch, seq, mem_len, min(512, seq))
    x1, x1p = _merge(o_sb, o_sw, o_m, p, x2, w_o_sb.astype(BF16), _padded_swa_out_weights(w_o_swa),
                w_o_mem.astype(BF16), w_out.astype(BF16),
                ln1_g.reshape(1, d), ln1_b.reshape(1, d), row_tile)

    out = _moe_ln(x1, x1p, w_router, router_bias, w_e_gu, w_e_down, w_s_gu, w_s_down,
                  ln2_g, ln2_b)
    return out.reshape(batch, seq, d)


def _moe_ln(x1, x1p, w_router, router_bias, w_e_gu, w_e_down, w_s_gu, w_s_down, ln2_g, ln2_b):
    t, d = x1.shape
    wr_t = w_router.T
    wr_hi = wr_t.astype(BF16)
    wr_lo = (wr_t - wr_hi.astype(F32)).astype(BF16)
    route_tile = min(256, t)
    idx, rank, wgt, cnt = _router(x1, wr_hi, wr_lo, router_bias.reshape(-1, 1).astype(F32),
                                  route_tile)
    counts = cnt[:, 0].astype(I32)
    padded = (counts + EXPERT_ROWS - 1) // EXPERT_ROWS * EXPERT_ROWS
    pad_end = jnp.cumsum(padded)
    pad_start = pad_end - padded
    n_blocks = t * TOP_K // EXPERT_ROWS + N_EXPERTS
    n_valid = pad_end[-1] // EXPERT_ROWS
    blk_start = jnp.arange(n_blocks, dtype=I32) * EXPERT_ROWS
    blk_e = jnp.minimum((pad_end[None, :] <= blk_start[:, None]).sum(-1), N_EXPERTS - 1)
    blk_e = jnp.where(jnp.arange(n_blocks) < n_valid, blk_e, blk_e[n_valid - 1]).astype(I32)
    dest = _dest(idx, rank, pad_start.reshape(-1, 1), route_tile)

    xs = _dispatch(x1p, dest, n_blocks * EXPERT_ROWS)
    ys = _experts(xs, blk_e, n_valid.reshape(1).astype(I32), w_e_gu, w_e_down)
    yg = _gather_rows(ys, dest.reshape(-1)).reshape(TOP_K, t, -1)
    return _combine(x1, wgt.T, yg, w_s_gu.astype(BF16), w_s_down.astype(BF16),
                    ln2_g.reshape(1, d), ln2_b.reshape(1, d), min(256, t))
```

```python
import functools

import jax
import jax.numpy as jnp
from jax import lax
from jax.experimental import pallas as pl
from jax.experimental.pallas import tpu as pltpu
from jax.experimental.pallas import tpu_sc as plsc

F32 = jnp.float32
BF16 = jnp.bfloat16
I32 = jnp.int32
U32 = jnp.uint32

HEAD_DIM = 64
SB_HEADS = 8
SWA_HEADS = 8
SWA_KV_HEADS = 2
SWA_GROUP = SWA_HEADS // SWA_KV_HEADS
SWA_WINDOW = 128
MEM_HEADS = 4
MEM_HEAD_DIM = 128
N_BRANCH = 3
N_EXPERTS = 256
TOP_K = 8
N_GROUPS = 8
GROUP_SIZE = N_EXPERTS // N_GROUPS
TOPK_GROUPS = 4
EXPERT_FF = 256
SHARED_FF = 256
ROUTED_SCALE = 2.5
LN_EPS = 1e-5
DEPTH = 1
DEEPNORM_ALPHA = (2 * DEPTH) ** 0.25

LANES = 128
SC_CORES = 2
SC_SUBCORES = 16
SC_WORKERS = SC_CORES * SC_SUBCORES
SC_INDEX_WINDOW = 128
SC_GATHER_ROWS = 64
VMEM_LIMIT = 56 * 1024 * 1024

D_GATE = 0
W_GATE = 3072
C_QSW = 3072
C_QSB = 4096
C_KSB = 4608
C_VSB = 5120
C_QM = 5632
C_KSW = 6144
C_VSW = 6272
PROJ_COLS = 6656
PROJ_CHUNK = 512

SB_SKIP = 110.0
SB_PAIRS_PER_STEP = 4

EXPERT_ROWS = 512


def _nt_dot(a, b):
    return lax.dot_general(a, b, (((1,), (1,)), ((), ())), preferred_element_type=F32)


def _sigmoid(x):
    return 1.0 / (1.0 + jnp.exp(-x))


def _layer_norm(h, g, b):
    mu = jnp.mean(h, axis=-1, keepdims=True)
    d = h - mu
    var = jnp.mean(d * d, axis=-1, keepdims=True)
    return d * lax.rsqrt(var + LN_EPS) * g + b


def _pack_bf16_pair(a, b):
    a_bits = lax.bitcast_convert_type(a.astype(BF16).astype(F32), U32)
    b_bits = lax.bitcast_convert_type(b.astype(BF16).astype(F32), U32)
    return (a_bits >> 16) | b_bits


def _unpack_bf16_pair(w):
    a = lax.bitcast_convert_type(w << 16, F32)
    b = lax.bitcast_convert_type(w & jnp.uint32(0xFFFF0000), F32)
    return a, b


def _resident(shape):
    nd = len(shape)
    return pl.BlockSpec(shape, lambda *_: (0,) * nd, pipeline_mode=pl.Buffered(1))


def _proj_kernel(x_ref, w_ref, b_ref, o_ref, *, gate_cols):
    xb = x_ref[...].astype(BF16)
    for j in range(o_ref.shape[1] // PROJ_CHUNK):
        cols = slice(j * PROJ_CHUNK, (j + 1) * PROJ_CHUNK)
        acc = jnp.dot(xb, w_ref[:, cols], preferred_element_type=F32)
        if (j + 1) * PROJ_CHUNK <= gate_cols:
            acc = _sigmoid(acc + b_ref[:, cols])
        o_ref[:, cols] = acc.astype(o_ref.dtype)


def _proj(x2, w_all, b_gate, tm):
    t, d = x2.shape
    n = w_all.shape[1]
    return pl.pallas_call(
        functools.partial(_proj_kernel, gate_cols=b_gate.shape[1]),
        grid=(t // tm,),
        in_specs=[pl.BlockSpec((tm, d), lambda i: (i, 0)),
                  _resident((d, n)),
                  _resident(b_gate.shape)],
        out_specs=pl.BlockSpec((tm, n), lambda i: (i, 0)),
        out_shape=jax.ShapeDtypeStruct((t, n), BF16),
        compiler_params=pltpu.CompilerParams(
            dimension_semantics=("parallel",), vmem_limit_bytes=VMEM_LIMIT),
        name="in_proj",
    )(x2, w_all, b_gate)


def _mm_kernel(x_ref, w_ref, o_ref):
    o_ref[...] = jnp.dot(x_ref[...].astype(BF16), w_ref[...],
                         preferred_element_type=F32).astype(o_ref.dtype)


def _matmul_bf16(x2, w, tm):
    t, d = x2.shape
    n = w.shape[1]
    return pl.pallas_call(
        _mm_kernel,
        grid=(t // tm,),
        in_specs=[pl.BlockSpec((tm, d), lambda i: (i, 0)), _resident((d, n))],
        out_specs=pl.BlockSpec((tm, n), lambda i: (i, 0)),
        out_shape=jax.ShapeDtypeStruct((t, n), BF16),
        compiler_params=pltpu.CompilerParams(
            dimension_semantics=("parallel",), vmem_limit_bytes=VMEM_LIMIT),
        name="mem_kv_proj",
    )(x2, w)


def _sb_kernel(q_ref, k_ref, v_ref, o_ref, *, tq):
    i = pl.program_id(2)
    pairs = q_ref.shape[1] // LANES
    lane = lax.broadcasted_iota(I32, (1, LANES), 1)
    r = lax.broadcasted_iota(I32, (tq, tq), 0)
    c = lax.broadcasted_iota(I32, (tq, tq), 1)
    tri = (r >= c).astype(BF16)
    causal = c < r
    nh = LANES // HEAD_DIM
    hmasks = [(lane >= h * HEAD_DIM) & (lane < (h + 1) * HEAD_DIM) for h in range(nh)]
    qs = []
    for p in range(pairs):
        q = q_ref[:, p * LANES:(p + 1) * LANES]
        qs.append(jnp.concatenate([jnp.where(hm, q, jnp.zeros_like(q)) for hm in hmasks], axis=0))
    causal2 = jnp.concatenate([causal] * (nh * pairs), axis=0)
    m = pairs * nh * tq

    def block(kb, carry, acc, diag):
        rows = pl.ds(pl.multiple_of(kb * tq, tq), tq)
        z = jnp.concatenate([_nt_dot(qs[p], k_ref[rows, p * LANES:(p + 1) * LANES])
                             for p in range(pairs)], axis=0)
        sp = jnp.maximum(z, 0.0) + jnp.log(1.0 + jnp.exp(-jnp.abs(z)))
        if diag:
            sp = jnp.where(causal2, sp, 0.0)
        hi = sp.astype(BF16)
        lo = (sp - hi.astype(F32)).astype(BF16)
        s2 = jnp.dot(jnp.concatenate([hi, lo], axis=0), tri, preferred_element_type=F32)
        suffix = s2[:m] + s2[m:]
        a = jnp.exp((z - carry) - suffix)
        if diag:
            a = jnp.where(causal2, a, 0.0)
        ab = a.astype(BF16)
        mp = nh * tq
        av = jnp.concatenate(
            [jnp.dot(ab[p * mp:(p + 1) * mp], v_ref[rows, p * LANES:(p + 1) * LANES],
                     preferred_element_type=F32) for p in range(pairs)], axis=0)
        return carry + suffix[:, 0:1], acc + av

    carry, acc = block(i, jnp.zeros((m, 1), F32), jnp.zeros((m, LANES), F32), True)

    def cond(s):
        kb, carry, _ = s
        return (kb >= 0) & (jnp.min(carry) < SB_SKIP)

    def body(s):
        kb, carry, acc = s
        carry, acc = block(kb, carry, acc, False)
        return kb - 1, carry, acc

    _, _, acc = lax.while_loop(cond, body, (i - 1, carry, acc))
    for p in range(pairs):
        lo_rows = acc[(p * nh) * tq:(p * nh + 1) * tq]
        hi_rows = acc[(p * nh + 1) * tq:(p * nh + 2) * tq]
        o_ref[:, p * LANES:(p + 1) * LANES] = jnp.where(hmasks[0], lo_rows, hi_rows).astype(o_ref.dtype)


def _sb_attention(p, batch, seq, tq, pairs):
    t = batch * seq
    nq = seq // tq
    w = pairs * LANES
    ngrp = SB_HEADS * HEAD_DIM // w
    qc, kc, vc = C_QSB // w, C_KSB // w, C_VSB // w
    return pl.pallas_call(
        functools.partial(_sb_kernel, tq=tq),
        grid=(batch, ngrp, nq),
        in_specs=[pl.BlockSpec((tq, w), lambda b, h, i: (b * nq + i, qc + h)),
                  pl.BlockSpec((seq, w), lambda b, h, i: (b, kc + h)),
                  pl.BlockSpec((seq, w), lambda b, h, i: (b, vc + h))],
        out_specs=pl.BlockSpec((tq, w), lambda b, h, i: (b * nq + i, h)),
        out_shape=jax.ShapeDtypeStruct((t, SB_HEADS * HEAD_DIM), BF16),
        compiler_params=pltpu.CompilerParams(
            dimension_semantics=("parallel", "parallel", "arbitrary"),
            vmem_limit_bytes=VMEM_LIMIT),
        name="sb_attention",
    )(p, p, p)


def _swa_kernel(hp_ref, q_ref, kp_ref, kc_ref, vp_ref, vc_ref, o_ref):
    n = pl.program_id(1)
    blk = q_ref.shape[0]
    nheads = q_ref.shape[1] // LANES
    sink = hp_ref[2]
    qs = jnp.concatenate([q_ref[:, g * LANES:(g + 1) * LANES] for g in range(nheads)], axis=0)
    neg = jnp.float32(-jnp.inf)
    zp = jnp.where(n > 0, _nt_dot(qs, kp_ref[...]) + hp_ref[0], neg)
    zc = _nt_dot(qs, kc_ref[...]) + hp_ref[1]
    m = jnp.maximum(jnp.max(jnp.maximum(zp, zc), axis=1, keepdims=True), sink)
    pp = jnp.exp(zp - m)
    pc = jnp.exp(zc - m)
    den = jnp.sum(pp + pc, axis=1, keepdims=True) + jnp.exp(sink - m)
    o = (jnp.dot(pp.astype(BF16), vp_ref[...], preferred_element_type=F32)
         + jnp.dot(pc.astype(BF16), vc_ref[...], preferred_element_type=F32)) / den
    lane = lax.broadcasted_iota(I32, (1, LANES), 1)
    for g in range(nheads):
        kv = g // SWA_GROUP
        kvmask = (lane >= kv * HEAD_DIM) & (lane < (kv + 1) * HEAD_DIM)
        o_ref[:, g * LANES:(g + 1) * LANES] = jnp.where(
            kvmask, o[g * blk:(g + 1) * blk], 0.0).astype(o_ref.dtype)


def _swa_tables(sinks):
    w = SWA_WINDOW
    slopes = jnp.exp2(-8.0 * jnp.arange(1, SWA_HEADS + 1, dtype=F32) / SWA_HEADS)[:, None, None]
    r = jnp.arange(w)[:, None]
    c = jnp.arange(w)[None, :]
    dist = (r - c).astype(F32)[None]
    neg = jnp.float32(-jnp.inf)
    bias_c = jnp.where((c <= r)[None], -slopes * dist, neg)
    bias_p = jnp.where((c > r)[None], -slopes * (dist + w), neg)
    sink = jnp.broadcast_to(sinks.astype(F32)[:, None, None], (SWA_HEADS, w, w))
    return jnp.stack([bias_p, bias_c, sink]).reshape(3, SWA_HEADS * w, w)


def _swa_attention(p, row_params, batch, seq):
    blk = SWA_WINDOW
    t = batch * seq
    nb = seq // blk
    qw = SWA_HEADS * LANES
    qc, kc, vc = C_QSW // qw, C_KSW // LANES, C_VSW // LANES
    cur = lambda col: (lambda b, n: (b * nb + n, col))
    prev = lambda col: (lambda b, n: (b * nb + jnp.maximum(n - 1, 0), col))
    return pl.pallas_call(
        _swa_kernel,
        grid=(batch, nb),
        in_specs=[_resident(row_params.shape),
                  pl.BlockSpec((blk, qw), lambda b, n: (b * nb + n, qc)),
                  pl.BlockSpec((blk, LANES), prev(kc)),
                  pl.BlockSpec((blk, LANES), cur(kc)),
                  pl.BlockSpec((blk, LANES), prev(vc)),
                  pl.BlockSpec((blk, LANES), cur(vc))],
        out_specs=pl.BlockSpec((blk, qw), lambda b, n: (b * nb + n, 0)),
        out_shape=jax.ShapeDtypeStruct((t, qw), BF16),
        compiler_params=pltpu.CompilerParams(
            dimension_semantics=("parallel", "arbitrary"),
            vmem_limit_bytes=VMEM_LIMIT),
        name="swa_attention",
    )(row_params, p, p, p, p, p)


def _mem_kernel(q_ref, mk_ref, mv_ref, o_ref):
    scale = MEM_HEAD_DIM ** -0.5
    for h in range(MEM_HEADS):
        cols = slice(h * MEM_HEAD_DIM, (h + 1) * MEM_HEAD_DIM)
        z = _nt_dot(q_ref[:, cols], mk_ref[:, cols]) * scale
        m = jnp.max(z, axis=1, keepdims=True)
        p = jnp.exp(z - m)
        den = jnp.sum(p, axis=1, keepdims=True)
        o = jnp.dot(p.astype(BF16), mv_ref[:, cols], preferred_element_type=F32) / den
        o_ref[:, cols] = o.astype(o_ref.dtype)


def _mem_attention(p, mkv, batch, seq, mem_len, tq):
    t = batch * seq
    nq = seq // tq
    w = MEM_HEADS * MEM_HEAD_DIM
    return pl.pallas_call(
        _mem_kernel,
        grid=(batch, nq),
        in_specs=[pl.BlockSpec((tq, w), lambda b, i: (b * nq + i, C_QM // w)),
                  pl.BlockSpec((mem_len, w), lambda b, i: (b, 0)),
                  pl.BlockSpec((mem_len, w), lambda b, i: (b, 1))],
        out_specs=pl.BlockSpec((tq, w), lambda b, i: (b * nq + i, 0)),
        out_shape=jax.ShapeDtypeStruct((t, w), BF16),
        compiler_params=pltpu.CompilerParams(
            dimension_semantics=("parallel", "arbitrary"), vmem_limit_bytes=VMEM_LIMIT),
        name="mem_attention",
    )(p, mkv, mkv)


def _merge_kernel(osb_ref, osw_ref, om_ref, g_ref, x_ref, wsb_ref, wsw_ref, wm_ref,
                  wout_ref, lng_ref, lnb_ref, x1_ref, x1p_ref):
    d = x_ref.shape[1]
    merged = g_ref[:, 0:d].astype(F32) * jnp.dot(
        osb_ref[...], wsb_ref[...], preferred_element_type=F32)
    merged += g_ref[:, d:2 * d].astype(F32) * jnp.dot(
        osw_ref[...], wsw_ref[...], preferred_element_type=F32)
    merged += g_ref[:, 2 * d:3 * d].astype(F32) * jnp.dot(
        om_ref[...], wm_ref[...], preferred_element_type=F32)
    y = jnp.dot(merged.astype(BF16), wout_ref[...], preferred_element_type=F32)
    x1 = _layer_norm(DEEPNORM_ALPHA * x_ref[...] + y, lng_ref[...], lnb_ref[...])
    x1_ref[...] = x1
    x1p_ref[...] = _pack_bf16_pair(x1[:, :d // 2], x1[:, d // 2:])


def _merge(o_sb, o_sw, o_m, p, x2, w_sb, w_sw, w_m, w_out, ln_g, ln_b, tm):
    t, d = x2.shape
    row = lambda w: pl.BlockSpec((tm, w), lambda i: (i, 0))
    return pl.pallas_call(
        _merge_kernel,
        grid=(t // tm,),
        in_specs=[row(o_sb.shape[1]), row(o_sw.shape[1]), row(o_m.shape[1]),
                  pl.BlockSpec((tm, N_BRANCH * d), lambda i: (i, 0)),
                  row(d),
                  _resident(w_sb.shape), _resident(w_sw.shape), _resident(w_m.shape),
                  _resident(w_out.shape), _resident(ln_g.shape), _resident(ln_b.shape)],
        out_specs=[row(d), row(d // 2)],
        out_shape=[jax.ShapeDtypeStruct((t, d), F32), jax.ShapeDtypeStruct((t, d // 2), U32)],
        compiler_params=pltpu.CompilerParams(
            dimension_semantics=("parallel",), vmem_limit_bytes=VMEM_LIMIT),
        name="merge_ln1",
    )(o_sb, o_sw, o_m, p, x2, w_sb, w_sw, w_m, w_out, ln_g, ln_b)


def _router_kernel(x_ref, wh_ref, wl_ref, bias_ref, idx_ref, rank_ref, wgt_ref, cnt_ref,
                   carry_ref):
    step = pl.program_id(0)
    tr = x_ref.shape[0]

    @pl.when(step == 0)
    def _():
        carry_ref[...] = jnp.zeros_like(carry_ref)

    x = x_ref[...]
    xh = x.astype(BF16)
    xl = (x - xh.astype(F32)).astype(BF16)
    logits = _nt_dot(wh_ref[...], xh) + _nt_dot(wh_ref[...], xl) + _nt_dot(wl_ref[...], xh)
    scores = _sigmoid(logits)
    biased = scores + bias_ref[...]
    neg = jnp.float32(-jnp.inf)

    sub = lax.broadcasted_iota(I32, (GROUP_SIZE, tr), 0)
    gscore = []
    for g in range(N_GROUPS):
        blk = biased[g * GROUP_SIZE:(g + 1) * GROUP_SIZE, :]
        m1 = jnp.max(blk, axis=0, keepdims=True)
        i1 = jnp.min(jnp.where(blk == m1, sub, GROUP_SIZE), axis=0, keepdims=True)
        m2 = jnp.max(jnp.where(sub == i1, neg, blk), axis=0, keepdims=True)
        gscore.append(m1 + m2)
    gs = jnp.concatenate(gscore, axis=0)

    giota = lax.broadcasted_iota(I32, (N_GROUPS, tr), 0)
    gsel = jnp.zeros((N_GROUPS, tr), F32)
    for _ in range(TOPK_GROUPS):
        m = jnp.max(gs, axis=0, keepdims=True)
        gi = jnp.min(jnp.where(gs == m, giota, N_GROUPS), axis=0, keepdims=True)
        hit = giota == gi
        gsel = jnp.where(hit, 1.0, gsel)
        gs = jnp.where(hit, neg, gs)

    masked = jnp.concatenate(
        [jnp.where(gsel[g:g + 1, :] > 0.0, biased[g * GROUP_SIZE:(g + 1) * GROUP_SIZE, :], neg)
         for g in range(N_GROUPS)], axis=0)

    eiota = lax.broadcasted_iota(I32, (N_EXPERTS, tr), 0)
    sel = jnp.zeros((N_EXPERTS, tr), F32)
    idx_rows, w_rows = [], []
    for _ in range(TOP_K):
        m = jnp.max(masked, axis=0, keepdims=True)
        ei = jnp.min(jnp.where(masked == m, eiota, N_EXPERTS), axis=0, keepdims=True)
        hit = eiota == ei
        idx_rows.append(ei)
        w_rows.append(jnp.sum(jnp.where(hit, scores, 0.0), axis=0, keepdims=True))
        sel = jnp.where(hit, 1.0, sel)
        masked = jnp.where(hit, neg, masked)

    wsum = w_rows[0]
    for wk in w_rows[1:]:
        wsum = wsum + wk
    wgt_ref[...] = jnp.concatenate(w_rows, axis=0) / wsum * ROUTED_SCALE
    idx_ref[...] = jnp.concatenate(idx_rows, axis=0)

    a = lax.broadcasted_iota(I32, (tr, tr), 0)
    b = lax.broadcasted_iota(I32, (tr, tr), 1)
    before = (a < b).astype(BF16)
    rank = jnp.dot(sel.astype(BF16), before, preferred_element_type=F32) + carry_ref[...]
    rank_rows = [jnp.sum(jnp.where(eiota == ei, rank, 0.0), axis=0, keepdims=True)
                 for ei in idx_rows]
    rank_ref[...] = jnp.concatenate(rank_rows, axis=0).astype(I32)
    carry_ref[...] = carry_ref[...] + jnp.sum(sel, axis=1, keepdims=True)
    cnt_ref[...] = carry_ref[...]


def _router(x1, wr_hi, wr_lo, bias_col, tr):
    t, d = x1.shape
    slot = pl.BlockSpec((TOP_K, tr), lambda i: (0, i))
    return pl.pallas_call(
        _router_kernel,
        grid=(t // tr,),
        in_specs=[pl.BlockSpec((tr, d), lambda i: (i, 0)),
                  _resident(wr_hi.shape), _resident(wr_lo.shape), _resident(bias_col.shape)],
        out_specs=[slot, slot, slot, pl.BlockSpec((N_EXPERTS, 1), lambda i: (0, 0))],
        out_shape=[jax.ShapeDtypeStruct((TOP_K, t), I32),
                   jax.ShapeDtypeStruct((TOP_K, t), I32),
                   jax.ShapeDtypeStruct((TOP_K, t), F32),
                   jax.ShapeDtypeStruct((N_EXPERTS, 1), F32)],
        scratch_shapes=[pltpu.VMEM((N_EXPERTS, 1), F32)],
        compiler_params=pltpu.CompilerParams(
            dimension_semantics=("arbitrary",), vmem_limit_bytes=VMEM_LIMIT),
        name="router",
    )(x1, wr_hi, wr_lo, bias_col)


def _dest_kernel(idx_ref, rank_ref, start_ref, dest_ref):
    tr = idx_ref.shape[1]
    eiota = lax.broadcasted_iota(I32, (N_EXPERTS, tr), 0)
    rows = []
    for k in range(TOP_K):
        hit = eiota == idx_ref[k:k + 1, :]
        rows.append(jnp.sum(jnp.where(hit, start_ref[...], 0), axis=0, keepdims=True))
    dest_ref[...] = jnp.concatenate(rows, axis=0) + rank_ref[...]


def _dest(idx, rank, start_col, tr):
    t = idx.shape[1]
    slot = pl.BlockSpec((TOP_K, tr), lambda i: (0, i))
    return pl.pallas_call(
        _dest_kernel,
        grid=(t // tr,),
        in_specs=[slot, slot, _resident(start_col.shape)],
        out_specs=slot,
        out_shape=jax.ShapeDtypeStruct((TOP_K, t), I32),
        compiler_params=pltpu.CompilerParams(dimension_semantics=("parallel",)),
        name="slot_dest",
    )(idx, rank, start_col)


def _sc_worker_id():
    return lax.axis_index("s") * SC_CORES + lax.axis_index("c")


def _dispatch(x1p, dest, n_rows):
    t, w = x1p.shape
    per = t // SC_WORKERS
    win = min(SC_INDEX_WINDOW, per)
    mesh = plsc.VectorSubcoreMesh(core_axis_name="c", subcore_axis_name="s")

    @functools.partial(
        pl.kernel, mesh=mesh,
        out_type=jax.ShapeDtypeStruct((n_rows, w), x1p.dtype),
        scratch_types=[pltpu.VMEM((TOP_K, win), I32),
                       pltpu.VMEM((win, w), x1p.dtype),
                       pltpu.SemaphoreType.DMA],
        name="sc_dispatch",
    )
    def scatter_rows(x_hbm, dest_hbm, xs_hbm, idx_v, rows_v, sem):
        base = _sc_worker_id() * per

        @pl.loop(0, per // win)
        def _(j):
            t0 = pl.multiple_of(base + j * win, win)
            pltpu.sync_copy(dest_hbm.at[:, pl.ds(t0, win)], idx_v)
            pltpu.sync_copy(x_hbm.at[pl.ds(t0, win)], rows_v)
            copies = [pltpu.async_copy(rows_v, xs_hbm.at[idx_v.at[k]], sem) for k in range(TOP_K)]
            for c in copies:
                c.wait()

    return scatter_rows(x1p, dest)


def _gather_rows(table, idx):
    n = idx.shape[0]
    w = table.shape[1]
    per = n // SC_WORKERS
    chunk = min(SC_GATHER_ROWS, per // 2)
    assert n % SC_WORKERS == 0 and per % (2 * chunk) == 0, (n, chunk)
    mesh = plsc.VectorSubcoreMesh(core_axis_name="c", subcore_axis_name="s")

    @functools.partial(
        pl.kernel, mesh=mesh,
        out_type=jax.ShapeDtypeStruct((n, w), table.dtype),
        scratch_types=[pltpu.VMEM((per,), I32),
                       pltpu.VMEM((2, chunk, w), table.dtype),
                       pltpu.SemaphoreType.DMA((2,)),
                       pltpu.SemaphoreType.DMA((2,))],
        name="sc_gather",
    )
    def gather_rows(table_hbm, idx_hbm, out_hbm, idx_v, rows_v, gather_sem, put_sem):
        base = _sc_worker_id() * per
        nchunks = per // chunk
        pltpu.sync_copy(idx_hbm.at[pl.ds(base, per)], idx_v)

        def gather(j, b):
            off = pl.multiple_of(j * chunk, chunk)
            return pltpu.make_async_copy(table_hbm.at[idx_v.at[pl.ds(off, chunk)]],
                                         rows_v.at[b], gather_sem.at[b])

        def put(j, b):
            off = pl.multiple_of(j * chunk, chunk)
            return pltpu.make_async_copy(rows_v.at[b], out_hbm.at[pl.ds(base + off, chunk)],
                                         put_sem.at[b])

        gather(0, 0).start()

        @pl.loop(0, nchunks, step=2)
        def _(j):
            for b in (0, 1):
                jj = j + b

                @pl.when(jj + 1 < nchunks)
                def _():
                    @pl.when(jj >= 1)
                    def _():
                        put(jj - 1, 1 - b).wait()
                    gather(jj + 1, 1 - b).start()

                gather(jj, b).wait()
                put(jj, b).start()

        put(nchunks - 2, 0).wait()
        put(nchunks - 1, 1).wait()

    return gather_rows(table, idx)


def _expert_kernel(blk_e_ref, nv_ref, xs_ref, wgu_ref, wd_ref, ys_ref, wgu_s, wd_s):
    n = pl.program_id(0)
    valid = n < nv_ref[0]
    e = blk_e_ref[n]
    e_prev = blk_e_ref[jnp.maximum(n - 1, 0)]

    @pl.when(valid & ((n == 0) | (e != e_prev)))
    def _():
        wgu_s[...] = wgu_ref[0].astype(BF16)
        wd_s[...] = wd_ref[0].astype(BF16)

    @pl.when(valid)
    def _():
        ff = wd_s.shape[0]
        half = xs_ref.shape[1]
        x_lo, x_hi = _unpack_bf16_pair(xs_ref[...])
        h = (jnp.dot(x_lo.astype(BF16), wgu_s[:half, :], preferred_element_type=F32)
             + jnp.dot(x_hi.astype(BF16), wgu_s[half:, :], preferred_element_type=F32))
        gate, up = h[:, :ff], h[:, ff:]
        act = gate * _sigmoid(gate) * up
        y = jnp.dot(act.astype(BF16), wd_s[...], preferred_element_type=F32)
        ys_ref[...] = _pack_bf16_pair(y[:, :half], y[:, half:])


def _experts(xs, blk_e, n_valid, w_gu, w_down):
    n_rows, half = xs.shape
    d = 2 * half
    nb = n_rows // EXPERT_ROWS
    ff2 = w_gu.shape[2]
    ff = w_down.shape[1]
    last = lambda n, nv: jnp.minimum(n, nv[0] - 1)
    grid_spec = pltpu.PrefetchScalarGridSpec(
        num_scalar_prefetch=2,
        grid=(nb,),
        in_specs=[pl.BlockSpec((EXPERT_ROWS, half), lambda n, be, nv: (last(n, nv), 0)),
                  pl.BlockSpec((1, d, ff2), lambda n, be, nv: (be[n], 0, 0)),
                  pl.BlockSpec((1, ff, d), lambda n, be, nv: (be[n], 0, 0))],
        out_specs=pl.BlockSpec((EXPERT_ROWS, half), lambda n, be, nv: (last(n, nv), 0)),
        scratch_shapes=[pltpu.VMEM((d, ff2), BF16), pltpu.VMEM((ff, d), BF16)],
    )
    return pl.pallas_call(
        _expert_kernel,
        grid_spec=grid_spec,
        out_shape=jax.ShapeDtypeStruct((n_rows, half), U32),
        compiler_params=pltpu.CompilerParams(
            dimension_semantics=("arbitrary",), vmem_limit_bytes=VMEM_LIMIT),
        name="experts",
    )(blk_e, n_valid, xs, w_gu, w_down)


def _combine_kernel(x1_ref, wt_ref, yg_ref, wsgu_ref, wsd_ref, lng_ref, lnb_ref, o_ref):
    tc = x1_ref.shape[0]
    x1 = x1_ref[...]
    ff = wsd_ref.shape[0]
    h = jnp.dot(x1.astype(BF16), wsgu_ref[...], preferred_element_type=F32)
    gate, up = h[:, :ff], h[:, ff:]
    act = gate * _sigmoid(gate) * up
    moe = jnp.dot(act.astype(BF16), wsd_ref[...], preferred_element_type=F32)

    half = yg_ref.shape[2]
    r_lo = jnp.zeros((tc, half), F32)
    r_hi = jnp.zeros((tc, half), F32)
    for k in range(TOP_K):
        y_lo, y_hi = _unpack_bf16_pair(yg_ref[k])
        w = wt_ref[:, k:k + 1]
        r_lo = r_lo + w * y_lo
        r_hi = r_hi + w * y_hi
    moe = moe + jnp.concatenate([r_lo, r_hi], axis=1)
    o_ref[...] = _layer_norm(DEEPNORM_ALPHA * x1 + moe, lng_ref[...], lnb_ref[...])


def _combine(x1, wgt_t, yg, ws_gu, ws_down, ln_g, ln_b, tc):
    t, d = x1.shape
    return pl.pallas_call(
        _combine_kernel,
        grid=(t // tc,),
        in_specs=[pl.BlockSpec((tc, d), lambda i: (i, 0)),
                  pl.BlockSpec((tc, TOP_K), lambda i: (i, 0)),
                  pl.BlockSpec((TOP_K, tc, yg.shape[2]), lambda i: (0, i, 0)),
                  _resident(ws_gu.shape), _resident(ws_down.shape),
                  _resident(ln_g.shape), _resident(ln_b.shape)],
        out_specs=pl.BlockSpec((tc, d), lambda i: (i, 0)),
        out_shape=jax.ShapeDtypeStruct((t, d), F32),
        compiler_params=pltpu.CompilerParams(
            dimension_semantics=("parallel",), vmem_limit_bytes=VMEM_LIMIT),
        name="combine_ln2",
    )(x1, wgt_t, yg, ws_gu, ws_down, ln_g, ln_b)


def _fused_in_weights(w_in):
    d = w_in.shape[0]
    sizes = (SB_HEADS * HEAD_DIM,) * 3 + (SWA_HEADS * HEAD_DIM, SWA_KV_HEADS * HEAD_DIM,
                                          SWA_KV_HEADS * HEAD_DIM, MEM_HEADS * MEM_HEAD_DIM)
    parts, off = [], 0
    for s in sizes:
        parts.append(w_in[:, off:off + s])
        off += s
    q_sb, k_sb, v_sb, q_sw, k_sw, v_sw, q_m = parts
    gates = w_in[:, off:]
    scale = HEAD_DIM ** -0.5
    q_sw = (q_sw * scale).reshape(d, SWA_KV_HEADS, SWA_GROUP, HEAD_DIM)
    zeros = jnp.zeros((d, SWA_GROUP, HEAD_DIM), w_in.dtype)
    q_sw = jnp.stack([jnp.concatenate([q_sw[:, 0], zeros], axis=-1),
                      jnp.concatenate([zeros, q_sw[:, 1]], axis=-1)], axis=1)
    q_sw = q_sw.reshape(d, SWA_HEADS * LANES)
    used = W_GATE + 3 * SB_HEADS * HEAD_DIM + SWA_HEADS * LANES + q_m.shape[1] + 2 * k_sw.shape[1]
    pad = jnp.zeros((d, PROJ_COLS - used), w_in.dtype)
    return jnp.concatenate([gates, q_sw, q_sb * scale, k_sb, v_sb, q_m, k_sw, v_sw, pad],
                           axis=1).astype(BF16)


def _padded_swa_out_weights(w_o_swa):
    d = w_o_swa.shape[1]
    w = w_o_swa.reshape(SWA_KV_HEADS, SWA_GROUP, HEAD_DIM, d)
    zeros = jnp.zeros((SWA_GROUP, HEAD_DIM, d), w_o_swa.dtype)
    w = jnp.stack([jnp.concatenate([w[0], zeros], axis=1),
                   jnp.concatenate([zeros, w[1]], axis=1)], axis=0)
    return w.reshape(SWA_HEADS * LANES, d).astype(BF16)


def kernel(x, mem, w_in, b_gate, w_mem_kv, sinks, w_o_sb, w_o_swa, w_o_mem, w_out,
           ln1_g, ln1_b, w_router, router_bias, w_e_gu, w_e_down, w_s_gu, w_s_down,
           ln2_g, ln2_b):
    batch, seq, d = x.shape
    mem_len = mem.shape[1]
    t = batch * seq
    x2 = x.reshape(t, d)
    row_tile = min(512, t)

    p = _proj(x2, _fused_in_weights(w_in), b_gate.reshape(1, -1), row_tile)
    mkv = _matmul_bf16(mem.reshape(batch * mem_len, d), w_mem_kv.astype(BF16), mem_len)
    o_sb = _sb_attention(p, batch, seq, min(256, seq), SB_PAIRS_PER_STEP)
    o_sw = _swa_attention(p, _swa_tables(sinks), batch, seq)
    o_m = _mem_attention(p, mkv, batch, seq, mem_len, min(512, seq))
    x1, x1p = _merge(o_sb, o_sw, o_m, p, x2, w_o_sb.astype(BF16), _padded_swa_out_weights(w_o_swa),
                w_o_mem.astype(BF16), w_out.astype(BF16),
                ln1_g.reshape(1, d), ln1_b.reshape(1, d), row_tile)

    out = _moe_ln(x1, x1p, w_router, router_bias, w_e_gu, w_e_down, w_s_gu, w_s_down,
                  ln2_g, ln2_b)
    return out.reshape(batch, seq, d)


def _moe_ln(x1, x1p, w_router, router_bias, w_e_gu, w_e_down, w_s_gu, w_s_down, ln2_g, ln2_b):
    t, d = x1.shape
    wr_t = w_router.T
    wr_hi = wr_t.astype(BF16)
    wr_lo = (wr_t - wr_hi.astype(F32)).astype(BF16)
    route_tile = min(256, t)
    idx, rank, wgt, cnt = _router(x1, wr_hi, wr_lo, router_bias.reshape(-1, 1).astype(F32),
                                  route_tile)
    counts = cnt[:, 0].astype(I32)
    padded = (counts + EXPERT_ROWS - 1) // EXPERT_ROWS * EXPERT_ROWS
    pad_end = jnp.cumsum(padded)
    pad_start = pad_end - padded
    n_blocks = t * TOP_K // EXPERT_ROWS + N_EXPERTS
    n_valid = pad_end[-1] // EXPERT_ROWS
    blk_start = jnp.arange(n_blocks, dtype=I32) * EXPERT_ROWS
    blk_e = jnp.minimum((pad_end[None, :] <= blk_start[:, None]).sum(-1), N_EXPERTS - 1)
    blk_e = jnp.where(jnp.arange(n_blocks) < n_valid, blk_e, blk_e[n_valid - 1]).astype(I32)
    dest = _dest(idx, rank, pad_start.reshape(-1, 1), route_tile)

    xs = _dispatch(x1p, dest, n_blocks * EXPERT_ROWS)
    ys = _experts(xs, blk_e, n_valid.reshape(1).astype(I32), w_e_gu, w_e_down)
    yg = _gather_rows(ys, dest.reshape(-1)).reshape(TOP_K, t, -1)
    return _combine(x1, wgt.T, yg, w_s_gu.astype(BF16), w_s_down.astype(BF16),
                    ln2_g.reshape(1, d), ln2_b.reshape(1, d), min(256, t))
```

```python
import functools

import jax
import jax.numpy as jnp
from jax import lax
from jax.experimental import pallas as pl
from jax.experimental.pallas import tpu as pltpu
from jax.experimental.pallas import tpu_sc as plsc

F32 = jnp.float32
BF16 = jnp.bfloat16
I32 = jnp.int32
U32 = jnp.uint32

HEAD_DIM = 64
SB_HEADS = 8
SWA_HEADS = 8
SWA_KV_HEADS = 2
SWA_GROUP = SWA_HEADS // SWA_KV_HEADS
SWA_WINDOW = 128
MEM_HEADS = 4
MEM_HEAD_DIM = 128
N_BRANCH = 3
N_EXPERTS = 256
TOP_K = 8
N_GROUPS = 8
GROUP_SIZE = N_EXPERTS // N_GROUPS
TOPK_GROUPS = 4
EXPERT_FF = 256
SHARED_FF = 256
ROUTED_SCALE = 2.5
LN_EPS = 1e-5
DEPTH = 1
DEEPNORM_ALPHA = (2 * DEPTH) ** 0.25

LANES = 128
SC_CORES = 2
SC_SUBCORES = 16
SC_WORKERS = SC_CORES * SC_SUBCORES
SC_INDEX_WINDOW = 128
SC_GATHER_ROWS = 64
VMEM_LIMIT = 56 * 1024 * 1024

D_GATE = 0
W_GATE = 3072
C_QSW = 3072
C_QSB = 4096
C_KSB = 4608
C_VSB = 5120
C_QM = 5632
C_KSW = 6144
C_VSW = 6272
PROJ_COLS = 6656
PROJ_CHUNK = 512

SB_SKIP = 110.0
SB_PAIRS_PER_STEP = 4

EXPERT_ROWS = 512


def _nt_dot(a, b):
    return lax.dot_general(a, b, (((1,), (1,)), ((), ())), preferred_element_type=F32)


def _sigmoid(x):
    return 1.0 / (1.0 + jnp.exp(-x))


def _layer_norm(h, g, b):
    mu = jnp.mean(h, axis=-1, keepdims=True)
    d = h - mu
    var = jnp.mean(d * d, axis=-1, keepdims=True)
    return d * lax.rsqrt(var + LN_EPS) * g + b


def _pack_bf16_pair(a, b):
    a_bits = lax.bitcast_convert_type(a.astype(BF16).astype(F32), U32)
    b_bits = lax.bitcast_convert_type(b.astype(BF16).astype(F32), U32)
    return (a_bits >> 16) | b_bits


def _unpack_bf16_pair(w):
    a = lax.bitcast_convert_type(w << 16, F32)
    b = lax.bitcast_convert_type(w & jnp.uint32(0xFFFF0000), F32)
    return a, b


def _resident(shape):
    nd = len(shape)
    return pl.BlockSpec(shape, lambda *_: (0,) * nd, pipeline_mode=pl.Buffered(1))


def _proj_kernel(x_ref, w_ref, b_ref, o_ref, *, gate_cols):
    xb = x_ref[...].astype(BF16)
    for j in range(o_ref.shape[1] // PROJ_CHUNK):
        cols = slice(j * PROJ_CHUNK, (j + 1) * PROJ_CHUNK)
        acc = jnp.dot(xb, w_ref[:, cols], preferred_element_type=F32)
        if (j + 1) * PROJ_CHUNK <= gate_cols:
            acc = _sigmoid(acc + b_ref[:, cols])
        o_ref[:, cols] = acc.astype(o_ref.dtype)


def _proj(x2, w_all, b_gate, tm):
    t, d = x2.shape
    n = w_all.shape[1]
    return pl.pallas_call(
        functools.partial(_proj_kernel, gate_cols=b_gate.shape[1]),
        grid=(t // tm,),
        in_specs=[pl.BlockSpec((tm, d), lambda i: (i, 0)),
                  _resident((d, n)),
                  _resident(b_gate.shape)],
        out_specs=pl.BlockSpec((tm, n), lambda i: (i, 0)),
        out_shape=jax.ShapeDtypeStruct((t, n), BF16),
        compiler_params=pltpu.CompilerParams(
            dimension_semantics=("parallel",), vmem_limit_bytes=VMEM_LIMIT),
        name="in_proj",
    )(x2, w_all, b_gate)


def _mm_kernel(x_ref, w_ref, o_ref):
    o_ref[...] = jnp.dot(x_ref[...].astype(BF16), w_ref[...],
                         preferred_element_type=F32).astype(o_ref.dtype)


def _matmul_bf16(x2, w, tm):
    t, d = x2.shape
    n = w.shape[1]
    return pl.pallas_call(
        _mm_kernel,
        grid=(t // tm,),
        in_specs=[pl.BlockSpec((tm, d), lambda i: (i, 0)), _resident((d, n))],
        out_specs=pl.BlockSpec((tm, n), lambda i: (i, 0)),
        out_shape=jax.ShapeDtypeStruct((t, n), BF16),
        compiler_params=pltpu.CompilerParams(
            dimension_semantics=("parallel",), vmem_limit_bytes=VMEM_LIMIT),
        name="mem_kv_proj",
    )(x2, w)


def _sb_kernel(q_ref, k_ref, v_ref, o_ref, *, tq):
    i = pl.program_id(2)
    pairs = q_ref.shape[1] // LANES
    lane = lax.broadcasted_iota(I32, (1, LANES), 1)
    r = lax.broadcasted_iota(I32, (tq, tq), 0)
    c = lax.broadcasted_iota(I32, (tq, tq), 1)
    tri = (r >= c).astype(BF16)
    causal = c < r
    nh = LANES // HEAD_DIM
    hmasks = [(lane >= h * HEAD_DIM) & (lane < (h + 1) * HEAD_DIM) for h in range(nh)]
    qs = []
    for p in range(pairs):
        q = q_ref[:, p * LANES:(p + 1) * LANES]
        qs.append(jnp.concatenate([jnp.where(hm, q, jnp.zeros_like(q)) for hm in hmasks], axis=0))
    causal2 = jnp.concatenate([causal] * (nh * pairs), axis=0)
    m = pairs * nh * tq

    def block(kb, carry, acc, diag):
        rows = pl.ds(pl.multiple_of(kb * tq, tq), tq)
        z = jnp.concatenate([_nt_dot(qs[p], k_ref[rows, p * LANES:(p + 1) * LANES])
                             for p in range(pairs)], axis=0)
        sp = jnp.maximum(z, 0.0) + jnp.log(1.0 + jnp.exp(-jnp.abs(z)))
        if diag:
            sp = jnp.where(causal2, sp, 0.0)
        hi = sp.astype(BF16)
        lo = (sp - hi.astype(F32)).astype(BF16)
        s2 = jnp.dot(jnp.concatenate([hi, lo], axis=0), tri, preferred_element_type=F32)
        suffix = s2[:m] + s2[m:]
        a = jnp.exp((z - carry) - suffix)
        if diag:
            a = jnp.where(causal2, a, 0.0)
        ab = a.astype(BF16)
        mp = nh * tq
        av = jnp.concatenate(
            [jnp.dot(ab[p * mp:(p + 1) * mp], v_ref[rows, p * LANES:(p + 1) * LANES],
                     preferred_element_type=F32) for p in range(pairs)], axis=0)
        return carry + suffix[:, 0:1], acc + av

    carry, acc = block(i, jnp.zeros((m, 1), F32), jnp.zeros((m, LANES), F32), True)

    def cond(s):
        kb, carry, _ = s
        return (kb >= 0) & (jnp.min(carry) < SB_SKIP)

    def body(s):
        kb, carry, acc = s
        carry, acc = block(kb, carry, acc, False)
        return kb - 1, carry, acc

    _, _, acc = lax.while_loop(cond, body, (i - 1, carry, acc))
    for p in range(pairs):
        lo_rows = acc[(p * nh) * tq:(p * nh + 1) * tq]
        hi_rows = acc[(p * nh + 1) * tq:(p * nh + 2) * tq]
        o_ref[:, p * LANES:(p + 1) * LANES] = jnp.where(hmasks[0], lo_rows, hi_rows).astype(o_ref.dtype)


def _sb_attention(p, batch, seq, tq, pairs):
    t = batch * seq
    nq = seq // tq
    w = pairs * LANES
    ngrp = SB_HEADS * HEAD_DIM // w
    qc, kc, vc = C_QSB // w, C_KSB // w, C_VSB // w
    return pl.pallas_call(
        functools.partial(_sb_kernel, tq=tq),
        grid=(batch, ngrp, nq),
        in_specs=[pl.BlockSpec((tq, w), lambda b, h, i: (b * nq + i, qc + h)),
                  pl.BlockSpec((seq, w), lambda b, h, i: (b, kc + h)),
                  pl.BlockSpec((seq, w), lambda b, h, i: (b, vc + h))],
        out_specs=pl.BlockSpec((tq, w), lambda b, h, i: (b * nq + i, h)),
        out_shape=jax.ShapeDtypeStruct((t, SB_HEADS * HEAD_DIM), BF16),
        compiler_params=pltpu.CompilerParams(
            dimension_semantics=("parallel", "parallel", "arbitrary"),
            vmem_limit_bytes=VMEM_LIMIT),
        name="sb_attention",
    )(p, p, p)


def _swa_kernel(hp_ref, q_ref, kp_ref, kc_ref, vp_ref, vc_ref, o_ref):
    n = pl.program_id(1)
    blk = q_ref.shape[0]
    nheads = q_ref.shape[1] // LANES
    sink = hp_ref[2]
    qs = jnp.concatenate([q_ref[:, g * LANES:(g + 1) * LANES] for g in range(nheads)], axis=0)
    neg = jnp.float32(-jnp.inf)
    zp = jnp.where(n > 0, _nt_dot(qs, kp_ref[...]) + hp_ref[0], neg)
    zc = _nt_dot(qs, kc_ref[...]) + hp_ref[1]
    m = jnp.maximum(jnp.max(jnp.maximum(zp, zc), axis=1, keepdims=True), sink)
    pp = jnp.exp(zp - m)
    pc = jnp.exp(zc - m)
    den = jnp.sum(pp + pc, axis=1, keepdims=True) + jnp.exp(sink - m)
    o = (jnp.dot(pp.astype(BF16), vp_ref[...], preferred_element_type=F32)
         + jnp.dot(pc.astype(BF16), vc_ref[...], preferred_element_type=F32)) / den
    lane = lax.broadcasted_iota(I32, (1, LANES), 1)
    for g in range(nheads):
        kv = g // SWA_GROUP
        kvmask = (lane >= kv * HEAD_DIM) & (lane < (kv + 1) * HEAD_DIM)
        o_ref[:, g * LANES:(g + 1) * LANES] = jnp.where(
            kvmask, o[g * blk:(g + 1) * blk], 0.0).astype(o_ref.dtype)


def _swa_tables(sinks):
    w = SWA_WINDOW
    slopes = jnp.exp2(-8.0 * jnp.arange(1, SWA_HEADS + 1, dtype=F32) / SWA_HEADS)[:, None, None]
    r = jnp.arange(w)[:, None]
    c = jnp.arange(w)[None, :]
    dist = (r - c).astype(F32)[None]
    neg = jnp.float32(-jnp.inf)
    bias_c = jnp.where((c <= r)[None], -slopes * dist, neg)
    bias_p = jnp.where((c > r)[None], -slopes * (dist + w), neg)
    sink = jnp.broadcast_to(sinks.astype(F32)[:, None, None], (SWA_HEADS, w, w))
    return jnp.stack([bias_p, bias_c, sink]).reshape(3, SWA_HEADS * w, w)


def _swa_attention(p, row_params, batch, seq):
    blk = SWA_WINDOW
    t = batch * seq
    nb = seq // blk
    qw = SWA_HEADS * LANES
    qc, kc, vc = C_QSW // qw, C_KSW // LANES, C_VSW // LANES
    cur = lambda col: (lambda b, n: (b * nb + n, col))
    prev = lambda col: (lambda b, n: (b * nb + jnp.maximum(n - 1, 0), col))
    return pl.pallas_call(
        _swa_kernel,
        grid=(batch, nb),
        in_specs=[_resident(row_params.shape),
                  pl.BlockSpec((blk, qw), lambda b, n: (b * nb + n, qc)),
                  pl.BlockSpec((blk, LANES), prev(kc)),
                  pl.BlockSpec((blk, LANES), cur(kc)),
                  pl.BlockSpec((blk, LANES), prev(vc)),
                  pl.BlockSpec((blk, LANES), cur(vc))],
        out_specs=pl.BlockSpec((blk, qw), lambda b, n: (b * nb + n, 0)),
        out_shape=jax.ShapeDtypeStruct((t, qw), BF16),
        compiler_params=pltpu.CompilerParams(
            dimension_semantics=("parallel", "arbitrary"),
            vmem_limit_bytes=VMEM_LIMIT),
        name="swa_attention",
    )(row_params, p, p, p, p, p)


def _mem_kernel(q_ref, mk_ref, mv_ref, o_ref):
    scale = MEM_HEAD_DIM ** -0.5
    for h in range(MEM_HEADS):
        cols = slice(h * MEM_HEAD_DIM, (h + 1) * MEM_HEAD_DIM)
        z = _nt_dot(q_ref[:, cols], mk_ref[:, cols]) * scale
        m = jnp.max(z, axis=1, keepdims=True)
        p = jnp.exp(z - m)
        den = jnp.sum(p, axis=1, keepdims=True)
        o = jnp.dot(p.astype(BF16), mv_ref[:, cols], preferred_element_type=F32) / den
        o_ref[:, cols] = o.astype(o_ref.dtype)


def _mem_attention(p, mkv, batch, seq, mem_len, tq):
    t = batch * seq
    nq = seq // tq
    w = MEM_HEADS * MEM_HEAD_DIM
    return pl.pallas_call(
        _mem_kernel,
        grid=(batch, nq),
        in_specs=[pl.BlockSpec((tq, w), lambda b, i: (b * nq + i, C_QM // w)),
                  pl.BlockSpec((mem_len, w), lambda b, i: (b, 0)),
                  pl.BlockSpec((mem_len, w), lambda b, i: (b, 1))],
        out_specs=pl.BlockSpec((tq, w), lambda b, i: (b * nq + i, 0)),
        out_shape=jax.ShapeDtypeStruct((t, w), BF16),
        compiler_params=pltpu.CompilerParams(
            dimension_semantics=("parallel", "arbitrary"), vmem_limit_bytes=VMEM_LIMIT),
        name="mem_attention",
    )(p, mkv, mkv)


def _merge_kernel(osb_ref, osw_ref, om_ref, g_ref, x_ref, wsb_ref, wsw_ref, wm_ref,
                  wout_ref, lng_ref, lnb_ref, x1_ref, x1p_ref):
    d = x_ref.shape[1]
    merged = g_ref[:, 0:d].astype(F32) * jnp.dot(
        osb_ref[...], wsb_ref[...], preferred_element_type=F32)
    merged += g_ref[:, d:2 * d].astype(F32) * jnp.dot(
        osw_ref[...], wsw_ref[...], preferred_element_type=F32)
    merged += g_ref[:, 2 * d:3 * d].astype(F32) * jnp.dot(
        om_ref[...], wm_ref[...], preferred_element_type=F32)
    y = jnp.dot(merged.astype(BF16), wout_ref[...], preferred_element_type=F32)
    x1 = _layer_norm(DEEPNORM_ALPHA * x_ref[...] + y, lng_ref[...], lnb_ref[...])
    x1_ref[...] = x1
    x1p_ref[...] = _pack_bf16_pair(x1[:, :d // 2], x1[:, d // 2:])


def _merge(o_sb, o_sw, o_m, p, x2, w_sb, w_sw, w_m, w_out, ln_g, ln_b, tm):
    t, d = x2.shape
    row = lambda w: pl.BlockSpec((tm, w), lambda i: (i, 0))
    return pl.pallas_call(
        _merge_kernel,
        grid=(t // tm,),
        in_specs=[row(o_sb.shape[1]), row(o_sw.shape[1]), row(o_m.shape[1]),
                  pl.BlockSpec((tm, N_BRANCH * d), lambda i: (i, 0)),
                  row(d),
                  _resident(w_sb.shape), _resident(w_sw.shape), _resident(w_m.shape),
                  _resident(w_out.shape), _resident(ln_g.shape), _resident(ln_b.shape)],
        out_specs=[row(d), row(d // 2)],
        out_shape=[jax.ShapeDtypeStruct((t, d), F32), jax.ShapeDtypeStruct((t, d // 2), U32)],
        compiler_params=pltpu.CompilerParams(
            dimension_semantics=("parallel",), vmem_limit_bytes=VMEM_LIMIT),
        name="merge_ln1",
    )(o_sb, o_sw, o_m, p, x2, w_sb, w_sw, w_m, w_out, ln_g, ln_b)


def _router_kernel(x_ref, wh_ref, wl_ref, bias_ref, idx_ref, rank_ref, wgt_ref, cnt_ref,
                   carry_ref):
    step = pl.program_id(0)
    tr = x_ref.shape[0]

    @pl.when(step == 0)
    def _():
        carry_ref[...] = jnp.zeros_like(carry_ref)

    x = x_ref[...]
    xh = x.astype(BF16)
    xl = (x - xh.astype(F32)).astype(BF16)
    logits = _nt_dot(wh_ref[...], xh) + _nt_dot(wh_ref[...], xl) + _nt_dot(wl_ref[...], xh)
    scores = _sigmoid(logits)
    biased = scores + bias_ref[...]
    neg = jnp.float32(-jnp.inf)

    sub = lax.broadcasted_iota(I32, (GROUP_SIZE, tr), 0)
    gscore = []
    for g in range(N_GROUPS):
        blk = biased[g * GROUP_SIZE:(g + 1) * GROUP_SIZE, :]
        m1 = jnp.max(blk, axis=0, keepdims=True)
        i1 = jnp.min(jnp.where(blk == m1, sub, GROUP_SIZE), axis=0, keepdims=True)
        m2 = jnp.max(jnp.where(sub == i1, neg, blk), axis=0, keepdims=True)
        gscore.append(m1 + m2)
    gs = jnp.concatenate(gscore, axis=0)

    giota = lax.broadcasted_iota(I32, (N_GROUPS, tr), 0)
    gsel = jnp.zeros((N_GROUPS, tr), F32)
    for _ in range(TOPK_GROUPS):
        m = jnp.max(gs, axis=0, keepdims=True)
        gi = jnp.min(jnp.where(gs == m, giota, N_GROUPS), axis=0, keepdims=True)
        hit = giota == gi
        gsel = jnp.where(hit, 1.0, gsel)
        gs = jnp.where(hit, neg, gs)

    masked = jnp.concatenate(
        [jnp.where(gsel[g:g + 1, :] > 0.0, biased[g * GROUP_SIZE:(g + 1) * GROUP_SIZE, :], neg)
         for g in range(N_GROUPS)], axis=0)

    eiota = lax.broadcasted_iota(I32, (N_EXPERTS, tr), 0)
    sel = jnp.zeros((N_EXPERTS, tr), F32)
    idx_rows, w_rows = [], []
    for _ in range(TOP_K):
        m = jnp.max(masked, axis=0, keepdims=True)
        ei = jnp.min(jnp.where(masked == m, eiota, N_EXPERTS), axis=0, keepdims=True)
        hit = eiota == ei
        idx_rows.append(ei)
        w_rows.append(jnp.sum(jnp.where(hit, scores, 0.0), axis=0, keepdims=True))
        sel = jnp.where(hit, 1.0, sel)
        masked = jnp.where(hit, neg, masked)

    wsum = w_rows[0]
    for wk in w_rows[1:]:
        wsum = wsum + wk
    wgt_ref[...] = jnp.concatenate(w_rows, axis=0) / wsum * ROUTED_SCALE
    idx_ref[...] = jnp.concatenate(idx_rows, axis=0)

    a = lax.broadcasted_iota(I32, (tr, tr), 0)
    b = lax.broadcasted_iota(I32, (tr, tr), 1)
    before = (a < b).astype(BF16)
    rank = jnp.dot(sel.astype(BF16), before, preferred_element_type=F32) + carry_ref[...]
    rank_rows = [jnp.sum(jnp.where(eiota == ei, rank, 0.0), axis=0, keepdims=True)
                 for ei in idx_rows]
    rank_ref[...] = jnp.concatenate(rank_rows, axis=0).astype(I32)
    carry_ref[...] = carry_ref[...] + jnp.sum(sel, axis=1, keepdims=True)
    cnt_ref[...] = carry_ref[...]


def _router(x1, wr_hi, wr_lo, bias_col, tr):
    t, d = x1.shape
    slot = pl.BlockSpec((TOP_K, tr), lambda i: (0, i))
    return pl.pallas_call(
        _router_kernel,
        grid=(t // tr,),
        in_specs=[pl.BlockSpec((tr, d), lambda i: (i, 0)),
                  _resident(wr_hi.shape), _resident(wr_lo.shape), _resident(bias_col.shape)],
        out_specs=[slot, slot, slot, pl.BlockSpec((N_EXPERTS, 1), lambda i: (0, 0))],
        out_shape=[jax.ShapeDtypeStruct((TOP_K, t), I32),
                   jax.ShapeDtypeStruct((TOP_K, t), I32),
                   jax.ShapeDtypeStruct((TOP_K, t), F32),
                   jax.ShapeDtypeStruct((N_EXPERTS, 1), F32)],
        scratch_shapes=[pltpu.VMEM((N_EXPERTS, 1), F32)],
        compiler_params=pltpu.CompilerParams(
            dimension_semantics=("arbitrary",), vmem_limit_bytes=VMEM_LIMIT),
        name="router",
    )(x1, wr_hi, wr_lo, bias_col)


def _dest_kernel(idx_ref, rank_ref, start_ref, dest_ref):
    tr = idx_ref.shape[1]
    eiota = lax.broadcasted_iota(I32, (N_EXPERTS, tr), 0)
    rows = []
    for k in range(TOP_K):
        hit = eiota == idx_ref[k:k + 1, :]
        rows.append(jnp.sum(jnp.where(hit, start_ref[...], 0), axis=0, keepdims=True))
    dest_ref[...] = jnp.concatenate(rows, axis=0) + rank_ref[...]


def _dest(idx, rank, start_col, tr):
    t = idx.shape[1]
    slot = pl.BlockSpec((TOP_K, tr), lambda i: (0, i))
    return pl.pallas_call(
        _dest_kernel,
        grid=(t // tr,),
        in_specs=[slot, slot, _resident(start_col.shape)],
        out_specs=slot,
        out_shape=jax.ShapeDtypeStruct((TOP_K, t), I32),
        compiler_params=pltpu.CompilerParams(dimension_semantics=("parallel",)),
        name="slot_dest",
    )(idx, rank, start_col)


def _sc_worker_id():
    return lax.axis_index("s") * SC_CORES + lax.axis_index("c")


def _dispatch(x1p, dest, n_rows):
    t, w = x1p.shape
    per = t // SC_WORKERS
    win = min(SC_INDEX_WINDOW, per)
    mesh = plsc.VectorSubcoreMesh(core_axis_name="c", subcore_axis_name="s")

    @functools.partial(
        pl.kernel, mesh=mesh,
        out_type=jax.ShapeDtypeStruct((n_rows, w), x1p.dtype),
        scratch_types=[pltpu.VMEM((TOP_K, win), I32),
                       pltpu.VMEM((win, w), x1p.dtype),
                       pltpu.SemaphoreType.DMA],
        name="sc_dispatch",
    )
    def scatter_rows(x_hbm, dest_hbm, xs_hbm, idx_v, rows_v, sem):
        base = _sc_worker_id() * per

        @pl.loop(0, per // win)
        def _(j):
            t0 = pl.multiple_of(base + j * win, win)
            pltpu.sync_copy(dest_hbm.at[:, pl.ds(t0, win)], idx_v)
            pltpu.sync_copy(x_hbm.at[pl.ds(t0, win)], rows_v)
            copies = [pltpu.async_copy(rows_v, xs_hbm.at[idx_v.at[k]], sem) for k in range(TOP_K)]
            for c in copies:
                c.wait()

    return scatter_rows(x1p, dest)


def _gather_rows(table, idx):
    n = idx.shape[0]
    w = table.shape[1]
    per = n // SC_WORKERS
    chunk = min(SC_GATHER_ROWS, per // 2)
    assert n % SC_WORKERS == 0 and per % (2 * chunk) == 0, (n, chunk)
    mesh = plsc.VectorSubcoreMesh(core_axis_name="c", subcore_axis_name="s")

    @functools.partial(
        pl.kernel, mesh=mesh,
        out_type=jax.ShapeDtypeStruct((n, w), table.dtype),
        scratch_types=[pltpu.VMEM((per,), I32),
                       pltpu.VMEM((2, chunk, w), table.dtype),
                       pltpu.SemaphoreType.DMA((2,)),
                       pltpu.SemaphoreType.DMA((2,))],
        name="sc_gather",
    )
    def gather_rows(table_hbm, idx_hbm, out_hbm, idx_v, rows_v, gather_sem, put_sem):
        base = _sc_worker_id() * per
        nchunks = per // chunk
        pltpu.sync_copy(idx_hbm.at[pl.ds(base, per)], idx_v)

        def gather(j, b):
            off = pl.multiple_of(j * chunk, chunk)
            return pltpu.make_async_copy(table_hbm.at[idx_v.at[pl.ds(off, chunk)]],
                                         rows_v.at[b], gather_sem.at[b])

        def put(j, b):
            off = pl.multiple_of(j * chunk, chunk)
            return pltpu.make_async_copy(rows_v.at[b], out_hbm.at[pl.ds(base + off, chunk)],
                                         put_sem.at[b])

        gather(0, 0).start()

        @pl.loop(0, nchunks, step=2)
        def _(j):
            for b in (0, 1):
                jj = j + b

                @pl.when(jj + 1 < nchunks)
                def _():
                    @pl.when(jj >= 1)
                    def _():
                        put(jj - 1, 1 - b).wait()
                    gather(jj + 1, 1 - b).start()

                gather(jj, b).wait()
                put(jj, b).start()

        put(nchunks - 2, 0).wait()
        put(nchunks - 1, 1).wait()

    return gather_rows(table, idx)


def _expert_kernel(first_ref, nblk_ref, total_ref, wgu_ref, wd_ref, xs_hbm, ys_hbm,
                   wgu_s, wd_s, xbuf, ybuf, in_sem, out_sem):
    e = pl.program_id(0)
    total = total_ref[0]
    rows = xbuf.shape[1]

    def block_rows(g):
        return pl.ds(pl.multiple_of(g * rows, rows), rows)

    def load(g, slot):
        return pltpu.make_async_copy(xs_hbm.at[block_rows(g), :], xbuf.at[slot], in_sem.at[slot])

    def store(g, slot):
        return pltpu.make_async_copy(ybuf.at[slot], ys_hbm.at[block_rows(g), :], out_sem.at[slot])

    @pl.when((e == 0) & (total > 0))
    def _():
        load(0, 0).start()

    wgu_s[...] = wgu_ref[0].astype(BF16)
    wd_s[...] = wd_ref[0].astype(BF16)
    g0 = first_ref[e]
    ff = wd_s.shape[0]
    half = xbuf.shape[2]

    def body(j, carry):
        g = g0 + j
        slot = g & 1
        load(g, slot).wait()

        @pl.when(g + 1 < total)
        def _():
            load(g + 1, 1 - slot).start()

        @pl.when(g >= 2)
        def _():
            store(g - 2, slot).wait()

        x_lo, x_hi = _unpack_bf16_pair(xbuf[slot])
        h = (jnp.dot(x_lo.astype(BF16), wgu_s[:half, :], preferred_element_type=F32)
             + jnp.dot(x_hi.astype(BF16), wgu_s[half:, :], preferred_element_type=F32))
        gate, up = h[:, :ff], h[:, ff:]
        act = gate * _sigmoid(gate) * up
        y = jnp.dot(act.astype(BF16), wd_s[...], preferred_element_type=F32)
        ybuf[slot] = _pack_bf16_pair(y[:, :half], y[:, half:])
        store(g, slot).start()
        return carry

    lax.fori_loop(0, nblk_ref[e], body, 0)

    @pl.when(e == pl.num_programs(0) - 1)
    def _():
        for back in (2, 1):
            @pl.when(total >= back)
            def _(back=back):
                store(total - back, (total - back) & 1).wait()


def _experts(xs, first_blk, n_blk, total_blk, w_gu, w_down):
    n_rows, half = xs.shape
    d = 2 * half
    n_exp, _, ff2 = w_gu.shape
    ff = w_down.shape[1]
    grid_spec = pltpu.PrefetchScalarGridSpec(
        num_scalar_prefetch=3,
        grid=(n_exp,),
        in_specs=[pl.BlockSpec((1, d, ff2), lambda e, *_: (e, 0, 0)),
                  pl.BlockSpec((1, ff, d), lambda e, *_: (e, 0, 0)),
                  pl.BlockSpec(memory_space=pl.ANY)],
        out_specs=pl.BlockSpec(memory_space=pl.ANY),
        scratch_shapes=[pltpu.VMEM((d, ff2), BF16), pltpu.VMEM((ff, d), BF16),
                        pltpu.VMEM((2, EXPERT_ROWS, half), xs.dtype),
                        pltpu.VMEM((2, EXPERT_ROWS, half), xs.dtype),
                        pltpu.SemaphoreType.DMA((2,)), pltpu.SemaphoreType.DMA((2,))],
    )
    return pl.pallas_call(
        _expert_kernel,
        grid_spec=grid_spec,
        out_shape=jax.ShapeDtypeStruct((n_rows, half), xs.dtype),
        compiler_params=pltpu.CompilerParams(
            dimension_semantics=("arbitrary",), vmem_limit_bytes=VMEM_LIMIT),
        name="experts",
    )(first_blk, n_blk, total_blk, w_gu, w_down, xs)


def _combine_kernel(x1_ref, wt_ref, yg_ref, wsgu_ref, wsd_ref, lng_ref, lnb_ref, o_ref):
    tc = x1_ref.shape[0]
    x1 = x1_ref[...]
    ff = wsd_ref.shape[0]
    h = jnp.dot(x1.astype(BF16), wsgu_ref[...], preferred_element_type=F32)
    gate, up = h[:, :ff], h[:, ff:]
    act = gate * _sigmoid(gate) * up
    moe = jnp.dot(act.astype(BF16), wsd_ref[...], preferred_element_type=F32)

    half = yg_ref.shape[2]
    r_lo = jnp.zeros((tc, half), F32)
    r_hi = jnp.zeros((tc, half), F32)
    for k in range(TOP_K):
        y_lo, y_hi = _unpack_bf16_pair(yg_ref[k])
        w = wt_ref[:, k:k + 1]
        r_lo = r_lo + w * y_lo
        r_hi = r_hi + w * y_hi
    moe = moe + jnp.concatenate([r_lo, r_hi], axis=1)
    o_ref[...] = _layer_norm(DEEPNORM_ALPHA * x1 + moe, lng_ref[...], lnb_ref[...])


def _combine(x1, wgt_t, yg, ws_gu, ws_down, ln_g, ln_b, tc):
    t, d = x1.shape
    return pl.pallas_call(
        _combine_kernel,
        grid=(t // tc,),
        in_specs=[pl.BlockSpec((tc, d), lambda i: (i, 0)),
                  pl.BlockSpec((tc, TOP_K), lambda i: (i, 0)),
                  pl.BlockSpec((TOP_K, tc, yg.shape[2]), lambda i: (0, i, 0)),
                  _resident(ws_gu.shape), _resident(ws_down.shape),
                  _resident(ln_g.shape), _resident(ln_b.shape)],
        out_specs=pl.BlockSpec((tc, d), lambda i: (i, 0)),
        out_shape=jax.ShapeDtypeStruct((t, d), F32),
        compiler_params=pltpu.CompilerParams(
            dimension_semantics=("parallel",), vmem_limit_bytes=VMEM_LIMIT),
        name="combine_ln2",
    )(x1, wgt_t, yg, ws_gu, ws_down, ln_g, ln_b)


def _fused_in_weights(w_in):
    d = w_in.shape[0]
    sizes = (SB_HEADS * HEAD_DIM,) * 3 + (SWA_HEADS * HEAD_DIM, SWA_KV_HEADS * HEAD_DIM,
                                          SWA_KV_HEADS * HEAD_DIM, MEM_HEADS * MEM_HEAD_DIM)
    parts, off = [], 0
    for s in sizes:
        parts.append(w_in[:, off:off + s])
        off += s
    q_sb, k_sb, v_sb, q_sw, k_sw, v_sw, q_m = parts
    gates = w_in[:, off:]
    scale = HEAD_DIM ** -0.5
    q_sw = (q_sw * scale).reshape(d, SWA_KV_HEADS, SWA_GROUP, HEAD_DIM)
    zeros = jnp.zeros((d, SWA_GROUP, HEAD_DIM), w_in.dtype)
    q_sw = jnp.stack([jnp.concatenate([q_sw[:, 0], zeros], axis=-1),
                      jnp.concatenate([zeros, q_sw[:, 1]], axis=-1)], axis=1)
    q_sw = q_sw.reshape(d, SWA_HEADS * LANES)
    used = W_GATE + 3 * SB_HEADS * HEAD_DIM + SWA_HEADS * LANES + q_m.shape[1] + 2 * k_sw.shape[1]
    pad = jnp.zeros((d, PROJ_COLS - used), w_in.dtype)
    return jnp.concatenate([gates, q_sw, q_sb * scale, k_sb, v_sb, q_m, k_sw, v_sw, pad],
                           axis=1).astype(BF16)


def _padded_swa_out_weights(w_o_swa):
    d = w_o_swa.shape[1]
    w = w_o_swa.reshape(SWA_KV_HEADS, SWA_GROUP, HEAD_DIM, d)
    zeros = jnp.zeros((SWA_GROUP, HEAD_DIM, d), w_o_swa.dtype)
    w = jnp.stack([jnp.concatenate([w[0], zeros], axis=1),
                   jnp.concatenate([zeros, w[1]], axis=1)], axis=0)
    return w.reshape(SWA_HEADS * LANES, d).astype(BF16)


def kernel(x, mem, w_in, b_gate, w_mem_kv, sinks, w_o_sb, w_o_swa, w_o_mem, w_out,
           ln1_g, ln1_b, w_router, router_bias, w_e_gu, w_e_down, w_s_gu, w_s_down,
           ln2_g, ln2_b):
    batch, seq, d = x.shape
    mem_len = mem.shape[1]
    t = batch * seq
    x2 = x.reshape(t, d)
    row_tile = min(512, t)

    p = _proj(x2, _fused_in_weights(w_in), b_gate.reshape(1, -1), row_tile)
    mkv = _matmul_bf16(mem.reshape(batch * mem_len, d), w_mem_kv.astype(BF16), mem_len)
    o_sb = _sb_attention(p, batch, seq, min(256, seq), SB_PAIRS_PER_STEP)
    o_sw = _swa_attention(p, _swa_tables(sinks), batch, seq)
    o_m = _mem_attention(p, mkv, batch, seq, mem_len, min(512, seq))
    x1, x1p = _merge(o_sb, o_sw, o_m, p, x2, w_o_sb.astype(BF16), _padded_swa_out_weights(w_o_swa),
                w_o_mem.astype(BF16), w_out.astype(BF16),
                ln1_g.reshape(1, d), ln1_b.reshape(1, d), row_tile)

    out = _moe_ln(x1, x1p, w_router, router_bias, w_e_gu, w_e_down, w_s_gu, w_s_down,
                  ln2_g, ln2_b)
    return out.reshape(batch, seq, d)


def _moe_ln(x1, x1p, w_router, router_bias, w_e_gu, w_e_down, w_s_gu, w_s_down, ln2_g, ln2_b):
    t, d = x1.shape
    wr_t = w_router.T
    wr_hi = wr_t.astype(BF16)
    wr_lo = (wr_t - wr_hi.astype(F32)).astype(BF16)
    route_tile = min(256, t)
    idx, rank, wgt, cnt = _router(x1, wr_hi, wr_lo, router_bias.reshape(-1, 1).astype(F32),
                                  route_tile)
    counts = cnt[:, 0].astype(I32)
    padded = (counts + EXPERT_ROWS - 1) // EXPERT_ROWS * EXPERT_ROWS
    pad_end = jnp.cumsum(padded)
    pad_start = pad_end - padded
    n_blocks = t * TOP_K // EXPERT_ROWS + N_EXPERTS
    dest = _dest(idx, rank, pad_start.reshape(-1, 1), route_tile)

    xs = _dispatch(x1p, dest, n_blocks * EXPERT_ROWS)
    ys = _experts(xs, pad_start // EXPERT_ROWS, padded // EXPERT_ROWS,
                  pad_end[-1:] // EXPERT_ROWS, w_e_gu, w_e_down)
    yg = _gather_rows(ys, dest.reshape(-1)).reshape(TOP_K, t, -1)
    return _combine(x1, wgt.T, yg, w_s_gu.astype(BF16), w_s_down.astype(BF16),
                    ln2_g.reshape(1, d), ln2_b.reshape(1, d), min(256, t))
```

```python
import functools

import jax
import jax.numpy as jnp
from jax import lax
from jax.experimental import pallas as pl
from jax.experimental.pallas import tpu as pltpu
from jax.experimental.pallas import tpu_sc as plsc

F32 = jnp.float32
BF16 = jnp.bfloat16
I32 = jnp.int32
U32 = jnp.uint32

HEAD_DIM = 64
SB_HEADS = 8
SWA_HEADS = 8
SWA_KV_HEADS = 2
SWA_GROUP = SWA_HEADS // SWA_KV_HEADS
SWA_WINDOW = 128
MEM_HEADS = 4
MEM_HEAD_DIM = 128
N_BRANCH = 3
N_EXPERTS = 256
TOP_K = 8
N_GROUPS = 8
GROUP_SIZE = N_EXPERTS // N_GROUPS
TOPK_GROUPS = 4
EXPERT_FF = 256
SHARED_FF = 256
ROUTED_SCALE = 2.5
LN_EPS = 1e-5
DEPTH = 1
DEEPNORM_ALPHA = (2 * DEPTH) ** 0.25

LANES = 128
SC_CORES = 2
SC_SUBCORES = 16
SC_WORKERS = SC_CORES * SC_SUBCORES
SC_INDEX_WINDOW = 128
SC_GATHER_ROWS = 64
VMEM_LIMIT = 56 * 1024 * 1024

D_GATE = 0
W_GATE = 3072
C_QSW = 3072
C_QSB = 4096
C_KSB = 4608
C_VSB = 5120
C_QM = 5632
C_KSW = 6144
C_VSW = 6272
PROJ_COLS = 6656
PROJ_CHUNK = 512

SB_SKIP = 110.0
SB_PAIRS_PER_STEP = 4

EXPERT_ROWS = 512
EXPERT_IN_SLOTS = 4
EXPERT_OUT_SLOTS = 2


def _nt_dot(a, b):
    return lax.dot_general(a, b, (((1,), (1,)), ((), ())), preferred_element_type=F32)


def _sigmoid(x):
    return 1.0 / (1.0 + jnp.exp(-x))


def _layer_norm(h, g, b):
    mu = jnp.mean(h, axis=-1, keepdims=True)
    d = h - mu
    var = jnp.mean(d * d, axis=-1, keepdims=True)
    return d * lax.rsqrt(var + LN_EPS) * g + b


def _pack_bf16_pair(a, b):
    a_bits = lax.bitcast_convert_type(a.astype(BF16).astype(F32), U32)
    b_bits = lax.bitcast_convert_type(b.astype(BF16).astype(F32), U32)
    return (a_bits >> 16) | b_bits


def _unpack_bf16_pair(w):
    a = lax.bitcast_convert_type(w << 16, F32)
    b = lax.bitcast_convert_type(w & jnp.uint32(0xFFFF0000), F32)
    return a, b


def _resident(shape):
    nd = len(shape)
    return pl.BlockSpec(shape, lambda *_: (0,) * nd, pipeline_mode=pl.Buffered(1))


def _proj_kernel(x_ref, w_ref, b_ref, o_ref, *, gate_cols):
    xb = x_ref[...].astype(BF16)
    for j in range(o_ref.shape[1] // PROJ_CHUNK):
        cols = slice(j * PROJ_CHUNK, (j + 1) * PROJ_CHUNK)
        acc = jnp.dot(xb, w_ref[:, cols], preferred_element_type=F32)
        if (j + 1) * PROJ_CHUNK <= gate_cols:
            acc = _sigmoid(acc + b_ref[:, cols])
        o_ref[:, cols] = acc.astype(o_ref.dtype)


def _proj(x2, w_all, b_gate, tm):
    t, d = x2.shape
    n = w_all.shape[1]
    return pl.pallas_call(
        functools.partial(_proj_kernel, gate_cols=b_gate.shape[1]),
        grid=(t // tm,),
        in_specs=[pl.BlockSpec((tm, d), lambda i: (i, 0)),
                  _resident((d, n)),
                  _resident(b_gate.shape)],
        out_specs=pl.BlockSpec((tm, n), lambda i: (i, 0)),
        out_shape=jax.ShapeDtypeStruct((t, n), BF16),
        compiler_params=pltpu.CompilerParams(
            dimension_semantics=("parallel",), vmem_limit_bytes=VMEM_LIMIT),
        name="in_proj",
    )(x2, w_all, b_gate)


def _mm_kernel(x_ref, w_ref, o_ref):
    o_ref[...] = jnp.dot(x_ref[...].astype(BF16), w_ref[...],
                         preferred_element_type=F32).astype(o_ref.dtype)


def _matmul_bf16(x2, w, tm):
    t, d = x2.shape
    n = w.shape[1]
    return pl.pallas_call(
        _mm_kernel,
        grid=(t // tm,),
        in_specs=[pl.BlockSpec((tm, d), lambda i: (i, 0)), _resident((d, n))],
        out_specs=pl.BlockSpec((tm, n), lambda i: (i, 0)),
        out_shape=jax.ShapeDtypeStruct((t, n), BF16),
        compiler_params=pltpu.CompilerParams(
            dimension_semantics=("parallel",), vmem_limit_bytes=VMEM_LIMIT),
        name="mem_kv_proj",
    )(x2, w)


def _sb_kernel(q_ref, k_ref, v_ref, o_ref, *, tq):
    i = pl.program_id(2)
    pairs = q_ref.shape[1] // LANES
    lane = lax.broadcasted_iota(I32, (1, LANES), 1)
    r = lax.broadcasted_iota(I32, (tq, tq), 0)
    c = lax.broadcasted_iota(I32, (tq, tq), 1)
    tri = (r >= c).astype(BF16)
    causal = c < r
    nh = LANES // HEAD_DIM
    hmasks = [(lane >= h * HEAD_DIM) & (lane < (h + 1) * HEAD_DIM) for h in range(nh)]
    qs = []
    for p in range(pairs):
        q = q_ref[:, p * LANES:(p + 1) * LANES]
        qs.append(jnp.concatenate([jnp.where(hm, q, jnp.zeros_like(q)) for hm in hmasks], axis=0))
    causal2 = jnp.concatenate([causal] * (nh * pairs), axis=0)
    m = pairs * nh * tq

    def block(kb, carry, acc, diag):
        rows = pl.ds(pl.multiple_of(kb * tq, tq), tq)
        z = jnp.concatenate([_nt_dot(qs[p], k_ref[rows, p * LANES:(p + 1) * LANES])
                             for p in range(pairs)], axis=0)
        sp = jnp.maximum(z, 0.0) + jnp.log(1.0 + jnp.exp(-jnp.abs(z)))
        if diag:
            sp = jnp.where(causal2, sp, 0.0)
        hi = sp.astype(BF16)
        lo = (sp - hi.astype(F32)).astype(BF16)
        s2 = jnp.dot(jnp.concatenate([hi, lo], axis=0), tri, preferred_element_type=F32)
        suffix = s2[:m] + s2[m:]
        a = jnp.exp((z - carry) - suffix)
        if diag:
            a = jnp.where(causal2, a, 0.0)
        ab = a.astype(BF16)
        mp = nh * tq
        av = jnp.concatenate(
            [jnp.dot(ab[p * mp:(p + 1) * mp], v_ref[rows, p * LANES:(p + 1) * LANES],
                     preferred_element_type=F32) for p in range(pairs)], axis=0)
        return carry + suffix[:, 0:1], acc + av

    carry, acc = block(i, jnp.zeros((m, 1), F32), jnp.zeros((m, LANES), F32), True)

    def cond(s):
        kb, carry, _ = s
        return (kb >= 0) & (jnp.min(carry) < SB_SKIP)

    def body(s):
        kb, carry, acc = s
        carry, acc = block(kb, carry, acc, False)
        return kb - 1, carry, acc

    _, _, acc = lax.while_loop(cond, body, (i - 1, carry, acc))
    for p in range(pairs):
        lo_rows = acc[(p * nh) * tq:(p * nh + 1) * tq]
        hi_rows = acc[(p * nh + 1) * tq:(p * nh + 2) * tq]
        o_ref[:, p * LANES:(p + 1) * LANES] = jnp.where(hmasks[0], lo_rows, hi_rows).astype(o_ref.dtype)


def _sb_attention(p, batch, seq, tq, pairs):
    t = batch * seq
    nq = seq // tq
    w = pairs * LANES
    ngrp = SB_HEADS * HEAD_DIM // w
    qc, kc, vc = C_QSB // w, C_KSB // w, C_VSB // w
    return pl.pallas_call(
        functools.partial(_sb_kernel, tq=tq),
        grid=(batch, ngrp, nq),
        in_specs=[pl.BlockSpec((tq, w), lambda b, h, i: (b * nq + i, qc + h)),
                  pl.BlockSpec((seq, w), lambda b, h, i: (b, kc + h)),
                  pl.BlockSpec((seq, w), lambda b, h, i: (b, vc + h))],
        out_specs=pl.BlockSpec((tq, w), lambda b, h, i: (b * nq + i, h)),
        out_shape=jax.ShapeDtypeStruct((t, SB_HEADS * HEAD_DIM), BF16),
        compiler_params=pltpu.CompilerParams(
            dimension_semantics=("parallel", "parallel", "arbitrary"),
            vmem_limit_bytes=VMEM_LIMIT),
        name="sb_attention",
    )(p, p, p)


def _swa_kernel(hp_ref, q_ref, kp_ref, kc_ref, vp_ref, vc_ref, o_ref):
    n = pl.program_id(1)
    blk = q_ref.shape[0]
    nheads = q_ref.shape[1] // LANES
    sink = hp_ref[2]
    qs = jnp.concatenate([q_ref[:, g * LANES:(g + 1) * LANES] for g in range(nheads)], axis=0)
    neg = jnp.float32(-jnp.inf)
    zp = jnp.where(n > 0, _nt_dot(qs, kp_ref[...]) + hp_ref[0], neg)
    zc = _nt_dot(qs, kc_ref[...]) + hp_ref[1]
    m = jnp.maximum(jnp.max(jnp.maximum(zp, zc), axis=1, keepdims=True), sink)
    pp = jnp.exp(zp - m)
    pc = jnp.exp(zc - m)
    den = jnp.sum(pp + pc, axis=1, keepdims=True) + jnp.exp(sink - m)
    o = (jnp.dot(pp.astype(BF16), vp_ref[...], preferred_element_type=F32)
         + jnp.dot(pc.astype(BF16), vc_ref[...], preferred_element_type=F32)) / den
    lane = lax.broadcasted_iota(I32, (1, LANES), 1)
    for g in range(nheads):
        kv = g // SWA_GROUP
        kvmask = (lane >= kv * HEAD_DIM) & (lane < (kv + 1) * HEAD_DIM)
        o_ref[:, g * LANES:(g + 1) * LANES] = jnp.where(
            kvmask, o[g * blk:(g + 1) * blk], 0.0).astype(o_ref.dtype)


def _swa_tables(sinks):
    w = SWA_WINDOW
    slopes = jnp.exp2(-8.0 * jnp.arange(1, SWA_HEADS + 1, dtype=F32) / SWA_HEADS)[:, None, None]
    r = jnp.arange(w)[:, None]
    c = jnp.arange(w)[None, :]
    dist = (r - c).astype(F32)[None]
    neg = jnp.float32(-jnp.inf)
    bias_c = jnp.where((c <= r)[None], -slopes * dist, neg)
    bias_p = jnp.where((c > r)[None], -slopes * (dist + w), neg)
    sink = jnp.broadcast_to(sinks.astype(F32)[:, None, None], (SWA_HEADS, w, w))
    return jnp.stack([bias_p, bias_c, sink]).reshape(3, SWA_HEADS * w, w)


def _swa_attention(p, row_params, batch, seq):
    blk = SWA_WINDOW
    t = batch * seq
    nb = seq // blk
    qw = SWA_HEADS * LANES
    qc, kc, vc = C_QSW // qw, C_KSW // LANES, C_VSW // LANES
    cur = lambda col: (lambda b, n: (b * nb + n, col))
    prev = lambda col: (lambda b, n: (b * nb + jnp.maximum(n - 1, 0), col))
    return pl.pallas_call(
        _swa_kernel,
        grid=(batch, nb),
        in_specs=[_resident(row_params.shape),
                  pl.BlockSpec((blk, qw), lambda b, n: (b * nb + n, qc)),
                  pl.BlockSpec((blk, LANES), prev(kc)),
                  pl.BlockSpec((blk, LANES), cur(kc)),
                  pl.BlockSpec((blk, LANES), prev(vc)),
                  pl.BlockSpec((blk, LANES), cur(vc))],
        out_specs=pl.BlockSpec((blk, qw), lambda b, n: (b * nb + n, 0)),
        out_shape=jax.ShapeDtypeStruct((t, qw), BF16),
        compiler_params=pltpu.CompilerParams(
            dimension_semantics=("parallel", "arbitrary"),
            vmem_limit_bytes=VMEM_LIMIT),
        name="swa_attention",
    )(row_params, p, p, p, p, p)


def _mem_kernel(q_ref, mk_ref, mv_ref, o_ref):
    scale = MEM_HEAD_DIM ** -0.5
    for h in range(MEM_HEADS):
        cols = slice(h * MEM_HEAD_DIM, (h + 1) * MEM_HEAD_DIM)
        z = _nt_dot(q_ref[:, cols], mk_ref[:, cols]) * scale
        m = jnp.max(z, axis=1, keepdims=True)
        p = jnp.exp(z - m)
        den = jnp.sum(p, axis=1, keepdims=True)
        o = jnp.dot(p.astype(BF16), mv_ref[:, cols], preferred_element_type=F32) / den
        o_ref[:, cols] = o.astype(o_ref.dtype)


def _mem_attention(p, mkv, batch, seq, mem_len, tq):
    t = batch * seq
    nq = seq // tq
    w = MEM_HEADS * MEM_HEAD_DIM
    return pl.pallas_call(
        _mem_kernel,
        grid=(batch, nq),
        in_specs=[pl.BlockSpec((tq, w), lambda b, i: (b * nq + i, C_QM // w)),
                  pl.BlockSpec((mem_len, w), lambda b, i: (b, 0)),
                  pl.BlockSpec((mem_len, w), lambda b, i: (b, 1))],
        out_specs=pl.BlockSpec((tq, w), lambda b, i: (b * nq + i, 0)),
        out_shape=jax.ShapeDtypeStruct((t, w), BF16),
        compiler_params=pltpu.CompilerParams(
            dimension_semantics=("parallel", "arbitrary"), vmem_limit_bytes=VMEM_LIMIT),
        name="mem_attention",
    )(p, mkv, mkv)


def _merge_kernel(osb_ref, osw_ref, om_ref, g_ref, x_ref, wsb_ref, wsw_ref, wm_ref,
                  wout_ref, lng_ref, lnb_ref, x1_ref, x1p_ref):
    d = x_ref.shape[1]
    merged = g_ref[:, 0:d].astype(F32) * jnp.dot(
        osb_ref[...], wsb_ref[...], preferred_element_type=F32)
    merged += g_ref[:, d:2 * d].astype(F32) * jnp.dot(
        osw_ref[...], wsw_ref[...], preferred_element_type=F32)
    merged += g_ref[:, 2 * d:3 * d].astype(F32) * jnp.dot(
        om_ref[...], wm_ref[...], preferred_element_type=F32)
    y = jnp.dot(merged.astype(BF16), wout_ref[...], preferred_element_type=F32)
    x1 = _layer_norm(DEEPNORM_ALPHA * x_ref[...] + y, lng_ref[...], lnb_ref[...])
    x1_ref[...] = x1
    x1p_ref[...] = _pack_bf16_pair(x1[:, :d // 2], x1[:, d // 2:])


def _merge(o_sb, o_sw, o_m, p, x2, w_sb, w_sw, w_m, w_out, ln_g, ln_b, tm):
    t, d = x2.shape
    row = lambda w: pl.BlockSpec((tm, w), lambda i: (i, 0))
    return pl.pallas_call(
        _merge_kernel,
        grid=(t // tm,),
        in_specs=[row(o_sb.shape[1]), row(o_sw.shape[1]), row(o_m.shape[1]),
                  pl.BlockSpec((tm, N_BRANCH * d), lambda i: (i, 0)),
                  row(d),
                  _resident(w_sb.shape), _resident(w_sw.shape), _resident(w_m.shape),
                  _resident(w_out.shape), _resident(ln_g.shape), _resident(ln_b.shape)],
        out_specs=[row(d), row(d // 2)],
        out_shape=[jax.ShapeDtypeStruct((t, d), F32), jax.ShapeDtypeStruct((t, d // 2), U32)],
        compiler_params=pltpu.CompilerParams(
            dimension_semantics=("parallel",), vmem_limit_bytes=VMEM_LIMIT),
        name="merge_ln1",
    )(o_sb, o_sw, o_m, p, x2, w_sb, w_sw, w_m, w_out, ln_g, ln_b)


def _router_kernel(x_ref, wh_ref, wl_ref, bias_ref, idx_ref, rank_ref, wgt_ref, cnt_ref,
                   carry_ref):
    step = pl.program_id(0)
    tr = x_ref.shape[0]

    @pl.when(step == 0)
    def _():
        carry_ref[...] = jnp.zeros_like(carry_ref)

    x = x_ref[...]
    xh = x.astype(BF16)
    xl = (x - xh.astype(F32)).astype(BF16)
    logits = _nt_dot(wh_ref[...], xh) + _nt_dot(wh_ref[...], xl) + _nt_dot(wl_ref[...], xh)
    scores = _sigmoid(logits)
    biased = scores + bias_ref[...]
    neg = jnp.float32(-jnp.inf)

    sub = lax.broadcasted_iota(I32, (GROUP_SIZE, tr), 0)
    gscore = []
    for g in range(N_GROUPS):
        blk = biased[g * GROUP_SIZE:(g + 1) * GROUP_SIZE, :]
        m1 = jnp.max(blk, axis=0, keepdims=True)
        i1 = jnp.min(jnp.where(blk == m1, sub, GROUP_SIZE), axis=0, keepdims=True)
        m2 = jnp.max(jnp.where(sub == i1, neg, blk), axis=0, keepdims=True)
        gscore.append(m1 + m2)
    gs = jnp.concatenate(gscore, axis=0)

    giota = lax.broadcasted_iota(I32, (N_GROUPS, tr), 0)
    gsel = jnp.zeros((N_GROUPS, tr), F32)
    for _ in range(TOPK_GROUPS):
        m = jnp.max(gs, axis=0, keepdims=True)
        gi = jnp.min(jnp.where(gs == m, giota, N_GROUPS), axis=0, keepdims=True)
        hit = giota == gi
        gsel = jnp.where(hit, 1.0, gsel)
        gs = jnp.where(hit, neg, gs)

    masked = jnp.concatenate(
        [jnp.where(gsel[g:g + 1, :] > 0.0, biased[g * GROUP_SIZE:(g + 1) * GROUP_SIZE, :], neg)
         for g in range(N_GROUPS)], axis=0)

    eiota = lax.broadcasted_iota(I32, (N_EXPERTS, tr), 0)
    sel = jnp.zeros((N_EXPERTS, tr), F32)
    idx_rows, w_rows = [], []
    for _ in range(TOP_K):
        m = jnp.max(masked, axis=0, keepdims=True)
        ei = jnp.min(jnp.where(masked == m, eiota, N_EXPERTS), axis=0, keepdims=True)
        hit = eiota == ei
        idx_rows.append(ei)
        w_rows.append(jnp.sum(jnp.where(hit, scores, 0.0), axis=0, keepdims=True))
        sel = jnp.where(hit, 1.0, sel)
        masked = jnp.where(hit, neg, masked)

    wsum = w_rows[0]
    for wk in w_rows[1:]:
        wsum = wsum + wk
    wgt_ref[...] = jnp.concatenate(w_rows, axis=0) / wsum * ROUTED_SCALE
    idx_ref[...] = jnp.concatenate(idx_rows, axis=0)

    a = lax.broadcasted_iota(I32, (tr, tr), 0)
    b = lax.broadcasted_iota(I32, (tr, tr), 1)
    before = (a < b).astype(BF16)
    rank = jnp.dot(sel.astype(BF16), before, preferred_element_type=F32) + carry_ref[...]
    rank_rows = [jnp.sum(jnp.where(eiota == ei, rank, 0.0), axis=0, keepdims=True)
                 for ei in idx_rows]
    rank_ref[...] = jnp.concatenate(rank_rows, axis=0).astype(I32)
    carry_ref[...] = carry_ref[...] + jnp.sum(sel, axis=1, keepdims=True)
    cnt_ref[...] = carry_ref[...]


def _router(x1, wr_hi, wr_lo, bias_col, tr):
    t, d = x1.shape
    slot = pl.BlockSpec((TOP_K, tr), lambda i: (0, i))
    return pl.pallas_call(
        _router_kernel,
        grid=(t // tr,),
        in_specs=[pl.BlockSpec((tr, d), lambda i: (i, 0)),
                  _resident(wr_hi.shape), _resident(wr_lo.shape), _resident(bias_col.shape)],
        out_specs=[slot, slot, slot, pl.BlockSpec((N_EXPERTS, 1), lambda i: (0, 0))],
        out_shape=[jax.ShapeDtypeStruct((TOP_K, t), I32),
                   jax.ShapeDtypeStruct((TOP_K, t), I32),
                   jax.ShapeDtypeStruct((TOP_K, t), F32),
                   jax.ShapeDtypeStruct((N_EXPERTS, 1), F32)],
        scratch_shapes=[pltpu.VMEM((N_EXPERTS, 1), F32)],
        compiler_params=pltpu.CompilerParams(
            dimension_semantics=("arbitrary",), vmem_limit_bytes=VMEM_LIMIT),
        name="router",
    )(x1, wr_hi, wr_lo, bias_col)


def _dest_kernel(idx_ref, rank_ref, start_ref, dest_ref):
    tr = idx_ref.shape[1]
    eiota = lax.broadcasted_iota(I32, (N_EXPERTS, tr), 0)
    rows = []
    for k in range(TOP_K):
        hit = eiota == idx_ref[k:k + 1, :]
        rows.append(jnp.sum(jnp.where(hit, start_ref[...], 0), axis=0, keepdims=True))
    dest_ref[...] = jnp.concatenate(rows, axis=0) + rank_ref[...]


def _dest(idx, rank, start_col, tr):
    t = idx.shape[1]
    slot = pl.BlockSpec((TOP_K, tr), lambda i: (0, i))
    return pl.pallas_call(
        _dest_kernel,
        grid=(t // tr,),
        in_specs=[slot, slot, _resident(start_col.shape)],
        out_specs=slot,
        out_shape=jax.ShapeDtypeStruct((TOP_K, t), I32),
        compiler_params=pltpu.CompilerParams(dimension_semantics=("parallel",)),
        name="slot_dest",
    )(idx, rank, start_col)


def _sc_worker_id():
    return lax.axis_index("s") * SC_CORES + lax.axis_index("c")


def _dispatch(x1p, dest, n_rows):
    t, w = x1p.shape
    per = t // SC_WORKERS
    win = min(SC_INDEX_WINDOW, per)
    mesh = plsc.VectorSubcoreMesh(core_axis_name="c", subcore_axis_name="s")

    @functools.partial(
        pl.kernel, mesh=mesh,
        out_type=jax.ShapeDtypeStruct((n_rows, w), x1p.dtype),
        scratch_types=[pltpu.VMEM((TOP_K, win), I32),
                       pltpu.VMEM((win, w), x1p.dtype),
                       pltpu.SemaphoreType.DMA],
        name="sc_dispatch",
    )
    def scatter_rows(x_hbm, dest_hbm, xs_hbm, idx_v, rows_v, sem):
        base = _sc_worker_id() * per

        @pl.loop(0, per // win)
        def _(j):
            t0 = pl.multiple_of(base + j * win, win)
            pltpu.sync_copy(dest_hbm.at[:, pl.ds(t0, win)], idx_v)
            pltpu.sync_copy(x_hbm.at[pl.ds(t0, win)], rows_v)
            copies = [pltpu.async_copy(rows_v, xs_hbm.at[idx_v.at[k]], sem) for k in range(TOP_K)]
            for c in copies:
                c.wait()

    return scatter_rows(x1p, dest)


def _gather_rows(table, idx):
    n = idx.shape[0]
    w = table.shape[1]
    per = n // SC_WORKERS
    chunk = min(SC_GATHER_ROWS, per // 2)
    assert n % SC_WORKERS == 0 and per % (2 * chunk) == 0, (n, chunk)
    mesh = plsc.VectorSubcoreMesh(core_axis_name="c", subcore_axis_name="s")

    @functools.partial(
        pl.kernel, mesh=mesh,
        out_type=jax.ShapeDtypeStruct((n, w), table.dtype),
        scratch_types=[pltpu.VMEM((per,), I32),
                       pltpu.VMEM((2, chunk, w), table.dtype),
                       pltpu.SemaphoreType.DMA((2,)),
                       pltpu.SemaphoreType.DMA((2,))],
        name="sc_gather",
    )
    def gather_rows(table_hbm, idx_hbm, out_hbm, idx_v, rows_v, gather_sem, put_sem):
        base = _sc_worker_id() * per
        nchunks = per // chunk
        pltpu.sync_copy(idx_hbm.at[pl.ds(base, per)], idx_v)

        def gather(j, b):
            off = pl.multiple_of(j * chunk, chunk)
            return pltpu.make_async_copy(table_hbm.at[idx_v.at[pl.ds(off, chunk)]],
                                         rows_v.at[b], gather_sem.at[b])

        def put(j, b):
            off = pl.multiple_of(j * chunk, chunk)
            return pltpu.make_async_copy(rows_v.at[b], out_hbm.at[pl.ds(base + off, chunk)],
                                         put_sem.at[b])

        gather(0, 0).start()

        @pl.loop(0, nchunks, step=2)
        def _(j):
            for b in (0, 1):
                jj = j + b

                @pl.when(jj + 1 < nchunks)
                def _():
                    @pl.when(jj >= 1)
                    def _():
                        put(jj - 1, 1 - b).wait()
                    gather(jj + 1, 1 - b).start()

                gather(jj, b).wait()
                put(jj, b).start()

        put(nchunks - 2, 0).wait()
        put(nchunks - 1, 1).wait()

    return gather_rows(table, idx)


def _expert_kernel(first_ref, nblk_ref, total_ref, wgu_ref, wd_ref, xs_hbm, ys_hbm,
                   wgu_s, wd_s, xbuf, ybuf, in_sem, out_sem):
    e = pl.program_id(0)
    total = total_ref[0]
    n_in, rows, half = xbuf.shape
    n_out = ybuf.shape[0]
    ahead = n_in - 1

    def block_rows(g):
        return pl.ds(pl.multiple_of(g * rows, rows), rows)

    def load(g):
        slot = g % n_in
        return pltpu.make_async_copy(xs_hbm.at[block_rows(g), :], xbuf.at[slot], in_sem.at[slot])

    def store(g):
        slot = g % n_out
        return pltpu.make_async_copy(ybuf.at[slot], ys_hbm.at[block_rows(g), :], out_sem.at[slot])

    @pl.when(e == 0)
    def _():
        for g in range(ahead):
            @pl.when(g < total)
            def _(g=g):
                load(g).start()

    wgu_s[...] = wgu_ref[0].astype(BF16)
    wd_s[...] = wd_ref[0].astype(BF16)
    g0 = first_ref[e]
    ff = wd_s.shape[0]

    def body(j, carry):
        g = g0 + j
        load(g).wait()

        @pl.when(g + ahead < total)
        def _():
            load(g + ahead).start()

        @pl.when(g >= n_out)
        def _():
            store(g - n_out).wait()

        x_lo, x_hi = _unpack_bf16_pair(xbuf[g % n_in])
        h = (jnp.dot(x_lo.astype(BF16), wgu_s[:half, :], preferred_element_type=F32)
             + jnp.dot(x_hi.astype(BF16), wgu_s[half:, :], preferred_element_type=F32))
        gate, up = h[:, :ff], h[:, ff:]
        act = gate * _sigmoid(gate) * up
        y = jnp.dot(act.astype(BF16), wd_s[...], preferred_element_type=F32)
        ybuf[g % n_out] = _pack_bf16_pair(y[:, :half], y[:, half:])
        store(g).start()
        return carry

    lax.fori_loop(0, nblk_ref[e], body, 0)

    @pl.when(e == pl.num_programs(0) - 1)
    def _():
        for back in range(n_out, 0, -1):
            @pl.when(total >= back)
            def _(back=back):
                store(total - back).wait()


def _experts(xs, first_blk, n_blk, total_blk, w_gu, w_down):
    n_rows, half = xs.shape
    d = 2 * half
    n_exp, _, ff2 = w_gu.shape
    ff = w_down.shape[1]
    grid_spec = pltpu.PrefetchScalarGridSpec(
        num_scalar_prefetch=3,
        grid=(n_exp,),
        in_specs=[pl.BlockSpec((1, d, ff2), lambda e, *_: (e, 0, 0)),
                  pl.BlockSpec((1, ff, d), lambda e, *_: (e, 0, 0)),
                  pl.BlockSpec(memory_space=pl.ANY)],
        out_specs=pl.BlockSpec(memory_space=pl.ANY),
        scratch_shapes=[pltpu.VMEM((d, ff2), BF16), pltpu.VMEM((ff, d), BF16),
                        pltpu.VMEM((EXPERT_IN_SLOTS, EXPERT_ROWS, half), xs.dtype),
                        pltpu.VMEM((EXPERT_OUT_SLOTS, EXPERT_ROWS, half), xs.dtype),
                        pltpu.SemaphoreType.DMA((EXPERT_IN_SLOTS,)),
                        pltpu.SemaphoreType.DMA((EXPERT_OUT_SLOTS,))],
    )
    return pl.pallas_call(
        _expert_kernel,
        grid_spec=grid_spec,
        out_shape=jax.ShapeDtypeStruct((n_rows, half), xs.dtype),
        compiler_params=pltpu.CompilerParams(
            dimension_semantics=("arbitrary",), vmem_limit_bytes=VMEM_LIMIT),
        name="experts",
    )(first_blk, n_blk, total_blk, w_gu, w_down, xs)


def _combine_kernel(x1_ref, wt_ref, yg_ref, wsgu_ref, wsd_ref, lng_ref, lnb_ref, o_ref):
    tc = x1_ref.shape[0]
    x1 = x1_ref[...]
    ff = wsd_ref.shape[0]
    h = jnp.dot(x1.astype(BF16), wsgu_ref[...], preferred_element_type=F32)
    gate, up = h[:, :ff], h[:, ff:]
    act = gate * _sigmoid(gate) * up
    moe = jnp.dot(act.astype(BF16), wsd_ref[...], preferred_element_type=F32)

    half = yg_ref.shape[2]
    r_lo = jnp.zeros((tc, half), F32)
    r_hi = jnp.zeros((tc, half), F32)
    for k in range(TOP_K):
        y_lo, y_hi = _unpack_bf16_pair(yg_ref[k])
        w = wt_ref[:, k:k + 1]
        r_lo = r_lo + w * y_lo
        r_hi = r_hi + w * y_hi
    moe = moe + jnp.concatenate([r_lo, r_hi], axis=1)
    o_ref[...] = _layer_norm(DEEPNORM_ALPHA * x1 + moe, lng_ref[...], lnb_ref[...])


def _combine(x1, wgt_t, yg, ws_gu, ws_down, ln_g, ln_b, tc):
    t, d = x1.shape
    return pl.pallas_call(
        _combine_kernel,
        grid=(t // tc,),
        in_specs=[pl.BlockSpec((tc, d), lambda i: (i, 0)),
                  pl.BlockSpec((tc, TOP_K), lambda i: (i, 0)),
                  pl.BlockSpec((TOP_K, tc, yg.shape[2]), lambda i: (0, i, 0)),
                  _resident(ws_gu.shape), _resident(ws_down.shape),
                  _resident(ln_g.shape), _resident(ln_b.shape)],
        out_specs=pl.BlockSpec((tc, d), lambda i: (i, 0)),
        out_shape=jax.ShapeDtypeStruct((t, d), F32),
        compiler_params=pltpu.CompilerParams(
            dimension_semantics=("parallel",), vmem_limit_bytes=VMEM_LIMIT),
        name="combine_ln2",
    )(x1, wgt_t, yg, ws_gu, ws_down, ln_g, ln_b)


def _fused_in_weights(w_in):
    d = w_in.shape[0]
    sizes = (SB_HEADS * HEAD_DIM,) * 3 + (SWA_HEADS * HEAD_DIM, SWA_KV_HEADS * HEAD_DIM,
                                          SWA_KV_HEADS * HEAD_DIM, MEM_HEADS * MEM_HEAD_DIM)
    parts, off = [], 0
    for s in sizes:
        parts.append(w_in[:, off:off + s])
        off += s
    q_sb, k_sb, v_sb, q_sw, k_sw, v_sw, q_m = parts
    gates = w_in[:, off:]
    scale = HEAD_DIM ** -0.5
    q_sw = (q_sw * scale).reshape(d, SWA_KV_HEADS, SWA_GROUP, HEAD_DIM)
    zeros = jnp.zeros((d, SWA_GROUP, HEAD_DIM), w_in.dtype)
    q_sw = jnp.stack([jnp.concatenate([q_sw[:, 0], zeros], axis=-1),
                      jnp.concatenate([zeros, q_sw[:, 1]], axis=-1)], axis=1)
    q_sw = q_sw.reshape(d, SWA_HEADS * LANES)
    used = W_GATE + 3 * SB_HEADS * HEAD_DIM + SWA_HEADS * LANES + q_m.shape[1] + 2 * k_sw.shape[1]
    pad = jnp.zeros((d, PROJ_COLS - used), w_in.dtype)
    return jnp.concatenate([gates, q_sw, q_sb * scale, k_sb, v_sb, q_m, k_sw, v_sw, pad],
                           axis=1).astype(BF16)


def _padded_swa_out_weights(w_o_swa):
    d = w_o_swa.shape[1]
    w = w_o_swa.reshape(SWA_KV_HEADS, SWA_GROUP, HEAD_DIM, d)
    zeros = jnp.zeros((SWA_GROUP, HEAD_DIM, d), w_o_swa.dtype)
    w = jnp.stack([jnp.concatenate([w[0], zeros], axis=1),
                   jnp.concatenate([zeros, w[1]], axis=1)], axis=0)
    return w.reshape(SWA_HEADS * LANES, d).astype(BF16)


def kernel(x, mem, w_in, b_gate, w_mem_kv, sinks, w_o_sb, w_o_swa, w_o_mem, w_out,
           ln1_g, ln1_b, w_router, router_bias, w_e_gu, w_e_down, w_s_gu, w_s_down,
           ln2_g, ln2_b):
    batch, seq, d = x.shape
    mem_len = mem.shape[1]
    t = batch * seq
    x2 = x.reshape(t, d)
    row_tile = min(512, t)

    p = _proj(x2, _fused_in_weights(w_in), b_gate.reshape(1, -1), row_tile)
    mkv = _matmul_bf16(mem.reshape(batch * mem_len, d), w_mem_kv.astype(BF16), mem_len)
    o_sb = _sb_attention(p, batch, seq, min(256, seq), SB_PAIRS_PER_STEP)
    o_sw = _swa_attention(p, _swa_tables(sinks), batch, seq)
    o_m = _mem_attention(p, mkv, batch, seq, mem_len, min(512, seq))
    x1, x1p = _merge(o_sb, o_sw, o_m, p, x2, w_o_sb.astype(BF16), _padded_swa_out_weights(w_o_swa),
                w_o_mem.astype(BF16), w_out.astype(BF16),
                ln1_g.reshape(1, d), ln1_b.reshape(1, d), row_tile)

    out = _moe_ln(x1, x1p, w_router, router_bias, w_e_gu, w_e_down, w_s_gu, w_s_down,
                  ln2_g, ln2_b)
    return out.reshape(batch, seq, d)


def _moe_ln(x1, x1p, w_router, router_bias, w_e_gu, w_e_down, w_s_gu, w_s_down, ln2_g, ln2_b):
    t, d = x1.shape
    wr_t = w_router.T
    wr_hi = wr_t.astype(BF16)
    wr_lo = (wr_t - wr_hi.astype(F32)).astype(BF16)
    route_tile = min(256, t)
    idx, rank, wgt, cnt = _router(x1, wr_hi, wr_lo, router_bias.reshape(-1, 1).astype(F32),
                                  route_tile)
    counts = cnt[:, 0].astype(I32)
    padded = (counts + EXPERT_ROWS - 1) // EXPERT_ROWS * EXPERT_ROWS
    pad_end = jnp.cumsum(padded)
    pad_start = pad_end - padded
    n_blocks = t * TOP_K // EXPERT_ROWS + N_EXPERTS
    dest = _dest(idx, rank, pad_start.reshape(-1, 1), route_tile)

    xs = _dispatch(x1p, dest, n_blocks * EXPERT_ROWS)
    ys = _experts(xs, pad_start // EXPERT_ROWS, padded // EXPERT_ROWS,
                  pad_end[-1:] // EXPERT_ROWS, w_e_gu, w_e_down)
    yg = _gather_rows(ys, dest.reshape(-1)).reshape(TOP_K, t, -1)
    return _combine(x1, wgt.T, yg, w_s_gu.astype(BF16), w_s_down.astype(BF16),
                    ln2_g.reshape(1, d), ln2_b.reshape(1, d), min(256, t))
```

```python
import functools

import jax
import jax.numpy as jnp
from jax import lax
from jax.experimental import pallas as pl
from jax.experimental.pallas import tpu as pltpu
from jax.experimental.pallas import tpu_sc as plsc

F32 = jnp.float32
BF16 = jnp.bfloat16
I32 = jnp.int32
U32 = jnp.uint32

HEAD_DIM = 64
SB_HEADS = 8
SWA_HEADS = 8
SWA_KV_HEADS = 2
SWA_GROUP = SWA_HEADS // SWA_KV_HEADS
SWA_WINDOW = 128
MEM_HEADS = 4
MEM_HEAD_DIM = 128
N_BRANCH = 3
N_EXPERTS = 256
TOP_K = 8
N_GROUPS = 8
GROUP_SIZE = N_EXPERTS // N_GROUPS
TOPK_GROUPS = 4
EXPERT_FF = 256
SHARED_FF = 256
ROUTED_SCALE = 2.5
LN_EPS = 1e-5
DEPTH = 1
DEEPNORM_ALPHA = (2 * DEPTH) ** 0.25

LANES = 128
SC_CORES = 2
SC_SUBCORES = 16
SC_WORKERS = SC_CORES * SC_SUBCORES
SC_INDEX_WINDOW = 128
SC_GATHER_ROWS = 64
VMEM_LIMIT = 56 * 1024 * 1024

D_GATE = 0
W_GATE = 3072
C_QSW = 3072
C_QSB = 4096
C_KSB = 4608
C_VSB = 5120
C_QM = 5632
C_KSW = 6144
C_VSW = 6272
PROJ_COLS = 6656
PROJ_CHUNK = 512

SB_SKIP = 110.0
SB_PAIRS_PER_STEP = 4

EXPERT_ROWS = 512
EXPERT_IN_SLOTS = 4
EXPERT_OUT_SLOTS = 2
COMBINE_CHUNKS = 4


def _nt_dot(a, b):
    return lax.dot_general(a, b, (((1,), (1,)), ((), ())), preferred_element_type=F32)


def _sigmoid(x):
    return 1.0 / (1.0 + jnp.exp(-x))


def _layer_norm(h, g, b):
    mu = jnp.mean(h, axis=-1, keepdims=True)
    d = h - mu
    var = jnp.mean(d * d, axis=-1, keepdims=True)
    return d * lax.rsqrt(var + LN_EPS) * g + b


def _pack_bf16_pair(a, b):
    a_bits = lax.bitcast_convert_type(a.astype(BF16).astype(F32), U32)
    b_bits = lax.bitcast_convert_type(b.astype(BF16).astype(F32), U32)
    return (a_bits >> 16) | b_bits


def _unpack_bf16_pair(w):
    a = lax.bitcast_convert_type(w << 16, F32)
    b = lax.bitcast_convert_type(w & jnp.uint32(0xFFFF0000), F32)
    return a, b


def _resident(shape):
    nd = len(shape)
    return pl.BlockSpec(shape, lambda *_: (0,) * nd, pipeline_mode=pl.Buffered(1))


def _proj_kernel(x_ref, w_ref, b_ref, o_ref, *, gate_cols):
    xb = x_ref[...].astype(BF16)
    for j in range(o_ref.shape[1] // PROJ_CHUNK):
        cols = slice(j * PROJ_CHUNK, (j + 1) * PROJ_CHUNK)
        acc = jnp.dot(xb, w_ref[:, cols], preferred_element_type=F32)
        if (j + 1) * PROJ_CHUNK <= gate_cols:
            acc = _sigmoid(acc + b_ref[:, cols])
        o_ref[:, cols] = acc.astype(o_ref.dtype)


def _proj(x2, w_all, b_gate, tm):
    t, d = x2.shape
    n = w_all.shape[1]
    return pl.pallas_call(
        functools.partial(_proj_kernel, gate_cols=b_gate.shape[1]),
        grid=(t // tm,),
        in_specs=[pl.BlockSpec((tm, d), lambda i: (i, 0)),
                  _resident((d, n)),
                  _resident(b_gate.shape)],
        out_specs=pl.BlockSpec((tm, n), lambda i: (i, 0)),
        out_shape=jax.ShapeDtypeStruct((t, n), BF16),
        compiler_params=pltpu.CompilerParams(
            dimension_semantics=("parallel",), vmem_limit_bytes=VMEM_LIMIT),
        name="in_proj",
    )(x2, w_all, b_gate)


def _mm_kernel(x_ref, w_ref, o_ref):
    o_ref[...] = jnp.dot(x_ref[...].astype(BF16), w_ref[...],
                         preferred_element_type=F32).astype(o_ref.dtype)


def _matmul_bf16(x2, w, tm):
    t, d = x2.shape
    n = w.shape[1]
    return pl.pallas_call(
        _mm_kernel,
        grid=(t // tm,),
        in_specs=[pl.BlockSpec((tm, d), lambda i: (i, 0)), _resident((d, n))],
        out_specs=pl.BlockSpec((tm, n), lambda i: (i, 0)),
        out_shape=jax.ShapeDtypeStruct((t, n), BF16),
        compiler_params=pltpu.CompilerParams(
            dimension_semantics=("parallel",), vmem_limit_bytes=VMEM_LIMIT),
        name="mem_kv_proj",
    )(x2, w)


def _sb_kernel(q_ref, k_ref, v_ref, o_ref, *, tq):
    i = pl.program_id(2)
    pairs = q_ref.shape[1] // LANES
    lane = lax.broadcasted_iota(I32, (1, LANES), 1)
    r = lax.broadcasted_iota(I32, (tq, tq), 0)
    c = lax.broadcasted_iota(I32, (tq, tq), 1)
    tri = (r >= c).astype(BF16)
    causal = c < r
    nh = LANES // HEAD_DIM
    hmasks = [(lane >= h * HEAD_DIM) & (lane < (h + 1) * HEAD_DIM) for h in range(nh)]
    qs = []
    for p in range(pairs):
        q = q_ref[:, p * LANES:(p + 1) * LANES]
        qs.append(jnp.concatenate([jnp.where(hm, q, jnp.zeros_like(q)) for hm in hmasks], axis=0))
    causal2 = jnp.concatenate([causal] * (nh * pairs), axis=0)
    m = pairs * nh * tq

    def block(kb, carry, acc, diag):
        rows = pl.ds(pl.multiple_of(kb * tq, tq), tq)
        z = jnp.concatenate([_nt_dot(qs[p], k_ref[rows, p * LANES:(p + 1) * LANES])
                             for p in range(pairs)], axis=0)
        sp = jnp.maximum(z, 0.0) + jnp.log(1.0 + jnp.exp(-jnp.abs(z)))
        if diag:
            sp = jnp.where(causal2, sp, 0.0)
        hi = sp.astype(BF16)
        lo = (sp - hi.astype(F32)).astype(BF16)
        s2 = jnp.dot(jnp.concatenate([hi, lo], axis=0), tri, preferred_element_type=F32)
        suffix = s2[:m] + s2[m:]
        a = jnp.exp((z - carry) - suffix)
        if diag:
            a = jnp.where(causal2, a, 0.0)
        ab = a.astype(BF16)
        mp = nh * tq
        av = jnp.concatenate(
            [jnp.dot(ab[p * mp:(p + 1) * mp], v_ref[rows, p * LANES:(p + 1) * LANES],
                     preferred_element_type=F32) for p in range(pairs)], axis=0)
        return carry + suffix[:, 0:1], acc + av

    carry, acc = block(i, jnp.zeros((m, 1), F32), jnp.zeros((m, LANES), F32), True)

    def cond(s):
        kb, carry, _ = s
        return (kb >= 0) & (jnp.min(carry) < SB_SKIP)

    def body(s):
        kb, carry, acc = s
        carry, acc = block(kb, carry, acc, False)
        return kb - 1, carry, acc

    _, _, acc = lax.while_loop(cond, body, (i - 1, carry, acc))
    for p in range(pairs):
        lo_rows = acc[(p * nh) * tq:(p * nh + 1) * tq]
        hi_rows = acc[(p * nh + 1) * tq:(p * nh + 2) * tq]
        o_ref[:, p * LANES:(p + 1) * LANES] = jnp.where(hmasks[0], lo_rows, hi_rows).astype(o_ref.dtype)


def _sb_attention(p, batch, seq, tq, pairs):
    t = batch * seq
    nq = seq // tq
    w = pairs * LANES
    ngrp = SB_HEADS * HEAD_DIM // w
    qc, kc, vc = C_QSB // w, C_KSB // w, C_VSB // w
    return pl.pallas_call(
        functools.partial(_sb_kernel, tq=tq),
        grid=(batch, ngrp, nq),
        in_specs=[pl.BlockSpec((tq, w), lambda b, h, i: (b * nq + i, qc + h)),
                  pl.BlockSpec((seq, w), lambda b, h, i: (b, kc + h)),
                  pl.BlockSpec((seq, w), lambda b, h, i: (b, vc + h))],
        out_specs=pl.BlockSpec((tq, w), lambda b, h, i: (b * nq + i, h)),
        out_shape=jax.ShapeDtypeStruct((t, SB_HEADS * HEAD_DIM), BF16),
        compiler_params=pltpu.CompilerParams(
            dimension_semantics=("parallel", "parallel", "arbitrary"),
            vmem_limit_bytes=VMEM_LIMIT),
        name="sb_attention",
    )(p, p, p)


def _swa_kernel(hp_ref, q_ref, kp_ref, kc_ref, vp_ref, vc_ref, o_ref):
    n = pl.program_id(1)
    blk = q_ref.shape[0]
    nheads = q_ref.shape[1] // LANES
    sink = hp_ref[2]
    qs = jnp.concatenate([q_ref[:, g * LANES:(g + 1) * LANES] for g in range(nheads)], axis=0)
    neg = jnp.float32(-jnp.inf)
    zp = jnp.where(n > 0, _nt_dot(qs, kp_ref[...]) + hp_ref[0], neg)
    zc = _nt_dot(qs, kc_ref[...]) + hp_ref[1]
    m = jnp.maximum(jnp.max(jnp.maximum(zp, zc), axis=1, keepdims=True), sink)
    pp = jnp.exp(zp - m)
    pc = jnp.exp(zc - m)
    den = jnp.sum(pp + pc, axis=1, keepdims=True) + jnp.exp(sink - m)
    o = (jnp.dot(pp.astype(BF16), vp_ref[...], preferred_element_type=F32)
         + jnp.dot(pc.astype(BF16), vc_ref[...], preferred_element_type=F32)) / den
    lane = lax.broadcasted_iota(I32, (1, LANES), 1)
    for g in range(nheads):
        kv = g // SWA_GROUP
        kvmask = (lane >= kv * HEAD_DIM) & (lane < (kv + 1) * HEAD_DIM)
        o_ref[:, g * LANES:(g + 1) * LANES] = jnp.where(
            kvmask, o[g * blk:(g + 1) * blk], 0.0).astype(o_ref.dtype)


def _swa_tables(sinks):
    w = SWA_WINDOW
    slopes = jnp.exp2(-8.0 * jnp.arange(1, SWA_HEADS + 1, dtype=F32) / SWA_HEADS)[:, None, None]
    r = jnp.arange(w)[:, None]
    c = jnp.arange(w)[None, :]
    dist = (r - c).astype(F32)[None]
    neg = jnp.float32(-jnp.inf)
    bias_c = jnp.where((c <= r)[None], -slopes * dist, neg)
    bias_p = jnp.where((c > r)[None], -slopes * (dist + w), neg)
    sink = jnp.broadcast_to(sinks.astype(F32)[:, None, None], (SWA_HEADS, w, w))
    return jnp.stack([bias_p, bias_c, sink]).reshape(3, SWA_HEADS * w, w)


def _swa_attention(p, row_params, batch, seq):
    blk = SWA_WINDOW
    t = batch * seq
    nb = seq // blk
    qw = SWA_HEADS * LANES
    qc, kc, vc = C_QSW // qw, C_KSW // LANES, C_VSW // LANES
    cur = lambda col: (lambda b, n: (b * nb + n, col))
    prev = lambda col: (lambda b, n: (b * nb + jnp.maximum(n - 1, 0), col))
    return pl.pallas_call(
        _swa_kernel,
        grid=(batch, nb),
        in_specs=[_resident(row_params.shape),
                  pl.BlockSpec((blk, qw), lambda b, n: (b * nb + n, qc)),
                  pl.BlockSpec((blk, LANES), prev(kc)),
                  pl.BlockSpec((blk, LANES), cur(kc)),
                  pl.BlockSpec((blk, LANES), prev(vc)),
                  pl.BlockSpec((blk, LANES), cur(vc))],
        out_specs=pl.BlockSpec((blk, qw), lambda b, n: (b * nb + n, 0)),
        out_shape=jax.ShapeDtypeStruct((t, qw), BF16),
        compiler_params=pltpu.CompilerParams(
            dimension_semantics=("parallel", "arbitrary"),
            vmem_limit_bytes=VMEM_LIMIT),
        name="swa_attention",
    )(row_params, p, p, p, p, p)


def _mem_kernel(q_ref, mk_ref, mv_ref, o_ref):
    scale = MEM_HEAD_DIM ** -0.5
    for h in range(MEM_HEADS):
        cols = slice(h * MEM_HEAD_DIM, (h + 1) * MEM_HEAD_DIM)
        z = _nt_dot(q_ref[:, cols], mk_ref[:, cols]) * scale
        m = jnp.max(z, axis=1, keepdims=True)
        p = jnp.exp(z - m)
        den = jnp.sum(p, axis=1, keepdims=True)
        o = jnp.dot(p.astype(BF16), mv_ref[:, cols], preferred_element_type=F32) / den
        o_ref[:, cols] = o.astype(o_ref.dtype)


def _mem_attention(p, mkv, batch, seq, mem_len, tq):
    t = batch * seq
    nq = seq // tq
    w = MEM_HEADS * MEM_HEAD_DIM
    return pl.pallas_call(
        _mem_kernel,
        grid=(batch, nq),
        in_specs=[pl.BlockSpec((tq, w), lambda b, i: (b * nq + i, C_QM // w)),
                  pl.BlockSpec((mem_len, w), lambda b, i: (b, 0)),
                  pl.BlockSpec((mem_len, w), lambda b, i: (b, 1))],
        out_specs=pl.BlockSpec((tq, w), lambda b, i: (b * nq + i, 0)),
        out_shape=jax.ShapeDtypeStruct((t, w), BF16),
        compiler_params=pltpu.CompilerParams(
            dimension_semantics=("parallel", "arbitrary"), vmem_limit_bytes=VMEM_LIMIT),
        name="mem_attention",
    )(p, mkv, mkv)


def _merge_kernel(osb_ref, osw_ref, om_ref, g_ref, x_ref, wsb_ref, wsw_ref, wm_ref,
                  wout_ref, lng_ref, lnb_ref, x1_ref, x1p_ref):
    d = x_ref.shape[1]
    merged = g_ref[:, 0:d].astype(F32) * jnp.dot(
        osb_ref[...], wsb_ref[...], preferred_element_type=F32)
    merged += g_ref[:, d:2 * d].astype(F32) * jnp.dot(
        osw_ref[...], wsw_ref[...], preferred_element_type=F32)
    merged += g_ref[:, 2 * d:3 * d].astype(F32) * jnp.dot(
        om_ref[...], wm_ref[...], preferred_element_type=F32)
    y = jnp.dot(merged.astype(BF16), wout_ref[...], preferred_element_type=F32)
    x1 = _layer_norm(DEEPNORM_ALPHA * x_ref[...] + y, lng_ref[...], lnb_ref[...])
    x1_ref[...] = x1
    x1p_ref[...] = _pack_bf16_pair(x1[:, :d // 2], x1[:, d // 2:])


def _merge(o_sb, o_sw, o_m, p, x2, w_sb, w_sw, w_m, w_out, ln_g, ln_b, tm):
    t, d = x2.shape
    row = lambda w: pl.BlockSpec((tm, w), lambda i: (i, 0))
    return pl.pallas_call(
        _merge_kernel,
        grid=(t // tm,),
        in_specs=[row(o_sb.shape[1]), row(o_sw.shape[1]), row(o_m.shape[1]),
                  pl.BlockSpec((tm, N_BRANCH * d), lambda i: (i, 0)),
                  row(d),
                  _resident(w_sb.shape), _resident(w_sw.shape), _resident(w_m.shape),
                  _resident(w_out.shape), _resident(ln_g.shape), _resident(ln_b.shape)],
        out_specs=[row(d), row(d // 2)],
        out_shape=[jax.ShapeDtypeStruct((t, d), F32), jax.ShapeDtypeStruct((t, d // 2), U32)],
        compiler_params=pltpu.CompilerParams(
            dimension_semantics=("parallel",), vmem_limit_bytes=VMEM_LIMIT),
        name="merge_ln1",
    )(o_sb, o_sw, o_m, p, x2, w_sb, w_sw, w_m, w_out, ln_g, ln_b)


def _router_kernel(x_ref, wh_ref, wl_ref, bias_ref, idx_ref, rank_ref, wgt_ref, cnt_ref,
                   carry_ref):
    step = pl.program_id(0)
    tr = x_ref.shape[0]

    @pl.when(step == 0)
    def _():
        carry_ref[...] = jnp.zeros_like(carry_ref)

    x = x_ref[...]
    xh = x.astype(BF16)
    xl = (x - xh.astype(F32)).astype(BF16)
    logits = _nt_dot(wh_ref[...], xh) + _nt_dot(wh_ref[...], xl) + _nt_dot(wl_ref[...], xh)
    scores = _sigmoid(logits)
    biased = scores + bias_ref[...]
    neg = jnp.float32(-jnp.inf)

    sub = lax.broadcasted_iota(I32, (GROUP_SIZE, tr), 0)
    gscore = []
    for g in range(N_GROUPS):
        blk = biased[g * GROUP_SIZE:(g + 1) * GROUP_SIZE, :]
        m1 = jnp.max(blk, axis=0, keepdims=True)
        i1 = jnp.min(jnp.where(blk == m1, sub, GROUP_SIZE), axis=0, keepdims=True)
        m2 = jnp.max(jnp.where(sub == i1, neg, blk), axis=0, keepdims=True)
        gscore.append(m1 + m2)
    gs = jnp.concatenate(gscore, axis=0)

    giota = lax.broadcasted_iota(I32, (N_GROUPS, tr), 0)
    gsel = jnp.zeros((N_GROUPS, tr), F32)
    for _ in range(TOPK_GROUPS):
        m = jnp.max(gs, axis=0, keepdims=True)
        gi = jnp.min(jnp.where(gs == m, giota, N_GROUPS), axis=0, keepdims=True)
        hit = giota == gi
        gsel = jnp.where(hit, 1.0, gsel)
        gs = jnp.where(hit, neg, gs)

    masked = jnp.concatenate(
        [jnp.where(gsel[g:g + 1, :] > 0.0, biased[g * GROUP_SIZE:(g + 1) * GROUP_SIZE, :], neg)
         for g in range(N_GROUPS)], axis=0)

    eiota = lax.broadcasted_iota(I32, (N_EXPERTS, tr), 0)
    sel = jnp.zeros((N_EXPERTS, tr), F32)
    idx_rows, w_rows = [], []
    for _ in range(TOP_K):
        m = jnp.max(masked, axis=0, keepdims=True)
        ei = jnp.min(jnp.where(masked == m, eiota, N_EXPERTS), axis=0, keepdims=True)
        hit = eiota == ei
        idx_rows.append(ei)
        w_rows.append(jnp.sum(jnp.where(hit, scores, 0.0), axis=0, keepdims=True))
        sel = jnp.where(hit, 1.0, sel)
        masked = jnp.where(hit, neg, masked)

    wsum = w_rows[0]
    for wk in w_rows[1:]:
        wsum = wsum + wk
    wgt_ref[...] = jnp.concatenate(w_rows, axis=0) / wsum * ROUTED_SCALE
    idx_ref[...] = jnp.concatenate(idx_rows, axis=0)

    a = lax.broadcasted_iota(I32, (tr, tr), 0)
    b = lax.broadcasted_iota(I32, (tr, tr), 1)
    before = (a < b).astype(BF16)
    rank = jnp.dot(sel.astype(BF16), before, preferred_element_type=F32) + carry_ref[...]
    rank_rows = [jnp.sum(jnp.where(eiota == ei, rank, 0.0), axis=0, keepdims=True)
                 for ei in idx_rows]
    rank_ref[...] = jnp.concatenate(rank_rows, axis=0).astype(I32)
    carry_ref[...] = carry_ref[...] + jnp.sum(sel, axis=1, keepdims=True)
    cnt_ref[...] = carry_ref[...]


def _router(x1, wr_hi, wr_lo, bias_col, tr):
    t, d = x1.shape
    slot = pl.BlockSpec((TOP_K, tr), lambda i: (0, i))
    return pl.pallas_call(
        _router_kernel,
        grid=(t // tr,),
        in_specs=[pl.BlockSpec((tr, d), lambda i: (i, 0)),
                  _resident(wr_hi.shape), _resident(wr_lo.shape), _resident(bias_col.shape)],
        out_specs=[slot, slot, slot, pl.BlockSpec((N_EXPERTS, 1), lambda i: (0, 0))],
        out_shape=[jax.ShapeDtypeStruct((TOP_K, t), I32),
                   jax.ShapeDtypeStruct((TOP_K, t), I32),
                   jax.ShapeDtypeStruct((TOP_K, t), F32),
                   jax.ShapeDtypeStruct((N_EXPERTS, 1), F32)],
        scratch_shapes=[pltpu.VMEM((N_EXPERTS, 1), F32)],
        compiler_params=pltpu.CompilerParams(
            dimension_semantics=("arbitrary",), vmem_limit_bytes=VMEM_LIMIT),
        name="router",
    )(x1, wr_hi, wr_lo, bias_col)


def _dest_kernel(idx_ref, rank_ref, start_ref, dest_ref):
    tr = idx_ref.shape[1]
    eiota = lax.broadcasted_iota(I32, (N_EXPERTS, tr), 0)
    rows = []
    for k in range(TOP_K):
        hit = eiota == idx_ref[k:k + 1, :]
        rows.append(jnp.sum(jnp.where(hit, start_ref[...], 0), axis=0, keepdims=True))
    dest_ref[...] = jnp.concatenate(rows, axis=0) + rank_ref[...]


def _dest(idx, rank, start_col, tr):
    t = idx.shape[1]
    slot = pl.BlockSpec((TOP_K, tr), lambda i: (0, i))
    return pl.pallas_call(
        _dest_kernel,
        grid=(t // tr,),
        in_specs=[slot, slot, _resident(start_col.shape)],
        out_specs=slot,
        out_shape=jax.ShapeDtypeStruct((TOP_K, t), I32),
        compiler_params=pltpu.CompilerParams(dimension_semantics=("parallel",)),
        name="slot_dest",
    )(idx, rank, start_col)


def _sc_worker_id():
    return lax.axis_index("s") * SC_CORES + lax.axis_index("c")


def _dispatch(x1p, dest, n_rows):
    t, w = x1p.shape
    per = t // SC_WORKERS
    win = min(SC_INDEX_WINDOW, per)
    mesh = plsc.VectorSubcoreMesh(core_axis_name="c", subcore_axis_name="s")

    @functools.partial(
        pl.kernel, mesh=mesh,
        out_type=jax.ShapeDtypeStruct((n_rows, w), x1p.dtype),
        scratch_types=[pltpu.VMEM((TOP_K, win), I32),
                       pltpu.VMEM((win, w), x1p.dtype),
                       pltpu.SemaphoreType.DMA],
        name="sc_dispatch",
    )
    def scatter_rows(x_hbm, dest_hbm, xs_hbm, idx_v, rows_v, sem):
        base = _sc_worker_id() * per

        @pl.loop(0, per // win)
        def _(j):
            t0 = pl.multiple_of(base + j * win, win)
            pltpu.sync_copy(dest_hbm.at[:, pl.ds(t0, win)], idx_v)
            pltpu.sync_copy(x_hbm.at[pl.ds(t0, win)], rows_v)
            copies = [pltpu.async_copy(rows_v, xs_hbm.at[idx_v.at[k]], sem) for k in range(TOP_K)]
            for c in copies:
                c.wait()

    return scatter_rows(x1p, dest)


def _gather_rows(table, idx):
    n = idx.shape[0]
    w = table.shape[1]
    per = n // SC_WORKERS
    chunk = min(SC_GATHER_ROWS, per // 2)
    assert n % SC_WORKERS == 0 and per % (2 * chunk) == 0, (n, chunk)
    mesh = plsc.VectorSubcoreMesh(core_axis_name="c", subcore_axis_name="s")

    @functools.partial(
        pl.kernel, mesh=mesh,
        out_type=jax.ShapeDtypeStruct((n, w), table.dtype),
        scratch_types=[pltpu.VMEM((per,), I32),
                       pltpu.VMEM((2, chunk, w), table.dtype),
                       pltpu.SemaphoreType.DMA((2,)),
                       pltpu.SemaphoreType.DMA((2,))],
        name="sc_gather",
    )
    def gather_rows(table_hbm, idx_hbm, out_hbm, idx_v, rows_v, gather_sem, put_sem):
        base = _sc_worker_id() * per
        nchunks = per // chunk
        pltpu.sync_copy(idx_hbm.at[pl.ds(base, per)], idx_v)

        def gather(j, b):
            off = pl.multiple_of(j * chunk, chunk)
            return pltpu.make_async_copy(table_hbm.at[idx_v.at[pl.ds(off, chunk)]],
                                         rows_v.at[b], gather_sem.at[b])

        def put(j, b):
            off = pl.multiple_of(j * chunk, chunk)
            return pltpu.make_async_copy(rows_v.at[b], out_hbm.at[pl.ds(base + off, chunk)],
                                         put_sem.at[b])

        gather(0, 0).start()

        @pl.loop(0, nchunks, step=2)
        def _(j):
            for b in (0, 1):
                jj = j + b

                @pl.when(jj + 1 < nchunks)
                def _():
                    @pl.when(jj >= 1)
                    def _():
                        put(jj - 1, 1 - b).wait()
                    gather(jj + 1, 1 - b).start()

                gather(jj, b).wait()
                put(jj, b).start()

        put(nchunks - 2, 0).wait()
        put(nchunks - 1, 1).wait()

    return gather_rows(table, idx)


def _expert_kernel(first_ref, nblk_ref, total_ref, wgu_ref, wd_ref, xs_hbm, ys_hbm,
                   wgu_s, wd_s, xbuf, ybuf, in_sem, out_sem):
    e = pl.program_id(0)
    total = total_ref[0]
    n_in, rows, half = xbuf.shape
    n_out = ybuf.shape[0]
    ahead = n_in - 1

    def block_rows(g):
        return pl.ds(pl.multiple_of(g * rows, rows), rows)

    def load(g):
        slot = g % n_in
        return pltpu.make_async_copy(xs_hbm.at[block_rows(g), :], xbuf.at[slot], in_sem.at[slot])

    def store(g):
        slot = g % n_out
        return pltpu.make_async_copy(ybuf.at[slot], ys_hbm.at[block_rows(g), :], out_sem.at[slot])

    @pl.when(e == 0)
    def _():
        for g in range(ahead):
            @pl.when(g < total)
            def _(g=g):
                load(g).start()

    wgu_s[...] = wgu_ref[0].astype(BF16)
    wd_s[...] = wd_ref[0].astype(BF16)
    g0 = first_ref[e]
    ff = wd_s.shape[0]

    def body(j, carry):
        g = g0 + j
        load(g).wait()

        @pl.when(g + ahead < total)
        def _():
            load(g + ahead).start()

        @pl.when(g >= n_out)
        def _():
            store(g - n_out).wait()

        x_lo, x_hi = _unpack_bf16_pair(xbuf[g % n_in])
        h = (jnp.dot(x_lo.astype(BF16), wgu_s[:half, :], preferred_element_type=F32)
             + jnp.dot(x_hi.astype(BF16), wgu_s[half:, :], preferred_element_type=F32))
        gate, up = h[:, :ff], h[:, ff:]
        act = gate * _sigmoid(gate) * up
        y = jnp.dot(act.astype(BF16), wd_s[...], preferred_element_type=F32)
        ybuf[g % n_out] = _pack_bf16_pair(y[:, :half], y[:, half:])
        store(g).start()
        return carry

    lax.fori_loop(0, nblk_ref[e], body, 0)

    @pl.when(e == pl.num_programs(0) - 1)
    def _():
        for back in range(n_out, 0, -1):
            @pl.when(total >= back)
            def _(back=back):
                store(total - back).wait()


def _experts(xs, first_blk, n_blk, total_blk, w_gu, w_down):
    n_rows, half = xs.shape
    d = 2 * half
    n_exp, _, ff2 = w_gu.shape
    ff = w_down.shape[1]
    grid_spec = pltpu.PrefetchScalarGridSpec(
        num_scalar_prefetch=3,
        grid=(n_exp,),
        in_specs=[pl.BlockSpec((1, d, ff2), lambda e, *_: (e, 0, 0)),
                  pl.BlockSpec((1, ff, d), lambda e, *_: (e, 0, 0)),
                  pl.BlockSpec(memory_space=pl.ANY)],
        out_specs=pl.BlockSpec(memory_space=pl.ANY),
        scratch_shapes=[pltpu.VMEM((d, ff2), BF16), pltpu.VMEM((ff, d), BF16),
                        pltpu.VMEM((EXPERT_IN_SLOTS, EXPERT_ROWS, half), xs.dtype),
                        pltpu.VMEM((EXPERT_OUT_SLOTS, EXPERT_ROWS, half), xs.dtype),
                        pltpu.SemaphoreType.DMA((EXPERT_IN_SLOTS,)),
                        pltpu.SemaphoreType.DMA((EXPERT_OUT_SLOTS,))],
    )
    return pl.pallas_call(
        _expert_kernel,
        grid_spec=grid_spec,
        out_shape=jax.ShapeDtypeStruct((n_rows, half), xs.dtype),
        compiler_params=pltpu.CompilerParams(
            dimension_semantics=("arbitrary",), vmem_limit_bytes=VMEM_LIMIT),
        name="experts",
    )(first_blk, n_blk, total_blk, w_gu, w_down, xs)


def _combine_kernel(x1_ref, wt_ref, yg_ref, wsgu_ref, wsd_ref, lng_ref, lnb_ref, o_ref):
    tc = x1_ref.shape[0]
    x1 = x1_ref[...]
    ff = wsd_ref.shape[0]
    h = jnp.dot(x1.astype(BF16), wsgu_ref[...], preferred_element_type=F32)
    gate, up = h[:, :ff], h[:, ff:]
    act = gate * _sigmoid(gate) * up
    moe = jnp.dot(act.astype(BF16), wsd_ref[...], preferred_element_type=F32)

    half = yg_ref.shape[2]
    r_lo = jnp.zeros((tc, half), F32)
    r_hi = jnp.zeros((tc, half), F32)
    for k in range(TOP_K):
        y_lo, y_hi = _unpack_bf16_pair(yg_ref[k])
        w = wt_ref[:, k:k + 1]
        r_lo = r_lo + w * y_lo
        r_hi = r_hi + w * y_hi
    moe = moe + jnp.concatenate([r_lo, r_hi], axis=1)
    o_ref[...] = _layer_norm(DEEPNORM_ALPHA * x1 + moe, lng_ref[...], lnb_ref[...])


def _combine(x1, wgt_t, yg, chunk, ws_gu, ws_down, ln_g, ln_b, tc):
    t, d = x1.shape
    steps = yg.shape[1] // tc
    tok = lambda i: (chunk * steps + i, 0)
    return pl.pallas_call(
        _combine_kernel,
        grid=(steps,),
        in_specs=[pl.BlockSpec((tc, d), tok),
                  pl.BlockSpec((tc, TOP_K), tok),
                  pl.BlockSpec((TOP_K, tc, yg.shape[2]), lambda i: (0, i, 0)),
                  _resident(ws_gu.shape), _resident(ws_down.shape),
                  _resident(ln_g.shape), _resident(ln_b.shape)],
        out_specs=pl.BlockSpec((tc, d), tok),
        out_shape=jax.ShapeDtypeStruct((t, d), F32),
        input_output_aliases={0: 0},
        compiler_params=pltpu.CompilerParams(
            dimension_semantics=("arbitrary",), vmem_limit_bytes=VMEM_LIMIT),
        name="combine_ln2",
    )(x1, wgt_t, yg, ws_gu, ws_down, ln_g, ln_b)


def _fused_in_weights(w_in):
    d = w_in.shape[0]
    sizes = (SB_HEADS * HEAD_DIM,) * 3 + (SWA_HEADS * HEAD_DIM, SWA_KV_HEADS * HEAD_DIM,
                                          SWA_KV_HEADS * HEAD_DIM, MEM_HEADS * MEM_HEAD_DIM)
    parts, off = [], 0
    for s in sizes:
        parts.append(w_in[:, off:off + s])
        off += s
    q_sb, k_sb, v_sb, q_sw, k_sw, v_sw, q_m = parts
    gates = w_in[:, off:]
    scale = HEAD_DIM ** -0.5
    q_sw = (q_sw * scale).reshape(d, SWA_KV_HEADS, SWA_GROUP, HEAD_DIM)
    zeros = jnp.zeros((d, SWA_GROUP, HEAD_DIM), w_in.dtype)
    q_sw = jnp.stack([jnp.concatenate([q_sw[:, 0], zeros], axis=-1),
                      jnp.concatenate([zeros, q_sw[:, 1]], axis=-1)], axis=1)
    q_sw = q_sw.reshape(d, SWA_HEADS * LANES)
    used = W_GATE + 3 * SB_HEADS * HEAD_DIM + SWA_HEADS * LANES + q_m.shape[1] + 2 * k_sw.shape[1]
    pad = jnp.zeros((d, PROJ_COLS - used), w_in.dtype)
    return jnp.concatenate([gates, q_sw, q_sb * scale, k_sb, v_sb, q_m, k_sw, v_sw, pad],
                           axis=1).astype(BF16)


def _padded_swa_out_weights(w_o_swa):
    d = w_o_swa.shape[1]
    w = w_o_swa.reshape(SWA_KV_HEADS, SWA_GROUP, HEAD_DIM, d)
    zeros = jnp.zeros((SWA_GROUP, HEAD_DIM, d), w_o_swa.dtype)
    w = jnp.stack([jnp.concatenate([w[0], zeros], axis=1),
                   jnp.concatenate([zeros, w[1]], axis=1)], axis=0)
    return w.reshape(SWA_HEADS * LANES, d).astype(BF16)


def kernel(x, mem, w_in, b_gate, w_mem_kv, sinks, w_o_sb, w_o_swa, w_o_mem, w_out,
           ln1_g, ln1_b, w_router, router_bias, w_e_gu, w_e_down, w_s_gu, w_s_down,
           ln2_g, ln2_b):
    batch, seq, d = x.shape
    mem_len = mem.shape[1]
    t = batch * seq
    x2 = x.reshape(t, d)
    row_tile = min(512, t)

    p = _proj(x2, _fused_in_weights(w_in), b_gate.reshape(1, -1), row_tile)
    mkv = _matmul_bf16(mem.reshape(batch * mem_len, d), w_mem_kv.astype(BF16), mem_len)
    o_sb = _sb_attention(p, batch, seq, min(256, seq), SB_PAIRS_PER_STEP)
    o_sw = _swa_attention(p, _swa_tables(sinks), batch, seq)
    o_m = _mem_attention(p, mkv, batch, seq, mem_len, min(512, seq))
    x1, x1p = _merge(o_sb, o_sw, o_m, p, x2, w_o_sb.astype(BF16), _padded_swa_out_weights(w_o_swa),
                w_o_mem.astype(BF16), w_out.astype(BF16),
                ln1_g.reshape(1, d), ln1_b.reshape(1, d), row_tile)

    out = _moe_ln(x1, x1p, w_router, router_bias, w_e_gu, w_e_down, w_s_gu, w_s_down,
                  ln2_g, ln2_b)
    return out.reshape(batch, seq, d)


def _moe_ln(x1, x1p, w_router, router_bias, w_e_gu, w_e_down, w_s_gu, w_s_down, ln2_g, ln2_b):
    t, d = x1.shape
    wr_t = w_router.T
    wr_hi = wr_t.astype(BF16)
    wr_lo = (wr_t - wr_hi.astype(F32)).astype(BF16)
    route_tile = min(256, t)
    idx, rank, wgt, cnt = _router(x1, wr_hi, wr_lo, router_bias.reshape(-1, 1).astype(F32),
                                  route_tile)
    counts = cnt[:, 0].astype(I32)
    padded = (counts + EXPERT_ROWS - 1) // EXPERT_ROWS * EXPERT_ROWS
    pad_end = jnp.cumsum(padded)
    pad_start = pad_end - padded
    n_blocks = t * TOP_K // EXPERT_ROWS + N_EXPERTS
    dest = _dest(idx, rank, pad_start.reshape(-1, 1), route_tile)

    xs = _dispatch(x1p, dest, n_blocks * EXPERT_ROWS)
    ys = _experts(xs, pad_start // EXPERT_ROWS, padded // EXPERT_ROWS,
                  pad_end[-1:] // EXPERT_ROWS, w_e_gu, w_e_down)
    tchunk = t // COMBINE_CHUNKS
    wgt_t = wgt.T
    ws_gu, ws_down = w_s_gu.astype(BF16), w_s_down.astype(BF16)
    out = x1
    for c in range(COMBINE_CHUNKS):
        slots = dest[:, c * tchunk:(c + 1) * tchunk].reshape(-1)
        yg = _gather_rows(ys, slots).reshape(TOP_K, tchunk, -1)
        out = _combine(out, wgt_t, yg, c, ws_gu, ws_down, ln2_g.reshape(1, d), ln2_b.reshape(1, d),
                       min(256, tchunk))
    return out
```

```python
import functools

import jax
import jax.numpy as jnp
from jax import lax
from jax.experimental import pallas as pl
from jax.experimental.pallas import tpu as pltpu
from jax.experimental.pallas import tpu_sc as plsc

F32 = jnp.float32
BF16 = jnp.bfloat16
I32 = jnp.int32
U32 = jnp.uint32

HEAD_DIM = 64
SB_HEADS = 8
SWA_HEADS = 8
SWA_KV_HEADS = 2
SWA_GROUP = SWA_HEADS // SWA_KV_HEADS
SWA_WINDOW = 128
MEM_HEADS = 4
MEM_HEAD_DIM = 128
N_BRANCH = 3
N_EXPERTS = 256
TOP_K = 8
N_GROUPS = 8
GROUP_SIZE = N_EXPERTS // N_GROUPS
TOPK_GROUPS = 4
EXPERT_FF = 256
SHARED_FF = 256
ROUTED_SCALE = 2.5
LN_EPS = 1e-5
DEPTH = 1
DEEPNORM_ALPHA = (2 * DEPTH) ** 0.25

LANES = 128
SC_CORES = 2
SC_SUBCORES = 16
SC_WORKERS = SC_CORES * SC_SUBCORES
SC_INDEX_WINDOW = 128
SC_GATHER_ROWS = 64
VMEM_LIMIT = 56 * 1024 * 1024

D_GATE = 0
W_GATE = 3072
C_QSW = 3072
C_QSB = 4096
C_KSB = 4608
C_VSB = 5120
C_QM = 5632
C_KSW = 6144
C_VSW = 6272
PROJ_COLS = 6400
PROJ_CHUNK = 256

SB_SKIP = 110.0
SB_PAIRS_PER_STEP = 4

EXPERT_ROWS = 512
EXPERT_IN_SLOTS = 4
EXPERT_OUT_SLOTS = 2
COMBINE_CHUNKS = 4


def _nt_dot(a, b):
    return lax.dot_general(a, b, (((1,), (1,)), ((), ())), preferred_element_type=F32)


def _sigmoid(x):
    return 1.0 / (1.0 + jnp.exp(-x))


def _layer_norm(h, g, b):
    mu = jnp.mean(h, axis=-1, keepdims=True)
    d = h - mu
    var = jnp.mean(d * d, axis=-1, keepdims=True)
    return d * lax.rsqrt(var + LN_EPS) * g + b


def _pack_bf16_pair(a, b):
    a_bits = lax.bitcast_convert_type(a.astype(BF16).astype(F32), U32)
    b_bits = lax.bitcast_convert_type(b.astype(BF16).astype(F32), U32)
    return (a_bits >> 16) | b_bits


def _unpack_bf16_pair(w):
    a = lax.bitcast_convert_type(w << 16, F32)
    b = lax.bitcast_convert_type(w & jnp.uint32(0xFFFF0000), F32)
    return a, b


def _resident(shape):
    nd = len(shape)
    return pl.BlockSpec(shape, lambda *_: (0,) * nd, pipeline_mode=pl.Buffered(1))


def _proj_kernel(x_ref, w_ref, b_ref, o_ref, *, gate_cols):
    xb = x_ref[...].astype(BF16)
    for j in range(o_ref.shape[1] // PROJ_CHUNK):
        cols = slice(j * PROJ_CHUNK, (j + 1) * PROJ_CHUNK)
        acc = jnp.dot(xb, w_ref[:, cols], preferred_element_type=F32)
        if (j + 1) * PROJ_CHUNK <= gate_cols:
            acc = _sigmoid(acc + b_ref[:, cols])
        o_ref[:, cols] = acc.astype(o_ref.dtype)


def _proj(x2, w_all, b_gate, tm):
    t, d = x2.shape
    n = w_all.shape[1]
    return pl.pallas_call(
        functools.partial(_proj_kernel, gate_cols=b_gate.shape[1]),
        grid=(t // tm,),
        in_specs=[pl.BlockSpec((tm, d), lambda i: (i, 0)),
                  _resident((d, n)),
                  _resident(b_gate.shape)],
        out_specs=pl.BlockSpec((tm, n), lambda i: (i, 0)),
        out_shape=jax.ShapeDtypeStruct((t, n), BF16),
        compiler_params=pltpu.CompilerParams(
            dimension_semantics=("parallel",), vmem_limit_bytes=VMEM_LIMIT),
        name="in_proj",
    )(x2, w_all, b_gate)


def _mm_kernel(x_ref, w_ref, o_ref):
    o_ref[...] = jnp.dot(x_ref[...].astype(BF16), w_ref[...],
                         preferred_element_type=F32).astype(o_ref.dtype)


def _matmul_bf16(x2, w, tm):
    t, d = x2.shape
    n = w.shape[1]
    return pl.pallas_call(
        _mm_kernel,
        grid=(t // tm,),
        in_specs=[pl.BlockSpec((tm, d), lambda i: (i, 0)), _resident((d, n))],
        out_specs=pl.BlockSpec((tm, n), lambda i: (i, 0)),
        out_shape=jax.ShapeDtypeStruct((t, n), BF16),
        compiler_params=pltpu.CompilerParams(
            dimension_semantics=("parallel",), vmem_limit_bytes=VMEM_LIMIT),
        name="mem_kv_proj",
    )(x2, w)


def _sb_kernel(q_ref, k_ref, v_ref, o_ref, *, tq):
    i = pl.program_id(2)
    pairs = q_ref.shape[1] // LANES
    lane = lax.broadcasted_iota(I32, (1, LANES), 1)
    r = lax.broadcasted_iota(I32, (tq, tq), 0)
    c = lax.broadcasted_iota(I32, (tq, tq), 1)
    tri = (r >= c).astype(BF16)
    causal = c < r
    nh = LANES // HEAD_DIM
    hmasks = [(lane >= h * HEAD_DIM) & (lane < (h + 1) * HEAD_DIM) for h in range(nh)]
    qs = []
    for p in range(pairs):
        q = q_ref[:, p * LANES:(p + 1) * LANES]
        qs.append(jnp.concatenate([jnp.where(hm, q, jnp.zeros_like(q)) for hm in hmasks], axis=0))
    causal2 = jnp.concatenate([causal] * (nh * pairs), axis=0)
    tri2 = jnp.concatenate([tri, tri], axis=0)
    m = pairs * nh * tq

    def block(kb, carry, acc, diag):
        rows = pl.ds(pl.multiple_of(kb * tq, tq), tq)
        z = jnp.concatenate([_nt_dot(qs[p], k_ref[rows, p * LANES:(p + 1) * LANES])
                             for p in range(pairs)], axis=0)
        sp = jnp.maximum(z, 0.0) + jnp.log(1.0 + jnp.exp(-jnp.abs(z)))
        if diag:
            sp = jnp.where(causal2, sp, 0.0)
        hi = sp.astype(BF16)
        lo = (sp - hi.astype(F32)).astype(BF16)
        suffix = jnp.dot(jnp.concatenate([hi, lo], axis=1), tri2,
                         preferred_element_type=F32)
        a = jnp.exp((z - carry) - suffix)
        if diag:
            a = jnp.where(causal2, a, 0.0)
        ab = a.astype(BF16)
        mp = nh * tq
        av = jnp.concatenate(
            [jnp.dot(ab[p * mp:(p + 1) * mp], v_ref[rows, p * LANES:(p + 1) * LANES],
                     preferred_element_type=F32) for p in range(pairs)], axis=0)
        return carry + suffix[:, 0:1], acc + av

    carry, acc = block(i, jnp.zeros((m, 1), F32), jnp.zeros((m, LANES), F32), True)

    def cond(s):
        kb, carry, _ = s
        return (kb >= 0) & (jnp.min(carry) < SB_SKIP)

    def body(s):
        kb, carry, acc = s
        carry, acc = block(kb, carry, acc, False)
        return kb - 1, carry, acc

    _, _, acc = lax.while_loop(cond, body, (i - 1, carry, acc))
    for p in range(pairs):
        lo_rows = acc[(p * nh) * tq:(p * nh + 1) * tq]
        hi_rows = acc[(p * nh + 1) * tq:(p * nh + 2) * tq]
        o_ref[:, p * LANES:(p + 1) * LANES] = jnp.where(hmasks[0], lo_rows, hi_rows).astype(o_ref.dtype)


def _sb_attention(p, batch, seq, tq, pairs):
    t = batch * seq
    nq = seq // tq
    w = pairs * LANES
    ngrp = SB_HEADS * HEAD_DIM // w
    qc, kc, vc = C_QSB // w, C_KSB // w, C_VSB // w
    return pl.pallas_call(
        functools.partial(_sb_kernel, tq=tq),
        grid=(batch, ngrp, nq),
        in_specs=[pl.BlockSpec((tq, w), lambda b, h, i: (b * nq + i, qc + h)),
                  pl.BlockSpec((seq, w), lambda b, h, i: (b, kc + h)),
                  pl.BlockSpec((seq, w), lambda b, h, i: (b, vc + h))],
        out_specs=pl.BlockSpec((tq, w), lambda b, h, i: (b * nq + i, h)),
        out_shape=jax.ShapeDtypeStruct((t, SB_HEADS * HEAD_DIM), BF16),
        compiler_params=pltpu.CompilerParams(
            dimension_semantics=("parallel", "parallel", "arbitrary"),
            vmem_limit_bytes=VMEM_LIMIT),
        name="sb_attention",
    )(p, p, p)


def _swa_kernel(hp_ref, q_ref, kp_ref, k0_ref, k1_ref, vp_ref, v0_ref, v1_ref, o_ref):
    j = pl.program_id(1)
    blk = k0_ref.shape[0]
    nheads = q_ref.shape[1] // LANES
    sink = hp_ref[2]
    neg = jnp.float32(-jnp.inf)
    lane = lax.broadcasted_iota(I32, (1, LANES), 1)
    blocks = ((kp_ref, k0_ref, vp_ref, v0_ref), (k0_ref, k1_ref, v0_ref, v1_ref))
    for half, (kprev, kcur, vprev, vcur) in enumerate(blocks):
        rows = slice(half * blk, (half + 1) * blk)
        qs = jnp.concatenate([q_ref[rows, g * LANES:(g + 1) * LANES] for g in range(nheads)],
                             axis=0)
        zp = _nt_dot(qs, kprev[...]) + hp_ref[0]
        if half == 0:
            zp = jnp.where(j > 0, zp, neg)
        zc = _nt_dot(qs, kcur[...]) + hp_ref[1]
        m = jnp.maximum(jnp.max(jnp.maximum(zp, zc), axis=1, keepdims=True), sink)
        pp = jnp.exp(zp - m)
        pc = jnp.exp(zc - m)
        den = jnp.sum(pp + pc, axis=1, keepdims=True) + jnp.exp(sink - m)
        o = (jnp.dot(pp.astype(BF16), vprev[...], preferred_element_type=F32)
             + jnp.dot(pc.astype(BF16), vcur[...], preferred_element_type=F32)) / den
        for g in range(nheads):
            kv = g // SWA_GROUP
            kvmask = (lane >= kv * HEAD_DIM) & (lane < (kv + 1) * HEAD_DIM)
            o_ref[rows, g * LANES:(g + 1) * LANES] = jnp.where(
                kvmask, o[g * blk:(g + 1) * blk], 0.0).astype(o_ref.dtype)


def _swa_tables(sinks):
    w = SWA_WINDOW
    slopes = jnp.exp2(-8.0 * jnp.arange(1, SWA_HEADS + 1, dtype=F32) / SWA_HEADS)[:, None, None]
    r = jnp.arange(w)[:, None]
    c = jnp.arange(w)[None, :]
    dist = (r - c).astype(F32)[None]
    neg = jnp.float32(-jnp.inf)
    bias_c = jnp.where((c <= r)[None], -slopes * dist, neg)
    bias_p = jnp.where((c > r)[None], -slopes * (dist + w), neg)
    sink = jnp.broadcast_to(sinks.astype(F32)[:, None, None], (SWA_HEADS, w, w))
    return jnp.stack([bias_p, bias_c, sink]).reshape(3, SWA_HEADS * w, w)


def _swa_attention(p, row_params, batch, seq):
    blk = SWA_WINDOW
    t = batch * seq
    nb = seq // blk
    npair = nb // 2
    qw = SWA_HEADS * LANES
    qc, kc, vc = C_QSW // qw, C_KSW // LANES, C_VSW // LANES
    kblock = lambda off, col: pl.BlockSpec(
        (blk, LANES), lambda b, j: (b * nb + jnp.maximum(2 * j + off, 0), col))
    return pl.pallas_call(
        _swa_kernel,
        grid=(batch, npair),
        in_specs=[_resident(row_params.shape),
                  pl.BlockSpec((2 * blk, qw), lambda b, j: (b * npair + j, qc)),
                  kblock(-1, kc), kblock(0, kc), kblock(1, kc),
                  kblock(-1, vc), kblock(0, vc), kblock(1, vc)],
        out_specs=pl.BlockSpec((2 * blk, qw), lambda b, j: (b * npair + j, 0)),
        out_shape=jax.ShapeDtypeStruct((t, qw), BF16),
        compiler_params=pltpu.CompilerParams(
            dimension_semantics=("parallel", "arbitrary"),
            vmem_limit_bytes=VMEM_LIMIT),
        name="swa_attention",
    )(row_params, p, p, p, p, p, p, p)


def _mem_kernel(q_ref, mk_ref, mv_ref, o_ref):
    scale = MEM_HEAD_DIM ** -0.5
    for h in range(MEM_HEADS):
        cols = slice(h * MEM_HEAD_DIM, (h + 1) * MEM_HEAD_DIM)
        z = _nt_dot(q_ref[:, cols], mk_ref[:, cols]) * scale
        m = jnp.max(z, axis=1, keepdims=True)
        p = jnp.exp(z - m)
        den = jnp.sum(p, axis=1, keepdims=True)
        o = jnp.dot(p.astype(BF16), mv_ref[:, cols], preferred_element_type=F32) / den
        o_ref[:, cols] = o.astype(o_ref.dtype)


def _mem_attention(p, mkv, batch, seq, mem_len, tq):
    t = batch * seq
    nq = seq // tq
    w = MEM_HEADS * MEM_HEAD_DIM
    return pl.pallas_call(
        _mem_kernel,
        grid=(batch, nq),
        in_specs=[pl.BlockSpec((tq, w), lambda b, i: (b * nq + i, C_QM // w)),
                  pl.BlockSpec((mem_len, w), lambda b, i: (b, 0)),
                  pl.BlockSpec((mem_len, w), lambda b, i: (b, 1))],
        out_specs=pl.BlockSpec((tq, w), lambda b, i: (b * nq + i, 0)),
        out_shape=jax.ShapeDtypeStruct((t, w), BF16),
        compiler_params=pltpu.CompilerParams(
            dimension_semantics=("parallel", "arbitrary"), vmem_limit_bytes=VMEM_LIMIT),
        name="mem_attention",
    )(p, mkv, mkv)


def _merge_kernel(osb_ref, osw_ref, om_ref, g_ref, x_ref, wsb_ref, wsw_ref, wm_ref,
                  wout_ref, lng_ref, lnb_ref, x1_ref, x1p_ref):
    d = x_ref.shape[1]
    merged = g_ref[:, 0:d].astype(F32) * jnp.dot(
        osb_ref[...], wsb_ref[...], preferred_element_type=F32)
    merged += g_ref[:, d:2 * d].astype(F32) * jnp.dot(
        osw_ref[...], wsw_ref[...], preferred_element_type=F32)
    merged += g_ref[:, 2 * d:3 * d].astype(F32) * jnp.dot(
        om_ref[...], wm_ref[...], preferred_element_type=F32)
    y = jnp.dot(merged.astype(BF16), wout_ref[...], preferred_element_type=F32)
    x1 = _layer_norm(DEEPNORM_ALPHA * x_ref[...] + y, lng_ref[...], lnb_ref[...])
    x1_ref[...] = x1
    x1p_ref[...] = _pack_bf16_pair(x1[:, :d // 2], x1[:, d // 2:])


def _merge(o_sb, o_sw, o_m, p, x2, w_sb, w_sw, w_m, w_out, ln_g, ln_b, tm):
    t, d = x2.shape
    row = lambda w: pl.BlockSpec((tm, w), lambda i: (i, 0))
    return pl.pallas_call(
        _merge_kernel,
        grid=(t // tm,),
        in_specs=[row(o_sb.shape[1]), row(o_sw.shape[1]), row(o_m.shape[1]),
                  pl.BlockSpec((tm, N_BRANCH * d), lambda i: (i, 0)),
                  row(d),
                  _resident(w_sb.shape), _resident(w_sw.shape), _resident(w_m.shape),
                  _resident(w_out.shape), _resident(ln_g.shape), _resident(ln_b.shape)],
        out_specs=[row(d), row(d // 2)],
        out_shape=[jax.ShapeDtypeStruct((t, d), F32), jax.ShapeDtypeStruct((t, d // 2), U32)],
        compiler_params=pltpu.CompilerParams(
            dimension_semantics=("parallel",), vmem_limit_bytes=VMEM_LIMIT),
        name="merge_ln1",
    )(o_sb, o_sw, o_m, p, x2, w_sb, w_sw, w_m, w_out, ln_g, ln_b)


def _router_kernel(x_ref, wh_ref, wl_ref, bias_ref, idx_ref, rank_ref, wgt_ref, cnt_ref,
                   carry_ref):
    step = pl.program_id(0)
    tr = x_ref.shape[0]

    @pl.when(step == 0)
    def _():
        carry_ref[...] = jnp.zeros_like(carry_ref)

    x = x_ref[...]
    xh = x.astype(BF16)
    xl = (x - xh.astype(F32)).astype(BF16)
    logits = _nt_dot(wh_ref[...], xh) + _nt_dot(wh_ref[...], xl) + _nt_dot(wl_ref[...], xh)
    scores = _sigmoid(logits)
    biased = scores + bias_ref[...]
    neg = jnp.float32(-jnp.inf)

    sub = lax.broadcasted_iota(I32, (GROUP_SIZE, tr), 0)
    gscore = []
    for g in range(N_GROUPS):
        blk = biased[g * GROUP_SIZE:(g + 1) * GROUP_SIZE, :]
        m1 = jnp.max(blk, axis=0, keepdims=True)
        i1 = jnp.min(jnp.where(blk == m1, sub, GROUP_SIZE), axis=0, keepdims=True)
        m2 = jnp.max(jnp.where(sub == i1, neg, blk), axis=0, keepdims=True)
        gscore.append(m1 + m2)
    gs = jnp.concatenate(gscore, axis=0)

    giota = lax.broadcasted_iota(I32, (N_GROUPS, tr), 0)
    gsel = jnp.zeros((N_GROUPS, tr), F32)
    for _ in range(TOPK_GROUPS):
        m = jnp.max(gs, axis=0, keepdims=True)
        gi = jnp.min(jnp.where(gs == m, giota, N_GROUPS), axis=0, keepdims=True)
        hit = giota == gi
        gsel = jnp.where(hit, 1.0, gsel)
        gs = jnp.where(hit, neg, gs)

    masked = jnp.concatenate(
        [jnp.where(gsel[g:g + 1, :] > 0.0, biased[g * GROUP_SIZE:(g + 1) * GROUP_SIZE, :], neg)
         for g in range(N_GROUPS)], axis=0)

    eiota = lax.broadcasted_iota(I32, (N_EXPERTS, tr), 0)
    sel = jnp.zeros((N_EXPERTS, tr), F32)
    idx_rows, w_rows = [], []
    for _ in range(TOP_K):
        m = jnp.max(masked, axis=0, keepdims=True)
        ei = jnp.min(jnp.where(masked == m, eiota, N_EXPERTS), axis=0, keepdims=True)
        hit = eiota == ei
        idx_rows.append(ei)
        w_rows.append(jnp.sum(jnp.where(hit, scores, 0.0), axis=0, keepdims=True))
        sel = jnp.where(hit, 1.0, sel)
        masked = jnp.where(hit, neg, masked)

    wsum = w_rows[0]
    for wk in w_rows[1:]:
        wsum = wsum + wk
    wgt_ref[...] = jnp.concatenate(w_rows, axis=0) / wsum * ROUTED_SCALE
    idx_ref[...] = jnp.concatenate(idx_rows, axis=0)

    a = lax.broadcasted_iota(I32, (tr, tr), 0)
    b = lax.broadcasted_iota(I32, (tr, tr), 1)
    before = (a < b).astype(BF16)
    rank = jnp.dot(sel.astype(BF16), before, preferred_element_type=F32) + carry_ref[...]
    rank_rows = [jnp.sum(jnp.where(eiota == ei, rank, 0.0), axis=0, keepdims=True)
                 for ei in idx_rows]
    rank_ref[...] = jnp.concatenate(rank_rows, axis=0).astype(I32)
    carry_ref[...] = carry_ref[...] + jnp.sum(sel, axis=1, keepdims=True)
    cnt_ref[...] = carry_ref[...]


def _router(x1, wr_hi, wr_lo, bias_col, tr):
    t, d = x1.shape
    slot = pl.BlockSpec((TOP_K, tr), lambda i: (0, i))
    return pl.pallas_call(
        _router_kernel,
        grid=(t // tr,),
        in_specs=[pl.BlockSpec((tr, d), lambda i: (i, 0)),
                  _resident(wr_hi.shape), _resident(wr_lo.shape), _resident(bias_col.shape)],
        out_specs=[slot, slot, slot, pl.BlockSpec((N_EXPERTS, 1), lambda i: (0, 0))],
        out_shape=[jax.ShapeDtypeStruct((TOP_K, t), I32),
                   jax.ShapeDtypeStruct((TOP_K, t), I32),
                   jax.ShapeDtypeStruct((TOP_K, t), F32),
                   jax.ShapeDtypeStruct((N_EXPERTS, 1), F32)],
        scratch_shapes=[pltpu.VMEM((N_EXPERTS, 1), F32)],
        compiler_params=pltpu.CompilerParams(
            dimension_semantics=("arbitrary",), vmem_limit_bytes=VMEM_LIMIT),
        name="router",
    )(x1, wr_hi, wr_lo, bias_col)


def _dest_kernel(idx_ref, rank_ref, start_ref, dest_ref):
    tr = idx_ref.shape[1]
    eiota = lax.broadcasted_iota(I32, (N_EXPERTS, tr), 0)
    rows = []
    for k in range(TOP_K):
        hit = eiota == idx_ref[k:k + 1, :]
        rows.append(jnp.sum(jnp.where(hit, start_ref[...], 0), axis=0, keepdims=True))
    dest_ref[...] = jnp.concatenate(rows, axis=0) + rank_ref[...]


def _dest(idx, rank, start_col, tr):
    t = idx.shape[1]
    slot = pl.BlockSpec((TOP_K, tr), lambda i: (0, i))
    return pl.pallas_call(
        _dest_kernel,
        grid=(t // tr,),
        in_specs=[slot, slot, _resident(start_col.shape)],
        out_specs=slot,
        out_shape=jax.ShapeDtypeStruct((TOP_K, t), I32),
        compiler_params=pltpu.CompilerParams(dimension_semantics=("parallel",)),
        name="slot_dest",
    )(idx, rank, start_col)


def _sc_worker_id():
    return lax.axis_index("s") * SC_CORES + lax.axis_index("c")


def _dispatch(x1p, dest, n_rows):
    t, w = x1p.shape
    per = t // SC_WORKERS
    win = min(SC_INDEX_WINDOW, per)
    mesh = plsc.VectorSubcoreMesh(core_axis_name="c", subcore_axis_name="s")

    @functools.partial(
        pl.kernel, mesh=mesh,
        out_type=jax.ShapeDtypeStruct((n_rows, w), x1p.dtype),
        scratch_types=[pltpu.VMEM((TOP_K, win), I32),
                       pltpu.VMEM((win, w), x1p.dtype),
                       pltpu.SemaphoreType.DMA],
        name="sc_dispatch",
    )
    def scatter_rows(x_hbm, dest_hbm, xs_hbm, idx_v, rows_v, sem):
        base = _sc_worker_id() * per

        @pl.loop(0, per // win)
        def _(j):
            t0 = pl.multiple_of(base + j * win, win)
            pltpu.sync_copy(dest_hbm.at[:, pl.ds(t0, win)], idx_v)
            pltpu.sync_copy(x_hbm.at[pl.ds(t0, win)], rows_v)
            copies = [pltpu.async_copy(rows_v, xs_hbm.at[idx_v.at[k]], sem) for k in range(TOP_K)]
            for c in copies:
                c.wait()

    return scatter_rows(x1p, dest)


def _gather_rows(table, idx):
    n = idx.shape[0]
    w = table.shape[1]
    per = n // SC_WORKERS
    chunk = min(SC_GATHER_ROWS, per // 2)
    assert n % SC_WORKERS == 0 and per % (2 * chunk) == 0, (n, chunk)
    mesh = plsc.VectorSubcoreMesh(core_axis_name="c", subcore_axis_name="s")

    @functools.partial(
        pl.kernel, mesh=mesh,
        out_type=jax.ShapeDtypeStruct((n, w), table.dtype),
        scratch_types=[pltpu.VMEM((per,), I32),
                       pltpu.VMEM((2, chunk, w), table.dtype),
                       pltpu.SemaphoreType.DMA((2,)),
                       pltpu.SemaphoreType.DMA((2,))],
        name="sc_gather",
    )
    def gather_rows(table_hbm, idx_hbm, out_hbm, idx_v, rows_v, gather_sem, put_sem):
        base = _sc_worker_id() * per
        nchunks = per // chunk
        pltpu.sync_copy(idx_hbm.at[pl.ds(base, per)], idx_v)

        def gather(j, b):
            off = pl.multiple_of(j * chunk, chunk)
            return pltpu.make_async_copy(table_hbm.at[idx_v.at[pl.ds(off, chunk)]],
                                         rows_v.at[b], gather_sem.at[b])

        def put(j, b):
            off = pl.multiple_of(j * chunk, chunk)
            return pltpu.make_async_copy(rows_v.at[b], out_hbm.at[pl.ds(base + off, chunk)],
                                         put_sem.at[b])

        gather(0, 0).start()

        @pl.loop(0, nchunks, step=2)
        def _(j):
            for b in (0, 1):
                jj = j + b

                @pl.when(jj + 1 < nchunks)
                def _():
                    @pl.when(jj >= 1)
                    def _():
                        put(jj - 1, 1 - b).wait()
                    gather(jj + 1, 1 - b).start()

                gather(jj, b).wait()
                put(jj, b).start()

        put(nchunks - 2, 0).wait()
        put(nchunks - 1, 1).wait()

    return gather_rows(table, idx)


def _expert_kernel(first_ref, nblk_ref, total_ref, wgu_ref, wd_ref, xs_hbm, ys_hbm,
                   wgu_s, wd_s, xbuf, ybuf, in_sem, out_sem):
    e = pl.program_id(0)
    total = total_ref[0]
    n_in, rows, half = xbuf.shape
    n_out = ybuf.shape[0]
    ahead = n_in - 1

    def block_rows(g):
        return pl.ds(pl.multiple_of(g * rows, rows), rows)

    def load(g):
        slot = g % n_in
        return pltpu.make_async_copy(xs_hbm.at[block_rows(g), :], xbuf.at[slot], in_sem.at[slot])

    def store(g):
        slot = g % n_out
        return pltpu.make_async_copy(ybuf.at[slot], ys_hbm.at[block_rows(g), :], out_sem.at[slot])

    @pl.when(e == 0)
    def _():
        for g in range(ahead):
            @pl.when(g < total)
            def _(g=g):
                load(g).start()

    wgu_s[...] = wgu_ref[0].astype(BF16)
    wd_s[...] = wd_ref[0].astype(BF16)
    g0 = first_ref[e]
    ff = wd_s.shape[0]

    def body(j, carry):
        g = g0 + j
        load(g).wait()

        @pl.when(g + ahead < total)
        def _():
            load(g + ahead).start()

        @pl.when(g >= n_out)
        def _():
            store(g - n_out).wait()

        x_lo, x_hi = _unpack_bf16_pair(xbuf[g % n_in])
        h = (jnp.dot(x_lo.astype(BF16), wgu_s[:half, :], preferred_element_type=F32)
             + jnp.dot(x_hi.astype(BF16), wgu_s[half:, :], preferred_element_type=F32))
        gate, up = h[:, :ff], h[:, ff:]
        act = gate * _sigmoid(gate) * up
        y = jnp.dot(act.astype(BF16), wd_s[...], preferred_element_type=F32)
        ybuf[g % n_out] = _pack_bf16_pair(y[:, :half], y[:, half:])
        store(g).start()
        return carry

    lax.fori_loop(0, nblk_ref[e], body, 0)

    @pl.when(e == pl.num_programs(0) - 1)
    def _():
        for back in range(n_out, 0, -1):
            @pl.when(total >= back)
            def _(back=back):
                store(total - back).wait()


def _experts(xs, first_blk, n_blk, total_blk, w_gu, w_down):
    n_rows, half = xs.shape
    d = 2 * half
    n_exp, _, ff2 = w_gu.shape
    ff = w_down.shape[1]
    grid_spec = pltpu.PrefetchScalarGridSpec(
        num_scalar_prefetch=3,
        grid=(n_exp,),
        in_specs=[pl.BlockSpec((1, d, ff2), lambda e, *_: (e, 0, 0)),
                  pl.BlockSpec((1, ff, d), lambda e, *_: (e, 0, 0)),
                  pl.BlockSpec(memory_space=pl.ANY)],
        out_specs=pl.BlockSpec(memory_space=pl.ANY),
        scratch_shapes=[pltpu.VMEM((d, ff2), BF16), pltpu.VMEM((ff, d), BF16),
                        pltpu.VMEM((EXPERT_IN_SLOTS, EXPERT_ROWS, half), xs.dtype),
                        pltpu.VMEM((EXPERT_OUT_SLOTS, EXPERT_ROWS, half), xs.dtype),
                        pltpu.SemaphoreType.DMA((EXPERT_IN_SLOTS,)),
                        pltpu.SemaphoreType.DMA((EXPERT_OUT_SLOTS,))],
    )
    return pl.pallas_call(
        _expert_kernel,
        grid_spec=grid_spec,
        out_shape=jax.ShapeDtypeStruct((n_rows, half), xs.dtype),
        compiler_params=pltpu.CompilerParams(
            dimension_semantics=("arbitrary",), vmem_limit_bytes=VMEM_LIMIT),
        name="experts",
    )(first_blk, n_blk, total_blk, w_gu, w_down, xs)


def _combine_kernel(x1_ref, wt_ref, yg_ref, wsgu_ref, wsd_ref, lng_ref, lnb_ref, o_ref):
    tc = x1_ref.shape[0]
    x1 = x1_ref[...]
    ff = wsd_ref.shape[0]
    h = jnp.dot(x1.astype(BF16), wsgu_ref[...], preferred_element_type=F32)
    gate, up = h[:, :ff], h[:, ff:]
    act = gate * _sigmoid(gate) * up
    moe = jnp.dot(act.astype(BF16), wsd_ref[...], preferred_element_type=F32)

    half = yg_ref.shape[2]
    r_lo = jnp.zeros((tc, half), F32)
    r_hi = jnp.zeros((tc, half), F32)
    for k in range(TOP_K):
        y_lo, y_hi = _unpack_bf16_pair(yg_ref[k])
        w = wt_ref[:, k:k + 1]
        r_lo = r_lo + w * y_lo
        r_hi = r_hi + w * y_hi
    moe = moe + jnp.concatenate([r_lo, r_hi], axis=1)
    o_ref[...] = _layer_norm(DEEPNORM_ALPHA * x1 + moe, lng_ref[...], lnb_ref[...])


def _combine(x1, wgt_t, yg, chunk, ws_gu, ws_down, ln_g, ln_b, tc):
    t, d = x1.shape
    steps = yg.shape[1] // tc
    tok = lambda i: (chunk * steps + i, 0)
    return pl.pallas_call(
        _combine_kernel,
        grid=(steps,),
        in_specs=[pl.BlockSpec((tc, d), tok),
                  pl.BlockSpec((tc, TOP_K), tok),
                  pl.BlockSpec((TOP_K, tc, yg.shape[2]), lambda i: (0, i, 0)),
                  _resident(ws_gu.shape), _resident(ws_down.shape),
                  _resident(ln_g.shape), _resident(ln_b.shape)],
        out_specs=pl.BlockSpec((tc, d), tok),
        out_shape=jax.ShapeDtypeStruct((t, d), F32),
        input_output_aliases={0: 0},
        compiler_params=pltpu.CompilerParams(
            dimension_semantics=("arbitrary",), vmem_limit_bytes=VMEM_LIMIT),
        name="combine_ln2",
    )(x1, wgt_t, yg, ws_gu, ws_down, ln_g, ln_b)


def _fused_in_weights(w_in):
    d = w_in.shape[0]
    sizes = (SB_HEADS * HEAD_DIM,) * 3 + (SWA_HEADS * HEAD_DIM, SWA_KV_HEADS * HEAD_DIM,
                                          SWA_KV_HEADS * HEAD_DIM, MEM_HEADS * MEM_HEAD_DIM)
    parts, off = [], 0
    for s in sizes:
        parts.append(w_in[:, off:off + s])
        off += s
    q_sb, k_sb, v_sb, q_sw, k_sw, v_sw, q_m = parts
    gates = w_in[:, off:]
    scale = HEAD_DIM ** -0.5
    q_sw = (q_sw * scale).reshape(d, SWA_KV_HEADS, SWA_GROUP, HEAD_DIM)
    zeros = jnp.zeros((d, SWA_GROUP, HEAD_DIM), w_in.dtype)
    q_sw = jnp.stack([jnp.concatenate([q_sw[:, 0], zeros], axis=-1),
                      jnp.concatenate([zeros, q_sw[:, 1]], axis=-1)], axis=1)
    q_sw = q_sw.reshape(d, SWA_HEADS * LANES)
    used = W_GATE + 3 * SB_HEADS * HEAD_DIM + SWA_HEADS * LANES + q_m.shape[1] + 2 * k_sw.shape[1]
    pad = jnp.zeros((d, PROJ_COLS - used), w_in.dtype)
    return jnp.concatenate([gates, q_sw, q_sb * scale, k_sb, v_sb, q_m, k_sw, v_sw, pad],
                           axis=1).astype(BF16)


def _padded_swa_out_weights(w_o_swa):
    d = w_o_swa.shape[1]
    w = w_o_swa.reshape(SWA_KV_HEADS, SWA_GROUP, HEAD_DIM, d)
    zeros = jnp.zeros((SWA_GROUP, HEAD_DIM, d), w_o_swa.dtype)
    w = jnp.stack([jnp.concatenate([w[0], zeros], axis=1),
                   jnp.concatenate([zeros, w[1]], axis=1)], axis=0)
    return w.reshape(SWA_HEADS * LANES, d).astype(BF16)


def kernel(x, mem, w_in, b_gate, w_mem_kv, sinks, w_o_sb, w_o_swa, w_o_mem, w_out,
           ln1_g, ln1_b, w_router, router_bias, w_e_gu, w_e_down, w_s_gu, w_s_down,
           ln2_g, ln2_b):
    batch, seq, d = x.shape
    mem_len = mem.shape[1]
    t = batch * seq
    x2 = x.reshape(t, d)
    row_tile = min(512, t)

    p = _proj(x2, _fused_in_weights(w_in), b_gate.reshape(1, -1), row_tile)
    mkv = _matmul_bf16(mem.reshape(batch * mem_len, d), w_mem_kv.astype(BF16), mem_len)
    o_sb = _sb_attention(p, batch, seq, min(256, seq), SB_PAIRS_PER_STEP)
    o_sw = _swa_attention(p, _swa_tables(sinks), batch, seq)
    o_m = _mem_attention(p, mkv, batch, seq, mem_len, min(512, seq))
    x1, x1p = _merge(o_sb, o_sw, o_m, p, x2, w_o_sb.astype(BF16), _padded_swa_out_weights(w_o_swa),
                w_o_mem.astype(BF16), w_out.astype(BF16),
                ln1_g.reshape(1, d), ln1_b.reshape(1, d), row_tile)

    out = _moe_ln(x1, x1p, w_router, router_bias, w_e_gu, w_e_down, w_s_gu, w_s_down,
                  ln2_g, ln2_b)
    return out.reshape(batch, seq, d)


def _moe_ln(x1, x1p, w_router, router_bias, w_e_gu, w_e_down, w_s_gu, w_s_down, ln2_g, ln2_b):
    t, d = x1.shape
    wr_t = w_router.T
    wr_hi = wr_t.astype(BF16)
    wr_lo = (wr_t - wr_hi.astype(F32)).astype(BF16)
    route_tile = min(256, t)
    idx, rank, wgt, cnt = _router(x1, wr_hi, wr_lo, router_bias.reshape(-1, 1).astype(F32),
                                  route_tile)
    counts = cnt[:, 0].astype(I32)
    padded = (counts + EXPERT_ROWS - 1) // EXPERT_ROWS * EXPERT_ROWS
    pad_end = jnp.cumsum(padded)
    pad_start = pad_end - padded
    n_blocks = t * TOP_K // EXPERT_ROWS + N_EXPERTS
    dest = _dest(idx, rank, pad_start.reshape(-1, 1), min(2048, t))

    xs = _dispatch(x1p, dest, n_blocks * EXPERT_ROWS)
    ys = _experts(xs, pad_start // EXPERT_ROWS, padded // EXPERT_ROWS,
                  pad_end[-1:] // EXPERT_ROWS, w_e_gu, w_e_down)
    tchunk = t // COMBINE_CHUNKS
    wgt_t = wgt.T
    ws_gu, ws_down = w_s_gu.astype(BF16), w_s_down.astype(BF16)
    out = x1
    for c in range(COMBINE_CHUNKS):
        slots = dest[:, c * tchunk:(c + 1) * tchunk].reshape(-1)
        yg = _gather_rows(ys, slots).reshape(TOP_K, tchunk, -1)
        out = _combine(out, wgt_t, yg, c, ws_gu, ws_down, ln2_g.reshape(1, d), ln2_b.reshape(1, d),
                       min(256, tchunk))
    return out
```

```python
import functools

import jax
import jax.numpy as jnp
from jax import lax
from jax.experimental import pallas as pl
from jax.experimental.pallas import tpu as pltpu
from jax.experimental.pallas import tpu_sc as plsc

F32 = jnp.float32
BF16 = jnp.bfloat16
I32 = jnp.int32
U32 = jnp.uint32

HEAD_DIM = 64
SB_HEADS = 8
SWA_HEADS = 8
SWA_KV_HEADS = 2
SWA_GROUP = SWA_HEADS // SWA_KV_HEADS
SWA_WINDOW = 128
MEM_HEADS = 4
MEM_HEAD_DIM = 128
N_BRANCH = 3
N_EXPERTS = 256
TOP_K = 8
N_GROUPS = 8
GROUP_SIZE = N_EXPERTS // N_GROUPS
TOPK_GROUPS = 4
EXPERT_FF = 256
SHARED_FF = 256
ROUTED_SCALE = 2.5
LN_EPS = 1e-5
DEPTH = 1
DEEPNORM_ALPHA = (2 * DEPTH) ** 0.25

LANES = 128
SC_CORES = 2
SC_SUBCORES = 16
SC_WORKERS = SC_CORES * SC_SUBCORES
SC_INDEX_WINDOW = 128
SC_GATHER_ROWS = 64
VMEM_LIMIT = 56 * 1024 * 1024

D_GATE = 0
W_GATE = 3072
C_QSW = 3072
C_QSB = 4096
C_KSB = 4608
C_VSB = 5120
C_QM = 5632
C_KSW = 6144
C_VSW = 6272
PROJ_COLS = 6400
PROJ_CHUNK = 256

SB_SKIP = 110.0
SB_PAIRS_PER_STEP = 4

EXPERT_ROWS = 512
EXPERT_IN_SLOTS = 4
EXPERT_OUT_SLOTS = 2
COMBINE_CHUNKS = 4


def _nt_dot(a, b):
    return lax.dot_general(a, b, (((1,), (1,)), ((), ())), preferred_element_type=F32)


def _sigmoid(x):
    return 1.0 / (1.0 + jnp.exp(-x))


def _layer_norm(h, g, b):
    mu = jnp.mean(h, axis=-1, keepdims=True)
    d = h - mu
    var = jnp.mean(d * d, axis=-1, keepdims=True)
    return d * lax.rsqrt(var + LN_EPS) * g + b


def _pack_bf16_pair(a, b):
    a_bits = lax.bitcast_convert_type(a.astype(BF16).astype(F32), U32)
    b_bits = lax.bitcast_convert_type(b.astype(BF16).astype(F32), U32)
    return (a_bits >> 16) | b_bits


def _unpack_bf16_pair(w):
    a = lax.bitcast_convert_type(w << 16, F32)
    b = lax.bitcast_convert_type(w & jnp.uint32(0xFFFF0000), F32)
    return a, b


def _resident(shape):
    nd = len(shape)
    return pl.BlockSpec(shape, lambda *_: (0,) * nd, pipeline_mode=pl.Buffered(1))


def _proj_kernel(x_ref, w_ref, b_ref, o_ref, *, gate_cols):
    xb = x_ref[...].astype(BF16)
    for j in range(o_ref.shape[1] // PROJ_CHUNK):
        cols = slice(j * PROJ_CHUNK, (j + 1) * PROJ_CHUNK)
        acc = jnp.dot(xb, w_ref[:, cols], preferred_element_type=F32)
        if (j + 1) * PROJ_CHUNK <= gate_cols:
            acc = _sigmoid(acc + b_ref[:, cols])
        o_ref[:, cols] = acc.astype(o_ref.dtype)


def _proj(x2, w_all, b_gate, tm):
    t, d = x2.shape
    n = w_all.shape[1]
    return pl.pallas_call(
        functools.partial(_proj_kernel, gate_cols=b_gate.shape[1]),
        grid=(t // tm,),
        in_specs=[pl.BlockSpec((tm, d), lambda i: (i, 0)),
                  _resident((d, n)),
                  _resident(b_gate.shape)],
        out_specs=pl.BlockSpec((tm, n), lambda i: (i, 0)),
        out_shape=jax.ShapeDtypeStruct((t, n), BF16),
        compiler_params=pltpu.CompilerParams(
            dimension_semantics=("parallel",), vmem_limit_bytes=VMEM_LIMIT),
        name="in_proj",
    )(x2, w_all, b_gate)


def _mm_kernel(x_ref, w_ref, o_ref):
    o_ref[...] = jnp.dot(x_ref[...].astype(BF16), w_ref[...],
                         preferred_element_type=F32).astype(o_ref.dtype)


def _matmul_bf16(x2, w, tm):
    t, d = x2.shape
    n = w.shape[1]
    return pl.pallas_call(
        _mm_kernel,
        grid=(t // tm,),
        in_specs=[pl.BlockSpec((tm, d), lambda i: (i, 0)), _resident((d, n))],
        out_specs=pl.BlockSpec((tm, n), lambda i: (i, 0)),
        out_shape=jax.ShapeDtypeStruct((t, n), BF16),
        compiler_params=pltpu.CompilerParams(
            dimension_semantics=("parallel",), vmem_limit_bytes=VMEM_LIMIT),
        name="mem_kv_proj",
    )(x2, w)


def _sb_kernel(q_ref, k_ref, v_ref, o_ref, *, tq):
    i = pl.program_id(2)
    pairs = q_ref.shape[1] // LANES
    lane = lax.broadcasted_iota(I32, (1, LANES), 1)
    r = lax.broadcasted_iota(I32, (tq, tq), 0)
    c = lax.broadcasted_iota(I32, (tq, tq), 1)
    tri = (r >= c).astype(BF16)
    causal = c < r
    nh = LANES // HEAD_DIM
    hmasks = [(lane >= h * HEAD_DIM) & (lane < (h + 1) * HEAD_DIM) for h in range(nh)]
    qs = []
    for p in range(pairs):
        q = q_ref[:, p * LANES:(p + 1) * LANES]
        qs.append(jnp.concatenate([jnp.where(hm, q, jnp.zeros_like(q)) for hm in hmasks], axis=0))
    causal2 = jnp.concatenate([causal] * (nh * pairs), axis=0)
    tri2 = jnp.concatenate([tri, tri], axis=0)
    m = pairs * nh * tq

    def block(kb, carry, acc, diag):
        rows = pl.ds(pl.multiple_of(kb * tq, tq), tq)
        z = jnp.concatenate([_nt_dot(qs[p], k_ref[rows, p * LANES:(p + 1) * LANES])
                             for p in range(pairs)], axis=0)
        sp = jnp.maximum(z, 0.0) + jnp.log(1.0 + jnp.exp(-jnp.abs(z)))
        if diag:
            sp = jnp.where(causal2, sp, 0.0)
        hi = sp.astype(BF16)
        lo = (sp - hi.astype(F32)).astype(BF16)
        suffix = jnp.dot(jnp.concatenate([hi, lo], axis=1), tri2,
                         preferred_element_type=F32)
        a = jnp.exp((z - carry) - suffix)
        if diag:
            a = jnp.where(causal2, a, 0.0)
        ab = a.astype(BF16)
        mp = nh * tq
        av = jnp.concatenate(
            [jnp.dot(ab[p * mp:(p + 1) * mp], v_ref[rows, p * LANES:(p + 1) * LANES],
                     preferred_element_type=F32) for p in range(pairs)], axis=0)
        return carry + suffix[:, 0:1], acc + av

    carry, acc = block(i, jnp.zeros((m, 1), F32), jnp.zeros((m, LANES), F32), True)

    def cond(s):
        kb, carry, _ = s
        return (kb >= 0) & (jnp.min(carry) < SB_SKIP)

    def body(s):
        kb, carry, acc = s
        carry, acc = block(kb, carry, acc, False)
        return kb - 1, carry, acc

    _, _, acc = lax.while_loop(cond, body, (i - 1, carry, acc))
    for p in range(pairs):
        lo_rows = acc[(p * nh) * tq:(p * nh + 1) * tq]
        hi_rows = acc[(p * nh + 1) * tq:(p * nh + 2) * tq]
        o_ref[:, p * LANES:(p + 1) * LANES] = jnp.where(hmasks[0], lo_rows, hi_rows).astype(o_ref.dtype)


def _sb_attention(p, batch, seq, tq, pairs):
    t = batch * seq
    nq = seq // tq
    w = pairs * LANES
    ngrp = SB_HEADS * HEAD_DIM // w
    qc, kc, vc = C_QSB // w, C_KSB // w, C_VSB // w
    return pl.pallas_call(
        functools.partial(_sb_kernel, tq=tq),
        grid=(batch, ngrp, nq),
        in_specs=[pl.BlockSpec((tq, w), lambda b, h, i: (b * nq + i, qc + h)),
                  pl.BlockSpec((seq, w), lambda b, h, i: (b, kc + h)),
                  pl.BlockSpec((seq, w), lambda b, h, i: (b, vc + h))],
        out_specs=pl.BlockSpec((tq, w), lambda b, h, i: (b * nq + i, h)),
        out_shape=jax.ShapeDtypeStruct((t, SB_HEADS * HEAD_DIM), BF16),
        compiler_params=pltpu.CompilerParams(
            dimension_semantics=("parallel", "parallel", "arbitrary"),
            vmem_limit_bytes=VMEM_LIMIT),
        name="sb_attention",
    )(p, p, p)


def _swa_kernel(hp_ref, q_ref, kp_ref, k0_ref, k1_ref, vp_ref, v0_ref, v1_ref, o_ref):
    j = pl.program_id(1)
    blk = k0_ref.shape[0]
    nheads = q_ref.shape[1] // LANES
    sink = hp_ref[2]
    neg = jnp.float32(-jnp.inf)
    lane = lax.broadcasted_iota(I32, (1, LANES), 1)
    blocks = ((kp_ref, k0_ref, vp_ref, v0_ref), (k0_ref, k1_ref, v0_ref, v1_ref))
    for half, (kprev, kcur, vprev, vcur) in enumerate(blocks):
        rows = slice(half * blk, (half + 1) * blk)
        qs = jnp.concatenate([q_ref[rows, g * LANES:(g + 1) * LANES] for g in range(nheads)],
                             axis=0)
        zp = _nt_dot(qs, kprev[...]) + hp_ref[0]
        if half == 0:
            zp = jnp.where(j > 0, zp, neg)
        zc = _nt_dot(qs, kcur[...]) + hp_ref[1]
        m = jnp.maximum(jnp.max(jnp.maximum(zp, zc), axis=1, keepdims=True), sink)
        pp = jnp.exp(zp - m)
        pc = jnp.exp(zc - m)
        den = jnp.sum(pp + pc, axis=1, keepdims=True) + jnp.exp(sink - m)
        o = (jnp.dot(pp.astype(BF16), vprev[...], preferred_element_type=F32)
             + jnp.dot(pc.astype(BF16), vcur[...], preferred_element_type=F32)) / den
        for g in range(nheads):
            kv = g // SWA_GROUP
            kvmask = (lane >= kv * HEAD_DIM) & (lane < (kv + 1) * HEAD_DIM)
            o_ref[rows, g * LANES:(g + 1) * LANES] = jnp.where(
                kvmask, o[g * blk:(g + 1) * blk], 0.0).astype(o_ref.dtype)


def _swa_tables(sinks):
    w = SWA_WINDOW
    slopes = jnp.exp2(-8.0 * jnp.arange(1, SWA_HEADS + 1, dtype=F32) / SWA_HEADS)[:, None, None]
    r = jnp.arange(w)[:, None]
    c = jnp.arange(w)[None, :]
    dist = (r - c).astype(F32)[None]
    neg = jnp.float32(-jnp.inf)
    bias_c = jnp.where((c <= r)[None], -slopes * dist, neg)
    bias_p = jnp.where((c > r)[None], -slopes * (dist + w), neg)
    sink = jnp.broadcast_to(sinks.astype(F32)[:, None, None], (SWA_HEADS, w, w))
    return jnp.stack([bias_p, bias_c, sink]).reshape(3, SWA_HEADS * w, w)


def _swa_attention(p, row_params, batch, seq):
    blk = SWA_WINDOW
    t = batch * seq
    nb = seq // blk
    npair = nb // 2
    qw = SWA_HEADS * LANES
    qc, kc, vc = C_QSW // qw, C_KSW // LANES, C_VSW // LANES
    kblock = lambda off, col: pl.BlockSpec(
        (blk, LANES), lambda b, j: (b * nb + jnp.maximum(2 * j + off, 0), col))
    return pl.pallas_call(
        _swa_kernel,
        grid=(batch, npair),
        in_specs=[_resident(row_params.shape),
                  pl.BlockSpec((2 * blk, qw), lambda b, j: (b * npair + j, qc)),
                  kblock(-1, kc), kblock(0, kc), kblock(1, kc),
                  kblock(-1, vc), kblock(0, vc), kblock(1, vc)],
        out_specs=pl.BlockSpec((2 * blk, qw), lambda b, j: (b * npair + j, 0)),
        out_shape=jax.ShapeDtypeStruct((t, qw), BF16),
        compiler_params=pltpu.CompilerParams(
            dimension_semantics=("parallel", "arbitrary"),
            vmem_limit_bytes=VMEM_LIMIT),
        name="swa_attention",
    )(row_params, p, p, p, p, p, p, p)


def _mem_kernel(q_ref, mk_ref, mv_ref, o_ref):
    scale = MEM_HEAD_DIM ** -0.5
    for h in range(MEM_HEADS):
        cols = slice(h * MEM_HEAD_DIM, (h + 1) * MEM_HEAD_DIM)
        z = _nt_dot(q_ref[:, cols], mk_ref[:, cols]) * scale
        m = jnp.max(z, axis=1, keepdims=True)
        p = jnp.exp(z - m)
        den = jnp.sum(p, axis=1, keepdims=True)
        o = jnp.dot(p.astype(BF16), mv_ref[:, cols], preferred_element_type=F32) / den
        o_ref[:, cols] = o.astype(o_ref.dtype)


def _mem_attention(p, mkv, batch, seq, mem_len, tq):
    t = batch * seq
    nq = seq // tq
    w = MEM_HEADS * MEM_HEAD_DIM
    return pl.pallas_call(
        _mem_kernel,
        grid=(batch, nq),
        in_specs=[pl.BlockSpec((tq, w), lambda b, i: (b * nq + i, C_QM // w)),
                  pl.BlockSpec((mem_len, w), lambda b, i: (b, 0)),
                  pl.BlockSpec((mem_len, w), lambda b, i: (b, 1))],
        out_specs=pl.BlockSpec((tq, w), lambda b, i: (b * nq + i, 0)),
        out_shape=jax.ShapeDtypeStruct((t, w), BF16),
        compiler_params=pltpu.CompilerParams(
            dimension_semantics=("parallel", "arbitrary"), vmem_limit_bytes=VMEM_LIMIT),
        name="mem_attention",
    )(p, mkv, mkv)


def _merge_route_kernel(osb_ref, osw_ref, om_ref, g_ref, x_ref, wsb_ref, wsw_ref, wm_ref,
                        wout_ref, lng_ref, lnb_ref, wrh_ref, wrl_ref, rbias_ref,
                        x1_ref, x1p_ref, idx_ref, rank_ref, wgt_ref, cnt_ref, x1_prev, carry_ref):
    i = pl.program_id(0)

    @pl.when(i == 0)
    def _():
        x1_prev[...] = jnp.zeros_like(x1_prev)
        carry_ref[...] = jnp.zeros_like(carry_ref)

    d = x_ref.shape[1]
    st = {}

    def branch(b, o_ref, w_ref):
        def run():
            term = g_ref[:, b * d:(b + 1) * d].astype(F32) * jnp.dot(
                o_ref[...], w_ref[...], preferred_element_type=F32)
            st["merged"] = term if b == 0 else st["merged"] + term
        return run

    def out_proj():
        st["y"] = jnp.dot(st["merged"].astype(BF16), wout_ref[...], preferred_element_type=F32)

    idx, rank, wgt, count = _route(
        x1_prev[...], wrh_ref[...], wrl_ref[...], rbias_ref[...], carry_ref[...],
        side_work=(branch(0, osb_ref, wsb_ref), branch(1, osw_ref, wsw_ref),
                   branch(2, om_ref, wm_ref), out_proj))
    x1 = _layer_norm(DEEPNORM_ALPHA * x_ref[...] + st["y"], lng_ref[...], lnb_ref[...])
    x1_ref[...] = x1
    x1p_ref[...] = _pack_bf16_pair(x1[:, :d // 2], x1[:, d // 2:])
    idx_ref[...] = idx
    rank_ref[...] = rank
    wgt_ref[...] = wgt
    carry_ref[...] = carry_ref[...] + jnp.where(i > 0, count, 0.0)
    cnt_ref[...] = carry_ref[...]
    x1_prev[...] = x1


def _merge_route(o_sb, o_sw, o_m, p, x2, w_sb, w_sw, w_m, w_out, ln_g, ln_b, wr_hi, wr_lo,
                 bias_col, tm):
    t, d = x2.shape
    n = t // tm
    cur = lambda i: (jnp.minimum(i, n - 1), 0)
    row = lambda w: pl.BlockSpec((tm, w), cur)
    slot = pl.BlockSpec((TOP_K, tm), lambda i: (0, jnp.maximum(i - 1, 0)))
    return pl.pallas_call(
        _merge_route_kernel,
        grid=(n + 1,),
        in_specs=[row(o_sb.shape[1]), row(o_sw.shape[1]), row(o_m.shape[1]),
                  pl.BlockSpec((tm, N_BRANCH * d), cur),
                  row(d),
                  _resident(w_sb.shape), _resident(w_sw.shape), _resident(w_m.shape),
                  _resident(w_out.shape), _resident(ln_g.shape), _resident(ln_b.shape),
                  _resident(wr_hi.shape), _resident(wr_lo.shape), _resident(bias_col.shape)],
        out_specs=[row(d), row(d // 2), slot, slot, slot,
                   pl.BlockSpec((N_EXPERTS, 1), lambda i: (0, 0))],
        out_shape=[jax.ShapeDtypeStruct((t, d), F32), jax.ShapeDtypeStruct((t, d // 2), U32),
                   jax.ShapeDtypeStruct((TOP_K, t), I32),
                   jax.ShapeDtypeStruct((TOP_K, t), I32),
                   jax.ShapeDtypeStruct((TOP_K, t), F32),
                   jax.ShapeDtypeStruct((N_EXPERTS, 1), F32)],
        scratch_shapes=[pltpu.VMEM((tm, d), F32), pltpu.VMEM((N_EXPERTS, 1), F32)],
        compiler_params=pltpu.CompilerParams(
            dimension_semantics=("arbitrary",), vmem_limit_bytes=VMEM_LIMIT),
        name="merge_route",
    )(o_sb, o_sw, o_m, p, x2, w_sb, w_sw, w_m, w_out, ln_g, ln_b, wr_hi, wr_lo, bias_col)


def _route(x, wh, wl, bias, carry, side_work=()):
    tr = x.shape[0]
    xh = x.astype(BF16)
    xl = (x - xh.astype(F32)).astype(BF16)
    logits = _nt_dot(wh, xh) + _nt_dot(wh, xl) + _nt_dot(wl, xh)
    scores = _sigmoid(logits)
    biased = scores + bias
    neg = jnp.float32(-jnp.inf)

    sub = lax.broadcasted_iota(I32, (GROUP_SIZE, tr), 0)
    gscore = []
    for g in range(N_GROUPS):
        blk = biased[g * GROUP_SIZE:(g + 1) * GROUP_SIZE, :]
        m1 = jnp.max(blk, axis=0, keepdims=True)
        i1 = jnp.min(jnp.where(blk == m1, sub, GROUP_SIZE), axis=0, keepdims=True)
        m2 = jnp.max(jnp.where(sub == i1, neg, blk), axis=0, keepdims=True)
        gscore.append(m1 + m2)
    gs = jnp.concatenate(gscore, axis=0)

    giota = lax.broadcasted_iota(I32, (N_GROUPS, tr), 0)
    gsel = jnp.zeros((N_GROUPS, tr), F32)
    for _ in range(TOPK_GROUPS):
        m = jnp.max(gs, axis=0, keepdims=True)
        gi = jnp.min(jnp.where(gs == m, giota, N_GROUPS), axis=0, keepdims=True)
        hit = giota == gi
        gsel = jnp.where(hit, 1.0, gsel)
        gs = jnp.where(hit, neg, gs)

    masked = jnp.concatenate(
        [jnp.where(gsel[g:g + 1, :] > 0.0, biased[g * GROUP_SIZE:(g + 1) * GROUP_SIZE, :], neg)
         for g in range(N_GROUPS)], axis=0)

    eiota = lax.broadcasted_iota(I32, (N_EXPERTS, tr), 0)
    sel = jnp.zeros((N_EXPERTS, tr), F32)
    idx_rows, w_rows = [], []
    side_work = list(side_work)
    for k in range(TOP_K):
        if side_work and k % 2 == 0:
            side_work.pop(0)()
        m = jnp.max(masked, axis=0, keepdims=True)
        ei = jnp.min(jnp.where(masked == m, eiota, N_EXPERTS), axis=0, keepdims=True)
        hit = eiota == ei
        idx_rows.append(ei)
        w_rows.append(jnp.sum(jnp.where(hit, scores, 0.0), axis=0, keepdims=True))
        sel = jnp.where(hit, 1.0, sel)
        masked = jnp.where(hit, neg, masked)

    wsum = w_rows[0]
    for wk in w_rows[1:]:
        wsum = wsum + wk
    wgt = jnp.concatenate(w_rows, axis=0) / wsum * ROUTED_SCALE
    idx = jnp.concatenate(idx_rows, axis=0)

    a = lax.broadcasted_iota(I32, (tr, tr), 0)
    b = lax.broadcasted_iota(I32, (tr, tr), 1)
    before = (a < b).astype(BF16)
    rank = jnp.dot(sel.astype(BF16), before, preferred_element_type=F32) + carry
    rank_rows = [jnp.sum(jnp.where(eiota == ei, rank, 0.0), axis=0, keepdims=True)
                 for ei in idx_rows]
    rank = jnp.concatenate(rank_rows, axis=0).astype(I32)
    return idx, rank, wgt, jnp.sum(sel, axis=1, keepdims=True)


def _dest_kernel(idx_ref, rank_ref, start_ref, dest_ref):
    tr = idx_ref.shape[1]
    eiota = lax.broadcasted_iota(I32, (N_EXPERTS, tr), 0)
    rows = []
    for k in range(TOP_K):
        hit = eiota == idx_ref[k:k + 1, :]
        rows.append(jnp.sum(jnp.where(hit, start_ref[...], 0), axis=0, keepdims=True))
    dest_ref[...] = jnp.concatenate(rows, axis=0) + rank_ref[...]


def _dest(idx, rank, start_col, tr):
    t = idx.shape[1]
    slot = pl.BlockSpec((TOP_K, tr), lambda i: (0, i))
    return pl.pallas_call(
        _dest_kernel,
        grid=(t // tr,),
        in_specs=[slot, slot, _resident(start_col.shape)],
        out_specs=slot,
        out_shape=jax.ShapeDtypeStruct((TOP_K, t), I32),
        compiler_params=pltpu.CompilerParams(dimension_semantics=("parallel",)),
        name="slot_dest",
    )(idx, rank, start_col)


def _sc_worker_id():
    return lax.axis_index("s") * SC_CORES + lax.axis_index("c")


def _dispatch(x1p, dest, n_rows):
    t, w = x1p.shape
    per = t // SC_WORKERS
    win = min(SC_INDEX_WINDOW, per)
    mesh = plsc.VectorSubcoreMesh(core_axis_name="c", subcore_axis_name="s")

    @functools.partial(
        pl.kernel, mesh=mesh,
        out_type=jax.ShapeDtypeStruct((n_rows, w), x1p.dtype),
        scratch_types=[pltpu.VMEM((TOP_K, win), I32),
                       pltpu.VMEM((win, w), x1p.dtype),
                       pltpu.SemaphoreType.DMA],
        name="sc_dispatch",
    )
    def scatter_rows(x_hbm, dest_hbm, xs_hbm, idx_v, rows_v, sem):
        base = _sc_worker_id() * per

        @pl.loop(0, per // win)
        def _(j):
            t0 = pl.multiple_of(base + j * win, win)
            pltpu.sync_copy(dest_hbm.at[:, pl.ds(t0, win)], idx_v)
            pltpu.sync_copy(x_hbm.at[pl.ds(t0, win)], rows_v)
            copies = [pltpu.async_copy(rows_v, xs_hbm.at[idx_v.at[k]], sem) for k in range(TOP_K)]
            for c in copies:
                c.wait()

    return scatter_rows(x1p, dest)


def _gather_rows(table, idx):
    n = idx.shape[0]
    w = table.shape[1]
    per = n // SC_WORKERS
    chunk = min(SC_GATHER_ROWS, per // 2)
    assert n % SC_WORKERS == 0 and per % (2 * chunk) == 0, (n, chunk)
    mesh = plsc.VectorSubcoreMesh(core_axis_name="c", subcore_axis_name="s")

    @functools.partial(
        pl.kernel, mesh=mesh,
        out_type=jax.ShapeDtypeStruct((n, w), table.dtype),
        scratch_types=[pltpu.VMEM((per,), I32),
                       pltpu.VMEM((2, chunk, w), table.dtype),
                       pltpu.SemaphoreType.DMA((2,)),
                       pltpu.SemaphoreType.DMA((2,))],
        name="sc_gather",
    )
    def gather_rows(table_hbm, idx_hbm, out_hbm, idx_v, rows_v, gather_sem, put_sem):
        base = _sc_worker_id() * per
        nchunks = per // chunk
        pltpu.sync_copy(idx_hbm.at[pl.ds(base, per)], idx_v)

        def gather(j, b):
            off = pl.multiple_of(j * chunk, chunk)
            return pltpu.make_async_copy(table_hbm.at[idx_v.at[pl.ds(off, chunk)]],
                                         rows_v.at[b], gather_sem.at[b])

        def put(j, b):
            off = pl.multiple_of(j * chunk, chunk)
            return pltpu.make_async_copy(rows_v.at[b], out_hbm.at[pl.ds(base + off, chunk)],
                                         put_sem.at[b])

        gather(0, 0).start()

        @pl.loop(0, nchunks, step=2)
        def _(j):
            for b in (0, 1):
                jj = j + b

                @pl.when(jj + 1 < nchunks)
                def _():
                    @pl.when(jj >= 1)
                    def _():
                        put(jj - 1, 1 - b).wait()
                    gather(jj + 1, 1 - b).start()

                gather(jj, b).wait()
                put(jj, b).start()

        put(nchunks - 2, 0).wait()
        put(nchunks - 1, 1).wait()

    return gather_rows(table, idx)


def _expert_kernel(first_ref, nblk_ref, total_ref, wgu_ref, wd_ref, xs_hbm, ys_hbm,
                   wgu_s, wd_s, xbuf, ybuf, in_sem, out_sem):
    e = pl.program_id(0)
    total = total_ref[0]
    n_in, rows, half = xbuf.shape
    n_out = ybuf.shape[0]
    ahead = n_in - 1

    def block_rows(g):
        return pl.ds(pl.multiple_of(g * rows, rows), rows)

    def load(g):
        slot = g % n_in
        return pltpu.make_async_copy(xs_hbm.at[block_rows(g), :], xbuf.at[slot], in_sem.at[slot])

    def store(g):
        slot = g % n_out
        return pltpu.make_async_copy(ybuf.at[slot], ys_hbm.at[block_rows(g), :], out_sem.at[slot])

    @pl.when(e == 0)
    def _():
        for g in range(ahead):
            @pl.when(g < total)
            def _(g=g):
                load(g).start()

    wgu_s[...] = wgu_ref[0].astype(BF16)
    wd_s[...] = wd_ref[0].astype(BF16)
    g0 = first_ref[e]
    ff = wd_s.shape[0]

    def body(j, carry):
        g = g0 + j
        load(g).wait()

        @pl.when(g + ahead < total)
        def _():
            load(g + ahead).start()

        @pl.when(g >= n_out)
        def _():
            store(g - n_out).wait()

        x_lo, x_hi = _unpack_bf16_pair(xbuf[g % n_in])
        h = (jnp.dot(x_lo.astype(BF16), wgu_s[:half, :], preferred_element_type=F32)
             + jnp.dot(x_hi.astype(BF16), wgu_s[half:, :], preferred_element_type=F32))
        gate, up = h[:, :ff], h[:, ff:]
        act = gate * _sigmoid(gate) * up
        y = jnp.dot(act.astype(BF16), wd_s[...], preferred_element_type=F32)
        ybuf[g % n_out] = _pack_bf16_pair(y[:, :half], y[:, half:])
        store(g).start()
        return carry

    lax.fori_loop(0, nblk_ref[e], body, 0)

    @pl.when(e == pl.num_programs(0) - 1)
    def _():
        for back in range(n_out, 0, -1):
            @pl.when(total >= back)
            def _(back=back):
                store(total - back).wait()


def _experts(xs, first_blk, n_blk, total_blk, w_gu, w_down):
    n_rows, half = xs.shape
    d = 2 * half
    n_exp, _, ff2 = w_gu.shape
    ff = w_down.shape[1]
    grid_spec = pltpu.PrefetchScalarGridSpec(
        num_scalar_prefetch=3,
        grid=(n_exp,),
        in_specs=[pl.BlockSpec((1, d, ff2), lambda e, *_: (e, 0, 0)),
                  pl.BlockSpec((1, ff, d), lambda e, *_: (e, 0, 0)),
                  pl.BlockSpec(memory_space=pl.ANY)],
        out_specs=pl.BlockSpec(memory_space=pl.ANY),
        scratch_shapes=[pltpu.VMEM((d, ff2), BF16), pltpu.VMEM((ff, d), BF16),
                        pltpu.VMEM((EXPERT_IN_SLOTS, EXPERT_ROWS, half), xs.dtype),
                        pltpu.VMEM((EXPERT_OUT_SLOTS, EXPERT_ROWS, half), xs.dtype),
                        pltpu.SemaphoreType.DMA((EXPERT_IN_SLOTS,)),
                        pltpu.SemaphoreType.DMA((EXPERT_OUT_SLOTS,))],
    )
    return pl.pallas_call(
        _expert_kernel,
        grid_spec=grid_spec,
        out_shape=jax.ShapeDtypeStruct((n_rows, half), xs.dtype),
        compiler_params=pltpu.CompilerParams(
            dimension_semantics=("arbitrary",), vmem_limit_bytes=VMEM_LIMIT),
        name="experts",
    )(first_blk, n_blk, total_blk, w_gu, w_down, xs)


def _combine_kernel(x1_ref, wt_ref, yg_ref, wsgu_ref, wsd_ref, lng_ref, lnb_ref, o_ref):
    tc = x1_ref.shape[0]
    x1 = x1_ref[...]
    ff = wsd_ref.shape[0]
    h = jnp.dot(x1.astype(BF16), wsgu_ref[...], preferred_element_type=F32)
    gate, up = h[:, :ff], h[:, ff:]
    act = gate * _sigmoid(gate) * up
    moe = jnp.dot(act.astype(BF16), wsd_ref[...], preferred_element_type=F32)

    half = yg_ref.shape[2]
    r_lo = jnp.zeros((tc, half), F32)
    r_hi = jnp.zeros((tc, half), F32)
    for k in range(TOP_K):
        y_lo, y_hi = _unpack_bf16_pair(yg_ref[k])
        w = wt_ref[:, k:k + 1]
        r_lo = r_lo + w * y_lo
        r_hi = r_hi + w * y_hi
    moe = moe + jnp.concatenate([r_lo, r_hi], axis=1)
    o_ref[...] = _layer_norm(DEEPNORM_ALPHA * x1 + moe, lng_ref[...], lnb_ref[...])


def _combine(x1, wgt_t, yg, chunk, ws_gu, ws_down, ln_g, ln_b, tc):
    t, d = x1.shape
    steps = yg.shape[1] // tc
    tok = lambda i: (chunk * steps + i, 0)
    return pl.pallas_call(
        _combine_kernel,
        grid=(steps,),
        in_specs=[pl.BlockSpec((tc, d), tok),
                  pl.BlockSpec((tc, TOP_K), tok),
                  pl.BlockSpec((TOP_K, tc, yg.shape[2]), lambda i: (0, i, 0)),
                  _resident(ws_gu.shape), _resident(ws_down.shape),
                  _resident(ln_g.shape), _resident(ln_b.shape)],
        out_specs=pl.BlockSpec((tc, d), tok),
        out_shape=jax.ShapeDtypeStruct((t, d), F32),
        input_output_aliases={0: 0},
        compiler_params=pltpu.CompilerParams(
            dimension_semantics=("arbitrary",), vmem_limit_bytes=VMEM_LIMIT),
        name="combine_ln2",
    )(x1, wgt_t, yg, ws_gu, ws_down, ln_g, ln_b)


def _fused_in_weights(w_in):
    d = w_in.shape[0]
    sizes = (SB_HEADS * HEAD_DIM,) * 3 + (SWA_HEADS * HEAD_DIM, SWA_KV_HEADS * HEAD_DIM,
                                          SWA_KV_HEADS * HEAD_DIM, MEM_HEADS * MEM_HEAD_DIM)
    parts, off = [], 0
    for s in sizes:
        parts.append(w_in[:, off:off + s])
        off += s
    q_sb, k_sb, v_sb, q_sw, k_sw, v_sw, q_m = parts
    gates = w_in[:, off:]
    scale = HEAD_DIM ** -0.5
    q_sw = (q_sw * scale).reshape(d, SWA_KV_HEADS, SWA_GROUP, HEAD_DIM)
    zeros = jnp.zeros((d, SWA_GROUP, HEAD_DIM), w_in.dtype)
    q_sw = jnp.stack([jnp.concatenate([q_sw[:, 0], zeros], axis=-1),
                      jnp.concatenate([zeros, q_sw[:, 1]], axis=-1)], axis=1)
    q_sw = q_sw.reshape(d, SWA_HEADS * LANES)
    used = W_GATE + 3 * SB_HEADS * HEAD_DIM + SWA_HEADS * LANES + q_m.shape[1] + 2 * k_sw.shape[1]
    pad = jnp.zeros((d, PROJ_COLS - used), w_in.dtype)
    return jnp.concatenate([gates, q_sw, q_sb * scale, k_sb, v_sb, q_m, k_sw, v_sw, pad],
                           axis=1).astype(BF16)


def _padded_swa_out_weights(w_o_swa):
    d = w_o_swa.shape[1]
    w = w_o_swa.reshape(SWA_KV_HEADS, SWA_GROUP, HEAD_DIM, d)
    zeros = jnp.zeros((SWA_GROUP, HEAD_DIM, d), w_o_swa.dtype)
    w = jnp.stack([jnp.concatenate([w[0], zeros], axis=1),
                   jnp.concatenate([zeros, w[1]], axis=1)], axis=0)
    return w.reshape(SWA_HEADS * LANES, d).astype(BF16)


def kernel(x, mem, w_in, b_gate, w_mem_kv, sinks, w_o_sb, w_o_swa, w_o_mem, w_out,
           ln1_g, ln1_b, w_router, router_bias, w_e_gu, w_e_down, w_s_gu, w_s_down,
           ln2_g, ln2_b):
    batch, seq, d = x.shape
    mem_len = mem.shape[1]
    t = batch * seq
    x2 = x.reshape(t, d)
    row_tile = min(512, t)

    p = _proj(x2, _fused_in_weights(w_in), b_gate.reshape(1, -1), row_tile)
    mkv = _matmul_bf16(mem.reshape(batch * mem_len, d), w_mem_kv.astype(BF16), mem_len)
    o_sb = _sb_attention(p, batch, seq, min(256, seq), SB_PAIRS_PER_STEP)
    o_sw = _swa_attention(p, _swa_tables(sinks), batch, seq)
    o_m = _mem_attention(p, mkv, batch, seq, mem_len, min(512, seq))
    wr_t = w_router.T
    wr_hi = wr_t.astype(BF16)
    wr_lo = (wr_t - wr_hi.astype(F32)).astype(BF16)
    x1, x1p, idx, rank, wgt, cnt = _merge_route(
        o_sb, o_sw, o_m, p, x2, w_o_sb.astype(BF16), _padded_swa_out_weights(w_o_swa),
        w_o_mem.astype(BF16), w_out.astype(BF16), ln1_g.reshape(1, d), ln1_b.reshape(1, d),
        wr_hi, wr_lo, router_bias.reshape(-1, 1).astype(F32), row_tile)

    out = _moe_ln(x1, x1p, idx, rank, wgt, cnt, w_e_gu, w_e_down, w_s_gu, w_s_down, ln2_g, ln2_b)
    return out.reshape(batch, seq, d)


def _moe_ln(x1, x1p, idx, rank, wgt, cnt, w_e_gu, w_e_down, w_s_gu, w_s_down, ln2_g, ln2_b):
    t, d = x1.shape
    counts = cnt[:, 0].astype(I32)
    padded = (counts + EXPERT_ROWS - 1) // EXPERT_ROWS * EXPERT_ROWS
    pad_end = jnp.cumsum(padded)
    pad_start = pad_end - padded
    n_blocks = t * TOP_K // EXPERT_ROWS + N_EXPERTS
    dest = _dest(idx, rank, pad_start.reshape(-1, 1), min(2048, t))

    xs = _dispatch(x1p, dest, n_blocks * EXPERT_ROWS)
    ys = _experts(xs, pad_start // EXPERT_ROWS, padded // EXPERT_ROWS,
                  pad_end[-1:] // EXPERT_ROWS, w_e_gu, w_e_down)
    tchunk = t // COMBINE_CHUNKS
    wgt_t = wgt.T
    ws_gu, ws_down = w_s_gu.astype(BF16), w_s_down.astype(BF16)
    out = x1
    for c in range(COMBINE_CHUNKS):
        slots = dest[:, c * tchunk:(c + 1) * tchunk].reshape(-1)
        yg = _gather_rows(ys, slots).reshape(TOP_K, tchunk, -1)
        out = _combine(out, wgt_t, yg, c, ws_gu, ws_down, ln2_g.reshape(1, d), ln2_b.reshape(1, d),
                       min(256, tchunk))
    return out
```

```python
import functools

import jax
import jax.numpy as jnp
from jax import lax
from jax.experimental import pallas as pl
from jax.experimental.pallas import tpu as pltpu
from jax.experimental.pallas import tpu_sc as plsc

F32 = jnp.float32
BF16 = jnp.bfloat16
I32 = jnp.int32
U32 = jnp.uint32

HEAD_DIM = 64
SB_HEADS = 8
SWA_HEADS = 8
SWA_KV_HEADS = 2
SWA_GROUP = SWA_HEADS // SWA_KV_HEADS
SWA_WINDOW = 128
MEM_HEADS = 4
MEM_HEAD_DIM = 128
N_BRANCH = 3
N_EXPERTS = 256
TOP_K = 8
N_GROUPS = 8
GROUP_SIZE = N_EXPERTS // N_GROUPS
TOPK_GROUPS = 4
EXPERT_FF = 256
SHARED_FF = 256
ROUTED_SCALE = 2.5
LN_EPS = 1e-5
DEPTH = 1
DEEPNORM_ALPHA = (2 * DEPTH) ** 0.25

LANES = 128
SC_CORES = 2
SC_SUBCORES = 16
SC_WORKERS = SC_CORES * SC_SUBCORES
SC_INDEX_WINDOW = 128
SC_GATHER_ROWS = 64
VMEM_LIMIT = 56 * 1024 * 1024

D_GATE = 0
W_GATE = 3072
C_QSW = 3072
C_QSB = 4096
C_KSB = 4608
C_VSB = 5120
C_QM = 5632
C_KSW = 6144
C_VSW = 6272
PROJ_COLS = 6400
PROJ_CHUNK = 256

SB_SKIP = 110.0
SB_PAIRS_PER_STEP = 4

EXPERT_ROWS = 512
EXPERT_IN_SLOTS = 4
EXPERT_OUT_SLOTS = 2
COMBINE_CHUNKS = 4


def _nt_dot(a, b):
    return lax.dot_general(a, b, (((1,), (1,)), ((), ())), preferred_element_type=F32)


def _sigmoid(x):
    return 1.0 / (1.0 + jnp.exp(-x))


def _layer_norm(h, g, b):
    mu = jnp.mean(h, axis=-1, keepdims=True)
    d = h - mu
    var = jnp.mean(d * d, axis=-1, keepdims=True)
    return d * lax.rsqrt(var + LN_EPS) * g + b


def _pack_bf16_pair(a, b):
    a_bits = lax.bitcast_convert_type(a.astype(BF16).astype(F32), U32)
    b_bits = lax.bitcast_convert_type(b.astype(BF16).astype(F32), U32)
    return (a_bits >> 16) | b_bits


def _unpack_bf16_pair(w):
    a = lax.bitcast_convert_type(w << 16, F32)
    b = lax.bitcast_convert_type(w & jnp.uint32(0xFFFF0000), F32)
    return a, b


def _resident(shape):
    nd = len(shape)
    return pl.BlockSpec(shape, lambda *_: (0,) * nd, pipeline_mode=pl.Buffered(1))


def _proj_kernel(x_ref, w_ref, b_ref, o_ref, *, gate_cols):
    xb = x_ref[...].astype(BF16)
    for j in range(o_ref.shape[1] // PROJ_CHUNK):
        cols = slice(j * PROJ_CHUNK, (j + 1) * PROJ_CHUNK)
        acc = jnp.dot(xb, w_ref[:, cols], preferred_element_type=F32)
        if (j + 1) * PROJ_CHUNK <= gate_cols:
            acc = _sigmoid(acc + b_ref[:, cols])
        o_ref[:, cols] = acc.astype(o_ref.dtype)


def _proj(x2, w_all, b_gate, tm):
    t, d = x2.shape
    n = w_all.shape[1]
    return pl.pallas_call(
        functools.partial(_proj_kernel, gate_cols=b_gate.shape[1]),
        grid=(t // tm,),
        in_specs=[pl.BlockSpec((tm, d), lambda i: (i, 0)),
                  _resident((d, n)),
                  _resident(b_gate.shape)],
        out_specs=pl.BlockSpec((tm, n), lambda i: (i, 0)),
        out_shape=jax.ShapeDtypeStruct((t, n), BF16),
        compiler_params=pltpu.CompilerParams(
            dimension_semantics=("parallel",), vmem_limit_bytes=VMEM_LIMIT),
        name="in_proj",
    )(x2, w_all, b_gate)


def _mm_kernel(x_ref, w_ref, o_ref):
    o_ref[...] = jnp.dot(x_ref[...].astype(BF16), w_ref[...],
                         preferred_element_type=F32).astype(o_ref.dtype)


def _matmul_bf16(x2, w, tm):
    t, d = x2.shape
    n = w.shape[1]
    return pl.pallas_call(
        _mm_kernel,
        grid=(t // tm,),
        in_specs=[pl.BlockSpec((tm, d), lambda i: (i, 0)), _resident((d, n))],
        out_specs=pl.BlockSpec((tm, n), lambda i: (i, 0)),
        out_shape=jax.ShapeDtypeStruct((t, n), BF16),
        compiler_params=pltpu.CompilerParams(
            dimension_semantics=("parallel",), vmem_limit_bytes=VMEM_LIMIT),
        name="mem_kv_proj",
    )(x2, w)


def _sb_kernel(q_ref, k_ref, v_ref, o_ref, *, tq):
    i = pl.program_id(2)
    pairs = q_ref.shape[1] // LANES
    lane = lax.broadcasted_iota(I32, (1, LANES), 1)
    r = lax.broadcasted_iota(I32, (tq, tq), 0)
    c = lax.broadcasted_iota(I32, (tq, tq), 1)
    tri = (r >= c).astype(BF16)
    causal = c < r
    nh = LANES // HEAD_DIM
    hmasks = [(lane >= h * HEAD_DIM) & (lane < (h + 1) * HEAD_DIM) for h in range(nh)]
    qs = []
    for p in range(pairs):
        q = q_ref[:, p * LANES:(p + 1) * LANES]
        qs.append(jnp.concatenate([jnp.where(hm, q, jnp.zeros_like(q)) for hm in hmasks], axis=0))
    causal2 = jnp.concatenate([causal] * nh, axis=0)
    tri2 = jnp.concatenate([tri, tri], axis=0)
    mp = nh * tq

    def block(kb, carry, acc, diag):
        rows = pl.ds(pl.multiple_of(kb * tq, tq), tq)
        z, hl, suffix, ab, av = {}, {}, {}, {}, {}

        def scores(p):
            z[p] = _nt_dot(qs[p], k_ref[rows, p * LANES:(p + 1) * LANES])

        def softplus_split(p):
            sp = jnp.maximum(z[p], 0.0) + jnp.log(1.0 + jnp.exp(-jnp.abs(z[p])))
            if diag:
                sp = jnp.where(causal2, sp, 0.0)
            hi = sp.astype(BF16)
            lo = (sp - hi.astype(F32)).astype(BF16)
            hl[p] = jnp.concatenate([hi, lo], axis=1)

        def cumsum(p):
            suffix[p] = jnp.dot(hl[p], tri2, preferred_element_type=F32)

        def weights(p):
            a = jnp.exp((z[p] - carry[p]) - suffix[p])
            if diag:
                a = jnp.where(causal2, a, 0.0)
            ab[p] = a.astype(BF16)

        def values(p):
            av[p] = jnp.dot(ab[p], v_ref[rows, p * LANES:(p + 1) * LANES],
                            preferred_element_type=F32)

        stages = (scores, softplus_split, cumsum, weights, values)
        for t in range(pairs + len(stages) - 1):
            for s in reversed(range(len(stages))):
                if 0 <= t - s < pairs:
                    stages[s](t - s)
        return ([carry[p] + suffix[p][:, 0:1] for p in range(pairs)],
                [acc[p] + av[p] for p in range(pairs)])

    carry, acc = block(i, [jnp.zeros((mp, 1), F32)] * pairs,
                       [jnp.zeros((mp, LANES), F32)] * pairs, True)

    def cond(s):
        kb, carry, _ = s
        lowest = carry[0]
        for cp in carry[1:]:
            lowest = jnp.minimum(lowest, cp)
        return (kb >= 0) & (jnp.min(lowest) < SB_SKIP)

    def body(s):
        kb, carry, acc = s
        carry, acc = block(kb, carry, acc, False)
        return kb - 1, carry, acc

    _, _, acc = lax.while_loop(cond, body, (i - 1, carry, acc))
    for p in range(pairs):
        o_ref[:, p * LANES:(p + 1) * LANES] = jnp.where(
            hmasks[0], acc[p][:tq], acc[p][tq:]).astype(o_ref.dtype)


def _sb_attention(p, batch, seq, tq, pairs):
    t = batch * seq
    nq = seq // tq
    w = pairs * LANES
    ngrp = SB_HEADS * HEAD_DIM // w
    qc, kc, vc = C_QSB // w, C_KSB // w, C_VSB // w
    return pl.pallas_call(
        functools.partial(_sb_kernel, tq=tq),
        grid=(batch, ngrp, nq),
        in_specs=[pl.BlockSpec((tq, w), lambda b, h, i: (b * nq + i, qc + h)),
                  pl.BlockSpec((seq, w), lambda b, h, i: (b, kc + h)),
                  pl.BlockSpec((seq, w), lambda b, h, i: (b, vc + h))],
        out_specs=pl.BlockSpec((tq, w), lambda b, h, i: (b * nq + i, h)),
        out_shape=jax.ShapeDtypeStruct((t, SB_HEADS * HEAD_DIM), BF16),
        compiler_params=pltpu.CompilerParams(
            dimension_semantics=("parallel", "parallel", "arbitrary"),
            vmem_limit_bytes=VMEM_LIMIT),
        name="sb_attention",
    )(p, p, p)


def _swa_kernel(hp_ref, q_ref, kp_ref, k0_ref, k1_ref, vp_ref, v0_ref, v1_ref, o_ref):
    j = pl.program_id(1)
    blk = k0_ref.shape[0]
    nheads = q_ref.shape[1] // LANES
    sink = hp_ref[2]
    neg = jnp.float32(-jnp.inf)
    lane = lax.broadcasted_iota(I32, (1, LANES), 1)
    blocks = ((kp_ref, k0_ref, vp_ref, v0_ref), (k0_ref, k1_ref, v0_ref, v1_ref))
    for half, (kprev, kcur, vprev, vcur) in enumerate(blocks):
        rows = slice(half * blk, (half + 1) * blk)
        qs = jnp.concatenate([q_ref[rows, g * LANES:(g + 1) * LANES] for g in range(nheads)],
                             axis=0)
        zp = _nt_dot(qs, kprev[...]) + hp_ref[0]
        if half == 0:
            zp = jnp.where(j > 0, zp, neg)
        zc = _nt_dot(qs, kcur[...]) + hp_ref[1]
        m = jnp.maximum(jnp.max(jnp.maximum(zp, zc), axis=1, keepdims=True), sink)
        pp = jnp.exp(zp - m)
        pc = jnp.exp(zc - m)
        den = jnp.sum(pp + pc, axis=1, keepdims=True) + jnp.exp(sink - m)
        o = (jnp.dot(pp.astype(BF16), vprev[...], preferred_element_type=F32)
             + jnp.dot(pc.astype(BF16), vcur[...], preferred_element_type=F32)) / den
        for g in range(nheads):
            kv = g // SWA_GROUP
            kvmask = (lane >= kv * HEAD_DIM) & (lane < (kv + 1) * HEAD_DIM)
            o_ref[rows, g * LANES:(g + 1) * LANES] = jnp.where(
                kvmask, o[g * blk:(g + 1) * blk], 0.0).astype(o_ref.dtype)


def _swa_tables(sinks):
    w = SWA_WINDOW
    slopes = jnp.exp2(-8.0 * jnp.arange(1, SWA_HEADS + 1, dtype=F32) / SWA_HEADS)[:, None, None]
    r = jnp.arange(w)[:, None]
    c = jnp.arange(w)[None, :]
    dist = (r - c).astype(F32)[None]
    neg = jnp.float32(-jnp.inf)
    bias_c = jnp.where((c <= r)[None], -slopes * dist, neg)
    bias_p = jnp.where((c > r)[None], -slopes * (dist + w), neg)
    sink = jnp.broadcast_to(sinks.astype(F32)[:, None, None], (SWA_HEADS, w, w))
    return jnp.stack([bias_p, bias_c, sink]).reshape(3, SWA_HEADS * w, w)


def _swa_attention(p, row_params, batch, seq):
    blk = SWA_WINDOW
    t = batch * seq
    nb = seq // blk
    npair = nb // 2
    qw = SWA_HEADS * LANES
    qc, kc, vc = C_QSW // qw, C_KSW // LANES, C_VSW // LANES
    kblock = lambda off, col: pl.BlockSpec(
        (blk, LANES), lambda b, j: (b * nb + jnp.maximum(2 * j + off, 0), col))
    return pl.pallas_call(
        _swa_kernel,
        grid=(batch, npair),
        in_specs=[_resident(row_params.shape),
                  pl.BlockSpec((2 * blk, qw), lambda b, j: (b * npair + j, qc)),
                  kblock(-1, kc), kblock(0, kc), kblock(1, kc),
                  kblock(-1, vc), kblock(0, vc), kblock(1, vc)],
        out_specs=pl.BlockSpec((2 * blk, qw), lambda b, j: (b * npair + j, 0)),
        out_shape=jax.ShapeDtypeStruct((t, qw), BF16),
        compiler_params=pltpu.CompilerParams(
            dimension_semantics=("parallel", "arbitrary"),
            vmem_limit_bytes=VMEM_LIMIT),
        name="swa_attention",
    )(row_params, p, p, p, p, p, p, p)


def _mem_kernel(q_ref, mk_ref, mv_ref, o_ref):
    scale = MEM_HEAD_DIM ** -0.5
    for h in range(MEM_HEADS):
        cols = slice(h * MEM_HEAD_DIM, (h + 1) * MEM_HEAD_DIM)
        z = _nt_dot(q_ref[:, cols], mk_ref[:, cols]) * scale
        m = jnp.max(z, axis=1, keepdims=True)
        p = jnp.exp(z - m)
        den = jnp.sum(p, axis=1, keepdims=True)
        o = jnp.dot(p.astype(BF16), mv_ref[:, cols], preferred_element_type=F32) / den
        o_ref[:, cols] = o.astype(o_ref.dtype)


def _mem_attention(p, mkv, batch, seq, mem_len, tq):
    t = batch * seq
    nq = seq // tq
    w = MEM_HEADS * MEM_HEAD_DIM
    return pl.pallas_call(
        _mem_kernel,
        grid=(batch, nq),
        in_specs=[pl.BlockSpec((tq, w), lambda b, i: (b * nq + i, C_QM // w)),
                  pl.BlockSpec((mem_len, w), lambda b, i: (b, 0)),
                  pl.BlockSpec((mem_len, w), lambda b, i: (b, 1))],
        out_specs=pl.BlockSpec((tq, w), lambda b, i: (b * nq + i, 0)),
        out_shape=jax.ShapeDtypeStruct((t, w), BF16),
        compiler_params=pltpu.CompilerParams(
            dimension_semantics=("parallel", "arbitrary"), vmem_limit_bytes=VMEM_LIMIT),
        name="mem_attention",
    )(p, mkv, mkv)


def _merge_route_kernel(osb_ref, osw_ref, om_ref, g_ref, x_ref, wsb_ref, wsw_ref, wm_ref,
                        wout_ref, lng_ref, lnb_ref, wrh_ref, wrl_ref, rbias_ref,
                        x1_ref, x1p_ref, idx_ref, rank_ref, wgt_ref, cnt_ref, x1_prev, carry_ref):
    i = pl.program_id(0)

    @pl.when(i == 0)
    def _():
        x1_prev[...] = jnp.zeros_like(x1_prev)
        carry_ref[...] = jnp.zeros_like(carry_ref)

    d = x_ref.shape[1]
    st = {}

    def branch(b, o_ref, w_ref):
        def run():
            term = g_ref[:, b * d:(b + 1) * d].astype(F32) * jnp.dot(
                o_ref[...], w_ref[...], preferred_element_type=F32)
            st["merged"] = term if b == 0 else st["merged"] + term
        return run

    def out_proj():
        st["y"] = jnp.dot(st["merged"].astype(BF16), wout_ref[...], preferred_element_type=F32)

    idx, rank, wgt, count = _route(
        x1_prev[...], wrh_ref[...], wrl_ref[...], rbias_ref[...], carry_ref[...],
        side_work=(branch(0, osb_ref, wsb_ref), branch(1, osw_ref, wsw_ref),
                   branch(2, om_ref, wm_ref), out_proj))
    x1 = _layer_norm(DEEPNORM_ALPHA * x_ref[...] + st["y"], lng_ref[...], lnb_ref[...])
    x1_ref[...] = x1
    x1p_ref[...] = _pack_bf16_pair(x1[:, :d // 2], x1[:, d // 2:])
    idx_ref[...] = idx
    rank_ref[...] = rank
    wgt_ref[...] = wgt
    carry_ref[...] = carry_ref[...] + jnp.where(i > 0, count, 0.0)
    cnt_ref[...] = carry_ref[...]
    x1_prev[...] = x1


def _merge_route(o_sb, o_sw, o_m, p, x2, w_sb, w_sw, w_m, w_out, ln_g, ln_b, wr_hi, wr_lo,
                 bias_col, tm):
    t, d = x2.shape
    n = t // tm
    cur = lambda i: (jnp.minimum(i, n - 1), 0)
    row = lambda w: pl.BlockSpec((tm, w), cur)
    slot = pl.BlockSpec((TOP_K, tm), lambda i: (0, jnp.maximum(i - 1, 0)))
    return pl.pallas_call(
        _merge_route_kernel,
        grid=(n + 1,),
        in_specs=[row(o_sb.shape[1]), row(o_sw.shape[1]), row(o_m.shape[1]),
                  pl.BlockSpec((tm, N_BRANCH * d), cur),
                  row(d),
                  _resident(w_sb.shape), _resident(w_sw.shape), _resident(w_m.shape),
                  _resident(w_out.shape), _resident(ln_g.shape), _resident(ln_b.shape),
                  _resident(wr_hi.shape), _resident(wr_lo.shape), _resident(bias_col.shape)],
        out_specs=[row(d), row(d // 2), slot, slot, slot,
                   pl.BlockSpec((N_EXPERTS, 1), lambda i: (0, 0))],
        out_shape=[jax.ShapeDtypeStruct((t, d), F32), jax.ShapeDtypeStruct((t, d // 2), U32),
                   jax.ShapeDtypeStruct((TOP_K, t), I32),
                   jax.ShapeDtypeStruct((TOP_K, t), I32),
                   jax.ShapeDtypeStruct((TOP_K, t), F32),
                   jax.ShapeDtypeStruct((N_EXPERTS, 1), F32)],
        scratch_shapes=[pltpu.VMEM((tm, d), F32), pltpu.VMEM((N_EXPERTS, 1), F32)],
        compiler_params=pltpu.CompilerParams(
            dimension_semantics=("arbitrary",), vmem_limit_bytes=VMEM_LIMIT),
        name="merge_route",
    )(o_sb, o_sw, o_m, p, x2, w_sb, w_sw, w_m, w_out, ln_g, ln_b, wr_hi, wr_lo, bias_col)


def _route(x, wh, wl, bias, carry, side_work=()):
    tr = x.shape[0]
    xh = x.astype(BF16)
    xl = (x - xh.astype(F32)).astype(BF16)
    logits = _nt_dot(wh, xh) + _nt_dot(wh, xl) + _nt_dot(wl, xh)
    scores = _sigmoid(logits)
    biased = scores + bias
    neg = jnp.float32(-jnp.inf)

    sub = lax.broadcasted_iota(I32, (GROUP_SIZE, tr), 0)
    gscore = []
    for g in range(N_GROUPS):
        blk = biased[g * GROUP_SIZE:(g + 1) * GROUP_SIZE, :]
        m1 = jnp.max(blk, axis=0, keepdims=True)
        i1 = jnp.min(jnp.where(blk == m1, sub, GROUP_SIZE), axis=0, keepdims=True)
        m2 = jnp.max(jnp.where(sub == i1, neg, blk), axis=0, keepdims=True)
        gscore.append(m1 + m2)
    gs = jnp.concatenate(gscore, axis=0)

    giota = lax.broadcasted_iota(I32, (N_GROUPS, tr), 0)
    gsel = jnp.zeros((N_GROUPS, tr), F32)
    for _ in range(TOPK_GROUPS):
        m = jnp.max(gs, axis=0, keepdims=True)
        gi = jnp.min(jnp.where(gs == m, giota, N_GROUPS), axis=0, keepdims=True)
        hit = giota == gi
        gsel = jnp.where(hit, 1.0, gsel)
        gs = jnp.where(hit, neg, gs)

    masked = jnp.concatenate(
        [jnp.where(gsel[g:g + 1, :] > 0.0, biased[g * GROUP_SIZE:(g + 1) * GROUP_SIZE, :], neg)
         for g in range(N_GROUPS)], axis=0)

    eiota = lax.broadcasted_iota(I32, (N_EXPERTS, tr), 0)
    sel = jnp.zeros((N_EXPERTS, tr), F32)
    idx_rows, w_rows = [], []
    side_work = list(side_work)
    for k in range(TOP_K):
        if side_work and k % 2 == 0:
            side_work.pop(0)()
        m = jnp.max(masked, axis=0, keepdims=True)
        ei = jnp.min(jnp.where(masked == m, eiota, N_EXPERTS), axis=0, keepdims=True)
        hit = eiota == ei
        idx_rows.append(ei)
        w_rows.append(jnp.sum(jnp.where(hit, scores, 0.0), axis=0, keepdims=True))
        sel = jnp.where(hit, 1.0, sel)
        masked = jnp.where(hit, neg, masked)

    wsum = w_rows[0]
    for wk in w_rows[1:]:
        wsum = wsum + wk
    wgt = jnp.concatenate(w_rows, axis=0) / wsum * ROUTED_SCALE
    idx = jnp.concatenate(idx_rows, axis=0)

    a = lax.broadcasted_iota(I32, (tr, tr), 0)
    b = lax.broadcasted_iota(I32, (tr, tr), 1)
    before = (a < b).astype(BF16)
    rank = jnp.dot(sel.astype(BF16), before, preferred_element_type=F32) + carry
    rank_rows = [jnp.sum(jnp.where(eiota == ei, rank, 0.0), axis=0, keepdims=True)
                 for ei in idx_rows]
    rank = jnp.concatenate(rank_rows, axis=0).astype(I32)
    return idx, rank, wgt, jnp.sum(sel, axis=1, keepdims=True)


def _dest_kernel(idx_ref, rank_ref, start_ref, dest_ref):
    tr = idx_ref.shape[1]
    eiota = lax.broadcasted_iota(I32, (N_EXPERTS, tr), 0)
    rows = []
    for k in range(TOP_K):
        hit = eiota == idx_ref[k:k + 1, :]
        rows.append(jnp.sum(jnp.where(hit, start_ref[...], 0), axis=0, keepdims=True))
    dest_ref[...] = jnp.concatenate(rows, axis=0) + rank_ref[...]


def _dest(idx, rank, start_col, tr):
    t = idx.shape[1]
    slot = pl.BlockSpec((TOP_K, tr), lambda i: (0, i))
    return pl.pallas_call(
        _dest_kernel,
        grid=(t // tr,),
        in_specs=[slot, slot, _resident(start_col.shape)],
        out_specs=slot,
        out_shape=jax.ShapeDtypeStruct((TOP_K, t), I32),
        compiler_params=pltpu.CompilerParams(dimension_semantics=("parallel",)),
        name="slot_dest",
    )(idx, rank, start_col)


def _sc_worker_id():
    return lax.axis_index("s") * SC_CORES + lax.axis_index("c")


def _dispatch(x1p, dest, n_rows):
    t, w = x1p.shape
    per = t // SC_WORKERS
    win = min(SC_INDEX_WINDOW, per)
    mesh = plsc.VectorSubcoreMesh(core_axis_name="c", subcore_axis_name="s")

    @functools.partial(
        pl.kernel, mesh=mesh,
        out_type=jax.ShapeDtypeStruct((n_rows, w), x1p.dtype),
        scratch_types=[pltpu.VMEM((TOP_K, win), I32),
                       pltpu.VMEM((win, w), x1p.dtype),
                       pltpu.SemaphoreType.DMA],
        name="sc_dispatch",
    )
    def scatter_rows(x_hbm, dest_hbm, xs_hbm, idx_v, rows_v, sem):
        base = _sc_worker_id() * per

        @pl.loop(0, per // win)
        def _(j):
            t0 = pl.multiple_of(base + j * win, win)
            pltpu.sync_copy(dest_hbm.at[:, pl.ds(t0, win)], idx_v)
            pltpu.sync_copy(x_hbm.at[pl.ds(t0, win)], rows_v)
            copies = [pltpu.async_copy(rows_v, xs_hbm.at[idx_v.at[k]], sem) for k in range(TOP_K)]
            for c in copies:
                c.wait()

    return scatter_rows(x1p, dest)


def _gather_rows(table, idx):
    n = idx.shape[0]
    w = table.shape[1]
    per = n // SC_WORKERS
    chunk = min(SC_GATHER_ROWS, per // 2)
    assert n % SC_WORKERS == 0 and per % (2 * chunk) == 0, (n, chunk)
    mesh = plsc.VectorSubcoreMesh(core_axis_name="c", subcore_axis_name="s")

    @functools.partial(
        pl.kernel, mesh=mesh,
        out_type=jax.ShapeDtypeStruct((n, w), table.dtype),
        scratch_types=[pltpu.VMEM((per,), I32),
                       pltpu.VMEM((2, chunk, w), table.dtype),
                       pltpu.SemaphoreType.DMA((2,)),
                       pltpu.SemaphoreType.DMA((2,))],
        name="sc_gather",
    )
    def gather_rows(table_hbm, idx_hbm, out_hbm, idx_v, rows_v, gather_sem, put_sem):
        base = _sc_worker_id() * per
        nchunks = per // chunk
        pltpu.sync_copy(idx_hbm.at[pl.ds(base, per)], idx_v)

        def gather(j, b):
            off = pl.multiple_of(j * chunk, chunk)
            return pltpu.make_async_copy(table_hbm.at[idx_v.at[pl.ds(off, chunk)]],
                                         rows_v.at[b], gather_sem.at[b])

        def put(j, b):
            off = pl.multiple_of(j * chunk, chunk)
            return pltpu.make_async_copy(rows_v.at[b], out_hbm.at[pl.ds(base + off, chunk)],
                                         put_sem.at[b])

        gather(0, 0).start()

        @pl.loop(0, nchunks, step=2)
        def _(j):
            for b in (0, 1):
                jj = j + b

                @pl.when(jj + 1 < nchunks)
                def _():
                    @pl.when(jj >= 1)
                    def _():
                        put(jj - 1, 1 - b).wait()
                    gather(jj + 1, 1 - b).start()

                gather(jj, b).wait()
                put(jj, b).start()

        put(nchunks - 2, 0).wait()
        put(nchunks - 1, 1).wait()

    return gather_rows(table, idx)


def _expert_kernel(first_ref, nblk_ref, total_ref, wgu_ref, wd_ref, xs_hbm, ys_hbm,
                   wgu_s, wd_s, xbuf, ybuf, in_sem, out_sem):
    e = pl.program_id(0)
    total = total_ref[0]
    n_in, rows, half = xbuf.shape
    n_out = ybuf.shape[0]
    ahead = n_in - 1

    def block_rows(g):
        return pl.ds(pl.multiple_of(g * rows, rows), rows)

    def load(g):
        slot = g % n_in
        return pltpu.make_async_copy(xs_hbm.at[block_rows(g), :], xbuf.at[slot], in_sem.at[slot])

    def store(g):
        slot = g % n_out
        return pltpu.make_async_copy(ybuf.at[slot], ys_hbm.at[block_rows(g), :], out_sem.at[slot])

    @pl.when(e == 0)
    def _():
        for g in range(ahead):
            @pl.when(g < total)
            def _(g=g):
                load(g).start()

    wgu_s[...] = wgu_ref[0].astype(BF16)
    wd_s[...] = wd_ref[0].astype(BF16)
    g0 = first_ref[e]
    ff = wd_s.shape[0]

    def body(j, carry):
        g = g0 + j
        load(g).wait()

        @pl.when(g + ahead < total)
        def _():
            load(g + ahead).start()

        @pl.when(g >= n_out)
        def _():
            store(g - n_out).wait()

        x_lo, x_hi = _unpack_bf16_pair(xbuf[g % n_in])
        h = (jnp.dot(x_lo.astype(BF16), wgu_s[:half, :], preferred_element_type=F32)
             + jnp.dot(x_hi.astype(BF16), wgu_s[half:, :], preferred_element_type=F32))
        gate, up = h[:, :ff], h[:, ff:]
        act = gate * _sigmoid(gate) * up
        y = jnp.dot(act.astype(BF16), wd_s[...], preferred_element_type=F32)
        ybuf[g % n_out] = _pack_bf16_pair(y[:, :half], y[:, half:])
        store(g).start()
        return carry

    lax.fori_loop(0, nblk_ref[e], body, 0)

    @pl.when(e == pl.num_programs(0) - 1)
    def _():
        for back in range(n_out, 0, -1):
            @pl.when(total >= back)
            def _(back=back):
                store(total - back).wait()


def _experts(xs, first_blk, n_blk, total_blk, w_gu, w_down):
    n_rows, half = xs.shape
    d = 2 * half
    n_exp, _, ff2 = w_gu.shape
    ff = w_down.shape[1]
    grid_spec = pltpu.PrefetchScalarGridSpec(
        num_scalar_prefetch=3,
        grid=(n_exp,),
        in_specs=[pl.BlockSpec((1, d, ff2), lambda e, *_: (e, 0, 0)),
                  pl.BlockSpec((1, ff, d), lambda e, *_: (e, 0, 0)),
                  pl.BlockSpec(memory_space=pl.ANY)],
        out_specs=pl.BlockSpec(memory_space=pl.ANY),
        scratch_shapes=[pltpu.VMEM((d, ff2), BF16), pltpu.VMEM((ff, d), BF16),
                        pltpu.VMEM((EXPERT_IN_SLOTS, EXPERT_ROWS, half), xs.dtype),
                        pltpu.VMEM((EXPERT_OUT_SLOTS, EXPERT_ROWS, half), xs.dtype),
                        pltpu.SemaphoreType.DMA((EXPERT_IN_SLOTS,)),
                        pltpu.SemaphoreType.DMA((EXPERT_OUT_SLOTS,))],
    )
    return pl.pallas_call(
        _expert_kernel,
        grid_spec=grid_spec,
        out_shape=jax.ShapeDtypeStruct((n_rows, half), xs.dtype),
        compiler_params=pltpu.CompilerParams(
            dimension_semantics=("arbitrary",), vmem_limit_bytes=VMEM_LIMIT),
        name="experts",
    )(first_blk, n_blk, total_blk, w_gu, w_down, xs)


def _combine_kernel(x1_ref, wt_ref, yg_ref, wsgu_ref, wsd_ref, lng_ref, lnb_ref, o_ref):
    tc = x1_ref.shape[0]
    x1 = x1_ref[...]
    ff = wsd_ref.shape[0]
    h = jnp.dot(x1.astype(BF16), wsgu_ref[...], preferred_element_type=F32)
    gate, up = h[:, :ff], h[:, ff:]
    act = gate * _sigmoid(gate) * up
    moe = jnp.dot(act.astype(BF16), wsd_ref[...], preferred_element_type=F32)

    half = yg_ref.shape[2]
    r_lo = jnp.zeros((tc, half), F32)
    r_hi = jnp.zeros((tc, half), F32)
    for k in range(TOP_K):
        y_lo, y_hi = _unpack_bf16_pair(yg_ref[k])
        w = wt_ref[:, k:k + 1]
        r_lo = r_lo + w * y_lo
        r_hi = r_hi + w * y_hi
    moe = moe + jnp.concatenate([r_lo, r_hi], axis=1)
    o_ref[...] = _layer_norm(DEEPNORM_ALPHA * x1 + moe, lng_ref[...], lnb_ref[...])


def _combine(x1, wgt_t, yg, chunk, ws_gu, ws_down, ln_g, ln_b, tc):
    t, d = x1.shape
    steps = yg.shape[1] // tc
    tok = lambda i: (chunk * steps + i, 0)
    return pl.pallas_call(
        _combine_kernel,
        grid=(steps,),
        in_specs=[pl.BlockSpec((tc, d), tok),
                  pl.BlockSpec((tc, TOP_K), tok),
                  pl.BlockSpec((TOP_K, tc, yg.shape[2]), lambda i: (0, i, 0)),
                  _resident(ws_gu.shape), _resident(ws_down.shape),
                  _resident(ln_g.shape), _resident(ln_b.shape)],
        out_specs=pl.BlockSpec((tc, d), tok),
        out_shape=jax.ShapeDtypeStruct((t, d), F32),
        input_output_aliases={0: 0},
        compiler_params=pltpu.CompilerParams(
            dimension_semantics=("arbitrary",), vmem_limit_bytes=VMEM_LIMIT),
        name="combine_ln2",
    )(x1, wgt_t, yg, ws_gu, ws_down, ln_g, ln_b)


def _fused_in_weights(w_in):
    d = w_in.shape[0]
    sizes = (SB_HEADS * HEAD_DIM,) * 3 + (SWA_HEADS * HEAD_DIM, SWA_KV_HEADS * HEAD_DIM,
                                          SWA_KV_HEADS * HEAD_DIM, MEM_HEADS * MEM_HEAD_DIM)
    parts, off = [], 0
    for s in sizes:
        parts.append(w_in[:, off:off + s])
        off += s
    q_sb, k_sb, v_sb, q_sw, k_sw, v_sw, q_m = parts
    gates = w_in[:, off:]
    scale = HEAD_DIM ** -0.5
    q_sw = (q_sw * scale).reshape(d, SWA_KV_HEADS, SWA_GROUP, HEAD_DIM)
    zeros = jnp.zeros((d, SWA_GROUP, HEAD_DIM), w_in.dtype)
    q_sw = jnp.stack([jnp.concatenate([q_sw[:, 0], zeros], axis=-1),
                      jnp.concatenate([zeros, q_sw[:, 1]], axis=-1)], axis=1)
    q_sw = q_sw.reshape(d, SWA_HEADS * LANES)
    used = W_GATE + 3 * SB_HEADS * HEAD_DIM + SWA_HEADS * LANES + q_m.shape[1] + 2 * k_sw.shape[1]
    pad = jnp.zeros((d, PROJ_COLS - used), w_in.dtype)
    return jnp.concatenate([gates, q_sw, q_sb * scale, k_sb, v_sb, q_m, k_sw, v_sw, pad],
                           axis=1).astype(BF16)


def _padded_swa_out_weights(w_o_swa):
    d = w_o_swa.shape[1]
    w = w_o_swa.reshape(SWA_KV_HEADS, SWA_GROUP, HEAD_DIM, d)
    zeros = jnp.zeros((SWA_GROUP, HEAD_DIM, d), w_o_swa.dtype)
    w = jnp.stack([jnp.concatenate([w[0], zeros], axis=1),
                   jnp.concatenate([zeros, w[1]], axis=1)], axis=0)
    return w.reshape(SWA_HEADS * LANES, d).astype(BF16)


def kernel(x, mem, w_in, b_gate, w_mem_kv, sinks, w_o_sb, w_o_swa, w_o_mem, w_out,
           ln1_g, ln1_b, w_router, router_bias, w_e_gu, w_e_down, w_s_gu, w_s_down,
           ln2_g, ln2_b):
    batch, seq, d = x.shape
    mem_len = mem.shape[1]
    t = batch * seq
    x2 = x.reshape(t, d)
    row_tile = min(512, t)

    p = _proj(x2, _fused_in_weights(w_in), b_gate.reshape(1, -1), row_tile)
    mkv = _matmul_bf16(mem.reshape(batch * mem_len, d), w_mem_kv.astype(BF16), mem_len)
    o_sb = _sb_attention(p, batch, seq, min(256, seq), SB_PAIRS_PER_STEP)
    o_sw = _swa_attention(p, _swa_tables(sinks), batch, seq)
    o_m = _mem_attention(p, mkv, batch, seq, mem_len, min(512, seq))
    wr_t = w_router.T
    wr_hi = wr_t.astype(BF16)
    wr_lo = (wr_t - wr_hi.astype(F32)).astype(BF16)
    x1, x1p, idx, rank, wgt, cnt = _merge_route(
        o_sb, o_sw, o_m, p, x2, w_o_sb.astype(BF16), _padded_swa_out_weights(w_o_swa),
        w_o_mem.astype(BF16), w_out.astype(BF16), ln1_g.reshape(1, d), ln1_b.reshape(1, d),
        wr_hi, wr_lo, router_bias.reshape(-1, 1).astype(F32), row_tile)

    out = _moe_ln(x1, x1p, idx, rank, wgt, cnt, w_e_gu, w_e_down, w_s_gu, w_s_down, ln2_g, ln2_b)
    return out.reshape(batch, seq, d)


def _moe_ln(x1, x1p, idx, rank, wgt, cnt, w_e_gu, w_e_down, w_s_gu, w_s_down, ln2_g, ln2_b):
    t, d = x1.shape
    counts = cnt[:, 0].astype(I32)
    padded = (counts + EXPERT_ROWS - 1) // EXPERT_ROWS * EXPERT_ROWS
    pad_end = jnp.cumsum(padded)
    pad_start = pad_end - padded
    n_blocks = t * TOP_K // EXPERT_ROWS + N_EXPERTS
    dest = _dest(idx, rank, pad_start.reshape(-1, 1), min(2048, t))

    xs = _dispatch(x1p, dest, n_blocks * EXPERT_ROWS)
    ys = _experts(xs, pad_start // EXPERT_ROWS, padded // EXPERT_ROWS,
                  pad_end[-1:] // EXPERT_ROWS, w_e_gu, w_e_down)
    tchunk = t // COMBINE_CHUNKS
    wgt_t = wgt.T
    ws_gu, ws_down = w_s_gu.astype(BF16), w_s_down.astype(BF16)
    out = x1
    for c in range(COMBINE_CHUNKS):
        slots = dest[:, c * tchunk:(c + 1) * tchunk].reshape(-1)
        yg = _gather_rows(ys, slots).reshape(TOP_K, tchunk, -1)
        out = _combine(out, wgt_t, yg, c, ws_gu, ws_down, ln2_g.reshape(1, d), ln2_b.reshape(1, d),
                       min(256, tchunk))
    return out
```

```python
import functools

import jax
import jax.numpy as jnp
from jax import lax
from jax.experimental import pallas as pl
from jax.experimental.pallas import tpu as pltpu
from jax.experimental.pallas import tpu_sc as plsc

F32 = jnp.float32
BF16 = jnp.bfloat16
I32 = jnp.int32
U32 = jnp.uint32

HEAD_DIM = 64
SB_HEADS = 8
SWA_HEADS = 8
SWA_KV_HEADS = 2
SWA_GROUP = SWA_HEADS // SWA_KV_HEADS
SWA_WINDOW = 128
MEM_HEADS = 4
MEM_HEAD_DIM = 128
N_BRANCH = 3
N_EXPERTS = 256
TOP_K = 8
N_GROUPS = 8
GROUP_SIZE = N_EXPERTS // N_GROUPS
TOPK_GROUPS = 4
EXPERT_FF = 256
SHARED_FF = 256
ROUTED_SCALE = 2.5
LN_EPS = 1e-5
DEPTH = 1
DEEPNORM_ALPHA = (2 * DEPTH) ** 0.25

LANES = 128
SC_CORES = 2
SC_SUBCORES = 16
SC_WORKERS = SC_CORES * SC_SUBCORES
SC_INDEX_WINDOW = 128
SC_GATHER_ROWS = 64
VMEM_LIMIT = 56 * 1024 * 1024

A_QSW = 0
A_QM = 1024
A_KSW = 1536
A_VSW = 1664
PA_COLS = 1792
W_GATE = 3072
C_QSB = 3072
C_KSB = 3584
C_VSB = 4096
PB_COLS = 4608
PROJ_CHUNK = 256
PROJ_ROWS = 512

SB_SKIP = 110.0
SB_PAIRS_PER_STEP = 4

EXPERT_ROWS = 512
EXPERT_IN_SLOTS = 4
EXPERT_OUT_SLOTS = 2
COMBINE_CHUNKS = 4


def _nt_dot(a, b):
    return lax.dot_general(a, b, (((1,), (1,)), ((), ())), preferred_element_type=F32)


def _sigmoid(x):
    return 1.0 / (1.0 + jnp.exp(-x))


def _layer_norm(h, g, b):
    mu = jnp.mean(h, axis=-1, keepdims=True)
    d = h - mu
    var = jnp.mean(d * d, axis=-1, keepdims=True)
    return d * lax.rsqrt(var + LN_EPS) * g + b


def _pack_bf16_pair(a, b):
    a_bits = lax.bitcast_convert_type(a.astype(BF16).astype(F32), U32)
    b_bits = lax.bitcast_convert_type(b.astype(BF16).astype(F32), U32)
    return (a_bits >> 16) | b_bits


def _unpack_bf16_pair(w):
    a = lax.bitcast_convert_type(w << 16, F32)
    b = lax.bitcast_convert_type(w & jnp.uint32(0xFFFF0000), F32)
    return a, b


def _resident(shape):
    nd = len(shape)
    return pl.BlockSpec(shape, lambda *_: (0,) * nd, pipeline_mode=pl.Buffered(1))


def _proj_attn_kernel(x_ref, w_ref, b_ref, hp_ref, qsw_ref, k_refs, v_refs, qm_ref, mk_ref, mv_ref,
                      p_ref, osw_ref, om_ref, *, gate_cols):
    j = pl.program_id(1)
    xb = x_ref[...].astype(BF16)

    def proj_chunk(c):
        cols = slice(c * PROJ_CHUNK, (c + 1) * PROJ_CHUNK)
        acc = jnp.dot(xb, w_ref[:, cols], preferred_element_type=F32)
        if (c + 1) * PROJ_CHUNK <= gate_cols:
            acc = _sigmoid(acc + b_ref[:, cols])
        p_ref[:, cols] = acc.astype(p_ref.dtype)

    blk = SWA_WINDOW
    nblk = x_ref.shape[0] // blk

    def swa(hb):
        rows = slice(hb * blk, (hb + 1) * blk)
        first = (j == 0) if hb == 0 else None
        _swa_block(hp_ref, qsw_ref, rows, k_refs[hb], k_refs[hb + 1], v_refs[hb], v_refs[hb + 1],
                   first, osw_ref)

    side = [functools.partial(swa, hb) for hb in range(nblk)]
    side.append(functools.partial(_mem_heads, qm_ref, mk_ref, mv_ref, om_ref))
    nchunk = p_ref.shape[1] // PROJ_CHUNK
    per = -(-nchunk // len(side))
    for s, work in enumerate(side):
        for c in range(s * per, min((s + 1) * per, nchunk)):
            proj_chunk(c)
        work()


def _proj_attn(x2, w_b, b_gate, p_a, mkv, swa_tables, batch, seq, mem_len):
    t, d = x2.shape
    tm = PROJ_ROWS
    nt = seq // tm
    blk = SWA_WINDOW
    nblk = tm // blk
    nb = seq // blk
    qw = SWA_HEADS * LANES
    mw = MEM_HEADS * MEM_HEAD_DIM
    tile = lambda w, col: pl.BlockSpec((tm, w), lambda b, j: (b * nt + j, col))
    kv = lambda col: [pl.BlockSpec((blk, LANES),
                                   functools.partial(lambda b, j, off, col: (
                                       b * nb + jnp.maximum(nblk * j + off, 0), col), off=off, col=col))
                      for off in range(-1, nblk)]
    return pl.pallas_call(
        functools.partial(_proj_attn_kernel, gate_cols=b_gate.shape[1]),
        grid=(batch, nt),
        in_specs=[tile(d, 0), _resident(w_b.shape), _resident(b_gate.shape),
                  _resident(swa_tables.shape),
                  tile(qw, A_QSW // qw),
                  kv(A_KSW // LANES), kv(A_VSW // LANES),
                  tile(mw, A_QM // mw),
                  pl.BlockSpec((mem_len, mw), lambda b, j: (b, 0)),
                  pl.BlockSpec((mem_len, mw), lambda b, j: (b, 1))],
        out_specs=[tile(w_b.shape[1], 0), tile(qw, 0), tile(mw, 0)],
        out_shape=[jax.ShapeDtypeStruct((t, w_b.shape[1]), BF16),
                   jax.ShapeDtypeStruct((t, qw), BF16),
                   jax.ShapeDtypeStruct((t, mw), BF16)],
        compiler_params=pltpu.CompilerParams(
            dimension_semantics=("parallel", "arbitrary"), vmem_limit_bytes=VMEM_LIMIT),
        name="in_proj_swa_mem",
    )(x2, w_b, b_gate, swa_tables, p_a, [p_a] * (nblk + 1), [p_a] * (nblk + 1), p_a, mkv, mkv)


def _mm_kernel(x_ref, w_ref, o_ref):
    o_ref[...] = jnp.dot(x_ref[...].astype(BF16), w_ref[...],
                         preferred_element_type=F32).astype(o_ref.dtype)


def _matmul_bf16(x2, w, tm, name):
    t, d = x2.shape
    n = w.shape[1]
    return pl.pallas_call(
        _mm_kernel,
        grid=(t // tm,),
        in_specs=[pl.BlockSpec((tm, d), lambda i: (i, 0)), _resident((d, n))],
        out_specs=pl.BlockSpec((tm, n), lambda i: (i, 0)),
        out_shape=jax.ShapeDtypeStruct((t, n), BF16),
        compiler_params=pltpu.CompilerParams(
            dimension_semantics=("parallel",), vmem_limit_bytes=VMEM_LIMIT),
        name=name,
    )(x2, w)


def _sb_kernel(q_ref, k_ref, v_ref, o_ref, *, tq):
    i = pl.program_id(2)
    pairs = q_ref.shape[1] // LANES
    lane = lax.broadcasted_iota(I32, (1, LANES), 1)
    r = lax.broadcasted_iota(I32, (tq, tq), 0)
    c = lax.broadcasted_iota(I32, (tq, tq), 1)
    tri = (r >= c).astype(BF16)
    causal = c < r
    nh = LANES // HEAD_DIM
    hmasks = [(lane >= h * HEAD_DIM) & (lane < (h + 1) * HEAD_DIM) for h in range(nh)]
    qs = []
    for p in range(pairs):
        q = q_ref[:, p * LANES:(p + 1) * LANES]
        qs.append(jnp.concatenate([jnp.where(hm, q, jnp.zeros_like(q)) for hm in hmasks], axis=0))
    causal2 = jnp.concatenate([causal] * nh, axis=0)
    tri2 = jnp.concatenate([tri, tri], axis=0)
    mp = nh * tq

    def block(kb, carry, acc, diag):
        rows = pl.ds(pl.multiple_of(kb * tq, tq), tq)
        z, hl, suffix, ab, av = {}, {}, {}, {}, {}

        def scores(p):
            z[p] = _nt_dot(qs[p], k_ref[rows, p * LANES:(p + 1) * LANES])

        def softplus_split(p):
            sp = jnp.maximum(z[p], 0.0) + jnp.log(1.0 + jnp.exp(-jnp.abs(z[p])))
            if diag:
                sp = jnp.where(causal2, sp, 0.0)
            hi = sp.astype(BF16)
            lo = (sp - hi.astype(F32)).astype(BF16)
            hl[p] = jnp.concatenate([hi, lo], axis=1)

        def cumsum(p):
            suffix[p] = jnp.dot(hl[p], tri2, preferred_element_type=F32)

        def weights(p):
            a = jnp.exp((z[p] - carry[p]) - suffix[p])
            if diag:
                a = jnp.where(causal2, a, 0.0)
            ab[p] = a.astype(BF16)

        def values(p):
            av[p] = jnp.dot(ab[p], v_ref[rows, p * LANES:(p + 1) * LANES],
                            preferred_element_type=F32)

        stages = (scores, softplus_split, cumsum, weights, values)
        for t in range(pairs + len(stages) - 1):
            for s in reversed(range(len(stages))):
                if 0 <= t - s < pairs:
                    stages[s](t - s)
        return ([carry[p] + suffix[p][:, 0:1] for p in range(pairs)],
                [acc[p] + av[p] for p in range(pairs)])

    carry, acc = block(i, [jnp.zeros((mp, 1), F32)] * pairs,
                       [jnp.zeros((mp, LANES), F32)] * pairs, True)

    def cond(s):
        kb, carry, _ = s
        lowest = carry[0]
        for cp in carry[1:]:
            lowest = jnp.minimum(lowest, cp)
        return (kb >= 0) & (jnp.min(lowest) < SB_SKIP)

    def body(s):
        kb, carry, acc = s
        carry, acc = block(kb, carry, acc, False)
        return kb - 1, carry, acc

    _, _, acc = lax.while_loop(cond, body, (i - 1, carry, acc))
    for p in range(pairs):
        o_ref[:, p * LANES:(p + 1) * LANES] = jnp.where(
            hmasks[0], acc[p][:tq], acc[p][tq:]).astype(o_ref.dtype)


def _sb_attention(p, batch, seq, tq, pairs):
    t = batch * seq
    nq = seq // tq
    w = pairs * LANES
    ngrp = SB_HEADS * HEAD_DIM // w
    qc, kc, vc = C_QSB // w, C_KSB // w, C_VSB // w
    return pl.pallas_call(
        functools.partial(_sb_kernel, tq=tq),
        grid=(batch, ngrp, nq),
        in_specs=[pl.BlockSpec((tq, w), lambda b, h, i: (b * nq + i, qc + h)),
                  pl.BlockSpec((seq, w), lambda b, h, i: (b, kc + h)),
                  pl.BlockSpec((seq, w), lambda b, h, i: (b, vc + h))],
        out_specs=pl.BlockSpec((tq, w), lambda b, h, i: (b * nq + i, h)),
        out_shape=jax.ShapeDtypeStruct((t, SB_HEADS * HEAD_DIM), BF16),
        compiler_params=pltpu.CompilerParams(
            dimension_semantics=("parallel", "parallel", "arbitrary"),
            vmem_limit_bytes=VMEM_LIMIT),
        name="sb_attention",
    )(p, p, p)


def _swa_block(hp_ref, q_ref, rows, kprev, kcur, vprev, vcur, first, o_ref):
    blk = kcur.shape[0]
    nheads = q_ref.shape[1] // LANES
    sink = hp_ref[:, 2 * blk:]
    lane = lax.broadcasted_iota(I32, (1, LANES), 1)
    qs = jnp.concatenate([q_ref[rows, g * LANES:(g + 1) * LANES] for g in range(nheads)], axis=0)
    keys = jnp.concatenate([kprev[...], kcur[...]], axis=0)
    vals = jnp.concatenate([vprev[...], vcur[...]], axis=0)
    z = _nt_dot(qs, keys) + hp_ref[:, :2 * blk]
    if first is not None:
        col = lax.broadcasted_iota(I32, (1, 2 * blk), 1)
        z = jnp.where(first & (col < blk), jnp.float32(-jnp.inf), z)
    m = jnp.maximum(jnp.max(z, axis=1, keepdims=True), sink)
    p = jnp.exp(z - jnp.concatenate([m, m], axis=1))
    den = jnp.sum(p, axis=1, keepdims=True) + jnp.exp(sink - m)
    o = jnp.dot(p.astype(BF16), vals, preferred_element_type=F32) / den
    for g in range(nheads):
        kv = g // SWA_GROUP
        kvmask = (lane >= kv * HEAD_DIM) & (lane < (kv + 1) * HEAD_DIM)
        o_ref[rows, g * LANES:(g + 1) * LANES] = jnp.where(
            kvmask, o[g * blk:(g + 1) * blk], 0.0).astype(o_ref.dtype)


def _swa_tables(sinks):
    w = SWA_WINDOW
    slopes = jnp.exp2(-8.0 * jnp.arange(1, SWA_HEADS + 1, dtype=F32) / SWA_HEADS)[:, None, None]
    r = jnp.arange(w)[:, None]
    c = jnp.arange(w)[None, :]
    dist = (r - c).astype(F32)[None]
    neg = jnp.float32(-jnp.inf)
    bias_c = jnp.where((c <= r)[None], -slopes * dist, neg)
    bias_p = jnp.where((c > r)[None], -slopes * (dist + w), neg)
    sink = jnp.broadcast_to(sinks.astype(F32)[:, None, None], (SWA_HEADS, w, w))
    return jnp.concatenate([bias_p, bias_c, sink], axis=2).reshape(SWA_HEADS * w, 3 * w)


def _mem_heads(q_ref, mk_ref, mv_ref, o_ref):
    scale = MEM_HEAD_DIM ** -0.5
    for h in range(MEM_HEADS):
        cols = slice(h * MEM_HEAD_DIM, (h + 1) * MEM_HEAD_DIM)
        z = _nt_dot(q_ref[:, cols], mk_ref[:, cols]) * scale
        m = jnp.max(z, axis=1, keepdims=True)
        p = jnp.exp(z - m)
        den = jnp.sum(p, axis=1, keepdims=True)
        o = jnp.dot(p.astype(BF16), mv_ref[:, cols], preferred_element_type=F32) / den
        o_ref[:, cols] = o.astype(o_ref.dtype)


def _merge_route_kernel(osb_ref, osw_ref, om_ref, g_ref, x_ref, wsb_ref, wsw_ref, wm_ref,
                        wout_ref, lng_ref, lnb_ref, wrh_ref, wrl_ref, rbias_ref,
                        x1_ref, x1p_ref, idx_ref, rank_ref, wgt_ref, cnt_ref, x1_prev, carry_ref):
    i = pl.program_id(0)

    @pl.when(i == 0)
    def _():
        x1_prev[...] = jnp.zeros_like(x1_prev)
        carry_ref[...] = jnp.zeros_like(carry_ref)

    d = x_ref.shape[1]
    st = {}

    def branch(b, o_ref, w_ref):
        def run():
            term = g_ref[:, b * d:(b + 1) * d].astype(F32) * jnp.dot(
                o_ref[...], w_ref[...], preferred_element_type=F32)
            st["merged"] = term if b == 0 else st["merged"] + term
        return run

    def out_proj():
        st["y"] = jnp.dot(st["merged"].astype(BF16), wout_ref[...], preferred_element_type=F32)

    idx, rank, wgt, count = _route(
        x1_prev[...], wrh_ref[...], wrl_ref[...], rbias_ref[...], carry_ref[...],
        side_work=(branch(0, osb_ref, wsb_ref), branch(1, osw_ref, wsw_ref),
                   branch(2, om_ref, wm_ref), out_proj))
    x1 = _layer_norm(DEEPNORM_ALPHA * x_ref[...] + st["y"], lng_ref[...], lnb_ref[...])
    x1_ref[...] = x1
    x1p_ref[...] = _pack_bf16_pair(x1[:, :d // 2], x1[:, d // 2:])
    idx_ref[...] = idx
    rank_ref[...] = rank
    wgt_ref[...] = wgt
    carry_ref[...] = carry_ref[...] + jnp.where(i > 0, count, 0.0)
    cnt_ref[...] = carry_ref[...]
    x1_prev[...] = x1


def _merge_route(o_sb, o_sw, o_m, p, x2, w_sb, w_sw, w_m, w_out, ln_g, ln_b, wr_hi, wr_lo,
                 bias_col, tm):
    t, d = x2.shape
    n = t // tm
    cur = lambda i: (jnp.minimum(i, n - 1), 0)
    row = lambda w: pl.BlockSpec((tm, w), cur)
    slot = pl.BlockSpec((TOP_K, tm), lambda i: (0, jnp.maximum(i - 1, 0)))
    return pl.pallas_call(
        _merge_route_kernel,
        grid=(n + 1,),
        in_specs=[row(o_sb.shape[1]), row(o_sw.shape[1]), row(o_m.shape[1]),
                  pl.BlockSpec((tm, N_BRANCH * d), cur),
                  row(d),
                  _resident(w_sb.shape), _resident(w_sw.shape), _resident(w_m.shape),
                  _resident(w_out.shape), _resident(ln_g.shape), _resident(ln_b.shape),
                  _resident(wr_hi.shape), _resident(wr_lo.shape), _resident(bias_col.shape)],
        out_specs=[row(d), row(d // 2), slot, slot, slot,
                   pl.BlockSpec((N_EXPERTS, 1), lambda i: (0, 0))],
        out_shape=[jax.ShapeDtypeStruct((t, d), F32), jax.ShapeDtypeStruct((t, d // 2), U32),
                   jax.ShapeDtypeStruct((TOP_K, t), I32),
                   jax.ShapeDtypeStruct((TOP_K, t), I32),
                   jax.ShapeDtypeStruct((TOP_K, t), F32),
                   jax.ShapeDtypeStruct((N_EXPERTS, 1), F32)],
        scratch_shapes=[pltpu.VMEM((tm, d), F32), pltpu.VMEM((N_EXPERTS, 1), F32)],
        compiler_params=pltpu.CompilerParams(
            dimension_semantics=("arbitrary",), vmem_limit_bytes=VMEM_LIMIT),
        name="merge_route",
    )(o_sb, o_sw, o_m, p, x2, w_sb, w_sw, w_m, w_out, ln_g, ln_b, wr_hi, wr_lo, bias_col)


def _route(x, wh, wl, bias, carry, side_work=()):
    tr = x.shape[0]
    xh = x.astype(BF16)
    xl = (x - xh.astype(F32)).astype(BF16)
    logits = _nt_dot(wh, xh) + _nt_dot(wh, xl) + _nt_dot(wl, xh)
    scores = _sigmoid(logits)
    biased = scores + bias
    neg = jnp.float32(-jnp.inf)

    sub = lax.broadcasted_iota(I32, (GROUP_SIZE, tr), 0)
    gscore = []
    for g in range(N_GROUPS):
        blk = biased[g * GROUP_SIZE:(g + 1) * GROUP_SIZE, :]
        m1 = jnp.max(blk, axis=0, keepdims=True)
        i1 = jnp.min(jnp.where(blk == m1, sub, GROUP_SIZE), axis=0, keepdims=True)
        m2 = jnp.max(jnp.where(sub == i1, neg, blk), axis=0, keepdims=True)
        gscore.append(m1 + m2)
    gs = jnp.concatenate(gscore, axis=0)

    giota = lax.broadcasted_iota(I32, (N_GROUPS, tr), 0)
    gsel = jnp.zeros((N_GROUPS, tr), F32)
    for _ in range(TOPK_GROUPS):
        m = jnp.max(gs, axis=0, keepdims=True)
        gi = jnp.min(jnp.where(gs == m, giota, N_GROUPS), axis=0, keepdims=True)
        hit = giota == gi
        gsel = jnp.where(hit, 1.0, gsel)
        gs = jnp.where(hit, neg, gs)

    masked = jnp.concatenate(
        [jnp.where(gsel[g:g + 1, :] > 0.0, biased[g * GROUP_SIZE:(g + 1) * GROUP_SIZE, :], neg)
         for g in range(N_GROUPS)], axis=0)

    eiota = lax.broadcasted_iota(I32, (N_EXPERTS, tr), 0)
    sel = jnp.zeros((N_EXPERTS, tr), F32)
    idx_rows, w_rows = [], []
    side_work = list(side_work)
    for k in range(TOP_K):
        if side_work and k % 2 == 0:
            side_work.pop(0)()
        m = jnp.max(masked, axis=0, keepdims=True)
        ei = jnp.min(jnp.where(masked == m, eiota, N_EXPERTS), axis=0, keepdims=True)
        hit = eiota == ei
        idx_rows.append(ei)
        w_rows.append(jnp.sum(jnp.where(hit, scores, 0.0), axis=0, keepdims=True))
        sel = jnp.where(hit, 1.0, sel)
        masked = jnp.where(hit, neg, masked)

    wsum = w_rows[0]
    for wk in w_rows[1:]:
        wsum = wsum + wk
    wgt = jnp.concatenate(w_rows, axis=0) / wsum * ROUTED_SCALE
    idx = jnp.concatenate(idx_rows, axis=0)

    a = lax.broadcasted_iota(I32, (tr, tr), 0)
    b = lax.broadcasted_iota(I32, (tr, tr), 1)
    before = (a < b).astype(BF16)
    rank = jnp.dot(sel.astype(BF16), before, preferred_element_type=F32) + carry
    rank_rows = [jnp.sum(jnp.where(eiota == ei, rank, 0.0), axis=0, keepdims=True)
                 for ei in idx_rows]
    rank = jnp.concatenate(rank_rows, axis=0).astype(I32)
    return idx, rank, wgt, jnp.sum(sel, axis=1, keepdims=True)


def _dest_kernel(idx_ref, rank_ref, start_ref, dest_ref):
    tr = idx_ref.shape[1]
    eiota = lax.broadcasted_iota(I32, (N_EXPERTS, tr), 0)
    rows = []
    for k in range(TOP_K):
        hit = eiota == idx_ref[k:k + 1, :]
        rows.append(jnp.sum(jnp.where(hit, start_ref[...], 0), axis=0, keepdims=True))
    dest_ref[...] = jnp.concatenate(rows, axis=0) + rank_ref[...]


def _dest(idx, rank, start_col, tr):
    t = idx.shape[1]
    slot = pl.BlockSpec((TOP_K, tr), lambda i: (0, i))
    return pl.pallas_call(
        _dest_kernel,
        grid=(t // tr,),
        in_specs=[slot, slot, _resident(start_col.shape)],
        out_specs=slot,
        out_shape=jax.ShapeDtypeStruct((TOP_K, t), I32),
        compiler_params=pltpu.CompilerParams(dimension_semantics=("parallel",)),
        name="slot_dest",
    )(idx, rank, start_col)


def _sc_worker_id():
    return lax.axis_index("s") * SC_CORES + lax.axis_index("c")


def _dispatch(x1p, dest, n_rows):
    t, w = x1p.shape
    per = t // SC_WORKERS
    win = min(SC_INDEX_WINDOW, per)
    mesh = plsc.VectorSubcoreMesh(core_axis_name="c", subcore_axis_name="s")

    @functools.partial(
        pl.kernel, mesh=mesh,
        out_type=jax.ShapeDtypeStruct((n_rows, w), x1p.dtype),
        scratch_types=[pltpu.VMEM((TOP_K, win), I32),
                       pltpu.VMEM((win, w), x1p.dtype),
                       pltpu.SemaphoreType.DMA],
        name="sc_dispatch",
    )
    def scatter_rows(x_hbm, dest_hbm, xs_hbm, idx_v, rows_v, sem):
        base = _sc_worker_id() * per

        @pl.loop(0, per // win)
        def _(j):
            t0 = pl.multiple_of(base + j * win, win)
            pltpu.sync_copy(dest_hbm.at[:, pl.ds(t0, win)], idx_v)
            pltpu.sync_copy(x_hbm.at[pl.ds(t0, win)], rows_v)
            copies = [pltpu.async_copy(rows_v, xs_hbm.at[idx_v.at[k]], sem) for k in range(TOP_K)]
            for c in copies:
                c.wait()

    return scatter_rows(x1p, dest)


def _gather_rows(table, idx):
    n = idx.shape[0]
    w = table.shape[1]
    per = n // SC_WORKERS
    chunk = min(SC_GATHER_ROWS, per // 2)
    assert n % SC_WORKERS == 0 and per % (2 * chunk) == 0, (n, chunk)
    mesh = plsc.VectorSubcoreMesh(core_axis_name="c", subcore_axis_name="s")

    @functools.partial(
        pl.kernel, mesh=mesh,
        out_type=jax.ShapeDtypeStruct((n, w), table.dtype),
        scratch_types=[pltpu.VMEM((per,), I32),
                       pltpu.VMEM((2, chunk, w), table.dtype),
                       pltpu.SemaphoreType.DMA((2,)),
                       pltpu.SemaphoreType.DMA((2,))],
        name="sc_gather",
    )
    def gather_rows(table_hbm, idx_hbm, out_hbm, idx_v, rows_v, gather_sem, put_sem):
        base = _sc_worker_id() * per
        nchunks = per // chunk
        pltpu.sync_copy(idx_hbm.at[pl.ds(base, per)], idx_v)

        def gather(j, b):
            off = pl.multiple_of(j * chunk, chunk)
            return pltpu.make_async_copy(table_hbm.at[idx_v.at[pl.ds(off, chunk)]],
                                         rows_v.at[b], gather_sem.at[b])

        def put(j, b):
            off = pl.multiple_of(j * chunk, chunk)
            return pltpu.make_async_copy(rows_v.at[b], out_hbm.at[pl.ds(base + off, chunk)],
                                         put_sem.at[b])

        gather(0, 0).start()

        @pl.loop(0, nchunks, step=2)
        def _(j):
            for b in (0, 1):
                jj = j + b

                @pl.when(jj + 1 < nchunks)
                def _():
                    @pl.when(jj >= 1)
                    def _():
                        put(jj - 1, 1 - b).wait()
                    gather(jj + 1, 1 - b).start()

                gather(jj, b).wait()
                put(jj, b).start()

        put(nchunks - 2, 0).wait()
        put(nchunks - 1, 1).wait()

    return gather_rows(table, idx)


def _expert_kernel(first_ref, nblk_ref, total_ref, wgu_ref, wd_ref, xs_hbm, ys_hbm,
                   wgu_s, wd_s, xbuf, ybuf, in_sem, out_sem):
    e = pl.program_id(0)
    total = total_ref[0]
    n_in, rows, half = xbuf.shape
    n_out = ybuf.shape[0]
    ahead = n_in - 1

    def block_rows(g):
        return pl.ds(pl.multiple_of(g * rows, rows), rows)

    def load(g):
        slot = g % n_in
        return pltpu.make_async_copy(xs_hbm.at[block_rows(g), :], xbuf.at[slot], in_sem.at[slot])

    def store(g):
        slot = g % n_out
        return pltpu.make_async_copy(ybuf.at[slot], ys_hbm.at[block_rows(g), :], out_sem.at[slot])

    @pl.when(e == 0)
    def _():
        for g in range(ahead):
            @pl.when(g < total)
            def _(g=g):
                load(g).start()

    wgu_s[...] = wgu_ref[0].astype(BF16)
    wd_s[...] = wd_ref[0].astype(BF16)
    g0 = first_ref[e]
    ff = wd_s.shape[0]

    def body(j, carry):
        g = g0 + j
        load(g).wait()

        @pl.when(g + ahead < total)
        def _():
            load(g + ahead).start()

        @pl.when(g >= n_out)
        def _():
            store(g - n_out).wait()

        x_lo, x_hi = _unpack_bf16_pair(xbuf[g % n_in])
        h = (jnp.dot(x_lo.astype(BF16), wgu_s[:half, :], preferred_element_type=F32)
             + jnp.dot(x_hi.astype(BF16), wgu_s[half:, :], preferred_element_type=F32))
        gate, up = h[:, :ff], h[:, ff:]
        act = gate * _sigmoid(gate) * up
        y = jnp.dot(act.astype(BF16), wd_s[...], preferred_element_type=F32)
        ybuf[g % n_out] = _pack_bf16_pair(y[:, :half], y[:, half:])
        store(g).start()
        return carry

    lax.fori_loop(0, nblk_ref[e], body, 0)

    @pl.when(e == pl.num_programs(0) - 1)
    def _():
        for back in range(n_out, 0, -1):
            @pl.when(total >= back)
            def _(back=back):
                store(total - back).wait()


def _experts(xs, first_blk, n_blk, total_blk, w_gu, w_down):
    n_rows, half = xs.shape
    d = 2 * half
    n_exp, _, ff2 = w_gu.shape
    ff = w_down.shape[1]
    grid_spec = pltpu.PrefetchScalarGridSpec(
        num_scalar_prefetch=3,
        grid=(n_exp,),
        in_specs=[pl.BlockSpec((1, d, ff2), lambda e, *_: (e, 0, 0)),
                  pl.BlockSpec((1, ff, d), lambda e, *_: (e, 0, 0)),
                  pl.BlockSpec(memory_space=pl.ANY)],
        out_specs=pl.BlockSpec(memory_space=pl.ANY),
        scratch_shapes=[pltpu.VMEM((d, ff2), BF16), pltpu.VMEM((ff, d), BF16),
                        pltpu.VMEM((EXPERT_IN_SLOTS, EXPERT_ROWS, half), xs.dtype),
                        pltpu.VMEM((EXPERT_OUT_SLOTS, EXPERT_ROWS, half), xs.dtype),
                        pltpu.SemaphoreType.DMA((EXPERT_IN_SLOTS,)),
                        pltpu.SemaphoreType.DMA((EXPERT_OUT_SLOTS,))],
    )
    return pl.pallas_call(
        _expert_kernel,
        grid_spec=grid_spec,
        out_shape=jax.ShapeDtypeStruct((n_rows, half), xs.dtype),
        compiler_params=pltpu.CompilerParams(
            dimension_semantics=("arbitrary",), vmem_limit_bytes=VMEM_LIMIT),
        name="experts",
    )(first_blk, n_blk, total_blk, w_gu, w_down, xs)


def _combine_kernel(x1_ref, wt_ref, yg_ref, wsgu_ref, wsd_ref, lng_ref, lnb_ref, o_ref):
    tc = x1_ref.shape[0]
    x1 = x1_ref[...]
    ff = wsd_ref.shape[0]
    h = jnp.dot(x1.astype(BF16), wsgu_ref[...], preferred_element_type=F32)
    gate, up = h[:, :ff], h[:, ff:]
    act = gate * _sigmoid(gate) * up
    moe = jnp.dot(act.astype(BF16), wsd_ref[...], preferred_element_type=F32)

    half = yg_ref.shape[2]
    r_lo = jnp.zeros((tc, half), F32)
    r_hi = jnp.zeros((tc, half), F32)
    for k in range(TOP_K):
        y_lo, y_hi = _unpack_bf16_pair(yg_ref[k])
        w = wt_ref[:, k:k + 1]
        r_lo = r_lo + w * y_lo
        r_hi = r_hi + w * y_hi
    moe = moe + jnp.concatenate([r_lo, r_hi], axis=1)
    o_ref[...] = _layer_norm(DEEPNORM_ALPHA * x1 + moe, lng_ref[...], lnb_ref[...])


def _combine(x1, wgt_t, yg, chunk, ws_gu, ws_down, ln_g, ln_b, tc):
    t, d = x1.shape
    steps = yg.shape[1] // tc
    tok = lambda i: (chunk * steps + i, 0)
    return pl.pallas_call(
        _combine_kernel,
        grid=(steps,),
        in_specs=[pl.BlockSpec((tc, d), tok),
                  pl.BlockSpec((tc, TOP_K), tok),
                  pl.BlockSpec((TOP_K, tc, yg.shape[2]), lambda i: (0, i, 0)),
                  _resident(ws_gu.shape), _resident(ws_down.shape),
                  _resident(ln_g.shape), _resident(ln_b.shape)],
        out_specs=pl.BlockSpec((tc, d), tok),
        out_shape=jax.ShapeDtypeStruct((t, d), F32),
        input_output_aliases={0: 0},
        compiler_params=pltpu.CompilerParams(
            dimension_semantics=("arbitrary",), vmem_limit_bytes=VMEM_LIMIT),
        name="combine_ln2",
    )(x1, wgt_t, yg, ws_gu, ws_down, ln_g, ln_b)


def _fused_in_weights(w_in):
    d = w_in.shape[0]
    sizes = (SB_HEADS * HEAD_DIM,) * 3 + (SWA_HEADS * HEAD_DIM, SWA_KV_HEADS * HEAD_DIM,
                                          SWA_KV_HEADS * HEAD_DIM, MEM_HEADS * MEM_HEAD_DIM)
    parts, off = [], 0
    for s in sizes:
        parts.append(w_in[:, off:off + s])
        off += s
    q_sb, k_sb, v_sb, q_sw, k_sw, v_sw, q_m = parts
    gates = w_in[:, off:]
    scale = HEAD_DIM ** -0.5
    q_sw = (q_sw * scale).reshape(d, SWA_KV_HEADS, SWA_GROUP, HEAD_DIM)
    zeros = jnp.zeros((d, SWA_GROUP, HEAD_DIM), w_in.dtype)
    q_sw = jnp.stack([jnp.concatenate([q_sw[:, 0], zeros], axis=-1),
                      jnp.concatenate([zeros, q_sw[:, 1]], axis=-1)], axis=1)
    q_sw = q_sw.reshape(d, SWA_HEADS * LANES)
    w_a = jnp.concatenate([q_sw, q_m, k_sw, v_sw], axis=1).astype(BF16)
    w_b = jnp.concatenate([gates, q_sb * scale, k_sb, v_sb], axis=1).astype(BF16)
    assert w_a.shape[1] == PA_COLS and w_b.shape[1] == PB_COLS
    return w_a, w_b


def _padded_swa_out_weights(w_o_swa):
    d = w_o_swa.shape[1]
    w = w_o_swa.reshape(SWA_KV_HEADS, SWA_GROUP, HEAD_DIM, d)
    zeros = jnp.zeros((SWA_GROUP, HEAD_DIM, d), w_o_swa.dtype)
    w = jnp.stack([jnp.concatenate([w[0], zeros], axis=1),
                   jnp.concatenate([zeros, w[1]], axis=1)], axis=0)
    return w.reshape(SWA_HEADS * LANES, d).astype(BF16)


def kernel(x, mem, w_in, b_gate, w_mem_kv, sinks, w_o_sb, w_o_swa, w_o_mem, w_out,
           ln1_g, ln1_b, w_router, router_bias, w_e_gu, w_e_down, w_s_gu, w_s_down,
           ln2_g, ln2_b):
    batch, seq, d = x.shape
    mem_len = mem.shape[1]
    t = batch * seq
    x2 = x.reshape(t, d)
    row_tile = min(512, t)

    w_a, w_b = _fused_in_weights(w_in)
    p_a = _matmul_bf16(x2, w_a, row_tile, "in_proj_attn")
    mkv = _matmul_bf16(mem.reshape(batch * mem_len, d), w_mem_kv.astype(BF16), mem_len,
                       "mem_kv_proj")
    p, o_sw, o_m = _proj_attn(x2, w_b, b_gate.reshape(1, -1), p_a, mkv, _swa_tables(sinks),
                              batch, seq, mem_len)
    o_sb = _sb_attention(p, batch, seq, min(256, seq), SB_PAIRS_PER_STEP)
    wr_t = w_router.T
    wr_hi = wr_t.astype(BF16)
    wr_lo = (wr_t - wr_hi.astype(F32)).astype(BF16)
    x1, x1p, idx, rank, wgt, cnt = _merge_route(
        o_sb, o_sw, o_m, p, x2, w_o_sb.astype(BF16), _padded_swa_out_weights(w_o_swa),
        w_o_mem.astype(BF16), w_out.astype(BF16), ln1_g.reshape(1, d), ln1_b.reshape(1, d),
        wr_hi, wr_lo, router_bias.reshape(-1, 1).astype(F32), row_tile)

    out = _moe_ln(x1, x1p, idx, rank, wgt, cnt, w_e_gu, w_e_down, w_s_gu, w_s_down, ln2_g, ln2_b)
    return out.reshape(batch, seq, d)


def _moe_ln(x1, x1p, idx, rank, wgt, cnt, w_e_gu, w_e_down, w_s_gu, w_s_down, ln2_g, ln2_b):
    t, d = x1.shape
    counts = cnt[:, 0].astype(I32)
    padded = (counts + EXPERT_ROWS - 1) // EXPERT_ROWS * EXPERT_ROWS
    pad_end = jnp.cumsum(padded)
    pad_start = pad_end - padded
    n_blocks = t * TOP_K // EXPERT_ROWS + N_EXPERTS
    dest = _dest(idx, rank, pad_start.reshape(-1, 1), min(2048, t))

    xs = _dispatch(x1p, dest, n_blocks * EXPERT_ROWS)
    ys = _experts(xs, pad_start // EXPERT_ROWS, padded // EXPERT_ROWS,
                  pad_end[-1:] // EXPERT_ROWS, w_e_gu, w_e_down)
    tchunk = t // COMBINE_CHUNKS
    wgt_t = wgt.T
    ws_gu, ws_down = w_s_gu.astype(BF16), w_s_down.astype(BF16)
    out = x1
    for c in range(COMBINE_CHUNKS):
        slots = dest[:, c * tchunk:(c + 1) * tchunk].reshape(-1)
        yg = _gather_rows(ys, slots).reshape(TOP_K, tchunk, -1)
        out = _combine(out, wgt_t, yg, c, ws_gu, ws_down, ln2_g.reshape(1, d), ln2_b.reshape(1, d),
                       min(256, tchunk))
    return out
```

```python
import functools

import jax
import jax.numpy as jnp
from jax import lax
from jax.experimental import pallas as pl
from jax.experimental.pallas import tpu as pltpu
from jax.experimental.pallas import tpu_sc as plsc

F32 = jnp.float32
BF16 = jnp.bfloat16
I32 = jnp.int32
U32 = jnp.uint32

HEAD_DIM = 64
SB_HEADS = 8
SWA_HEADS = 8
SWA_KV_HEADS = 2
SWA_GROUP = SWA_HEADS // SWA_KV_HEADS
SWA_WINDOW = 128
MEM_HEADS = 4
MEM_HEAD_DIM = 128
N_BRANCH = 3
N_EXPERTS = 256
TOP_K = 8
N_GROUPS = 8
GROUP_SIZE = N_EXPERTS // N_GROUPS
TOPK_GROUPS = 4
EXPERT_FF = 256
SHARED_FF = 256
ROUTED_SCALE = 2.5
LN_EPS = 1e-5
DEPTH = 1
DEEPNORM_ALPHA = (2 * DEPTH) ** 0.25

LANES = 128
SC_CORES = 2
SC_SUBCORES = 16
SC_WORKERS = SC_CORES * SC_SUBCORES
SC_INDEX_WINDOW = 128
SC_GATHER_ROWS = 64
VMEM_LIMIT = 56 * 1024 * 1024

A_QSW = 0
A_QM = 1024
A_KSW = 1536
A_VSW = 1664
PA_COLS = 1792
W_GATE = 3072
C_QSB = 3072
C_KSB = 3584
C_VSB = 4096
PB_COLS = 4608
PROJ_CHUNK = 256
PROJ_ROWS = 512

SB_SKIP = 110.0
SB_PAIRS_PER_STEP = 4

EXPERT_ROWS = 512
EXPERT_IN_SLOTS = 4
EXPERT_OUT_SLOTS = 2
COMBINE_CHUNKS = 8


def _nt_dot(a, b):
    return lax.dot_general(a, b, (((1,), (1,)), ((), ())), preferred_element_type=F32)


def _sigmoid(x):
    return 1.0 / (1.0 + jnp.exp(-x))


def _layer_norm(h, g, b):
    mu = jnp.mean(h, axis=-1, keepdims=True)
    d = h - mu
    var = jnp.mean(d * d, axis=-1, keepdims=True)
    return d * lax.rsqrt(var + LN_EPS) * g + b


def _pack_bf16_pair(a, b):
    a_bits = lax.bitcast_convert_type(a.astype(BF16).astype(F32), U32)
    b_bits = lax.bitcast_convert_type(b.astype(BF16).astype(F32), U32)
    return (a_bits >> 16) | b_bits


def _unpack_bf16_pair(w):
    a = lax.bitcast_convert_type(w << 16, F32)
    b = lax.bitcast_convert_type(w & jnp.uint32(0xFFFF0000), F32)
    return a, b


def _resident(shape):
    nd = len(shape)
    return pl.BlockSpec(shape, lambda *_: (0,) * nd, pipeline_mode=pl.Buffered(1))


def _proj_attn_kernel(x_ref, w_ref, b_ref, hp_ref, qsw_ref, k_refs, v_refs, qm_ref, mk_ref, mv_ref,
                      p_ref, osw_ref, om_ref, *, gate_cols):
    j = pl.program_id(1)
    xb = x_ref[...].astype(BF16)

    def proj_chunk(c):
        cols = slice(c * PROJ_CHUNK, (c + 1) * PROJ_CHUNK)
        acc = jnp.dot(xb, w_ref[:, cols], preferred_element_type=F32)
        if (c + 1) * PROJ_CHUNK <= gate_cols:
            acc = _sigmoid(acc + b_ref[:, cols])
        p_ref[:, cols] = acc.astype(p_ref.dtype)

    blk = SWA_WINDOW
    nblk = x_ref.shape[0] // blk

    def swa(hb):
        rows = slice(hb * blk, (hb + 1) * blk)
        first = (j == 0) if hb == 0 else None
        _swa_block(hp_ref, qsw_ref, rows, k_refs[hb], k_refs[hb + 1], v_refs[hb], v_refs[hb + 1],
                   first, osw_ref)

    side = [functools.partial(swa, hb) for hb in range(nblk)]
    side.append(functools.partial(_mem_heads, qm_ref, mk_ref, mv_ref, om_ref))
    nchunk = p_ref.shape[1] // PROJ_CHUNK
    per = -(-nchunk // len(side))
    for s, work in enumerate(side):
        for c in range(s * per, min((s + 1) * per, nchunk)):
            proj_chunk(c)
        work()


def _proj_attn(x2, w_b, b_gate, p_a, mkv, swa_tables, batch, seq, mem_len):
    t, d = x2.shape
    tm = PROJ_ROWS
    nt = seq // tm
    blk = SWA_WINDOW
    nblk = tm // blk
    nb = seq // blk
    qw = SWA_HEADS * LANES
    mw = MEM_HEADS * MEM_HEAD_DIM
    tile = lambda w, col: pl.BlockSpec((tm, w), lambda b, j: (b * nt + j, col))
    kv = lambda col: [pl.BlockSpec((blk, LANES),
                                   functools.partial(lambda b, j, off, col: (
                                       b * nb + jnp.maximum(nblk * j + off, 0), col), off=off, col=col))
                      for off in range(-1, nblk)]
    return pl.pallas_call(
        functools.partial(_proj_attn_kernel, gate_cols=b_gate.shape[1]),
        grid=(batch, nt),
        in_specs=[tile(d, 0), _resident(w_b.shape), _resident(b_gate.shape),
                  _resident(swa_tables.shape),
                  tile(qw, A_QSW // qw),
                  kv(A_KSW // LANES), kv(A_VSW // LANES),
                  tile(mw, A_QM // mw),
                  pl.BlockSpec((mem_len, mw), lambda b, j: (b, 0)),
                  pl.BlockSpec((mem_len, mw), lambda b, j: (b, 1))],
        out_specs=[tile(w_b.shape[1], 0), tile(qw, 0), tile(mw, 0)],
        out_shape=[jax.ShapeDtypeStruct((t, w_b.shape[1]), BF16),
                   jax.ShapeDtypeStruct((t, qw), BF16),
                   jax.ShapeDtypeStruct((t, mw), BF16)],
        compiler_params=pltpu.CompilerParams(
            dimension_semantics=("parallel", "arbitrary"), vmem_limit_bytes=VMEM_LIMIT),
        name="in_proj_swa_mem",
    )(x2, w_b, b_gate, swa_tables, p_a, [p_a] * (nblk + 1), [p_a] * (nblk + 1), p_a, mkv, mkv)


def _mm_kernel(x_ref, w_ref, o_ref):
    o_ref[...] = jnp.dot(x_ref[...].astype(BF16), w_ref[...],
                         preferred_element_type=F32).astype(o_ref.dtype)


def _matmul_bf16(x2, w, tm, name):
    t, d = x2.shape
    n = w.shape[1]
    return pl.pallas_call(
        _mm_kernel,
        grid=(t // tm,),
        in_specs=[pl.BlockSpec((tm, d), lambda i: (i, 0)), _resident((d, n))],
        out_specs=pl.BlockSpec((tm, n), lambda i: (i, 0)),
        out_shape=jax.ShapeDtypeStruct((t, n), BF16),
        compiler_params=pltpu.CompilerParams(
            dimension_semantics=("parallel",), vmem_limit_bytes=VMEM_LIMIT),
        name=name,
    )(x2, w)


def _sb_kernel(q_ref, k_ref, v_ref, o_ref, *, tq):
    i = pl.program_id(2)
    pairs = q_ref.shape[1] // LANES
    lane = lax.broadcasted_iota(I32, (1, LANES), 1)
    r = lax.broadcasted_iota(I32, (tq, tq), 0)
    c = lax.broadcasted_iota(I32, (tq, tq), 1)
    tri = (r >= c).astype(BF16)
    causal = c < r
    nh = LANES // HEAD_DIM
    hmasks = [(lane >= h * HEAD_DIM) & (lane < (h + 1) * HEAD_DIM) for h in range(nh)]
    qs = []
    for p in range(pairs):
        q = q_ref[:, p * LANES:(p + 1) * LANES]
        qs.append(jnp.concatenate([jnp.where(hm, q, jnp.zeros_like(q)) for hm in hmasks], axis=0))
    causal2 = jnp.concatenate([causal] * nh, axis=0)
    tri2 = jnp.concatenate([tri, tri], axis=0)
    mp = nh * tq

    def block(kb, carry, acc, diag):
        rows = pl.ds(pl.multiple_of(kb * tq, tq), tq)
        z, hl, suffix, ab, av = {}, {}, {}, {}, {}

        def scores(p):
            z[p] = _nt_dot(qs[p], k_ref[rows, p * LANES:(p + 1) * LANES])

        def softplus_split(p):
            sp = jnp.maximum(z[p], 0.0) + jnp.log(1.0 + jnp.exp(-jnp.abs(z[p])))
            if diag:
                sp = jnp.where(causal2, sp, 0.0)
            hi = sp.astype(BF16)
            lo = (sp - hi.astype(F32)).astype(BF16)
            hl[p] = jnp.concatenate([hi, lo], axis=1)

        def cumsum(p):
            suffix[p] = jnp.dot(hl[p], tri2, preferred_element_type=F32)

        def weights(p):
            a = jnp.exp((z[p] - carry[p]) - suffix[p])
            if diag:
                a = jnp.where(causal2, a, 0.0)
            ab[p] = a.astype(BF16)

        def values(p):
            av[p] = jnp.dot(ab[p], v_ref[rows, p * LANES:(p + 1) * LANES],
                            preferred_element_type=F32)

        stages = (scores, softplus_split, cumsum, weights, values)
        for t in range(pairs + len(stages) - 1):
            for s in reversed(range(len(stages))):
                if 0 <= t - s < pairs:
                    stages[s](t - s)
        return ([carry[p] + suffix[p][:, 0:1] for p in range(pairs)],
                [acc[p] + av[p] for p in range(pairs)])

    carry, acc = block(i, [jnp.zeros((mp, 1), F32)] * pairs,
                       [jnp.zeros((mp, LANES), F32)] * pairs, True)

    def cond(s):
        kb, carry, _ = s
        lowest = carry[0]
        for cp in carry[1:]:
            lowest = jnp.minimum(lowest, cp)
        return (kb >= 0) & (jnp.min(lowest) < SB_SKIP)

    def body(s):
        kb, carry, acc = s
        carry, acc = block(kb, carry, acc, False)
        return kb - 1, carry, acc

    _, _, acc = lax.while_loop(cond, body, (i - 1, carry, acc))
    for p in range(pairs):
        o_ref[:, p * LANES:(p + 1) * LANES] = jnp.where(
            hmasks[0], acc[p][:tq], acc[p][tq:]).astype(o_ref.dtype)


def _sb_attention(p, batch, seq, tq, pairs):
    t = batch * seq
    nq = seq // tq
    w = pairs * LANES
    ngrp = SB_HEADS * HEAD_DIM // w
    qc, kc, vc = C_QSB // w, C_KSB // w, C_VSB // w
    return pl.pallas_call(
        functools.partial(_sb_kernel, tq=tq),
        grid=(batch, ngrp, nq),
        in_specs=[pl.BlockSpec((tq, w), lambda b, h, i: (b * nq + i, qc + h)),
                  pl.BlockSpec((seq, w), lambda b, h, i: (b, kc + h)),
                  pl.BlockSpec((seq, w), lambda b, h, i: (b, vc + h))],
        out_specs=pl.BlockSpec((tq, w), lambda b, h, i: (b * nq + i, h)),
        out_shape=jax.ShapeDtypeStruct((t, SB_HEADS * HEAD_DIM), BF16),
        compiler_params=pltpu.CompilerParams(
            dimension_semantics=("parallel", "parallel", "arbitrary"),
            vmem_limit_bytes=VMEM_LIMIT),
        name="sb_attention",
    )(p, p, p)


def _swa_block(hp_ref, q_ref, rows, kprev, kcur, vprev, vcur, first, o_ref):
    blk = kcur.shape[0]
    nheads = q_ref.shape[1] // LANES
    sink = hp_ref[:, 2 * blk:]
    lane = lax.broadcasted_iota(I32, (1, LANES), 1)
    qs = jnp.concatenate([q_ref[rows, g * LANES:(g + 1) * LANES] for g in range(nheads)], axis=0)
    keys = jnp.concatenate([kprev[...], kcur[...]], axis=0)
    vals = jnp.concatenate([vprev[...], vcur[...]], axis=0)
    z = _nt_dot(qs, keys) + hp_ref[:, :2 * blk]
    if first is not None:
        col = lax.broadcasted_iota(I32, (1, 2 * blk), 1)
        z = jnp.where(first & (col < blk), jnp.float32(-jnp.inf), z)
    m = jnp.maximum(jnp.max(z, axis=1, keepdims=True), sink)
    p = jnp.exp(z - jnp.concatenate([m, m], axis=1))
    den = jnp.sum(p, axis=1, keepdims=True) + jnp.exp(sink - m)
    o = jnp.dot(p.astype(BF16), vals, preferred_element_type=F32) / den
    for g in range(nheads):
        kv = g // SWA_GROUP
        kvmask = (lane >= kv * HEAD_DIM) & (lane < (kv + 1) * HEAD_DIM)
        o_ref[rows, g * LANES:(g + 1) * LANES] = jnp.where(
            kvmask, o[g * blk:(g + 1) * blk], 0.0).astype(o_ref.dtype)


def _swa_tables(sinks):
    w = SWA_WINDOW
    slopes = jnp.exp2(-8.0 * jnp.arange(1, SWA_HEADS + 1, dtype=F32) / SWA_HEADS)[:, None, None]
    r = jnp.arange(w)[:, None]
    c = jnp.arange(w)[None, :]
    dist = (r - c).astype(F32)[None]
    neg = jnp.float32(-jnp.inf)
    bias_c = jnp.where((c <= r)[None], -slopes * dist, neg)
    bias_p = jnp.where((c > r)[None], -slopes * (dist + w), neg)
    sink = jnp.broadcast_to(sinks.astype(F32)[:, None, None], (SWA_HEADS, w, w))
    return jnp.concatenate([bias_p, bias_c, sink], axis=2).reshape(SWA_HEADS * w, 3 * w)


def _mem_heads(q_ref, mk_ref, mv_ref, o_ref):
    scale = MEM_HEAD_DIM ** -0.5
    for h in range(MEM_HEADS):
        cols = slice(h * MEM_HEAD_DIM, (h + 1) * MEM_HEAD_DIM)
        z = _nt_dot(q_ref[:, cols], mk_ref[:, cols]) * scale
        m = jnp.max(z, axis=1, keepdims=True)
        p = jnp.exp(z - m)
        den = jnp.sum(p, axis=1, keepdims=True)
        o = jnp.dot(p.astype(BF16), mv_ref[:, cols], preferred_element_type=F32) / den
        o_ref[:, cols] = o.astype(o_ref.dtype)


def _merge_route_kernel(osb_ref, osw_ref, om_ref, g_ref, x_ref, wsb_ref, wsw_ref, wm_ref,
                        wout_ref, lng_ref, lnb_ref, wrh_ref, wrl_ref, rbias_ref,
                        x1_ref, x1p_ref, idx_ref, rank_ref, wgt_ref, cnt_ref, x1_prev, carry_ref):
    i = pl.program_id(0)

    @pl.when(i == 0)
    def _():
        x1_prev[...] = jnp.zeros_like(x1_prev)
        carry_ref[...] = jnp.zeros_like(carry_ref)

    d = x_ref.shape[1]
    st = {}

    def branch(b, o_ref, w_ref):
        def run():
            term = g_ref[:, b * d:(b + 1) * d].astype(F32) * jnp.dot(
                o_ref[...], w_ref[...], preferred_element_type=F32)
            st["merged"] = term if b == 0 else st["merged"] + term
        return run

    def out_proj():
        st["y"] = jnp.dot(st["merged"].astype(BF16), wout_ref[...], preferred_element_type=F32)

    idx, rank, wgt, count = _route(
        x1_prev[...], wrh_ref[...], wrl_ref[...], rbias_ref[...], carry_ref[...],
        side_work=(branch(0, osb_ref, wsb_ref), branch(1, osw_ref, wsw_ref),
                   branch(2, om_ref, wm_ref), out_proj))
    x1 = _layer_norm(DEEPNORM_ALPHA * x_ref[...] + st["y"], lng_ref[...], lnb_ref[...])
    x1_ref[...] = x1
    x1p_ref[...] = _pack_bf16_pair(x1[:, :d // 2], x1[:, d // 2:])
    idx_ref[...] = idx
    rank_ref[...] = rank
    wgt_ref[...] = wgt
    carry_ref[...] = carry_ref[...] + jnp.where(i > 0, count, 0.0)
    cnt_ref[...] = carry_ref[...]
    x1_prev[...] = x1


def _merge_route(o_sb, o_sw, o_m, p, x2, w_sb, w_sw, w_m, w_out, ln_g, ln_b, wr_hi, wr_lo,
                 bias_col, tm):
    t, d = x2.shape
    n = t // tm
    cur = lambda i: (jnp.minimum(i, n - 1), 0)
    row = lambda w: pl.BlockSpec((tm, w), cur)
    slot = pl.BlockSpec((TOP_K, tm), lambda i: (0, jnp.maximum(i - 1, 0)))
    return pl.pallas_call(
        _merge_route_kernel,
        grid=(n + 1,),
        in_specs=[row(o_sb.shape[1]), row(o_sw.shape[1]), row(o_m.shape[1]),
                  pl.BlockSpec((tm, N_BRANCH * d), cur),
                  row(d),
                  _resident(w_sb.shape), _resident(w_sw.shape), _resident(w_m.shape),
                  _resident(w_out.shape), _resident(ln_g.shape), _resident(ln_b.shape),
                  _resident(wr_hi.shape), _resident(wr_lo.shape), _resident(bias_col.shape)],
        out_specs=[row(d), row(d // 2), slot, slot, slot,
                   pl.BlockSpec((N_EXPERTS, 1), lambda i: (0, 0))],
        out_shape=[jax.ShapeDtypeStruct((t, d), F32), jax.ShapeDtypeStruct((t, d // 2), U32),
                   jax.ShapeDtypeStruct((TOP_K, t), I32),
                   jax.ShapeDtypeStruct((TOP_K, t), I32),
                   jax.ShapeDtypeStruct((TOP_K, t), F32),
                   jax.ShapeDtypeStruct((N_EXPERTS, 1), F32)],
        scratch_shapes=[pltpu.VMEM((tm, d), F32), pltpu.VMEM((N_EXPERTS, 1), F32)],
        compiler_params=pltpu.CompilerParams(
            dimension_semantics=("arbitrary",), vmem_limit_bytes=VMEM_LIMIT),
        name="merge_route",
    )(o_sb, o_sw, o_m, p, x2, w_sb, w_sw, w_m, w_out, ln_g, ln_b, wr_hi, wr_lo, bias_col)


def _route(x, wh, wl, bias, carry, side_work=()):
    tr = x.shape[0]
    xh = x.astype(BF16)
    xl = (x - xh.astype(F32)).astype(BF16)
    logits = _nt_dot(wh, xh) + _nt_dot(wh, xl) + _nt_dot(wl, xh)
    scores = _sigmoid(logits)
    biased = scores + bias
    neg = jnp.float32(-jnp.inf)

    sub = lax.broadcasted_iota(I32, (GROUP_SIZE, tr), 0)
    gscore = []
    for g in range(N_GROUPS):
        blk = biased[g * GROUP_SIZE:(g + 1) * GROUP_SIZE, :]
        m1 = jnp.max(blk, axis=0, keepdims=True)
        i1 = jnp.min(jnp.where(blk == m1, sub, GROUP_SIZE), axis=0, keepdims=True)
        m2 = jnp.max(jnp.where(sub == i1, neg, blk), axis=0, keepdims=True)
        gscore.append(m1 + m2)
    gs = jnp.concatenate(gscore, axis=0)

    giota = lax.broadcasted_iota(I32, (N_GROUPS, tr), 0)
    gsel = jnp.zeros((N_GROUPS, tr), F32)
    for _ in range(TOPK_GROUPS):
        m = jnp.max(gs, axis=0, keepdims=True)
        gi = jnp.min(jnp.where(gs == m, giota, N_GROUPS), axis=0, keepdims=True)
        hit = giota == gi
        gsel = jnp.where(hit, 1.0, gsel)
        gs = jnp.where(hit, neg, gs)

    masked = jnp.concatenate(
        [jnp.where(gsel[g:g + 1, :] > 0.0, biased[g * GROUP_SIZE:(g + 1) * GROUP_SIZE, :], neg)
         for g in range(N_GROUPS)], axis=0)

    eiota = lax.broadcasted_iota(I32, (N_EXPERTS, tr), 0)
    sel = jnp.zeros((N_EXPERTS, tr), F32)
    idx_rows, w_rows = [], []
    side_work = list(side_work)
    for k in range(TOP_K):
        if side_work and k % 2 == 0:
            side_work.pop(0)()
        m = jnp.max(masked, axis=0, keepdims=True)
        ei = jnp.min(jnp.where(masked == m, eiota, N_EXPERTS), axis=0, keepdims=True)
        hit = eiota == ei
        idx_rows.append(ei)
        w_rows.append(jnp.sum(jnp.where(hit, scores, 0.0), axis=0, keepdims=True))
        sel = jnp.where(hit, 1.0, sel)
        masked = jnp.where(hit, neg, masked)

    wsum = w_rows[0]
    for wk in w_rows[1:]:
        wsum = wsum + wk
    wgt = jnp.concatenate(w_rows, axis=0) / wsum * ROUTED_SCALE
    idx = jnp.concatenate(idx_rows, axis=0)

    a = lax.broadcasted_iota(I32, (tr, tr), 0)
    b = lax.broadcasted_iota(I32, (tr, tr), 1)
    before = (a < b).astype(BF16)
    rank = jnp.dot(sel.astype(BF16), before, preferred_element_type=F32) + carry
    rank_rows = [jnp.sum(jnp.where(eiota == ei, rank, 0.0), axis=0, keepdims=True)
                 for ei in idx_rows]
    rank = jnp.concatenate(rank_rows, axis=0).astype(I32)
    return idx, rank, wgt, jnp.sum(sel, axis=1, keepdims=True)


def _dest_kernel(idx_ref, rank_ref, start_ref, dest_ref):
    tr = idx_ref.shape[1]
    eiota = lax.broadcasted_iota(I32, (N_EXPERTS, tr), 0)
    rows = []
    for k in range(TOP_K):
        hit = eiota == idx_ref[k:k + 1, :]
        rows.append(jnp.sum(jnp.where(hit, start_ref[...], 0), axis=0, keepdims=True))
    dest_ref[...] = jnp.concatenate(rows, axis=0) + rank_ref[...]


def _dest(idx, rank, start_col, tr):
    t = idx.shape[1]
    slot = pl.BlockSpec((TOP_K, tr), lambda i: (0, i))
    return pl.pallas_call(
        _dest_kernel,
        grid=(t // tr,),
        in_specs=[slot, slot, _resident(start_col.shape)],
        out_specs=slot,
        out_shape=jax.ShapeDtypeStruct((TOP_K, t), I32),
        compiler_params=pltpu.CompilerParams(dimension_semantics=("parallel",)),
        name="slot_dest",
    )(idx, rank, start_col)


def _sc_worker_id():
    return lax.axis_index("s") * SC_CORES + lax.axis_index("c")


def _dispatch(x1p, dest, n_rows):
    t, w = x1p.shape
    per = t // SC_WORKERS
    win = min(SC_INDEX_WINDOW, per)
    mesh = plsc.VectorSubcoreMesh(core_axis_name="c", subcore_axis_name="s")

    @functools.partial(
        pl.kernel, mesh=mesh,
        out_type=jax.ShapeDtypeStruct((n_rows, w), x1p.dtype),
        scratch_types=[pltpu.VMEM((TOP_K, win), I32),
                       pltpu.VMEM((win, w), x1p.dtype),
                       pltpu.SemaphoreType.DMA],
        name="sc_dispatch",
    )
    def scatter_rows(x_hbm, dest_hbm, xs_hbm, idx_v, rows_v, sem):
        base = _sc_worker_id() * per

        @pl.loop(0, per // win)
        def _(j):
            t0 = pl.multiple_of(base + j * win, win)
            pltpu.sync_copy(dest_hbm.at[:, pl.ds(t0, win)], idx_v)
            pltpu.sync_copy(x_hbm.at[pl.ds(t0, win)], rows_v)
            copies = [pltpu.async_copy(rows_v, xs_hbm.at[idx_v.at[k]], sem) for k in range(TOP_K)]
            for c in copies:
                c.wait()

    return scatter_rows(x1p, dest)


def _gather_rows(table, idx):
    n = idx.shape[0]
    w = table.shape[1]
    per = n // SC_WORKERS
    chunk = min(SC_GATHER_ROWS, per // 2)
    assert n % SC_WORKERS == 0 and per % (2 * chunk) == 0, (n, chunk)
    mesh = plsc.VectorSubcoreMesh(core_axis_name="c", subcore_axis_name="s")

    @functools.partial(
        pl.kernel, mesh=mesh,
        out_type=jax.ShapeDtypeStruct((n, w), table.dtype),
        scratch_types=[pltpu.VMEM((per,), I32),
                       pltpu.VMEM((2, chunk, w), table.dtype),
                       pltpu.SemaphoreType.DMA((2,)),
                       pltpu.SemaphoreType.DMA((2,))],
        name="sc_gather",
    )
    def gather_rows(table_hbm, idx_hbm, out_hbm, idx_v, rows_v, gather_sem, put_sem):
        base = _sc_worker_id() * per
        nchunks = per // chunk
        pltpu.sync_copy(idx_hbm.at[pl.ds(base, per)], idx_v)

        def gather(j, b):
            off = pl.multiple_of(j * chunk, chunk)
            return pltpu.make_async_copy(table_hbm.at[idx_v.at[pl.ds(off, chunk)]],
                                         rows_v.at[b], gather_sem.at[b])

        def put(j, b):
            off = pl.multiple_of(j * chunk, chunk)
            return pltpu.make_async_copy(rows_v.at[b], out_hbm.at[pl.ds(base + off, chunk)],
                                         put_sem.at[b])

        gather(0, 0).start()

        @pl.loop(0, nchunks, step=2)
        def _(j):
            for b in (0, 1):
                jj = j + b

                @pl.when(jj + 1 < nchunks)
                def _():
                    @pl.when(jj >= 1)
                    def _():
                        put(jj - 1, 1 - b).wait()
                    gather(jj + 1, 1 - b).start()

                gather(jj, b).wait()
                put(jj, b).start()

        put(nchunks - 2, 0).wait()
        put(nchunks - 1, 1).wait()

    return gather_rows(table, idx)


def _expert_kernel(first_ref, nblk_ref, total_ref, wgu_ref, wd_ref, xs_hbm, ys_hbm,
                   wgu_s, wd_s, xbuf, ybuf, in_sem, out_sem):
    e = pl.program_id(0)
    total = total_ref[0]
    n_in, rows, half = xbuf.shape
    n_out = ybuf.shape[0]
    ahead = n_in - 1

    def block_rows(g):
        return pl.ds(pl.multiple_of(g * rows, rows), rows)

    def load(g):
        slot = g % n_in
        return pltpu.make_async_copy(xs_hbm.at[block_rows(g), :], xbuf.at[slot], in_sem.at[slot])

    def store(g):
        slot = g % n_out
        return pltpu.make_async_copy(ybuf.at[slot], ys_hbm.at[block_rows(g), :], out_sem.at[slot])

    @pl.when(e == 0)
    def _():
        for g in range(ahead):
            @pl.when(g < total)
            def _(g=g):
                load(g).start()

    wgu_s[...] = wgu_ref[0].astype(BF16)
    wd_s[...] = wd_ref[0].astype(BF16)
    g0 = first_ref[e]
    ff = wd_s.shape[0]

    def body(j, carry):
        g = g0 + j
        load(g).wait()

        @pl.when(g + ahead < total)
        def _():
            load(g + ahead).start()

        @pl.when(g >= n_out)
        def _():
            store(g - n_out).wait()

        x_lo, x_hi = _unpack_bf16_pair(xbuf[g % n_in])
        h = (jnp.dot(x_lo.astype(BF16), wgu_s[:half, :], preferred_element_type=F32)
             + jnp.dot(x_hi.astype(BF16), wgu_s[half:, :], preferred_element_type=F32))
        gate, up = h[:, :ff], h[:, ff:]
        act = gate * _sigmoid(gate) * up
        y = jnp.dot(act.astype(BF16), wd_s[...], preferred_element_type=F32)
        ybuf[g % n_out] = _pack_bf16_pair(y[:, :half], y[:, half:])
        store(g).start()
        return carry

    lax.fori_loop(0, nblk_ref[e], body, 0)

    @pl.when(e == pl.num_programs(0) - 1)
    def _():
        for back in range(n_out, 0, -1):
            @pl.when(total >= back)
            def _(back=back):
                store(total - back).wait()


def _experts(xs, first_blk, n_blk, total_blk, w_gu, w_down):
    n_rows, half = xs.shape
    d = 2 * half
    n_exp, _, ff2 = w_gu.shape
    ff = w_down.shape[1]
    grid_spec = pltpu.PrefetchScalarGridSpec(
        num_scalar_prefetch=3,
        grid=(n_exp,),
        in_specs=[pl.BlockSpec((1, d, ff2), lambda e, *_: (e, 0, 0)),
                  pl.BlockSpec((1, ff, d), lambda e, *_: (e, 0, 0)),
                  pl.BlockSpec(memory_space=pl.ANY)],
        out_specs=pl.BlockSpec(memory_space=pl.ANY),
        scratch_shapes=[pltpu.VMEM((d, ff2), BF16), pltpu.VMEM((ff, d), BF16),
                        pltpu.VMEM((EXPERT_IN_SLOTS, EXPERT_ROWS, half), xs.dtype),
                        pltpu.VMEM((EXPERT_OUT_SLOTS, EXPERT_ROWS, half), xs.dtype),
                        pltpu.SemaphoreType.DMA((EXPERT_IN_SLOTS,)),
                        pltpu.SemaphoreType.DMA((EXPERT_OUT_SLOTS,))],
    )
    return pl.pallas_call(
        _expert_kernel,
        grid_spec=grid_spec,
        out_shape=jax.ShapeDtypeStruct((n_rows, half), xs.dtype),
        compiler_params=pltpu.CompilerParams(
            dimension_semantics=("arbitrary",), vmem_limit_bytes=VMEM_LIMIT),
        name="experts",
    )(first_blk, n_blk, total_blk, w_gu, w_down, xs)


def _combine_kernel(x1_ref, w_ref, yg_ref, wsgu_ref, wsd_ref, lng_ref, lnb_ref, o_ref):
    tc = x1_ref.shape[0]
    x1 = x1_ref[...]
    ff = wsd_ref.shape[0]
    h = jnp.dot(x1.astype(BF16), wsgu_ref[...], preferred_element_type=F32)
    gate, up = h[:, :ff], h[:, ff:]
    act = gate * _sigmoid(gate) * up
    moe = jnp.dot(act.astype(BF16), wsd_ref[...], preferred_element_type=F32)

    half = yg_ref.shape[2]
    wt = jnp.transpose(w_ref[...])
    r_lo = jnp.zeros((tc, half), F32)
    r_hi = jnp.zeros((tc, half), F32)
    for k in range(TOP_K):
        y_lo, y_hi = _unpack_bf16_pair(yg_ref[k])
        w = wt[:, k:k + 1]
        r_lo = r_lo + w * y_lo
        r_hi = r_hi + w * y_hi
    moe = moe + jnp.concatenate([r_lo, r_hi], axis=1)
    o_ref[...] = _layer_norm(DEEPNORM_ALPHA * x1 + moe, lng_ref[...], lnb_ref[...])


def _combine(x1, wgt, yg, chunk, ws_gu, ws_down, ln_g, ln_b, tc):
    t, d = x1.shape
    steps = yg.shape[1] // tc
    tok = lambda i: (chunk * steps + i, 0)
    return pl.pallas_call(
        _combine_kernel,
        grid=(steps,),
        in_specs=[pl.BlockSpec((tc, d), tok),
                  pl.BlockSpec((TOP_K, tc), lambda i: (0, chunk * steps + i)),
                  pl.BlockSpec((TOP_K, tc, yg.shape[2]), lambda i: (0, i, 0)),
                  _resident(ws_gu.shape), _resident(ws_down.shape),
                  _resident(ln_g.shape), _resident(ln_b.shape)],
        out_specs=pl.BlockSpec((tc, d), tok),
        out_shape=jax.ShapeDtypeStruct((t, d), F32),
        input_output_aliases={0: 0},
        compiler_params=pltpu.CompilerParams(
            dimension_semantics=("arbitrary",), vmem_limit_bytes=VMEM_LIMIT),
        name="combine_ln2",
    )(x1, wgt, yg, ws_gu, ws_down, ln_g, ln_b)


def _fused_in_weights(w_in):
    d = w_in.shape[0]
    sizes = (SB_HEADS * HEAD_DIM,) * 3 + (SWA_HEADS * HEAD_DIM, SWA_KV_HEADS * HEAD_DIM,
                                          SWA_KV_HEADS * HEAD_DIM, MEM_HEADS * MEM_HEAD_DIM)
    parts, off = [], 0
    for s in sizes:
        parts.append(w_in[:, off:off + s])
        off += s
    q_sb, k_sb, v_sb, q_sw, k_sw, v_sw, q_m = parts
    gates = w_in[:, off:]
    scale = HEAD_DIM ** -0.5
    q_sw = (q_sw * scale).reshape(d, SWA_KV_HEADS, SWA_GROUP, HEAD_DIM)
    zeros = jnp.zeros((d, SWA_GROUP, HEAD_DIM), w_in.dtype)
    q_sw = jnp.stack([jnp.concatenate([q_sw[:, 0], zeros], axis=-1),
                      jnp.concatenate([zeros, q_sw[:, 1]], axis=-1)], axis=1)
    q_sw = q_sw.reshape(d, SWA_HEADS * LANES)
    w_a = jnp.concatenate([q_sw, q_m, k_sw, v_sw], axis=1).astype(BF16)
    w_b = jnp.concatenate([gates, q_sb * scale, k_sb, v_sb], axis=1).astype(BF16)
    assert w_a.shape[1] == PA_COLS and w_b.shape[1] == PB_COLS
    return w_a, w_b


def _padded_swa_out_weights(w_o_swa):
    d = w_o_swa.shape[1]
    w = w_o_swa.reshape(SWA_KV_HEADS, SWA_GROUP, HEAD_DIM, d)
    zeros = jnp.zeros((SWA_GROUP, HEAD_DIM, d), w_o_swa.dtype)
    w = jnp.stack([jnp.concatenate([w[0], zeros], axis=1),
                   jnp.concatenate([zeros, w[1]], axis=1)], axis=0)
    return w.reshape(SWA_HEADS * LANES, d).astype(BF16)


def kernel(x, mem, w_in, b_gate, w_mem_kv, sinks, w_o_sb, w_o_swa, w_o_mem, w_out,
           ln1_g, ln1_b, w_router, router_bias, w_e_gu, w_e_down, w_s_gu, w_s_down,
           ln2_g, ln2_b):
    batch, seq, d = x.shape
    mem_len = mem.shape[1]
    t = batch * seq
    x2 = x.reshape(t, d)
    row_tile = min(512, t)

    w_a, w_b = _fused_in_weights(w_in)
    p_a = _matmul_bf16(x2, w_a, row_tile, "in_proj_attn")
    mkv = _matmul_bf16(mem.reshape(batch * mem_len, d), w_mem_kv.astype(BF16), mem_len,
                       "mem_kv_proj")
    p, o_sw, o_m = _proj_attn(x2, w_b, b_gate.reshape(1, -1), p_a, mkv, _swa_tables(sinks),
                              batch, seq, mem_len)
    o_sb = _sb_attention(p, batch, seq, min(256, seq), SB_PAIRS_PER_STEP)
    wr_t = w_router.T
    wr_hi = wr_t.astype(BF16)
    wr_lo = (wr_t - wr_hi.astype(F32)).astype(BF16)
    x1, x1p, idx, rank, wgt, cnt = _merge_route(
        o_sb, o_sw, o_m, p, x2, w_o_sb.astype(BF16), _padded_swa_out_weights(w_o_swa),
        w_o_mem.astype(BF16), w_out.astype(BF16), ln1_g.reshape(1, d), ln1_b.reshape(1, d),
        wr_hi, wr_lo, router_bias.reshape(-1, 1).astype(F32), row_tile)

    out = _moe_ln(x1, x1p, idx, rank, wgt, cnt, w_e_gu, w_e_down, w_s_gu, w_s_down, ln2_g, ln2_b)
    return out.reshape(batch, seq, d)


def _moe_ln(x1, x1p, idx, rank, wgt, cnt, w_e_gu, w_e_down, w_s_gu, w_s_down, ln2_g, ln2_b):
    t, d = x1.shape
    counts = cnt[:, 0].astype(I32)
    padded = (counts + EXPERT_ROWS - 1) // EXPERT_ROWS * EXPERT_ROWS
    pad_end = jnp.cumsum(padded)
    pad_start = pad_end - padded
    n_blocks = t * TOP_K // EXPERT_ROWS + N_EXPERTS
    dest = _dest(idx, rank, pad_start.reshape(-1, 1), min(2048, t))

    xs = _dispatch(x1p, dest, n_blocks * EXPERT_ROWS)
    ys = _experts(xs, pad_start // EXPERT_ROWS, padded // EXPERT_ROWS,
                  pad_end[-1:] // EXPERT_ROWS, w_e_gu, w_e_down)
    tchunk = t // COMBINE_CHUNKS
    ws_gu, ws_down = w_s_gu.astype(BF16), w_s_down.astype(BF16)
    out = x1
    for c in range(COMBINE_CHUNKS):
        slots = dest[:, c * tchunk:(c + 1) * tchunk].reshape(-1)
        yg = _gather_rows(ys, slots).reshape(TOP_K, tchunk, -1)
        out = _combine(out, wgt, yg, c, ws_gu, ws_down, ln2_g.reshape(1, d), ln2_b.reshape(1, d),
                       min(256, tchunk))
    return out
```

```python
import functools

import jax
import jax.numpy as jnp
from jax import lax
from jax.experimental import pallas as pl
from jax.experimental.pallas import tpu as pltpu
from jax.experimental.pallas import tpu_sc as plsc

F32 = jnp.float32
BF16 = jnp.bfloat16
I32 = jnp.int32
U32 = jnp.uint32

HEAD_DIM = 64
SB_HEADS = 8
SWA_HEADS = 8
SWA_KV_HEADS = 2
SWA_GROUP = SWA_HEADS // SWA_KV_HEADS
SWA_WINDOW = 128
MEM_HEADS = 4
MEM_HEAD_DIM = 128
N_BRANCH = 3
N_EXPERTS = 256
TOP_K = 8
N_GROUPS = 8
GROUP_SIZE = N_EXPERTS // N_GROUPS
TOPK_GROUPS = 4
EXPERT_FF = 256
SHARED_FF = 256
ROUTED_SCALE = 2.5
LN_EPS = 1e-5
DEPTH = 1
DEEPNORM_ALPHA = (2 * DEPTH) ** 0.25

LANES = 128
SC_CORES = 2
SC_SUBCORES = 16
SC_WORKERS = SC_CORES * SC_SUBCORES
SC_INDEX_WINDOW = 128
SC_GATHER_ROWS = 64
VMEM_LIMIT = 56 * 1024 * 1024

A_QSW = 0
A_QM = 1024
A_KSW = 1536
A_VSW = 1664
PA_COLS = 1792
W_GATE = 3072
C_QSB = 3072
C_KSB = 3584
C_VSB = 4096
PB_COLS = 4608
PROJ_CHUNK = 256
PROJ_ROWS = 512

SB_SKIP = 110.0
SB_PAIRS_PER_STEP = 4

EXPERT_ROWS = 512
EXPERT_IN_SLOTS = 4
EXPERT_OUT_SLOTS = 2
COMBINE_CHUNKS = 8


def _nt_dot(a, b):
    return lax.dot_general(a, b, (((1,), (1,)), ((), ())), preferred_element_type=F32)


def _sigmoid(x):
    return 1.0 / (1.0 + jnp.exp(-x))


def _layer_norm(h, g, b):
    mu = jnp.mean(h, axis=-1, keepdims=True)
    d = h - mu
    var = jnp.mean(d * d, axis=-1, keepdims=True)
    return d * lax.rsqrt(var + LN_EPS) * g + b


def _pack_bf16_pair(a, b):
    a_bits = lax.bitcast_convert_type(a.astype(BF16).astype(F32), U32)
    b_bits = lax.bitcast_convert_type(b.astype(BF16).astype(F32), U32)
    return (a_bits >> 16) | b_bits


def _unpack_bf16_pair(w):
    a = lax.bitcast_convert_type(w << 16, F32)
    b = lax.bitcast_convert_type(w & jnp.uint32(0xFFFF0000), F32)
    return a, b


def _resident(shape):
    nd = len(shape)
    return pl.BlockSpec(shape, lambda *_: (0,) * nd, pipeline_mode=pl.Buffered(1))


def _proj_attn_kernel(x_ref, w_ref, b_ref, hp_ref, qsw_ref, k_refs, v_refs, qm_ref, mk_ref, mv_ref,
                      p_ref, osw_ref, om_ref, *, gate_cols):
    j = pl.program_id(1)
    xb = x_ref[...].astype(BF16)

    def proj_chunk(c):
        cols = slice(c * PROJ_CHUNK, (c + 1) * PROJ_CHUNK)
        acc = jnp.dot(xb, w_ref[:, cols], preferred_element_type=F32)
        if (c + 1) * PROJ_CHUNK <= gate_cols:
            acc = _sigmoid(acc + b_ref[:, cols])
        p_ref[:, cols] = acc.astype(p_ref.dtype)

    blk = SWA_WINDOW
    nblk = x_ref.shape[0] // blk

    def swa(hb):
        rows = slice(hb * blk, (hb + 1) * blk)
        first = (j == 0) if hb == 0 else None
        _swa_block(hp_ref, qsw_ref, rows, k_refs[hb], k_refs[hb + 1], v_refs[hb], v_refs[hb + 1],
                   first, osw_ref)

    side = [functools.partial(swa, hb) for hb in range(nblk)]
    side.append(functools.partial(_mem_heads, qm_ref, mk_ref, mv_ref, om_ref))
    nchunk = p_ref.shape[1] // PROJ_CHUNK
    per = -(-nchunk // len(side))
    for s, work in enumerate(side):
        for c in range(s * per, min((s + 1) * per, nchunk)):
            proj_chunk(c)
        work()


def _proj_attn(x2, w_b, b_gate, p_a, mkv, swa_tables, batch, seq, mem_len):
    t, d = x2.shape
    tm = PROJ_ROWS
    nt = seq // tm
    blk = SWA_WINDOW
    nblk = tm // blk
    nb = seq // blk
    qw = SWA_HEADS * LANES
    mw = MEM_HEADS * MEM_HEAD_DIM
    tile = lambda w, col: pl.BlockSpec((tm, w), lambda b, j: (b * nt + j, col))
    kv = lambda col: [pl.BlockSpec((blk, LANES),
                                   functools.partial(lambda b, j, off, col: (
                                       b * nb + jnp.maximum(nblk * j + off, 0), col), off=off, col=col))
                      for off in range(-1, nblk)]
    return pl.pallas_call(
        functools.partial(_proj_attn_kernel, gate_cols=b_gate.shape[1]),
        grid=(batch, nt),
        in_specs=[tile(d, 0), _resident(w_b.shape), _resident(b_gate.shape),
                  _resident(swa_tables.shape),
                  tile(qw, A_QSW // qw),
                  kv(A_KSW // LANES), kv(A_VSW // LANES),
                  tile(mw, A_QM // mw),
                  pl.BlockSpec((mem_len, mw), lambda b, j: (b, 0)),
                  pl.BlockSpec((mem_len, mw), lambda b, j: (b, 1))],
        out_specs=[tile(w_b.shape[1], 0), tile(qw, 0), tile(mw, 0)],
        out_shape=[jax.ShapeDtypeStruct((t, w_b.shape[1]), BF16),
                   jax.ShapeDtypeStruct((t, qw), BF16),
                   jax.ShapeDtypeStruct((t, mw), BF16)],
        compiler_params=pltpu.CompilerParams(
            dimension_semantics=("parallel", "arbitrary"), vmem_limit_bytes=VMEM_LIMIT),
        name="in_proj_swa_mem",
    )(x2, w_b, b_gate, swa_tables, p_a, [p_a] * (nblk + 1), [p_a] * (nblk + 1), p_a, mkv, mkv)


def _mm_kernel(x_ref, w_ref, o_ref):
    o_ref[...] = jnp.dot(x_ref[...].astype(BF16), w_ref[...],
                         preferred_element_type=F32).astype(o_ref.dtype)


def _matmul_bf16(x2, w, tm, name):
    t, d = x2.shape
    n = w.shape[1]
    return pl.pallas_call(
        _mm_kernel,
        grid=(t // tm,),
        in_specs=[pl.BlockSpec((tm, d), lambda i: (i, 0)), _resident((d, n))],
        out_specs=pl.BlockSpec((tm, n), lambda i: (i, 0)),
        out_shape=jax.ShapeDtypeStruct((t, n), BF16),
        compiler_params=pltpu.CompilerParams(
            dimension_semantics=("parallel",), vmem_limit_bytes=VMEM_LIMIT),
        name=name,
    )(x2, w)


def _sb_kernel(q_ref, k_ref, v_ref, o_ref, *, tq):
    i = pl.program_id(2)
    pairs = q_ref.shape[1] // LANES
    lane = lax.broadcasted_iota(I32, (1, LANES), 1)
    r = lax.broadcasted_iota(I32, (tq, tq), 0)
    c = lax.broadcasted_iota(I32, (tq, tq), 1)
    tri = (r >= c).astype(BF16)
    causal = c < r
    nh = LANES // HEAD_DIM
    hmasks = [(lane >= h * HEAD_DIM) & (lane < (h + 1) * HEAD_DIM) for h in range(nh)]
    qs = []
    for p in range(pairs):
        q = q_ref[:, p * LANES:(p + 1) * LANES]
        qs.append(jnp.concatenate([jnp.where(hm, q, jnp.zeros_like(q)) for hm in hmasks], axis=0))
    causal2 = jnp.concatenate([causal] * nh, axis=0)
    tri2 = jnp.concatenate([tri, tri], axis=0)
    mp = nh * tq

    def blocks(jobs, carry, acc):
        chains = [(jb, p) for jb in range(len(jobs)) for p in range(pairs)]
        z, hl, suffix, ab, av, cin = {}, {}, {}, {}, {}, {}

        def rows(jb):
            return pl.ds(pl.multiple_of(jobs[jb][0] * tq, tq), tq)

        def scores(jb, p):
            z[jb, p] = _nt_dot(qs[p], k_ref[rows(jb), p * LANES:(p + 1) * LANES])

        def softplus_split(jb, p):
            sp = jnp.maximum(z[jb, p], 0.0) + jnp.log(1.0 + jnp.exp(-jnp.abs(z[jb, p])))
            if jobs[jb][1]:
                sp = jnp.where(causal2, sp, 0.0)
            hi = sp.astype(BF16)
            lo = (sp - hi.astype(F32)).astype(BF16)
            hl[jb, p] = jnp.concatenate([hi, lo], axis=1)

        def cumsum(jb, p):
            suffix[jb, p] = jnp.dot(hl[jb, p], tri2, preferred_element_type=F32)

        def weights(jb, p):
            cin[jb, p] = carry[p] if jb == 0 else cin[jb - 1, p] + step_sum(jb - 1, p)
            a = jnp.exp((z[jb, p] - cin[jb, p]) - suffix[jb, p])
            if jobs[jb][1]:
                a = jnp.where(causal2, a, 0.0)
            ab[jb, p] = a.astype(BF16)

        def values(jb, p):
            out = jnp.dot(ab[jb, p], v_ref[rows(jb), p * LANES:(p + 1) * LANES],
                          preferred_element_type=F32)
            live = jobs[jb][2]
            av[jb, p] = out if live is None else jnp.where(live, out, 0.0)

        def step_sum(jb, p):
            total = suffix[jb, p][:, 0:1]
            live = jobs[jb][2]
            return total if live is None else jnp.where(live, total, 0.0)

        stages = (scores, softplus_split, cumsum, weights, values)
        for t in range(len(chains) + len(stages) - 1):
            for s in reversed(range(len(stages))):
                if 0 <= t - s < len(chains):
                    stages[s](*chains[t - s])
        last = len(jobs) - 1
        new_acc = []
        for p in range(pairs):
            total = acc[p]
            for jb in range(len(jobs)):
                total = total + av[jb, p]
            new_acc.append(total)
        return [cin[last, p] + step_sum(last, p) for p in range(pairs)], new_acc

    def block(kb, carry, acc, diag):
        return blocks([(kb, diag, None)], carry, acc)

    carry, acc = blocks([(i, True, None), (jnp.maximum(i - 1, 0), False, i > 0)],
                        [jnp.zeros((mp, 1), F32)] * pairs,
                        [jnp.zeros((mp, LANES), F32)] * pairs)

    def cond(s):
        kb, carry, _ = s
        lowest = carry[0]
        for cp in carry[1:]:
            lowest = jnp.minimum(lowest, cp)
        return (kb >= 0) & (jnp.min(lowest) < SB_SKIP)

    def body(s):
        kb, carry, acc = s
        carry, acc = block(kb, carry, acc, False)
        return kb - 1, carry, acc

    _, _, acc = lax.while_loop(cond, body, (i - 2, carry, acc))
    for p in range(pairs):
        o_ref[:, p * LANES:(p + 1) * LANES] = jnp.where(
            hmasks[0], acc[p][:tq], acc[p][tq:]).astype(o_ref.dtype)


def _sb_attention(p, batch, seq, tq, pairs):
    t = batch * seq
    nq = seq // tq
    w = pairs * LANES
    ngrp = SB_HEADS * HEAD_DIM // w
    qc, kc, vc = C_QSB // w, C_KSB // w, C_VSB // w
    return pl.pallas_call(
        functools.partial(_sb_kernel, tq=tq),
        grid=(batch, ngrp, nq),
        in_specs=[pl.BlockSpec((tq, w), lambda b, h, i: (b * nq + i, qc + h)),
                  pl.BlockSpec((seq, w), lambda b, h, i: (b, kc + h)),
                  pl.BlockSpec((seq, w), lambda b, h, i: (b, vc + h))],
        out_specs=pl.BlockSpec((tq, w), lambda b, h, i: (b * nq + i, h)),
        out_shape=jax.ShapeDtypeStruct((t, SB_HEADS * HEAD_DIM), BF16),
        compiler_params=pltpu.CompilerParams(
            dimension_semantics=("parallel", "parallel", "arbitrary"),
            vmem_limit_bytes=VMEM_LIMIT),
        name="sb_attention",
    )(p, p, p)


def _swa_block(hp_ref, q_ref, rows, kprev, kcur, vprev, vcur, first, o_ref):
    blk = kcur.shape[0]
    nheads = q_ref.shape[1] // LANES
    sink = hp_ref[:, 2 * blk:]
    lane = lax.broadcasted_iota(I32, (1, LANES), 1)
    qs = jnp.concatenate([q_ref[rows, g * LANES:(g + 1) * LANES] for g in range(nheads)], axis=0)
    keys = jnp.concatenate([kprev[...], kcur[...]], axis=0)
    vals = jnp.concatenate([vprev[...], vcur[...]], axis=0)
    z = _nt_dot(qs, keys) + hp_ref[:, :2 * blk]
    if first is not None:
        col = lax.broadcasted_iota(I32, (1, 2 * blk), 1)
        z = jnp.where(first & (col < blk), jnp.float32(-jnp.inf), z)
    m = jnp.maximum(jnp.max(z, axis=1, keepdims=True), sink)
    p = jnp.exp(z - jnp.concatenate([m, m], axis=1))
    den = jnp.sum(p, axis=1, keepdims=True) + jnp.exp(sink - m)
    o = jnp.dot(p.astype(BF16), vals, preferred_element_type=F32) / den
    for g in range(nheads):
        kv = g // SWA_GROUP
        kvmask = (lane >= kv * HEAD_DIM) & (lane < (kv + 1) * HEAD_DIM)
        o_ref[rows, g * LANES:(g + 1) * LANES] = jnp.where(
            kvmask, o[g * blk:(g + 1) * blk], 0.0).astype(o_ref.dtype)


def _swa_tables(sinks):
    w = SWA_WINDOW
    slopes = jnp.exp2(-8.0 * jnp.arange(1, SWA_HEADS + 1, dtype=F32) / SWA_HEADS)[:, None, None]
    r = jnp.arange(w)[:, None]
    c = jnp.arange(w)[None, :]
    dist = (r - c).astype(F32)[None]
    neg = jnp.float32(-jnp.inf)
    bias_c = jnp.where((c <= r)[None], -slopes * dist, neg)
    bias_p = jnp.where((c > r)[None], -slopes * (dist + w), neg)
    sink = jnp.broadcast_to(sinks.astype(F32)[:, None, None], (SWA_HEADS, w, w))
    return jnp.concatenate([bias_p, bias_c, sink], axis=2).reshape(SWA_HEADS * w, 3 * w)


def _mem_heads(q_ref, mk_ref, mv_ref, o_ref):
    scale = MEM_HEAD_DIM ** -0.5
    for h in range(MEM_HEADS):
        cols = slice(h * MEM_HEAD_DIM, (h + 1) * MEM_HEAD_DIM)
        z = _nt_dot(q_ref[:, cols], mk_ref[:, cols]) * scale
        m = jnp.max(z, axis=1, keepdims=True)
        p = jnp.exp(z - m)
        den = jnp.sum(p, axis=1, keepdims=True)
        o = jnp.dot(p.astype(BF16), mv_ref[:, cols], preferred_element_type=F32) / den
        o_ref[:, cols] = o.astype(o_ref.dtype)


def _merge_route_kernel(osb_ref, osw_ref, om_ref, g_ref, x_ref, wsb_ref, wsw_ref, wm_ref,
                        wout_ref, lng_ref, lnb_ref, wrh_ref, wrl_ref, rbias_ref,
                        x1_ref, x1p_ref, idx_ref, rank_ref, wgt_ref, cnt_ref, x1_prev, carry_ref):
    i = pl.program_id(0)

    @pl.when(i == 0)
    def _():
        x1_prev[...] = jnp.zeros_like(x1_prev)
        carry_ref[...] = jnp.zeros_like(carry_ref)

    d = x_ref.shape[1]
    st = {}

    def branch(b, o_ref, w_ref):
        def run():
            term = g_ref[:, b * d:(b + 1) * d].astype(F32) * jnp.dot(
                o_ref[...], w_ref[...], preferred_element_type=F32)
            st["merged"] = term if b == 0 else st["merged"] + term
        return run

    def out_proj():
        st["y"] = jnp.dot(st["merged"].astype(BF16), wout_ref[...], preferred_element_type=F32)

    idx, rank, wgt, count = _route(
        x1_prev[...], wrh_ref[...], wrl_ref[...], rbias_ref[...], carry_ref[...],
        side_work=(branch(0, osb_ref, wsb_ref), branch(1, osw_ref, wsw_ref),
                   branch(2, om_ref, wm_ref), out_proj))
    x1 = _layer_norm(DEEPNORM_ALPHA * x_ref[...] + st["y"], lng_ref[...], lnb_ref[...])
    x1_ref[...] = x1
    x1p_ref[...] = _pack_bf16_pair(x1[:, :d // 2], x1[:, d // 2:])
    idx_ref[...] = idx
    rank_ref[...] = rank
    wgt_ref[...] = wgt
    carry_ref[...] = carry_ref[...] + jnp.where(i > 0, count, 0.0)
    cnt_ref[...] = carry_ref[...]
    x1_prev[...] = x1


def _merge_route(o_sb, o_sw, o_m, p, x2, w_sb, w_sw, w_m, w_out, ln_g, ln_b, wr_hi, wr_lo,
                 bias_col, tm):
    t, d = x2.shape
    n = t // tm
    cur = lambda i: (jnp.minimum(i, n - 1), 0)
    row = lambda w: pl.BlockSpec((tm, w), cur)
    slot = pl.BlockSpec((TOP_K, tm), lambda i: (0, jnp.maximum(i - 1, 0)))
    return pl.pallas_call(
        _merge_route_kernel,
        grid=(n + 1,),
        in_specs=[row(o_sb.shape[1]), row(o_sw.shape[1]), row(o_m.shape[1]),
                  pl.BlockSpec((tm, N_BRANCH * d), cur),
                  row(d),
                  _resident(w_sb.shape), _resident(w_sw.shape), _resident(w_m.shape),
                  _resident(w_out.shape), _resident(ln_g.shape), _resident(ln_b.shape),
                  _resident(wr_hi.shape), _resident(wr_lo.shape), _resident(bias_col.shape)],
        out_specs=[row(d), row(d // 2), slot, slot, slot,
                   pl.BlockSpec((N_EXPERTS, 1), lambda i: (0, 0))],
        out_shape=[jax.ShapeDtypeStruct((t, d), F32), jax.ShapeDtypeStruct((t, d // 2), U32),
                   jax.ShapeDtypeStruct((TOP_K, t), I32),
                   jax.ShapeDtypeStruct((TOP_K, t), I32),
                   jax.ShapeDtypeStruct((TOP_K, t), F32),
                   jax.ShapeDtypeStruct((N_EXPERTS, 1), F32)],
        scratch_shapes=[pltpu.VMEM((tm, d), F32), pltpu.VMEM((N_EXPERTS, 1), F32)],
        compiler_params=pltpu.CompilerParams(
            dimension_semantics=("arbitrary",), vmem_limit_bytes=VMEM_LIMIT),
        name="merge_route",
    )(o_sb, o_sw, o_m, p, x2, w_sb, w_sw, w_m, w_out, ln_g, ln_b, wr_hi, wr_lo, bias_col)


def _route(x, wh, wl, bias, carry, side_work=()):
    tr = x.shape[0]
    xh = x.astype(BF16)
    xl = (x - xh.astype(F32)).astype(BF16)
    logits = _nt_dot(wh, xh) + _nt_dot(wh, xl) + _nt_dot(wl, xh)
    scores = _sigmoid(logits)
    biased = scores + bias
    neg = jnp.float32(-jnp.inf)

    sub = lax.broadcasted_iota(I32, (GROUP_SIZE, tr), 0)
    gscore = []
    for g in range(N_GROUPS):
        blk = biased[g * GROUP_SIZE:(g + 1) * GROUP_SIZE, :]
        m1 = jnp.max(blk, axis=0, keepdims=True)
        i1 = jnp.min(jnp.where(blk == m1, sub, GROUP_SIZE), axis=0, keepdims=True)
        m2 = jnp.max(jnp.where(sub == i1, neg, blk), axis=0, keepdims=True)
        gscore.append(m1 + m2)
    gs = jnp.concatenate(gscore, axis=0)

    giota = lax.broadcasted_iota(I32, (N_GROUPS, tr), 0)
    gsel = jnp.zeros((N_GROUPS, tr), F32)
    for _ in range(TOPK_GROUPS):
        m = jnp.max(gs, axis=0, keepdims=True)
        gi = jnp.min(jnp.where(gs == m, giota, N_GROUPS), axis=0, keepdims=True)
        hit = giota == gi
        gsel = jnp.where(hit, 1.0, gsel)
        gs = jnp.where(hit, neg, gs)

    masked = jnp.concatenate(
        [jnp.where(gsel[g:g + 1, :] > 0.0, biased[g * GROUP_SIZE:(g + 1) * GROUP_SIZE, :], neg)
         for g in range(N_GROUPS)], axis=0)

    eiota = lax.broadcasted_iota(I32, (N_EXPERTS, tr), 0)
    sel = jnp.zeros((N_EXPERTS, tr), F32)
    idx_rows, w_rows = [], []
    side_work = list(side_work)
    for k in range(TOP_K):
        if side_work and k % 2 == 0:
            side_work.pop(0)()
        m = jnp.max(masked, axis=0, keepdims=True)
        ei = jnp.min(jnp.where(masked == m, eiota, N_EXPERTS), axis=0, keepdims=True)
        hit = eiota == ei
        idx_rows.append(ei)
        w_rows.append(jnp.sum(jnp.where(hit, scores, 0.0), axis=0, keepdims=True))
        sel = jnp.where(hit, 1.0, sel)
        masked = jnp.where(hit, neg, masked)

    wsum = w_rows[0]
    for wk in w_rows[1:]:
        wsum = wsum + wk
    wgt = jnp.concatenate(w_rows, axis=0) / wsum * ROUTED_SCALE
    idx = jnp.concatenate(idx_rows, axis=0)

    a = lax.broadcasted_iota(I32, (tr, tr), 0)
    b = lax.broadcasted_iota(I32, (tr, tr), 1)
    before = (a < b).astype(BF16)
    rank = jnp.dot(sel.astype(BF16), before, preferred_element_type=F32) + carry
    rank_rows = [jnp.sum(jnp.where(eiota == ei, rank, 0.0), axis=0, keepdims=True)
                 for ei in idx_rows]
    rank = jnp.concatenate(rank_rows, axis=0).astype(I32)
    return idx, rank, wgt, jnp.sum(sel, axis=1, keepdims=True)


def _dest_kernel(idx_ref, rank_ref, start_ref, dest_ref):
    tr = idx_ref.shape[1]
    eiota = lax.broadcasted_iota(I32, (N_EXPERTS, tr), 0)
    rows = []
    for k in range(TOP_K):
        hit = eiota == idx_ref[k:k + 1, :]
        rows.append(jnp.sum(jnp.where(hit, start_ref[...], 0), axis=0, keepdims=True))
    dest_ref[...] = jnp.concatenate(rows, axis=0) + rank_ref[...]


def _dest(idx, rank, start_col, tr):
    t = idx.shape[1]
    slot = pl.BlockSpec((TOP_K, tr), lambda i: (0, i))
    return pl.pallas_call(
        _dest_kernel,
        grid=(t // tr,),
        in_specs=[slot, slot, _resident(start_col.shape)],
        out_specs=slot,
        out_shape=jax.ShapeDtypeStruct((TOP_K, t), I32),
        compiler_params=pltpu.CompilerParams(dimension_semantics=("parallel",)),
        name="slot_dest",
    )(idx, rank, start_col)


def _sc_worker_id():
    return lax.axis_index("s") * SC_CORES + lax.axis_index("c")


def _dispatch(x1p, dest, n_rows):
    t, w = x1p.shape
    per = t // SC_WORKERS
    win = min(SC_INDEX_WINDOW, per)
    mesh = plsc.VectorSubcoreMesh(core_axis_name="c", subcore_axis_name="s")

    @functools.partial(
        pl.kernel, mesh=mesh,
        out_type=jax.ShapeDtypeStruct((n_rows, w), x1p.dtype),
        scratch_types=[pltpu.VMEM((TOP_K, win), I32),
                       pltpu.VMEM((win, w), x1p.dtype),
                       pltpu.SemaphoreType.DMA],
        name="sc_dispatch",
    )
    def scatter_rows(x_hbm, dest_hbm, xs_hbm, idx_v, rows_v, sem):
        base = _sc_worker_id() * per

        @pl.loop(0, per // win)
        def _(j):
            t0 = pl.multiple_of(base + j * win, win)
            pltpu.sync_copy(dest_hbm.at[:, pl.ds(t0, win)], idx_v)
            pltpu.sync_copy(x_hbm.at[pl.ds(t0, win)], rows_v)
            copies = [pltpu.async_copy(rows_v, xs_hbm.at[idx_v.at[k]], sem) for k in range(TOP_K)]
            for c in copies:
                c.wait()

    return scatter_rows(x1p, dest)


def _gather_rows(table, idx):
    n = idx.shape[0]
    w = table.shape[1]
    per = n // SC_WORKERS
    chunk = min(SC_GATHER_ROWS, per // 2)
    assert n % SC_WORKERS == 0 and per % (2 * chunk) == 0, (n, chunk)
    mesh = plsc.VectorSubcoreMesh(core_axis_name="c", subcore_axis_name="s")

    @functools.partial(
        pl.kernel, mesh=mesh,
        out_type=jax.ShapeDtypeStruct((n, w), table.dtype),
        scratch_types=[pltpu.VMEM((per,), I32),
                       pltpu.VMEM((2, chunk, w), table.dtype),
                       pltpu.SemaphoreType.DMA((2,)),
                       pltpu.SemaphoreType.DMA((2,))],
        name="sc_gather",
    )
    def gather_rows(table_hbm, idx_hbm, out_hbm, idx_v, rows_v, gather_sem, put_sem):
        base = _sc_worker_id() * per
        nchunks = per // chunk
        pltpu.sync_copy(idx_hbm.at[pl.ds(base, per)], idx_v)

        def gather(j, b):
            off = pl.multiple_of(j * chunk, chunk)
            return pltpu.make_async_copy(table_hbm.at[idx_v.at[pl.ds(off, chunk)]],
                                         rows_v.at[b], gather_sem.at[b])

        def put(j, b):
            off = pl.multiple_of(j * chunk, chunk)
            return pltpu.make_async_copy(rows_v.at[b], out_hbm.at[pl.ds(base + off, chunk)],
                                         put_sem.at[b])

        gather(0, 0).start()

        @pl.loop(0, nchunks, step=2)
        def _(j):
            for b in (0, 1):
                jj = j + b

                @pl.when(jj + 1 < nchunks)
                def _():
                    @pl.when(jj >= 1)
                    def _():
                        put(jj - 1, 1 - b).wait()
                    gather(jj + 1, 1 - b).start()

                gather(jj, b).wait()
                put(jj, b).start()

        put(nchunks - 2, 0).wait()
        put(nchunks - 1, 1).wait()

    return gather_rows(table, idx)


def _expert_kernel(first_ref, nblk_ref, total_ref, wgu_ref, wd_ref, xs_hbm, ys_hbm,
                   wgu_s, wd_s, xbuf, ybuf, in_sem, out_sem):
    e = pl.program_id(0)
    total = total_ref[0]
    n_in, rows, half = xbuf.shape
    n_out = ybuf.shape[0]
    ahead = n_in - 1

    def block_rows(g):
        return pl.ds(pl.multiple_of(g * rows, rows), rows)

    def load(g):
        slot = g % n_in
        return pltpu.make_async_copy(xs_hbm.at[block_rows(g), :], xbuf.at[slot], in_sem.at[slot])

    def store(g):
        slot = g % n_out
        return pltpu.make_async_copy(ybuf.at[slot], ys_hbm.at[block_rows(g), :], out_sem.at[slot])

    @pl.when(e == 0)
    def _():
        for g in range(ahead):
            @pl.when(g < total)
            def _(g=g):
                load(g).start()

    wgu_s[...] = wgu_ref[0].astype(BF16)
    wd_s[...] = wd_ref[0].astype(BF16)
    g0 = first_ref[e]
    ff = wd_s.shape[0]

    def body(j, carry):
        g = g0 + j
        load(g).wait()

        @pl.when(g + ahead < total)
        def _():
            load(g + ahead).start()

        @pl.when(g >= n_out)
        def _():
            store(g - n_out).wait()

        x_lo, x_hi = _unpack_bf16_pair(xbuf[g % n_in])
        h = (jnp.dot(x_lo.astype(BF16), wgu_s[:half, :], preferred_element_type=F32)
             + jnp.dot(x_hi.astype(BF16), wgu_s[half:, :], preferred_element_type=F32))
        gate, up = h[:, :ff], h[:, ff:]
        act = gate * _sigmoid(gate) * up
        y = jnp.dot(act.astype(BF16), wd_s[...], preferred_element_type=F32)
        ybuf[g % n_out] = _pack_bf16_pair(y[:, :half], y[:, half:])
        store(g).start()
        return carry

    lax.fori_loop(0, nblk_ref[e], body, 0)

    @pl.when(e == pl.num_programs(0) - 1)
    def _():
        for back in range(n_out, 0, -1):
            @pl.when(total >= back)
            def _(back=back):
                store(total - back).wait()


def _experts(xs, first_blk, n_blk, total_blk, w_gu, w_down):
    n_rows, half = xs.shape
    d = 2 * half
    n_exp, _, ff2 = w_gu.shape
    ff = w_down.shape[1]
    grid_spec = pltpu.PrefetchScalarGridSpec(
        num_scalar_prefetch=3,
        grid=(n_exp,),
        in_specs=[pl.BlockSpec((1, d, ff2), lambda e, *_: (e, 0, 0)),
                  pl.BlockSpec((1, ff, d), lambda e, *_: (e, 0, 0)),
                  pl.BlockSpec(memory_space=pl.ANY)],
        out_specs=pl.BlockSpec(memory_space=pl.ANY),
        scratch_shapes=[pltpu.VMEM((d, ff2), BF16), pltpu.VMEM((ff, d), BF16),
                        pltpu.VMEM((EXPERT_IN_SLOTS, EXPERT_ROWS, half), xs.dtype),
                        pltpu.VMEM((EXPERT_OUT_SLOTS, EXPERT_ROWS, half), xs.dtype),
                        pltpu.SemaphoreType.DMA((EXPERT_IN_SLOTS,)),
                        pltpu.SemaphoreType.DMA((EXPERT_OUT_SLOTS,))],
    )
    return pl.pallas_call(
        _expert_kernel,
        grid_spec=grid_spec,
        out_shape=jax.ShapeDtypeStruct((n_rows, half), xs.dtype),
        compiler_params=pltpu.CompilerParams(
            dimension_semantics=("arbitrary",), vmem_limit_bytes=VMEM_LIMIT),
        name="experts",
    )(first_blk, n_blk, total_blk, w_gu, w_down, xs)


def _combine_kernel(x1_ref, w_ref, yg_ref, wsgu_ref, wsd_ref, lng_ref, lnb_ref, o_ref):
    tc = x1_ref.shape[0]
    x1 = x1_ref[...]
    ff = wsd_ref.shape[0]
    h = jnp.dot(x1.astype(BF16), wsgu_ref[...], preferred_element_type=F32)
    gate, up = h[:, :ff], h[:, ff:]
    act = gate * _sigmoid(gate) * up
    moe = jnp.dot(act.astype(BF16), wsd_ref[...], preferred_element_type=F32)

    half = yg_ref.shape[2]
    wt = jnp.transpose(w_ref[...])
    r_lo = jnp.zeros((tc, half), F32)
    r_hi = jnp.zeros((tc, half), F32)
    for k in range(TOP_K):
        y_lo, y_hi = _unpack_bf16_pair(yg_ref[k])
        w = wt[:, k:k + 1]
        r_lo = r_lo + w * y_lo
        r_hi = r_hi + w * y_hi
    moe = moe + jnp.concatenate([r_lo, r_hi], axis=1)
    o_ref[...] = _layer_norm(DEEPNORM_ALPHA * x1 + moe, lng_ref[...], lnb_ref[...])


def _combine(x1, wgt, yg, chunk, ws_gu, ws_down, ln_g, ln_b, tc):
    t, d = x1.shape
    steps = yg.shape[1] // tc
    tok = lambda i: (chunk * steps + i, 0)
    return pl.pallas_call(
        _combine_kernel,
        grid=(steps,),
        in_specs=[pl.BlockSpec((tc, d), tok),
                  pl.BlockSpec((TOP_K, tc), lambda i: (0, chunk * steps + i)),
                  pl.BlockSpec((TOP_K, tc, yg.shape[2]), lambda i: (0, i, 0)),
                  _resident(ws_gu.shape), _resident(ws_down.shape),
                  _resident(ln_g.shape), _resident(ln_b.shape)],
        out_specs=pl.BlockSpec((tc, d), tok),
        out_shape=jax.ShapeDtypeStruct((t, d), F32),
        input_output_aliases={0: 0},
        compiler_params=pltpu.CompilerParams(
            dimension_semantics=("arbitrary",), vmem_limit_bytes=VMEM_LIMIT),
        name="combine_ln2",
    )(x1, wgt, yg, ws_gu, ws_down, ln_g, ln_b)


def _fused_in_weights(w_in):
    d = w_in.shape[0]
    sizes = (SB_HEADS * HEAD_DIM,) * 3 + (SWA_HEADS * HEAD_DIM, SWA_KV_HEADS * HEAD_DIM,
                                          SWA_KV_HEADS * HEAD_DIM, MEM_HEADS * MEM_HEAD_DIM)
    parts, off = [], 0
    for s in sizes:
        parts.append(w_in[:, off:off + s])
        off += s
    q_sb, k_sb, v_sb, q_sw, k_sw, v_sw, q_m = parts
    gates = w_in[:, off:]
    scale = HEAD_DIM ** -0.5
    q_sw = (q_sw * scale).reshape(d, SWA_KV_HEADS, SWA_GROUP, HEAD_DIM)
    zeros = jnp.zeros((d, SWA_GROUP, HEAD_DIM), w_in.dtype)
    q_sw = jnp.stack([jnp.concatenate([q_sw[:, 0], zeros], axis=-1),
                      jnp.concatenate([zeros, q_sw[:, 1]], axis=-1)], axis=1)
    q_sw = q_sw.reshape(d, SWA_HEADS * LANES)
    w_a = jnp.concatenate([q_sw, q_m, k_sw, v_sw], axis=1).astype(BF16)
    w_b = jnp.concatenate([gates, q_sb * scale, k_sb, v_sb], axis=1).astype(BF16)
    assert w_a.shape[1] == PA_COLS and w_b.shape[1] == PB_COLS
    return w_a, w_b


def _padded_swa_out_weights(w_o_swa):
    d = w_o_swa.shape[1]
    w = w_o_swa.reshape(SWA_KV_HEADS, SWA_GROUP, HEAD_DIM, d)
    zeros = jnp.zeros((SWA_GROUP, HEAD_DIM, d), w_o_swa.dtype)
    w = jnp.stack([jnp.concatenate([w[0], zeros], axis=1),
                   jnp.concatenate([zeros, w[1]], axis=1)], axis=0)
    return w.reshape(SWA_HEADS * LANES, d).astype(BF16)


def kernel(x, mem, w_in, b_gate, w_mem_kv, sinks, w_o_sb, w_o_swa, w_o_mem, w_out,
           ln1_g, ln1_b, w_router, router_bias, w_e_gu, w_e_down, w_s_gu, w_s_down,
           ln2_g, ln2_b):
    batch, seq, d = x.shape
    mem_len = mem.shape[1]
    t = batch * seq
    x2 = x.reshape(t, d)
    row_tile = min(512, t)

    w_a, w_b = _fused_in_weights(w_in)
    p_a = _matmul_bf16(x2, w_a, row_tile, "in_proj_attn")
    mkv = _matmul_bf16(mem.reshape(batch * mem_len, d), w_mem_kv.astype(BF16), mem_len,
                       "mem_kv_proj")
    p, o_sw, o_m = _proj_attn(x2, w_b, b_gate.reshape(1, -1), p_a, mkv, _swa_tables(sinks),
                              batch, seq, mem_len)
    o_sb = _sb_attention(p, batch, seq, min(256, seq), SB_PAIRS_PER_STEP)
    wr_t = w_router.T
    wr_hi = wr_t.astype(BF16)
    wr_lo = (wr_t - wr_hi.astype(F32)).astype(BF16)
    x1, x1p, idx, rank, wgt, cnt = _merge_route(
        o_sb, o_sw, o_m, p, x2, w_o_sb.astype(BF16), _padded_swa_out_weights(w_o_swa),
        w_o_mem.astype(BF16), w_out.astype(BF16), ln1_g.reshape(1, d), ln1_b.reshape(1, d),
        wr_hi, wr_lo, router_bias.reshape(-1, 1).astype(F32), row_tile)

    out = _moe_ln(x1, x1p, idx, rank, wgt, cnt, w_e_gu, w_e_down, w_s_gu, w_s_down, ln2_g, ln2_b)
    return out.reshape(batch, seq, d)


def _moe_ln(x1, x1p, idx, rank, wgt, cnt, w_e_gu, w_e_down, w_s_gu, w_s_down, ln2_g, ln2_b):
    t, d = x1.shape
    counts = cnt[:, 0].astype(I32)
    padded = (counts + EXPERT_ROWS - 1) // EXPERT_ROWS * EXPERT_ROWS
    pad_end = jnp.cumsum(padded)
    pad_start = pad_end - padded
    n_blocks = t * TOP_K // EXPERT_ROWS + N_EXPERTS
    dest = _dest(idx, rank, pad_start.reshape(-1, 1), min(2048, t))

    xs = _dispatch(x1p, dest, n_blocks * EXPERT_ROWS)
    ys = _experts(xs, pad_start // EXPERT_ROWS, padded // EXPERT_ROWS,
                  pad_end[-1:] // EXPERT_ROWS, w_e_gu, w_e_down)
    tchunk = t // COMBINE_CHUNKS
    ws_gu, ws_down = w_s_gu.astype(BF16), w_s_down.astype(BF16)
    out = x1
    for c in range(COMBINE_CHUNKS):
        slots = dest[:, c * tchunk:(c + 1) * tchunk].reshape(-1)
        yg = _gather_rows(ys, slots).reshape(TOP_K, tchunk, -1)
        out = _combine(out, wgt, yg, c, ws_gu, ws_down, ln2_g.reshape(1, d), ln2_b.reshape(1, d),
                       min(256, tchunk))
    return out
```

```python
import functools

import jax
import jax.numpy as jnp
from jax import lax
from jax.experimental import pallas as pl
from jax.experimental.pallas import tpu as pltpu
from jax.experimental.pallas import tpu_sc as plsc

F32 = jnp.float32
BF16 = jnp.bfloat16
I32 = jnp.int32
U32 = jnp.uint32

HEAD_DIM = 64
SB_HEADS = 8
SWA_HEADS = 8
SWA_KV_HEADS = 2
SWA_GROUP = SWA_HEADS // SWA_KV_HEADS
SWA_WINDOW = 128
MEM_HEADS = 4
MEM_HEAD_DIM = 128
N_BRANCH = 3
N_EXPERTS = 256
TOP_K = 8
N_GROUPS = 8
GROUP_SIZE = N_EXPERTS // N_GROUPS
TOPK_GROUPS = 4
EXPERT_FF = 256
SHARED_FF = 256
ROUTED_SCALE = 2.5
LN_EPS = 1e-5
DEPTH = 1
DEEPNORM_ALPHA = (2 * DEPTH) ** 0.25

LANES = 128
SC_CORES = 2
SC_SUBCORES = 16
SC_WORKERS = SC_CORES * SC_SUBCORES
SC_INDEX_WINDOW = 128
SC_GATHER_ROWS = 64
VMEM_LIMIT = 56 * 1024 * 1024

A_QSW = 0
A_QM = 1024
A_KSW = 1536
A_VSW = 1664
PA_COLS = 1792
W_GATE = 3072
C_QSB = 3072
C_KSB = 3584
C_VSB = 4096
PB_COLS = 4608
PROJ_CHUNK = 256
PROJ_ROWS = 512

SB_SKIP = 110.0
SB_PAIRS_PER_STEP = 4

EXPERT_ROWS = 512
EXPERT_AHEAD = 4
EXPERT_IN_SLOTS = EXPERT_AHEAD + 2
EXPERT_OUT_SLOTS = 4
COMBINE_CHUNKS = 8


def _nt_dot(a, b):
    return lax.dot_general(a, b, (((1,), (1,)), ((), ())), preferred_element_type=F32)


def _sigmoid(x):
    return 1.0 / (1.0 + jnp.exp(-x))


def _layer_norm(h, g, b):
    mu = jnp.mean(h, axis=-1, keepdims=True)
    d = h - mu
    var = jnp.mean(d * d, axis=-1, keepdims=True)
    return d * lax.rsqrt(var + LN_EPS) * g + b


def _pack_bf16_pair(a, b):
    a_bits = lax.bitcast_convert_type(a.astype(BF16).astype(F32), U32)
    b_bits = lax.bitcast_convert_type(b.astype(BF16).astype(F32), U32)
    return (a_bits >> 16) | b_bits


def _unpack_bf16_pair(w):
    a = lax.bitcast_convert_type(w << 16, F32)
    b = lax.bitcast_convert_type(w & jnp.uint32(0xFFFF0000), F32)
    return a, b


def _resident(shape):
    nd = len(shape)
    return pl.BlockSpec(shape, lambda *_: (0,) * nd, pipeline_mode=pl.Buffered(1))


def _proj_attn_kernel(x_ref, w_ref, b_ref, hp_ref, qsw_ref, k_refs, v_refs, qm_ref, mk_ref, mv_ref,
                      p_ref, osw_ref, om_ref, *, gate_cols):
    j = pl.program_id(1)
    xb = x_ref[...].astype(BF16)

    def proj_chunk(c):
        cols = slice(c * PROJ_CHUNK, (c + 1) * PROJ_CHUNK)
        acc = jnp.dot(xb, w_ref[:, cols], preferred_element_type=F32)
        if (c + 1) * PROJ_CHUNK <= gate_cols:
            acc = _sigmoid(acc + b_ref[:, cols])
        p_ref[:, cols] = acc.astype(p_ref.dtype)

    blk = SWA_WINDOW
    nblk = x_ref.shape[0] // blk

    def swa(hb):
        rows = slice(hb * blk, (hb + 1) * blk)
        first = (j == 0) if hb == 0 else None
        _swa_block(hp_ref, qsw_ref, rows, k_refs[hb], k_refs[hb + 1], v_refs[hb], v_refs[hb + 1],
                   first, osw_ref)

    side = [functools.partial(swa, hb) for hb in range(nblk)]
    side.append(functools.partial(_mem_heads, qm_ref, mk_ref, mv_ref, om_ref))
    nchunk = p_ref.shape[1] // PROJ_CHUNK
    per = -(-nchunk // len(side))
    for s, work in enumerate(side):
        for c in range(s * per, min((s + 1) * per, nchunk)):
            proj_chunk(c)
        work()


def _proj_attn(x2, w_b, b_gate, p_a, mkv, swa_tables, batch, seq, mem_len):
    t, d = x2.shape
    tm = PROJ_ROWS
    nt = seq // tm
    blk = SWA_WINDOW
    nblk = tm // blk
    nb = seq // blk
    qw = SWA_HEADS * LANES
    mw = MEM_HEADS * MEM_HEAD_DIM
    tile = lambda w, col: pl.BlockSpec((tm, w), lambda b, j: (b * nt + j, col))
    kv = lambda col: [pl.BlockSpec((blk, LANES),
                                   functools.partial(lambda b, j, off, col: (
                                       b * nb + jnp.maximum(nblk * j + off, 0), col), off=off, col=col))
                      for off in range(-1, nblk)]
    return pl.pallas_call(
        functools.partial(_proj_attn_kernel, gate_cols=b_gate.shape[1]),
        grid=(batch, nt),
        in_specs=[tile(d, 0), _resident(w_b.shape), _resident(b_gate.shape),
                  _resident(swa_tables.shape),
                  tile(qw, A_QSW // qw),
                  kv(A_KSW // LANES), kv(A_VSW // LANES),
                  tile(mw, A_QM // mw),
                  pl.BlockSpec((mem_len, mw), lambda b, j: (b, 0)),
                  pl.BlockSpec((mem_len, mw), lambda b, j: (b, 1))],
        out_specs=[tile(w_b.shape[1], 0), tile(qw, 0), tile(mw, 0)],
        out_shape=[jax.ShapeDtypeStruct((t, w_b.shape[1]), BF16),
                   jax.ShapeDtypeStruct((t, qw), BF16),
                   jax.ShapeDtypeStruct((t, mw), BF16)],
        compiler_params=pltpu.CompilerParams(
            dimension_semantics=("parallel", "arbitrary"), vmem_limit_bytes=VMEM_LIMIT),
        name="in_proj_swa_mem",
    )(x2, w_b, b_gate, swa_tables, p_a, [p_a] * (nblk + 1), [p_a] * (nblk + 1), p_a, mkv, mkv)


def _mm_kernel(x_ref, w_ref, o_ref):
    o_ref[...] = jnp.dot(x_ref[...].astype(BF16), w_ref[...],
                         preferred_element_type=F32).astype(o_ref.dtype)


def _matmul_bf16(x2, w, tm, name):
    t, d = x2.shape
    n = w.shape[1]
    return pl.pallas_call(
        _mm_kernel,
        grid=(t // tm,),
        in_specs=[pl.BlockSpec((tm, d), lambda i: (i, 0)), _resident((d, n))],
        out_specs=pl.BlockSpec((tm, n), lambda i: (i, 0)),
        out_shape=jax.ShapeDtypeStruct((t, n), BF16),
        compiler_params=pltpu.CompilerParams(
            dimension_semantics=("parallel",), vmem_limit_bytes=VMEM_LIMIT),
        name=name,
    )(x2, w)


def _sb_kernel(q_ref, k_ref, v_ref, o_ref, *, tq):
    i = pl.program_id(2)
    pairs = q_ref.shape[1] // LANES
    lane = lax.broadcasted_iota(I32, (1, LANES), 1)
    r = lax.broadcasted_iota(I32, (tq, tq), 0)
    c = lax.broadcasted_iota(I32, (tq, tq), 1)
    tri = (r >= c).astype(BF16)
    causal = c < r
    nh = LANES // HEAD_DIM
    hmasks = [(lane >= h * HEAD_DIM) & (lane < (h + 1) * HEAD_DIM) for h in range(nh)]
    qs = []
    for p in range(pairs):
        q = q_ref[:, p * LANES:(p + 1) * LANES]
        qs.append(jnp.concatenate([jnp.where(hm, q, jnp.zeros_like(q)) for hm in hmasks], axis=0))
    causal2 = jnp.concatenate([causal] * nh, axis=0)
    tri2 = jnp.concatenate([tri, tri], axis=0)
    mp = nh * tq

    def blocks(jobs, carry, acc):
        chains = [(jb, p) for jb in range(len(jobs)) for p in range(pairs)]
        z, hl, suffix, ab, av, cin = {}, {}, {}, {}, {}, {}

        def rows(jb):
            return pl.ds(pl.multiple_of(jobs[jb][0] * tq, tq), tq)

        def scores(jb, p):
            z[jb, p] = _nt_dot(qs[p], k_ref[rows(jb), p * LANES:(p + 1) * LANES])

        def softplus_split(jb, p):
            sp = jnp.maximum(z[jb, p], 0.0) + jnp.log(1.0 + jnp.exp(-jnp.abs(z[jb, p])))
            if jobs[jb][1]:
                sp = jnp.where(causal2, sp, 0.0)
            hi = sp.astype(BF16)
            lo = (sp - hi.astype(F32)).astype(BF16)
            hl[jb, p] = jnp.concatenate([hi, lo], axis=1)

        def cumsum(jb, p):
            suffix[jb, p] = jnp.dot(hl[jb, p], tri2, preferred_element_type=F32)

        def weights(jb, p):
            cin[jb, p] = carry[p] if jb == 0 else cin[jb - 1, p] + step_sum(jb - 1, p)
            a = jnp.exp((z[jb, p] - cin[jb, p]) - suffix[jb, p])
            if jobs[jb][1]:
                a = jnp.where(causal2, a, 0.0)
            ab[jb, p] = a.astype(BF16)

        def values(jb, p):
            out = jnp.dot(ab[jb, p], v_ref[rows(jb), p * LANES:(p + 1) * LANES],
                          preferred_element_type=F32)
            live = jobs[jb][2]
            av[jb, p] = out if live is None else jnp.where(live, out, 0.0)

        def step_sum(jb, p):
            total = suffix[jb, p][:, 0:1]
            live = jobs[jb][2]
            return total if live is None else jnp.where(live, total, 0.0)

        stages = (scores, softplus_split, cumsum, weights, values)
        for t in range(len(chains) + len(stages) - 1):
            for s in reversed(range(len(stages))):
                if 0 <= t - s < len(chains):
                    stages[s](*chains[t - s])
        last = len(jobs) - 1
        new_acc = []
        for p in range(pairs):
            total = acc[p]
            for jb in range(len(jobs)):
                total = total + av[jb, p]
            new_acc.append(total)
        return [cin[last, p] + step_sum(last, p) for p in range(pairs)], new_acc

    def block(kb, carry, acc, diag):
        return blocks([(kb, diag, None)], carry, acc)

    carry, acc = blocks([(i, True, None), (jnp.maximum(i - 1, 0), False, i > 0)],
                        [jnp.zeros((mp, 1), F32)] * pairs,
                        [jnp.zeros((mp, LANES), F32)] * pairs)

    def cond(s):
        kb, carry, _ = s
        lowest = carry[0]
        for cp in carry[1:]:
            lowest = jnp.minimum(lowest, cp)
        return (kb >= 0) & (jnp.min(lowest) < SB_SKIP)

    def body(s):
        kb, carry, acc = s
        carry, acc = block(kb, carry, acc, False)
        return kb - 1, carry, acc

    _, _, acc = lax.while_loop(cond, body, (i - 2, carry, acc))
    for p in range(pairs):
        o_ref[:, p * LANES:(p + 1) * LANES] = jnp.where(
            hmasks[0], acc[p][:tq], acc[p][tq:]).astype(o_ref.dtype)


def _sb_attention(p, batch, seq, tq, pairs):
    t = batch * seq
    nq = seq // tq
    w = pairs * LANES
    ngrp = SB_HEADS * HEAD_DIM // w
    qc, kc, vc = C_QSB // w, C_KSB // w, C_VSB // w
    return pl.pallas_call(
        functools.partial(_sb_kernel, tq=tq),
        grid=(batch, ngrp, nq),
        in_specs=[pl.BlockSpec((tq, w), lambda b, h, i: (b * nq + i, qc + h)),
                  pl.BlockSpec((seq, w), lambda b, h, i: (b, kc + h)),
                  pl.BlockSpec((seq, w), lambda b, h, i: (b, vc + h))],
        out_specs=pl.BlockSpec((tq, w), lambda b, h, i: (b * nq + i, h)),
        out_shape=jax.ShapeDtypeStruct((t, SB_HEADS * HEAD_DIM), BF16),
        compiler_params=pltpu.CompilerParams(
            dimension_semantics=("parallel", "parallel", "arbitrary"),
            vmem_limit_bytes=VMEM_LIMIT),
        name="sb_attention",
    )(p, p, p)


def _swa_block(hp_ref, q_ref, rows, kprev, kcur, vprev, vcur, first, o_ref):
    blk = kcur.shape[0]
    nheads = q_ref.shape[1] // LANES
    sink = hp_ref[:, 2 * blk:]
    lane = lax.broadcasted_iota(I32, (1, LANES), 1)
    qs = jnp.concatenate([q_ref[rows, g * LANES:(g + 1) * LANES] for g in range(nheads)], axis=0)
    keys = jnp.concatenate([kprev[...], kcur[...]], axis=0)
    vals = jnp.concatenate([vprev[...], vcur[...]], axis=0)
    z = _nt_dot(qs, keys) + hp_ref[:, :2 * blk]
    if first is not None:
        col = lax.broadcasted_iota(I32, (1, 2 * blk), 1)
        z = jnp.where(first & (col < blk), jnp.float32(-jnp.inf), z)
    m = jnp.maximum(jnp.max(z, axis=1, keepdims=True), sink)
    p = jnp.exp(z - jnp.concatenate([m, m], axis=1))
    den = jnp.sum(p, axis=1, keepdims=True) + jnp.exp(sink - m)
    o = jnp.dot(p.astype(BF16), vals, preferred_element_type=F32) / den
    for g in range(nheads):
        kv = g // SWA_GROUP
        kvmask = (lane >= kv * HEAD_DIM) & (lane < (kv + 1) * HEAD_DIM)
        o_ref[rows, g * LANES:(g + 1) * LANES] = jnp.where(
            kvmask, o[g * blk:(g + 1) * blk], 0.0).astype(o_ref.dtype)


def _swa_tables(sinks):
    w = SWA_WINDOW
    slopes = jnp.exp2(-8.0 * jnp.arange(1, SWA_HEADS + 1, dtype=F32) / SWA_HEADS)[:, None, None]
    r = jnp.arange(w)[:, None]
    c = jnp.arange(w)[None, :]
    dist = (r - c).astype(F32)[None]
    neg = jnp.float32(-jnp.inf)
    bias_c = jnp.where((c <= r)[None], -slopes * dist, neg)
    bias_p = jnp.where((c > r)[None], -slopes * (dist + w), neg)
    sink = jnp.broadcast_to(sinks.astype(F32)[:, None, None], (SWA_HEADS, w, w))
    return jnp.concatenate([bias_p, bias_c, sink], axis=2).reshape(SWA_HEADS * w, 3 * w)


def _mem_heads(q_ref, mk_ref, mv_ref, o_ref):
    scale = MEM_HEAD_DIM ** -0.5
    for h in range(MEM_HEADS):
        cols = slice(h * MEM_HEAD_DIM, (h + 1) * MEM_HEAD_DIM)
        z = _nt_dot(q_ref[:, cols], mk_ref[:, cols]) * scale
        m = jnp.max(z, axis=1, keepdims=True)
        p = jnp.exp(z - m)
        den = jnp.sum(p, axis=1, keepdims=True)
        o = jnp.dot(p.astype(BF16), mv_ref[:, cols], preferred_element_type=F32) / den
        o_ref[:, cols] = o.astype(o_ref.dtype)


def _merge_route_kernel(osb_ref, osw_ref, om_ref, g_ref, x_ref, wsb_ref, wsw_ref, wm_ref,
                        wout_ref, lng_ref, lnb_ref, wrh_ref, wrl_ref, rbias_ref,
                        x1_ref, x1p_ref, idx_ref, rank_ref, wgt_ref, cnt_ref, x1_prev, carry_ref):
    i = pl.program_id(0)

    @pl.when(i == 0)
    def _():
        x1_prev[...] = jnp.zeros_like(x1_prev)
        carry_ref[...] = jnp.zeros_like(carry_ref)

    d = x_ref.shape[1]
    st = {}

    def branch(b, o_ref, w_ref):
        def run():
            term = g_ref[:, b * d:(b + 1) * d].astype(F32) * jnp.dot(
                o_ref[...], w_ref[...], preferred_element_type=F32)
            st["merged"] = term if b == 0 else st["merged"] + term
        return run

    def out_proj():
        st["y"] = jnp.dot(st["merged"].astype(BF16), wout_ref[...], preferred_element_type=F32)

    idx, rank, wgt, count = _route(
        x1_prev[...], wrh_ref[...], wrl_ref[...], rbias_ref[...], carry_ref[...],
        side_work=(branch(0, osb_ref, wsb_ref), branch(1, osw_ref, wsw_ref),
                   branch(2, om_ref, wm_ref), out_proj))
    x1 = _layer_norm(DEEPNORM_ALPHA * x_ref[...] + st["y"], lng_ref[...], lnb_ref[...])
    x1_ref[...] = x1
    x1p_ref[...] = _pack_bf16_pair(x1[:, :d // 2], x1[:, d // 2:])
    idx_ref[...] = idx
    rank_ref[...] = rank
    wgt_ref[...] = wgt
    carry_ref[...] = carry_ref[...] + jnp.where(i > 0, count, 0.0)
    cnt_ref[...] = carry_ref[...]
    x1_prev[...] = x1


def _merge_route(o_sb, o_sw, o_m, p, x2, w_sb, w_sw, w_m, w_out, ln_g, ln_b, wr_hi, wr_lo,
                 bias_col, tm):
    t, d = x2.shape
    n = t // tm
    cur = lambda i: (jnp.minimum(i, n - 1), 0)
    row = lambda w: pl.BlockSpec((tm, w), cur)
    slot = pl.BlockSpec((TOP_K, tm), lambda i: (0, jnp.maximum(i - 1, 0)))
    return pl.pallas_call(
        _merge_route_kernel,
        grid=(n + 1,),
        in_specs=[row(o_sb.shape[1]), row(o_sw.shape[1]), row(o_m.shape[1]),
                  pl.BlockSpec((tm, N_BRANCH * d), cur),
                  row(d),
                  _resident(w_sb.shape), _resident(w_sw.shape), _resident(w_m.shape),
                  _resident(w_out.shape), _resident(ln_g.shape), _resident(ln_b.shape),
                  _resident(wr_hi.shape), _resident(wr_lo.shape), _resident(bias_col.shape)],
        out_specs=[row(d), row(d // 2), slot, slot, slot,
                   pl.BlockSpec((N_EXPERTS, 1), lambda i: (0, 0))],
        out_shape=[jax.ShapeDtypeStruct((t, d), F32), jax.ShapeDtypeStruct((t, d // 2), U32),
                   jax.ShapeDtypeStruct((TOP_K, t), I32),
                   jax.ShapeDtypeStruct((TOP_K, t), I32),
                   jax.ShapeDtypeStruct((TOP_K, t), F32),
                   jax.ShapeDtypeStruct((N_EXPERTS, 1), F32)],
        scratch_shapes=[pltpu.VMEM((tm, d), F32), pltpu.VMEM((N_EXPERTS, 1), F32)],
        compiler_params=pltpu.CompilerParams(
            dimension_semantics=("arbitrary",), vmem_limit_bytes=VMEM_LIMIT),
        name="merge_route",
    )(o_sb, o_sw, o_m, p, x2, w_sb, w_sw, w_m, w_out, ln_g, ln_b, wr_hi, wr_lo, bias_col)


def _route(x, wh, wl, bias, carry, side_work=()):
    tr = x.shape[0]
    xh = x.astype(BF16)
    xl = (x - xh.astype(F32)).astype(BF16)
    logits = _nt_dot(wh, xh) + _nt_dot(wh, xl) + _nt_dot(wl, xh)
    scores = _sigmoid(logits)
    biased = scores + bias
    neg = jnp.float32(-jnp.inf)

    sub = lax.broadcasted_iota(I32, (GROUP_SIZE, tr), 0)
    gscore = []
    for g in range(N_GROUPS):
        blk = biased[g * GROUP_SIZE:(g + 1) * GROUP_SIZE, :]
        m1 = jnp.max(blk, axis=0, keepdims=True)
        i1 = jnp.min(jnp.where(blk == m1, sub, GROUP_SIZE), axis=0, keepdims=True)
        m2 = jnp.max(jnp.where(sub == i1, neg, blk), axis=0, keepdims=True)
        gscore.append(m1 + m2)
    gs = jnp.concatenate(gscore, axis=0)

    giota = lax.broadcasted_iota(I32, (N_GROUPS, tr), 0)
    gsel = jnp.zeros((N_GROUPS, tr), F32)
    for _ in range(TOPK_GROUPS):
        m = jnp.max(gs, axis=0, keepdims=True)
        gi = jnp.min(jnp.where(gs == m, giota, N_GROUPS), axis=0, keepdims=True)
        hit = giota == gi
        gsel = jnp.where(hit, 1.0, gsel)
        gs = jnp.where(hit, neg, gs)

    masked = jnp.concatenate(
        [jnp.where(gsel[g:g + 1, :] > 0.0, biased[g * GROUP_SIZE:(g + 1) * GROUP_SIZE, :], neg)
         for g in range(N_GROUPS)], axis=0)

    eiota = lax.broadcasted_iota(I32, (N_EXPERTS, tr), 0)
    sel = jnp.zeros((N_EXPERTS, tr), F32)
    idx_rows, w_rows = [], []
    side_work = list(side_work)
    for k in range(TOP_K):
        if side_work and k % 2 == 0:
            side_work.pop(0)()
        m = jnp.max(masked, axis=0, keepdims=True)
        ei = jnp.min(jnp.where(masked == m, eiota, N_EXPERTS), axis=0, keepdims=True)
        hit = eiota == ei
        idx_rows.append(ei)
        w_rows.append(jnp.sum(jnp.where(hit, scores, 0.0), axis=0, keepdims=True))
        sel = jnp.where(hit, 1.0, sel)
        masked = jnp.where(hit, neg, masked)

    wsum = w_rows[0]
    for wk in w_rows[1:]:
        wsum = wsum + wk
    wgt = jnp.concatenate(w_rows, axis=0) / wsum * ROUTED_SCALE
    idx = jnp.concatenate(idx_rows, axis=0)

    a = lax.broadcasted_iota(I32, (tr, tr), 0)
    b = lax.broadcasted_iota(I32, (tr, tr), 1)
    before = (a < b).astype(BF16)
    rank = jnp.dot(sel.astype(BF16), before, preferred_element_type=F32) + carry
    rank_rows = [jnp.sum(jnp.where(eiota == ei, rank, 0.0), axis=0, keepdims=True)
                 for ei in idx_rows]
    rank = jnp.concatenate(rank_rows, axis=0).astype(I32)
    return idx, rank, wgt, jnp.sum(sel, axis=1, keepdims=True)


def _dest_kernel(idx_ref, rank_ref, start_ref, dest_ref):
    tr = idx_ref.shape[1]
    eiota = lax.broadcasted_iota(I32, (N_EXPERTS, tr), 0)
    rows = []
    for k in range(TOP_K):
        hit = eiota == idx_ref[k:k + 1, :]
        rows.append(jnp.sum(jnp.where(hit, start_ref[...], 0), axis=0, keepdims=True))
    dest_ref[...] = jnp.concatenate(rows, axis=0) + rank_ref[...]


def _dest(idx, rank, start_col, tr):
    t = idx.shape[1]
    slot = pl.BlockSpec((TOP_K, tr), lambda i: (0, i))
    return pl.pallas_call(
        _dest_kernel,
        grid=(t // tr,),
        in_specs=[slot, slot, _resident(start_col.shape)],
        out_specs=slot,
        out_shape=jax.ShapeDtypeStruct((TOP_K, t), I32),
        compiler_params=pltpu.CompilerParams(dimension_semantics=("parallel",)),
        name="slot_dest",
    )(idx, rank, start_col)


def _sc_worker_id():
    return lax.axis_index("s") * SC_CORES + lax.axis_index("c")


def _dispatch(x1p, dest, n_rows):
    t, w = x1p.shape
    per = t // SC_WORKERS
    win = min(SC_INDEX_WINDOW, per)
    mesh = plsc.VectorSubcoreMesh(core_axis_name="c", subcore_axis_name="s")

    @functools.partial(
        pl.kernel, mesh=mesh,
        out_type=jax.ShapeDtypeStruct((n_rows, w), x1p.dtype),
        scratch_types=[pltpu.VMEM((TOP_K, win), I32),
                       pltpu.VMEM((win, w), x1p.dtype),
                       pltpu.SemaphoreType.DMA],
        name="sc_dispatch",
    )
    def scatter_rows(x_hbm, dest_hbm, xs_hbm, idx_v, rows_v, sem):
        base = _sc_worker_id() * per

        @pl.loop(0, per // win)
        def _(j):
            t0 = pl.multiple_of(base + j * win, win)
            pltpu.sync_copy(dest_hbm.at[:, pl.ds(t0, win)], idx_v)
            pltpu.sync_copy(x_hbm.at[pl.ds(t0, win)], rows_v)
            copies = [pltpu.async_copy(rows_v, xs_hbm.at[idx_v.at[k]], sem) for k in range(TOP_K)]
            for c in copies:
                c.wait()

    return scatter_rows(x1p, dest)


def _gather_rows(table, idx):
    n = idx.shape[0]
    w = table.shape[1]
    per = n // SC_WORKERS
    chunk = min(SC_GATHER_ROWS, per // 2)
    assert n % SC_WORKERS == 0 and per % (2 * chunk) == 0, (n, chunk)
    mesh = plsc.VectorSubcoreMesh(core_axis_name="c", subcore_axis_name="s")

    @functools.partial(
        pl.kernel, mesh=mesh,
        out_type=jax.ShapeDtypeStruct((n, w), table.dtype),
        scratch_types=[pltpu.VMEM((per,), I32),
                       pltpu.VMEM((2, chunk, w), table.dtype),
                       pltpu.SemaphoreType.DMA((2,)),
                       pltpu.SemaphoreType.DMA((2,))],
        name="sc_gather",
    )
    def gather_rows(table_hbm, idx_hbm, out_hbm, idx_v, rows_v, gather_sem, put_sem):
        base = _sc_worker_id() * per
        nchunks = per // chunk
        pltpu.sync_copy(idx_hbm.at[pl.ds(base, per)], idx_v)

        def gather(j, b):
            off = pl.multiple_of(j * chunk, chunk)
            return pltpu.make_async_copy(table_hbm.at[idx_v.at[pl.ds(off, chunk)]],
                                         rows_v.at[b], gather_sem.at[b])

        def put(j, b):
            off = pl.multiple_of(j * chunk, chunk)
            return pltpu.make_async_copy(rows_v.at[b], out_hbm.at[pl.ds(base + off, chunk)],
                                         put_sem.at[b])

        gather(0, 0).start()

        @pl.loop(0, nchunks, step=2)
        def _(j):
            for b in (0, 1):
                jj = j + b

                @pl.when(jj + 1 < nchunks)
                def _():
                    @pl.when(jj >= 1)
                    def _():
                        put(jj - 1, 1 - b).wait()
                    gather(jj + 1, 1 - b).start()

                gather(jj, b).wait()
                put(jj, b).start()

        put(nchunks - 2, 0).wait()
        put(nchunks - 1, 1).wait()

    return gather_rows(table, idx)


def _expert_kernel(first_ref, nblk_ref, total_ref, wgu_ref, wd_ref, xs_hbm, ys_hbm,
                   wgu_s, wd_s, xbuf, ybuf, in_sem, out_sem):
    e = pl.program_id(0)
    total = total_ref[0]
    n_in, rows, half = xbuf.shape
    n_out = ybuf.shape[0]
    ahead = EXPERT_AHEAD

    def block_rows(g):
        return pl.ds(pl.multiple_of(g * rows, rows), rows)

    def load(g):
        slot = g % n_in
        return pltpu.make_async_copy(xs_hbm.at[block_rows(g), :], xbuf.at[slot], in_sem.at[slot])

    def store(g):
        slot = g % n_out
        return pltpu.make_async_copy(ybuf.at[slot], ys_hbm.at[block_rows(g), :], out_sem.at[slot])

    @pl.when(e == 0)
    def _():
        for g in range(ahead):
            @pl.when(g < total)
            def _(g=g):
                load(g).start()

    wgu_s[...] = wgu_ref[0].astype(BF16)
    wd_s[...] = wd_ref[0].astype(BF16)
    g0 = first_ref[e]
    nblk = nblk_ref[e]
    ff = wd_s.shape[0]

    def acquire(g):
        load(g).wait()

        @pl.when(g + ahead < total)
        def _():
            load(g + ahead).start()

        @pl.when(g >= n_out)
        def _():
            store(g - n_out).wait()

    def compute(gs):
        st = {}

        def unpack(k):
            x_lo, x_hi = _unpack_bf16_pair(xbuf[gs[k] % n_in])
            st[k, "x"] = (x_lo.astype(BF16), x_hi.astype(BF16))

        def up_proj(k):
            x_lo, x_hi = st[k, "x"]
            st[k, "h"] = (jnp.dot(x_lo, wgu_s[:half, :], preferred_element_type=F32)
                          + jnp.dot(x_hi, wgu_s[half:, :], preferred_element_type=F32))

        def activate(k):
            gate, up = st[k, "h"][:, :ff], st[k, "h"][:, ff:]
            st[k, "a"] = (gate * _sigmoid(gate) * up).astype(BF16)

        def down_proj(k):
            st[k, "y"] = jnp.dot(st[k, "a"], wd_s[...], preferred_element_type=F32)

        def pack(k):
            y = st[k, "y"]
            ybuf[gs[k] % n_out] = _pack_bf16_pair(y[:, :half], y[:, half:])

        stages = (unpack, up_proj, activate, down_proj, pack)
        for t in range(len(gs) + len(stages) - 1):
            for s in reversed(range(len(stages))):
                if 0 <= t - s < len(gs):
                    stages[s](t - s)

    def pair(jj, carry):
        g = g0 + 2 * jj
        acquire(g)
        acquire(g + 1)
        compute([g, g + 1])
        store(g).start()
        store(g + 1).start()
        return carry

    lax.fori_loop(0, nblk // 2, pair, 0)

    @pl.when(nblk % 2 == 1)
    def _():
        g = g0 + nblk - 1
        acquire(g)
        compute([g])
        store(g).start()

    @pl.when(e == pl.num_programs(0) - 1)
    def _():
        for back in range(n_out, 0, -1):
            @pl.when(total >= back)
            def _(back=back):
                store(total - back).wait()


def _experts(xs, first_blk, n_blk, total_blk, w_gu, w_down):
    n_rows, half = xs.shape
    d = 2 * half
    n_exp, _, ff2 = w_gu.shape
    ff = w_down.shape[1]
    grid_spec = pltpu.PrefetchScalarGridSpec(
        num_scalar_prefetch=3,
        grid=(n_exp,),
        in_specs=[pl.BlockSpec((1, d, ff2), lambda e, *_: (e, 0, 0)),
                  pl.BlockSpec((1, ff, d), lambda e, *_: (e, 0, 0)),
                  pl.BlockSpec(memory_space=pl.ANY)],
        out_specs=pl.BlockSpec(memory_space=pl.ANY),
        scratch_shapes=[pltpu.VMEM((d, ff2), BF16), pltpu.VMEM((ff, d), BF16),
                        pltpu.VMEM((EXPERT_IN_SLOTS, EXPERT_ROWS, half), xs.dtype),
                        pltpu.VMEM((EXPERT_OUT_SLOTS, EXPERT_ROWS, half), xs.dtype),
                        pltpu.SemaphoreType.DMA((EXPERT_IN_SLOTS,)),
                        pltpu.SemaphoreType.DMA((EXPERT_OUT_SLOTS,))],
    )
    return pl.pallas_call(
        _expert_kernel,
        grid_spec=grid_spec,
        out_shape=jax.ShapeDtypeStruct((n_rows, half), xs.dtype),
        compiler_params=pltpu.CompilerParams(
            dimension_semantics=("arbitrary",), vmem_limit_bytes=VMEM_LIMIT),
        name="experts",
    )(first_blk, n_blk, total_blk, w_gu, w_down, xs)


def _combine_kernel(x1_ref, w_ref, yg_ref, wsgu_ref, wsd_ref, lng_ref, lnb_ref, o_ref):
    tc = x1_ref.shape[0]
    x1 = x1_ref[...]
    ff = wsd_ref.shape[0]
    h = jnp.dot(x1.astype(BF16), wsgu_ref[...], preferred_element_type=F32)
    gate, up = h[:, :ff], h[:, ff:]
    act = gate * _sigmoid(gate) * up
    moe = jnp.dot(act.astype(BF16), wsd_ref[...], preferred_element_type=F32)

    half = yg_ref.shape[2]
    wt = jnp.transpose(w_ref[...])
    r_lo = jnp.zeros((tc, half), F32)
    r_hi = jnp.zeros((tc, half), F32)
    for k in range(TOP_K):
        y_lo, y_hi = _unpack_bf16_pair(yg_ref[k])
        w = wt[:, k:k + 1]
        r_lo = r_lo + w * y_lo
        r_hi = r_hi + w * y_hi
    moe = moe + jnp.concatenate([r_lo, r_hi], axis=1)
    o_ref[...] = _layer_norm(DEEPNORM_ALPHA * x1 + moe, lng_ref[...], lnb_ref[...])


def _combine(x1, wgt, yg, chunk, ws_gu, ws_down, ln_g, ln_b, tc):
    t, d = x1.shape
    steps = yg.shape[1] // tc
    tok = lambda i: (chunk * steps + i, 0)
    return pl.pallas_call(
        _combine_kernel,
        grid=(steps,),
        in_specs=[pl.BlockSpec((tc, d), tok),
                  pl.BlockSpec((TOP_K, tc), lambda i: (0, chunk * steps + i)),
                  pl.BlockSpec((TOP_K, tc, yg.shape[2]), lambda i: (0, i, 0)),
                  _resident(ws_gu.shape), _resident(ws_down.shape),
                  _resident(ln_g.shape), _resident(ln_b.shape)],
        out_specs=pl.BlockSpec((tc, d), tok),
        out_shape=jax.ShapeDtypeStruct((t, d), F32),
        input_output_aliases={0: 0},
        compiler_params=pltpu.CompilerParams(
            dimension_semantics=("arbitrary",), vmem_limit_bytes=VMEM_LIMIT),
        name="combine_ln2",
    )(x1, wgt, yg, ws_gu, ws_down, ln_g, ln_b)


def _fused_in_weights(w_in):
    d = w_in.shape[0]
    sizes = (SB_HEADS * HEAD_DIM,) * 3 + (SWA_HEADS * HEAD_DIM, SWA_KV_HEADS * HEAD_DIM,
                                          SWA_KV_HEADS * HEAD_DIM, MEM_HEADS * MEM_HEAD_DIM)
    parts, off = [], 0
    for s in sizes:
        parts.append(w_in[:, off:off + s])
        off += s
    q_sb, k_sb, v_sb, q_sw, k_sw, v_sw, q_m = parts
    gates = w_in[:, off:]
    scale = HEAD_DIM ** -0.5
    q_sw = (q_sw * scale).reshape(d, SWA_KV_HEADS, SWA_GROUP, HEAD_DIM)
    zeros = jnp.zeros((d, SWA_GROUP, HEAD_DIM), w_in.dtype)
    q_sw = jnp.stack([jnp.concatenate([q_sw[:, 0], zeros], axis=-1),
                      jnp.concatenate([zeros, q_sw[:, 1]], axis=-1)], axis=1)
    q_sw = q_sw.reshape(d, SWA_HEADS * LANES)
    w_a = jnp.concatenate([q_sw, q_m, k_sw, v_sw], axis=1).astype(BF16)
    w_b = jnp.concatenate([gates, q_sb * scale, k_sb, v_sb], axis=1).astype(BF16)
    assert w_a.shape[1] == PA_COLS and w_b.shape[1] == PB_COLS
    return w_a, w_b


def _padded_swa_out_weights(w_o_swa):
    d = w_o_swa.shape[1]
    w = w_o_swa.reshape(SWA_KV_HEADS, SWA_GROUP, HEAD_DIM, d)
    zeros = jnp.zeros((SWA_GROUP, HEAD_DIM, d), w_o_swa.dtype)
    w = jnp.stack([jnp.concatenate([w[0], zeros], axis=1),
                   jnp.concatenate([zeros, w[1]], axis=1)], axis=0)
    return w.reshape(SWA_HEADS * LANES, d).astype(BF16)


def kernel(x, mem, w_in, b_gate, w_mem_kv, sinks, w_o_sb, w_o_swa, w_o_mem, w_out,
           ln1_g, ln1_b, w_router, router_bias, w_e_gu, w_e_down, w_s_gu, w_s_down,
           ln2_g, ln2_b):
    batch, seq, d = x.shape
    mem_len = mem.shape[1]
    t = batch * seq
    x2 = x.reshape(t, d)
    row_tile = min(512, t)

    w_a, w_b = _fused_in_weights(w_in)
    p_a = _matmul_bf16(x2, w_a, row_tile, "in_proj_attn")
    mkv = _matmul_bf16(mem.reshape(batch * mem_len, d), w_mem_kv.astype(BF16), mem_len,
                       "mem_kv_proj")
    p, o_sw, o_m = _proj_attn(x2, w_b, b_gate.reshape(1, -1), p_a, mkv, _swa_tables(sinks),
                              batch, seq, mem_len)
    o_sb = _sb_attention(p, batch, seq, min(256, seq), SB_PAIRS_PER_STEP)
    wr_t = w_router.T
    wr_hi = wr_t.astype(BF16)
    wr_lo = (wr_t - wr_hi.astype(F32)).astype(BF16)
    x1, x1p, idx, rank, wgt, cnt = _merge_route(
        o_sb, o_sw, o_m, p, x2, w_o_sb.astype(BF16), _padded_swa_out_weights(w_o_swa),
        w_o_mem.astype(BF16), w_out.astype(BF16), ln1_g.reshape(1, d), ln1_b.reshape(1, d),
        wr_hi, wr_lo, router_bias.reshape(-1, 1).astype(F32), row_tile)

    out = _moe_ln(x1, x1p, idx, rank, wgt, cnt, w_e_gu, w_e_down, w_s_gu, w_s_down, ln2_g, ln2_b)
    return out.reshape(batch, seq, d)


def _moe_ln(x1, x1p, idx, rank, wgt, cnt, w_e_gu, w_e_down, w_s_gu, w_s_down, ln2_g, ln2_b):
    t, d = x1.shape
    counts = cnt[:, 0].astype(I32)
    padded = (counts + EXPERT_ROWS - 1) // EXPERT_ROWS * EXPERT_ROWS
    pad_end = jnp.cumsum(padded)
    pad_start = pad_end - padded
    n_blocks = t * TOP_K // EXPERT_ROWS + N_EXPERTS
    dest = _dest(idx, rank, pad_start.reshape(-1, 1), min(2048, t))

    xs = _dispatch(x1p, dest, n_blocks * EXPERT_ROWS)
    ys = _experts(xs, pad_start // EXPERT_ROWS, padded // EXPERT_ROWS,
                  pad_end[-1:] // EXPERT_ROWS, w_e_gu, w_e_down)
    tchunk = t // COMBINE_CHUNKS
    ws_gu, ws_down = w_s_gu.astype(BF16), w_s_down.astype(BF16)
    out = x1
    for c in range(COMBINE_CHUNKS):
        slots = dest[:, c * tchunk:(c + 1) * tchunk].reshape(-1)
        yg = _gather_rows(ys, slots).reshape(TOP_K, tchunk, -1)
        out = _combine(out, wgt, yg, c, ws_gu, ws_down, ln2_g.reshape(1, d), ln2_b.reshape(1, d),
                       min(256, tchunk))
    return out
```

```python
import functools

import jax
import jax.numpy as jnp
from jax import lax
from jax.experimental import pallas as pl
from jax.experimental.pallas import tpu as pltpu
from jax.experimental.pallas import tpu_sc as plsc

F32 = jnp.float32
BF16 = jnp.bfloat16
I32 = jnp.int32
U32 = jnp.uint32

HEAD_DIM = 64
SB_HEADS = 8
SWA_HEADS = 8
SWA_KV_HEADS = 2
SWA_GROUP = SWA_HEADS // SWA_KV_HEADS
SWA_WINDOW = 128
MEM_HEADS = 4
MEM_HEAD_DIM = 128
N_BRANCH = 3
N_EXPERTS = 256
TOP_K = 8
N_GROUPS = 8
GROUP_SIZE = N_EXPERTS // N_GROUPS
TOPK_GROUPS = 4
EXPERT_FF = 256
SHARED_FF = 256
ROUTED_SCALE = 2.5
LN_EPS = 1e-5
DEPTH = 1
DEEPNORM_ALPHA = (2 * DEPTH) ** 0.25

LANES = 128
SC_CORES = 2
SC_SUBCORES = 16
SC_WORKERS = SC_CORES * SC_SUBCORES
SC_INDEX_WINDOW = 128
SC_GATHER_ROWS = 64
VMEM_LIMIT = 56 * 1024 * 1024

A_QSW = 0
A_QM = 1024
A_KSW = 1536
A_VSW = 1664
PA_COLS = 1792
W_GATE = 3072
C_QSB = 3072
C_KSB = 3584
C_VSB = 4096
PB_COLS = 4608
PROJ_CHUNK = 256
PROJ_ROWS = 512

SB_SKIP = 110.0
SB_MASKED = -1e30
SB_PAIRS_PER_STEP = 4

EXPERT_ROWS = 512
EXPERT_AHEAD = 4
EXPERT_IN_SLOTS = EXPERT_AHEAD + 2
EXPERT_OUT_SLOTS = 4
COMBINE_CHUNKS = 8


def _nt_dot(a, b):
    return lax.dot_general(a, b, (((1,), (1,)), ((), ())), preferred_element_type=F32)


def _sigmoid(x):
    return 1.0 / (1.0 + jnp.exp(-x))


def _layer_norm(h, g, b):
    mu = jnp.mean(h, axis=-1, keepdims=True)
    d = h - mu
    var = jnp.mean(d * d, axis=-1, keepdims=True)
    return d * lax.rsqrt(var + LN_EPS) * g + b


def _pack_bf16_pair(a, b):
    a_bits = lax.bitcast_convert_type(a.astype(BF16).astype(F32), U32)
    b_bits = lax.bitcast_convert_type(b.astype(BF16).astype(F32), U32)
    return (a_bits >> 16) | b_bits


def _unpack_bf16_pair(w):
    a = lax.bitcast_convert_type(w << 16, F32)
    b = lax.bitcast_convert_type(w & jnp.uint32(0xFFFF0000), F32)
    return a, b


def _resident(shape):
    nd = len(shape)
    return pl.BlockSpec(shape, lambda *_: (0,) * nd, pipeline_mode=pl.Buffered(1))


def _proj_attn_kernel(x_ref, w_ref, b_ref, hp_ref, qsw_ref, k_refs, v_refs, qm_ref, mk_ref, mv_ref,
                      p_ref, osw_ref, om_ref, *, gate_cols):
    j = pl.program_id(1)
    xb = x_ref[...].astype(BF16)

    def proj_chunk(c):
        cols = slice(c * PROJ_CHUNK, (c + 1) * PROJ_CHUNK)
        acc = jnp.dot(xb, w_ref[:, cols], preferred_element_type=F32)
        if (c + 1) * PROJ_CHUNK <= gate_cols:
            acc = _sigmoid(acc + b_ref[:, cols])
        p_ref[:, cols] = acc.astype(p_ref.dtype)

    blk = SWA_WINDOW
    nblk = x_ref.shape[0] // blk

    def swa(hb):
        rows = slice(hb * blk, (hb + 1) * blk)
        first = (j == 0) if hb == 0 else None
        _swa_block(hp_ref, qsw_ref, rows, k_refs[hb], k_refs[hb + 1], v_refs[hb], v_refs[hb + 1],
                   first, osw_ref)

    side = [functools.partial(swa, hb) for hb in range(nblk)]
    side.append(functools.partial(_mem_heads, qm_ref, mk_ref, mv_ref, om_ref))
    nchunk = p_ref.shape[1] // PROJ_CHUNK
    per = -(-nchunk // len(side))
    for s, work in enumerate(side):
        for c in range(s * per, min((s + 1) * per, nchunk)):
            proj_chunk(c)
        work()


def _proj_attn(x2, w_b, b_gate, p_a, mkv, swa_tables, batch, seq, mem_len):
    t, d = x2.shape
    tm = PROJ_ROWS
    nt = seq // tm
    blk = SWA_WINDOW
    nblk = tm // blk
    nb = seq // blk
    qw = SWA_HEADS * LANES
    mw = MEM_HEADS * MEM_HEAD_DIM
    tile = lambda w, col: pl.BlockSpec((tm, w), lambda b, j: (b * nt + j, col))
    kv = lambda col: [pl.BlockSpec((blk, LANES),
                                   functools.partial(lambda b, j, off, col: (
                                       b * nb + jnp.maximum(nblk * j + off, 0), col), off=off, col=col))
                      for off in range(-1, nblk)]
    return pl.pallas_call(
        functools.partial(_proj_attn_kernel, gate_cols=b_gate.shape[1]),
        grid=(batch, nt),
        in_specs=[tile(d, 0), _resident(w_b.shape), _resident(b_gate.shape),
                  _resident(swa_tables.shape),
                  tile(qw, A_QSW // qw),
                  kv(A_KSW // LANES), kv(A_VSW // LANES),
                  tile(mw, A_QM // mw),
                  pl.BlockSpec((mem_len, mw), lambda b, j: (b, 0)),
                  pl.BlockSpec((mem_len, mw), lambda b, j: (b, 1))],
        out_specs=[tile(w_b.shape[1], 0), tile(qw, 0), tile(mw, 0)],
        out_shape=[jax.ShapeDtypeStruct((t, w_b.shape[1]), BF16),
                   jax.ShapeDtypeStruct((t, qw), BF16),
                   jax.ShapeDtypeStruct((t, mw), BF16)],
        compiler_params=pltpu.CompilerParams(
            dimension_semantics=("parallel", "arbitrary"), vmem_limit_bytes=VMEM_LIMIT),
        name="in_proj_swa_mem",
    )(x2, w_b, b_gate, swa_tables, p_a, [p_a] * (nblk + 1), [p_a] * (nblk + 1), p_a, mkv, mkv)


def _mm_kernel(x_ref, w_ref, o_ref):
    o_ref[...] = jnp.dot(x_ref[...].astype(BF16), w_ref[...],
                         preferred_element_type=F32).astype(o_ref.dtype)


def _matmul_bf16(x2, w, tm, name):
    t, d = x2.shape
    n = w.shape[1]
    return pl.pallas_call(
        _mm_kernel,
        grid=(t // tm,),
        in_specs=[pl.BlockSpec((tm, d), lambda i: (i, 0)), _resident((d, n))],
        out_specs=pl.BlockSpec((tm, n), lambda i: (i, 0)),
        out_shape=jax.ShapeDtypeStruct((t, n), BF16),
        compiler_params=pltpu.CompilerParams(
            dimension_semantics=("parallel",), vmem_limit_bytes=VMEM_LIMIT),
        name=name,
    )(x2, w)


def _sb_kernel(q_ref, k_ref, v_ref, o_ref, *, tq):
    i = pl.program_id(2)
    pairs = q_ref.shape[1] // LANES
    lane = lax.broadcasted_iota(I32, (1, LANES), 1)
    r = lax.broadcasted_iota(I32, (tq, tq), 0)
    c = lax.broadcasted_iota(I32, (tq, tq), 1)
    tri = (r >= c).astype(BF16)
    causal = c < r
    nh = LANES // HEAD_DIM
    hmasks = [(lane >= h * HEAD_DIM) & (lane < (h + 1) * HEAD_DIM) for h in range(nh)]
    qs = []
    for p in range(pairs):
        q = q_ref[:, p * LANES:(p + 1) * LANES]
        qs.append(jnp.concatenate([jnp.where(hm, q, jnp.zeros_like(q)) for hm in hmasks], axis=0))
    causal2 = jnp.concatenate([causal] * nh, axis=0)
    tri2 = jnp.concatenate([tri, tri], axis=0)
    mp = nh * tq

    def blocks(jobs, carry, acc):
        chains = [(jb, p) for jb in range(len(jobs)) for p in range(pairs)]
        z, hl, suffix, ab, av, cin = {}, {}, {}, {}, {}, {}

        def rows(jb):
            return pl.ds(pl.multiple_of(jobs[jb][0] * tq, tq), tq)

        def scores(jb, p):
            zz = _nt_dot(qs[p], k_ref[rows(jb), p * LANES:(p + 1) * LANES])
            if jobs[jb][1]:
                zz = jnp.where(causal2, zz, SB_MASKED)
            z[jb, p] = zz

        def softplus_split(jb, p):
            sp = jnp.maximum(z[jb, p], 0.0) + jnp.log(1.0 + jnp.exp(-jnp.abs(z[jb, p])))
            hi = sp.astype(BF16)
            lo = (sp - hi.astype(F32)).astype(BF16)
            hl[jb, p] = jnp.concatenate([hi, lo], axis=1)

        def cumsum(jb, p):
            suffix[jb, p] = jnp.dot(hl[jb, p], tri2, preferred_element_type=F32)

        def weights(jb, p):
            cin[jb, p] = carry[p] if jb == 0 else cin[jb - 1, p] + step_sum(jb - 1, p)
            a = jnp.exp((z[jb, p] - cin[jb, p]) - suffix[jb, p])
            ab[jb, p] = a.astype(BF16)

        def values(jb, p):
            out = jnp.dot(ab[jb, p], v_ref[rows(jb), p * LANES:(p + 1) * LANES],
                          preferred_element_type=F32)
            live = jobs[jb][2]
            av[jb, p] = out if live is None else jnp.where(live, out, 0.0)

        def step_sum(jb, p):
            total = suffix[jb, p][:, 0:1]
            live = jobs[jb][2]
            return total if live is None else jnp.where(live, total, 0.0)

        stages = (scores, softplus_split, cumsum, weights, values)
        for t in range(len(chains) + len(stages) - 1):
            for s in reversed(range(len(stages))):
                if 0 <= t - s < len(chains):
                    stages[s](*chains[t - s])
        last = len(jobs) - 1
        new_acc = []
        for p in range(pairs):
            total = acc[p]
            for jb in range(len(jobs)):
                total = total + av[jb, p]
            new_acc.append(total)
        return [cin[last, p] + step_sum(last, p) for p in range(pairs)], new_acc

    def block(kb, carry, acc, diag):
        return blocks([(kb, diag, None)], carry, acc)

    carry, acc = blocks([(i, True, None), (jnp.maximum(i - 1, 0), False, i > 0)],
                        [jnp.zeros((mp, 1), F32)] * pairs,
                        [jnp.zeros((mp, LANES), F32)] * pairs)

    def cond(s):
        kb, carry, _ = s
        lowest = carry[0]
        for cp in carry[1:]:
            lowest = jnp.minimum(lowest, cp)
        return (kb >= 0) & (jnp.min(lowest) < SB_SKIP)

    def body(s):
        kb, carry, acc = s
        carry, acc = block(kb, carry, acc, False)
        return kb - 1, carry, acc

    _, _, acc = lax.while_loop(cond, body, (i - 2, carry, acc))
    for p in range(pairs):
        o_ref[:, p * LANES:(p + 1) * LANES] = jnp.where(
            hmasks[0], acc[p][:tq], acc[p][tq:]).astype(o_ref.dtype)


def _sb_attention(p, batch, seq, tq, pairs):
    t = batch * seq
    nq = seq // tq
    w = pairs * LANES
    ngrp = SB_HEADS * HEAD_DIM // w
    qc, kc, vc = C_QSB // w, C_KSB // w, C_VSB // w
    return pl.pallas_call(
        functools.partial(_sb_kernel, tq=tq),
        grid=(batch, ngrp, nq),
        in_specs=[pl.BlockSpec((tq, w), lambda b, h, i: (b * nq + i, qc + h)),
                  pl.BlockSpec((seq, w), lambda b, h, i: (b, kc + h)),
                  pl.BlockSpec((seq, w), lambda b, h, i: (b, vc + h))],
        out_specs=pl.BlockSpec((tq, w), lambda b, h, i: (b * nq + i, h)),
        out_shape=jax.ShapeDtypeStruct((t, SB_HEADS * HEAD_DIM), BF16),
        compiler_params=pltpu.CompilerParams(
            dimension_semantics=("parallel", "parallel", "arbitrary"),
            vmem_limit_bytes=VMEM_LIMIT),
        name="sb_attention",
    )(p, p, p)


def _swa_block(hp_ref, q_ref, rows, kprev, kcur, vprev, vcur, first, o_ref):
    blk = kcur.shape[0]
    nheads = q_ref.shape[1] // LANES
    sink = hp_ref[:, 2 * blk:]
    lane = lax.broadcasted_iota(I32, (1, LANES), 1)
    qs = jnp.concatenate([q_ref[rows, g * LANES:(g + 1) * LANES] for g in range(nheads)], axis=0)
    keys = jnp.concatenate([kprev[...], kcur[...]], axis=0)
    vals = jnp.concatenate([vprev[...], vcur[...]], axis=0)
    z = _nt_dot(qs, keys) + hp_ref[:, :2 * blk]
    if first is not None:
        col = lax.broadcasted_iota(I32, (1, 2 * blk), 1)
        z = jnp.where(first & (col < blk), jnp.float32(-jnp.inf), z)
    m = jnp.maximum(jnp.max(z, axis=1, keepdims=True), sink)
    p = jnp.exp(z - jnp.concatenate([m, m], axis=1))
    den = jnp.sum(p, axis=1, keepdims=True) + jnp.exp(sink - m)
    o = jnp.dot(p.astype(BF16), vals, preferred_element_type=F32) / den
    for g in range(nheads):
        kv = g // SWA_GROUP
        kvmask = (lane >= kv * HEAD_DIM) & (lane < (kv + 1) * HEAD_DIM)
        o_ref[rows, g * LANES:(g + 1) * LANES] = jnp.where(
            kvmask, o[g * blk:(g + 1) * blk], 0.0).astype(o_ref.dtype)


def _swa_tables(sinks):
    w = SWA_WINDOW
    slopes = jnp.exp2(-8.0 * jnp.arange(1, SWA_HEADS + 1, dtype=F32) / SWA_HEADS)[:, None, None]
    r = jnp.arange(w)[:, None]
    c = jnp.arange(w)[None, :]
    dist = (r - c).astype(F32)[None]
    neg = jnp.float32(-jnp.inf)
    bias_c = jnp.where((c <= r)[None], -slopes * dist, neg)
    bias_p = jnp.where((c > r)[None], -slopes * (dist + w), neg)
    sink = jnp.broadcast_to(sinks.astype(F32)[:, None, None], (SWA_HEADS, w, w))
    return jnp.concatenate([bias_p, bias_c, sink], axis=2).reshape(SWA_HEADS * w, 3 * w)


def _mem_heads(q_ref, mk_ref, mv_ref, o_ref):
    scale = MEM_HEAD_DIM ** -0.5
    for h in range(MEM_HEADS):
        cols = slice(h * MEM_HEAD_DIM, (h + 1) * MEM_HEAD_DIM)
        z = _nt_dot(q_ref[:, cols], mk_ref[:, cols]) * scale
        m = jnp.max(z, axis=1, keepdims=True)
        p = jnp.exp(z - m)
        den = jnp.sum(p, axis=1, keepdims=True)
        o = jnp.dot(p.astype(BF16), mv_ref[:, cols], preferred_element_type=F32) / den
        o_ref[:, cols] = o.astype(o_ref.dtype)


def _merge_route_kernel(osb_ref, osw_ref, om_ref, g_ref, x_ref, wsb_ref, wsw_ref, wm_ref,
                        wout_ref, lng_ref, lnb_ref, wrh_ref, wrl_ref, rbias_ref,
                        x1_ref, x1p_ref, idx_ref, rank_ref, wgt_ref, cnt_ref, x1_prev, carry_ref):
    i = pl.program_id(0)

    @pl.when(i == 0)
    def _():
        x1_prev[...] = jnp.zeros_like(x1_prev)
        carry_ref[...] = jnp.zeros_like(carry_ref)

    d = x_ref.shape[1]
    st = {}

    def branch(b, o_ref, w_ref):
        def run():
            term = g_ref[:, b * d:(b + 1) * d].astype(F32) * jnp.dot(
                o_ref[...], w_ref[...], preferred_element_type=F32)
            st["merged"] = term if b == 0 else st["merged"] + term
        return run

    def out_proj():
        st["y"] = jnp.dot(st["merged"].astype(BF16), wout_ref[...], preferred_element_type=F32)

    idx, rank, wgt, count = _route(
        x1_prev[...], wrh_ref[...], wrl_ref[...], rbias_ref[...], carry_ref[...],
        side_work=(branch(0, osb_ref, wsb_ref), branch(1, osw_ref, wsw_ref),
                   branch(2, om_ref, wm_ref), out_proj))
    x1 = _layer_norm(DEEPNORM_ALPHA * x_ref[...] + st["y"], lng_ref[...], lnb_ref[...])
    x1_ref[...] = x1
    x1p_ref[...] = _pack_bf16_pair(x1[:, :d // 2], x1[:, d // 2:])
    idx_ref[...] = idx
    rank_ref[...] = rank
    wgt_ref[...] = wgt
    carry_ref[...] = carry_ref[...] + jnp.where(i > 0, count, 0.0)
    cnt_ref[...] = carry_ref[...]
    x1_prev[...] = x1


def _merge_route(o_sb, o_sw, o_m, p, x2, w_sb, w_sw, w_m, w_out, ln_g, ln_b, wr_hi, wr_lo,
                 bias_col, tm):
    t, d = x2.shape
    n = t // tm
    cur = lambda i: (jnp.minimum(i, n - 1), 0)
    row = lambda w: pl.BlockSpec((tm, w), cur)
    slot = pl.BlockSpec((TOP_K, tm), lambda i: (0, jnp.maximum(i - 1, 0)))
    return pl.pallas_call(
        _merge_route_kernel,
        grid=(n + 1,),
        in_specs=[row(o_sb.shape[1]), row(o_sw.shape[1]), row(o_m.shape[1]),
                  pl.BlockSpec((tm, N_BRANCH * d), cur),
                  row(d),
                  _resident(w_sb.shape), _resident(w_sw.shape), _resident(w_m.shape),
                  _resident(w_out.shape), _resident(ln_g.shape), _resident(ln_b.shape),
                  _resident(wr_hi.shape), _resident(wr_lo.shape), _resident(bias_col.shape)],
        out_specs=[row(d), row(d // 2), slot, slot, slot,
                   pl.BlockSpec((N_EXPERTS, 1), lambda i: (0, 0))],
        out_shape=[jax.ShapeDtypeStruct((t, d), F32), jax.ShapeDtypeStruct((t, d // 2), U32),
                   jax.ShapeDtypeStruct((TOP_K, t), I32),
                   jax.ShapeDtypeStruct((TOP_K, t), I32),
                   jax.ShapeDtypeStruct((TOP_K, t), F32),
                   jax.ShapeDtypeStruct((N_EXPERTS, 1), F32)],
        scratch_shapes=[pltpu.VMEM((tm, d), F32), pltpu.VMEM((N_EXPERTS, 1), F32)],
        compiler_params=pltpu.CompilerParams(
            dimension_semantics=("arbitrary",), vmem_limit_bytes=VMEM_LIMIT),
        name="merge_route",
    )(o_sb, o_sw, o_m, p, x2, w_sb, w_sw, w_m, w_out, ln_g, ln_b, wr_hi, wr_lo, bias_col)


def _route(x, wh, wl, bias, carry, side_work=()):
    tr = x.shape[0]
    xh = x.astype(BF16)
    xl = (x - xh.astype(F32)).astype(BF16)
    logits = _nt_dot(wh, xh) + _nt_dot(wh, xl) + _nt_dot(wl, xh)
    scores = _sigmoid(logits)
    biased = scores + bias
    neg = jnp.float32(-jnp.inf)

    sub = lax.broadcasted_iota(I32, (GROUP_SIZE, tr), 0)
    gscore = []
    for g in range(N_GROUPS):
        blk = biased[g * GROUP_SIZE:(g + 1) * GROUP_SIZE, :]
        m1 = jnp.max(blk, axis=0, keepdims=True)
        i1 = jnp.min(jnp.where(blk == m1, sub, GROUP_SIZE), axis=0, keepdims=True)
        m2 = jnp.max(jnp.where(sub == i1, neg, blk), axis=0, keepdims=True)
        gscore.append(m1 + m2)
    gs = jnp.concatenate(gscore, axis=0)

    giota = lax.broadcasted_iota(I32, (N_GROUPS, tr), 0)
    gsel = jnp.zeros((N_GROUPS, tr), F32)
    for _ in range(TOPK_GROUPS):
        m = jnp.max(gs, axis=0, keepdims=True)
        gi = jnp.min(jnp.where(gs == m, giota, N_GROUPS), axis=0, keepdims=True)
        hit = giota == gi
        gsel = jnp.where(hit, 1.0, gsel)
        gs = jnp.where(hit, neg, gs)

    masked = jnp.concatenate(
        [jnp.where(gsel[g:g + 1, :] > 0.0, biased[g * GROUP_SIZE:(g + 1) * GROUP_SIZE, :], neg)
         for g in range(N_GROUPS)], axis=0)

    eiota = lax.broadcasted_iota(I32, (N_EXPERTS, tr), 0)
    sel = jnp.zeros((N_EXPERTS, tr), F32)
    idx_rows, w_rows = [], []
    side_work = list(side_work)
    for k in range(TOP_K):
        if side_work and k % 2 == 0:
            side_work.pop(0)()
        m = jnp.max(masked, axis=0, keepdims=True)
        ei = jnp.min(jnp.where(masked == m, eiota, N_EXPERTS), axis=0, keepdims=True)
        hit = eiota == ei
        idx_rows.append(ei)
        w_rows.append(jnp.sum(jnp.where(hit, scores, 0.0), axis=0, keepdims=True))
        sel = jnp.where(hit, 1.0, sel)
        masked = jnp.where(hit, neg, masked)

    wsum = w_rows[0]
    for wk in w_rows[1:]:
        wsum = wsum + wk
    wgt = jnp.concatenate(w_rows, axis=0) / wsum * ROUTED_SCALE
    idx = jnp.concatenate(idx_rows, axis=0)

    a = lax.broadcasted_iota(I32, (tr, tr), 0)
    b = lax.broadcasted_iota(I32, (tr, tr), 1)
    before = (a < b).astype(BF16)
    rank = jnp.dot(sel.astype(BF16), before, preferred_element_type=F32) + carry
    rank_rows = [jnp.sum(jnp.where(eiota == ei, rank, 0.0), axis=0, keepdims=True)
                 for ei in idx_rows]
    rank = jnp.concatenate(rank_rows, axis=0).astype(I32)
    return idx, rank, wgt, jnp.sum(sel, axis=1, keepdims=True)


def _dest_kernel(idx_ref, rank_ref, start_ref, dest_ref):
    tr = idx_ref.shape[1]
    eiota = lax.broadcasted_iota(I32, (N_EXPERTS, tr), 0)
    rows = []
    for k in range(TOP_K):
        hit = eiota == idx_ref[k:k + 1, :]
        rows.append(jnp.sum(jnp.where(hit, start_ref[...], 0), axis=0, keepdims=True))
    dest_ref[...] = jnp.concatenate(rows, axis=0) + rank_ref[...]


def _dest(idx, rank, start_col, tr):
    t = idx.shape[1]
    slot = pl.BlockSpec((TOP_K, tr), lambda i: (0, i))
    return pl.pallas_call(
        _dest_kernel,
        grid=(t // tr,),
        in_specs=[slot, slot, _resident(start_col.shape)],
        out_specs=slot,
        out_shape=jax.ShapeDtypeStruct((TOP_K, t), I32),
        compiler_params=pltpu.CompilerParams(dimension_semantics=("parallel",)),
        name="slot_dest",
    )(idx, rank, start_col)


def _sc_worker_id():
    return lax.axis_index("s") * SC_CORES + lax.axis_index("c")


def _dispatch(x1p, dest, n_rows):
    t, w = x1p.shape
    per = t // SC_WORKERS
    win = min(SC_INDEX_WINDOW, per)
    mesh = plsc.VectorSubcoreMesh(core_axis_name="c", subcore_axis_name="s")

    @functools.partial(
        pl.kernel, mesh=mesh,
        out_type=jax.ShapeDtypeStruct((n_rows, w), x1p.dtype),
        scratch_types=[pltpu.VMEM((TOP_K, win), I32),
                       pltpu.VMEM((win, w), x1p.dtype),
                       pltpu.SemaphoreType.DMA],
        name="sc_dispatch",
    )
    def scatter_rows(x_hbm, dest_hbm, xs_hbm, idx_v, rows_v, sem):
        base = _sc_worker_id() * per

        @pl.loop(0, per // win)
        def _(j):
            t0 = pl.multiple_of(base + j * win, win)
            pltpu.sync_copy(dest_hbm.at[:, pl.ds(t0, win)], idx_v)
            pltpu.sync_copy(x_hbm.at[pl.ds(t0, win)], rows_v)
            copies = [pltpu.async_copy(rows_v, xs_hbm.at[idx_v.at[k]], sem) for k in range(TOP_K)]
            for c in copies:
                c.wait()

    return scatter_rows(x1p, dest)


def _gather_rows(table, idx):
    n = idx.shape[0]
    w = table.shape[1]
    per = n // SC_WORKERS
    chunk = min(SC_GATHER_ROWS, per // 2)
    assert n % SC_WORKERS == 0 and per % (2 * chunk) == 0, (n, chunk)
    mesh = plsc.VectorSubcoreMesh(core_axis_name="c", subcore_axis_name="s")

    @functools.partial(
        pl.kernel, mesh=mesh,
        out_type=jax.ShapeDtypeStruct((n, w), table.dtype),
        scratch_types=[pltpu.VMEM((per,), I32),
                       pltpu.VMEM((2, chunk, w), table.dtype),
                       pltpu.SemaphoreType.DMA((2,)),
                       pltpu.SemaphoreType.DMA((2,))],
        name="sc_gather",
    )
    def gather_rows(table_hbm, idx_hbm, out_hbm, idx_v, rows_v, gather_sem, put_sem):
        base = _sc_worker_id() * per
        nchunks = per // chunk
        pltpu.sync_copy(idx_hbm.at[pl.ds(base, per)], idx_v)

        def gather(j, b):
            off = pl.multiple_of(j * chunk, chunk)
            return pltpu.make_async_copy(table_hbm.at[idx_v.at[pl.ds(off, chunk)]],
                                         rows_v.at[b], gather_sem.at[b])

        def put(j, b):
            off = pl.multiple_of(j * chunk, chunk)
            return pltpu.make_async_copy(rows_v.at[b], out_hbm.at[pl.ds(base + off, chunk)],
                                         put_sem.at[b])

        gather(0, 0).start()

        @pl.loop(0, nchunks, step=2)
        def _(j):
            for b in (0, 1):
                jj = j + b

                @pl.when(jj + 1 < nchunks)
                def _():
                    @pl.when(jj >= 1)
                    def _():
                        put(jj - 1, 1 - b).wait()
                    gather(jj + 1, 1 - b).start()

                gather(jj, b).wait()
                put(jj, b).start()

        put(nchunks - 2, 0).wait()
        put(nchunks - 1, 1).wait()

    return gather_rows(table, idx)


def _expert_kernel(first_ref, nblk_ref, total_ref, wgu_ref, wd_ref, xs_hbm, ys_hbm,
                   wgu_s, wd_s, xbuf, ybuf, in_sem, out_sem):
    e = pl.program_id(0)
    total = total_ref[0]
    n_in, rows, half = xbuf.shape
    n_out = ybuf.shape[0]
    ahead = EXPERT_AHEAD

    def block_rows(g):
        return pl.ds(pl.multiple_of(g * rows, rows), rows)

    def load(g):
        slot = g % n_in
        return pltpu.make_async_copy(xs_hbm.at[block_rows(g), :], xbuf.at[slot], in_sem.at[slot])

    def store(g):
        slot = g % n_out
        return pltpu.make_async_copy(ybuf.at[slot], ys_hbm.at[block_rows(g), :], out_sem.at[slot])

    @pl.when(e == 0)
    def _():
        for g in range(ahead):
            @pl.when(g < total)
            def _(g=g):
                load(g).start()

    wgu_s[...] = wgu_ref[0].astype(BF16)
    wd_s[...] = wd_ref[0].astype(BF16)
    g0 = first_ref[e]
    nblk = nblk_ref[e]
    ff = wd_s.shape[0]

    def acquire(g):
        load(g).wait()

        @pl.when(g + ahead < total)
        def _():
            load(g + ahead).start()

        @pl.when(g >= n_out)
        def _():
            store(g - n_out).wait()

    def compute(gs):
        st = {}

        def unpack(k):
            x_lo, x_hi = _unpack_bf16_pair(xbuf[gs[k] % n_in])
            st[k, "x"] = (x_lo.astype(BF16), x_hi.astype(BF16))

        def up_proj(k):
            x_lo, x_hi = st[k, "x"]
            st[k, "h"] = (jnp.dot(x_lo, wgu_s[:half, :], preferred_element_type=F32)
                          + jnp.dot(x_hi, wgu_s[half:, :], preferred_element_type=F32))

        def activate(k):
            gate, up = st[k, "h"][:, :ff], st[k, "h"][:, ff:]
            st[k, "a"] = (gate * _sigmoid(gate) * up).astype(BF16)

        def down_proj(k):
            st[k, "y"] = jnp.dot(st[k, "a"], wd_s[...], preferred_element_type=F32)

        def pack(k):
            y = st[k, "y"]
            ybuf[gs[k] % n_out] = _pack_bf16_pair(y[:, :half], y[:, half:])

        stages = (unpack, up_proj, activate, down_proj, pack)
        for t in range(len(gs) + len(stages) - 1):
            for s in reversed(range(len(stages))):
                if 0 <= t - s < len(gs):
                    stages[s](t - s)

    def pair(jj, carry):
        g = g0 + 2 * jj
        acquire(g)
        acquire(g + 1)
        compute([g, g + 1])
        store(g).start()
        store(g + 1).start()
        return carry

    lax.fori_loop(0, nblk // 2, pair, 0)

    @pl.when(nblk % 2 == 1)
    def _():
        g = g0 + nblk - 1
        acquire(g)
        compute([g])
        store(g).start()

    @pl.when(e == pl.num_programs(0) - 1)
    def _():
        for back in range(n_out, 0, -1):
            @pl.when(total >= back)
            def _(back=back):
                store(total - back).wait()


def _experts(xs, first_blk, n_blk, total_blk, w_gu, w_down):
    n_rows, half = xs.shape
    d = 2 * half
    n_exp, _, ff2 = w_gu.shape
    ff = w_down.shape[1]
    grid_spec = pltpu.PrefetchScalarGridSpec(
        num_scalar_prefetch=3,
        grid=(n_exp,),
        in_specs=[pl.BlockSpec((1, d, ff2), lambda e, *_: (e, 0, 0)),
                  pl.BlockSpec((1, ff, d), lambda e, *_: (e, 0, 0)),
                  pl.BlockSpec(memory_space=pl.ANY)],
        out_specs=pl.BlockSpec(memory_space=pl.ANY),
        scratch_shapes=[pltpu.VMEM((d, ff2), BF16), pltpu.VMEM((ff, d), BF16),
                        pltpu.VMEM((EXPERT_IN_SLOTS, EXPERT_ROWS, half), xs.dtype),
                        pltpu.VMEM((EXPERT_OUT_SLOTS, EXPERT_ROWS, half), xs.dtype),
                        pltpu.SemaphoreType.DMA((EXPERT_IN_SLOTS,)),
                        pltpu.SemaphoreType.DMA((EXPERT_OUT_SLOTS,))],
    )
    return pl.pallas_call(
        _expert_kernel,
        grid_spec=grid_spec,
        out_shape=jax.ShapeDtypeStruct((n_rows, half), xs.dtype),
        compiler_params=pltpu.CompilerParams(
            dimension_semantics=("arbitrary",), vmem_limit_bytes=VMEM_LIMIT),
        name="experts",
    )(first_blk, n_blk, total_blk, w_gu, w_down, xs)


def _combine_kernel(x1_ref, w_ref, yg_ref, wsgu_ref, wsd_ref, lng_ref, lnb_ref, o_ref):
    tc = x1_ref.shape[0]
    x1 = x1_ref[...]
    ff = wsd_ref.shape[0]
    h = jnp.dot(x1.astype(BF16), wsgu_ref[...], preferred_element_type=F32)
    gate, up = h[:, :ff], h[:, ff:]
    act = gate * _sigmoid(gate) * up
    moe = jnp.dot(act.astype(BF16), wsd_ref[...], preferred_element_type=F32)

    half = yg_ref.shape[2]
    wt = jnp.transpose(w_ref[...])
    r_lo = jnp.zeros((tc, half), F32)
    r_hi = jnp.zeros((tc, half), F32)
    for k in range(TOP_K):
        y_lo, y_hi = _unpack_bf16_pair(yg_ref[k])
        w = wt[:, k:k + 1]
        r_lo = r_lo + w * y_lo
        r_hi = r_hi + w * y_hi
    moe = moe + jnp.concatenate([r_lo, r_hi], axis=1)
    o_ref[...] = _layer_norm(DEEPNORM_ALPHA * x1 + moe, lng_ref[...], lnb_ref[...])


def _combine(x1, wgt, yg, chunk, ws_gu, ws_down, ln_g, ln_b, tc):
    t, d = x1.shape
    steps = yg.shape[1] // tc
    tok = lambda i: (chunk * steps + i, 0)
    return pl.pallas_call(
        _combine_kernel,
        grid=(steps,),
        in_specs=[pl.BlockSpec((tc, d), tok),
                  pl.BlockSpec((TOP_K, tc), lambda i: (0, chunk * steps + i)),
                  pl.BlockSpec((TOP_K, tc, yg.shape[2]), lambda i: (0, i, 0)),
                  _resident(ws_gu.shape), _resident(ws_down.shape),
                  _resident(ln_g.shape), _resident(ln_b.shape)],
        out_specs=pl.BlockSpec((tc, d), tok),
        out_shape=jax.ShapeDtypeStruct((t, d), F32),
        input_output_aliases={0: 0},
        compiler_params=pltpu.CompilerParams(
            dimension_semantics=("arbitrary",), vmem_limit_bytes=VMEM_LIMIT),
        name="combine_ln2",
    )(x1, wgt, yg, ws_gu, ws_down, ln_g, ln_b)


def _fused_in_weights(w_in):
    d = w_in.shape[0]
    sizes = (SB_HEADS * HEAD_DIM,) * 3 + (SWA_HEADS * HEAD_DIM, SWA_KV_HEADS * HEAD_DIM,
                                          SWA_KV_HEADS * HEAD_DIM, MEM_HEADS * MEM_HEAD_DIM)
    parts, off = [], 0
    for s in sizes:
        parts.append(w_in[:, off:off + s])
        off += s
    q_sb, k_sb, v_sb, q_sw, k_sw, v_sw, q_m = parts
    gates = w_in[:, off:]
    scale = HEAD_DIM ** -0.5
    q_sw = (q_sw * scale).reshape(d, SWA_KV_HEADS, SWA_GROUP, HEAD_DIM)
    zeros = jnp.zeros((d, SWA_GROUP, HEAD_DIM), w_in.dtype)
    q_sw = jnp.stack([jnp.concatenate([q_sw[:, 0], zeros], axis=-1),
                      jnp.concatenate([zeros, q_sw[:, 1]], axis=-1)], axis=1)
    q_sw = q_sw.reshape(d, SWA_HEADS * LANES)
    w_a = jnp.concatenate([q_sw, q_m, k_sw, v_sw], axis=1).astype(BF16)
    w_b = jnp.concatenate([gates, q_sb * scale, k_sb, v_sb], axis=1).astype(BF16)
    assert w_a.shape[1] == PA_COLS and w_b.shape[1] == PB_COLS
    return w_a, w_b


def _padded_swa_out_weights(w_o_swa):
    d = w_o_swa.shape[1]
    w = w_o_swa.reshape(SWA_KV_HEADS, SWA_GROUP, HEAD_DIM, d)
    zeros = jnp.zeros((SWA_GROUP, HEAD_DIM, d), w_o_swa.dtype)
    w = jnp.stack([jnp.concatenate([w[0], zeros], axis=1),
                   jnp.concatenate([zeros, w[1]], axis=1)], axis=0)
    return w.reshape(SWA_HEADS * LANES, d).astype(BF16)


def kernel(x, mem, w_in, b_gate, w_mem_kv, sinks, w_o_sb, w_o_swa, w_o_mem, w_out,
           ln1_g, ln1_b, w_router, router_bias, w_e_gu, w_e_down, w_s_gu, w_s_down,
           ln2_g, ln2_b):
    batch, seq, d = x.shape
    mem_len = mem.shape[1]
    t = batch * seq
    x2 = x.reshape(t, d)
    row_tile = min(512, t)

    w_a, w_b = _fused_in_weights(w_in)
    p_a = _matmul_bf16(x2, w_a, row_tile, "in_proj_attn")
    mkv = _matmul_bf16(mem.reshape(batch * mem_len, d), w_mem_kv.astype(BF16), mem_len,
                       "mem_kv_proj")
    p, o_sw, o_m = _proj_attn(x2, w_b, b_gate.reshape(1, -1), p_a, mkv, _swa_tables(sinks),
                              batch, seq, mem_len)
    o_sb = _sb_attention(p, batch, seq, min(256, seq), SB_PAIRS_PER_STEP)
    wr_t = w_router.T
    wr_hi = wr_t.astype(BF16)
    wr_lo = (wr_t - wr_hi.astype(F32)).astype(BF16)
    x1, x1p, idx, rank, wgt, cnt = _merge_route(
        o_sb, o_sw, o_m, p, x2, w_o_sb.astype(BF16), _padded_swa_out_weights(w_o_swa),
        w_o_mem.astype(BF16), w_out.astype(BF16), ln1_g.reshape(1, d), ln1_b.reshape(1, d),
        wr_hi, wr_lo, router_bias.reshape(-1, 1).astype(F32), row_tile)

    out = _moe_ln(x1, x1p, idx, rank, wgt, cnt, w_e_gu, w_e_down, w_s_gu, w_s_down, ln2_g, ln2_b)
    return out.reshape(batch, seq, d)


def _moe_ln(x1, x1p, idx, rank, wgt, cnt, w_e_gu, w_e_down, w_s_gu, w_s_down, ln2_g, ln2_b):
    t, d = x1.shape
    counts = cnt[:, 0].astype(I32)
    padded = (counts + EXPERT_ROWS - 1) // EXPERT_ROWS * EXPERT_ROWS
    pad_end = jnp.cumsum(padded)
    pad_start = pad_end - padded
    n_blocks = t * TOP_K // EXPERT_ROWS + N_EXPERTS
    dest = _dest(idx, rank, pad_start.reshape(-1, 1), min(2048, t))

    xs = _dispatch(x1p, dest, n_blocks * EXPERT_ROWS)
    ys = _experts(xs, pad_start // EXPERT_ROWS, padded // EXPERT_ROWS,
                  pad_end[-1:] // EXPERT_ROWS, w_e_gu, w_e_down)
    tchunk = t // COMBINE_CHUNKS
    ws_gu, ws_down = w_s_gu.astype(BF16), w_s_down.astype(BF16)
    out = x1
    for c in range(COMBINE_CHUNKS):
        slots = dest[:, c * tchunk:(c + 1) * tchunk].reshape(-1)
        yg = _gather_rows(ys, slots).reshape(TOP_K, tchunk, -1)
        out = _combine(out, wgt, yg, c, ws_gu, ws_down, ln2_g.reshape(1, d), ln2_b.reshape(1, d),
                       min(256, tchunk))
    return out
```

```python
import functools

import jax
import jax.numpy as jnp
from jax import lax
from jax.experimental import pallas as pl
from jax.experimental.pallas import tpu as pltpu
from jax.experimental.pallas import tpu_sc as plsc

F32 = jnp.float32
BF16 = jnp.bfloat16
I32 = jnp.int32
U32 = jnp.uint32

HEAD_DIM = 64
SB_HEADS = 8
SWA_HEADS = 8
SWA_KV_HEADS = 2
SWA_GROUP = SWA_HEADS // SWA_KV_HEADS
SWA_WINDOW = 128
MEM_HEADS = 4
MEM_HEAD_DIM = 128
N_BRANCH = 3
N_EXPERTS = 256
TOP_K = 8
N_GROUPS = 8
GROUP_SIZE = N_EXPERTS // N_GROUPS
TOPK_GROUPS = 4
EXPERT_FF = 256
SHARED_FF = 256
ROUTED_SCALE = 2.5
LN_EPS = 1e-5
DEPTH = 1
DEEPNORM_ALPHA = (2 * DEPTH) ** 0.25

LANES = 128
SC_CORES = 2
SC_SUBCORES = 16
SC_WORKERS = SC_CORES * SC_SUBCORES
SC_INDEX_WINDOW = 128
SC_GATHER_ROWS = 64
VMEM_LIMIT = 56 * 1024 * 1024

A_QSW = 0
A_QM = 512
A_KSW = 1024
A_VSW = 1152
PA_COLS = 1280
W_GATE = 3072
C_QSB = 3072
C_KSB = 3584
C_VSB = 4096
PB_COLS = 4608
PROJ_CHUNK = 256
PROJ_ROWS = 512

SB_SKIP = 110.0
SB_MASKED = -1e30
SB_PAIRS_PER_STEP = 4

EXPERT_ROWS = 512
EXPERT_AHEAD = 4
EXPERT_IN_SLOTS = EXPERT_AHEAD + 2
EXPERT_OUT_SLOTS = 4
COMBINE_CHUNKS = 8


def _nt_dot(a, b):
    return lax.dot_general(a, b, (((1,), (1,)), ((), ())), preferred_element_type=F32)


def _sigmoid(x):
    return 1.0 / (1.0 + jnp.exp(-x))


def _layer_norm(h, g, b):
    mu = jnp.mean(h, axis=-1, keepdims=True)
    d = h - mu
    var = jnp.mean(d * d, axis=-1, keepdims=True)
    return d * lax.rsqrt(var + LN_EPS) * g + b


def _pack_bf16_pair(a, b):
    a_bits = lax.bitcast_convert_type(a.astype(BF16).astype(F32), U32)
    b_bits = lax.bitcast_convert_type(b.astype(BF16).astype(F32), U32)
    return (a_bits >> 16) | b_bits


def _unpack_bf16_pair(w):
    a = lax.bitcast_convert_type(w << 16, F32)
    b = lax.bitcast_convert_type(w & jnp.uint32(0xFFFF0000), F32)
    return a, b


def _resident(shape):
    nd = len(shape)
    return pl.BlockSpec(shape, lambda *_: (0,) * nd, pipeline_mode=pl.Buffered(1))


def _proj_attn_kernel(x_ref, w_ref, b_ref, hp_ref, qsw_ref, k_refs, v_refs, qm_ref, mk_ref, mv_ref,
                      p_ref, osw_ref, om_ref, *, gate_cols):
    j = pl.program_id(1)
    xb = x_ref[...].astype(BF16)

    def proj_chunk(c):
        cols = slice(c * PROJ_CHUNK, (c + 1) * PROJ_CHUNK)
        acc = jnp.dot(xb, w_ref[:, cols], preferred_element_type=F32)
        if (c + 1) * PROJ_CHUNK <= gate_cols:
            acc = _sigmoid(acc + b_ref[:, cols])
        p_ref[:, cols] = acc.astype(p_ref.dtype)

    blk = SWA_WINDOW
    nblk = x_ref.shape[0] // blk

    def swa(hb):
        rows = slice(hb * blk, (hb + 1) * blk)
        first = (j == 0) if hb == 0 else None
        _swa_block(hp_ref, qsw_ref, rows, k_refs[hb], k_refs[hb + 1], v_refs[hb], v_refs[hb + 1],
                   first, osw_ref)

    side = [functools.partial(swa, hb) for hb in range(nblk)]
    side.append(functools.partial(_mem_heads, qm_ref, mk_ref, mv_ref, om_ref))
    nchunk = p_ref.shape[1] // PROJ_CHUNK
    per = -(-nchunk // len(side))
    for s, work in enumerate(side):
        for c in range(s * per, min((s + 1) * per, nchunk)):
            proj_chunk(c)
        work()


def _proj_attn(x2, w_b, b_gate, p_a, mkv, swa_tables, batch, seq, mem_len):
    t, d = x2.shape
    tm = PROJ_ROWS
    nt = seq // tm
    blk = SWA_WINDOW
    nblk = tm // blk
    nb = seq // blk
    qw = SWA_HEADS * HEAD_DIM
    mw = MEM_HEADS * MEM_HEAD_DIM
    tile = lambda w, col: pl.BlockSpec((tm, w), lambda b, j: (b * nt + j, col))
    kv = lambda col: [pl.BlockSpec((blk, LANES),
                                   functools.partial(lambda b, j, off, col: (
                                       b * nb + jnp.maximum(nblk * j + off, 0), col), off=off, col=col))
                      for off in range(-1, nblk)]
    return pl.pallas_call(
        functools.partial(_proj_attn_kernel, gate_cols=b_gate.shape[1]),
        grid=(batch, nt),
        in_specs=[tile(d, 0), _resident(w_b.shape), _resident(b_gate.shape),
                  _resident(swa_tables.shape),
                  tile(qw, A_QSW // qw),
                  kv(A_KSW // LANES), kv(A_VSW // LANES),
                  tile(mw, A_QM // mw),
                  pl.BlockSpec((mem_len, mw), lambda b, j: (b, 0)),
                  pl.BlockSpec((mem_len, mw), lambda b, j: (b, 1))],
        out_specs=[tile(w_b.shape[1], 0), tile(qw, 0), tile(mw, 0)],
        out_shape=[jax.ShapeDtypeStruct((t, w_b.shape[1]), BF16),
                   jax.ShapeDtypeStruct((t, qw), BF16),
                   jax.ShapeDtypeStruct((t, mw), BF16)],
        compiler_params=pltpu.CompilerParams(
            dimension_semantics=("parallel", "arbitrary"), vmem_limit_bytes=VMEM_LIMIT),
        name="in_proj_swa_mem",
    )(x2, w_b, b_gate, swa_tables, p_a, [p_a] * (nblk + 1), [p_a] * (nblk + 1), p_a, mkv, mkv)


def _mm_kernel(x_ref, w_ref, o_ref):
    o_ref[...] = jnp.dot(x_ref[...].astype(BF16), w_ref[...],
                         preferred_element_type=F32).astype(o_ref.dtype)


def _matmul_bf16(x2, w, tm, name):
    t, d = x2.shape
    n = w.shape[1]
    return pl.pallas_call(
        _mm_kernel,
        grid=(t // tm,),
        in_specs=[pl.BlockSpec((tm, d), lambda i: (i, 0)), _resident((d, n))],
        out_specs=pl.BlockSpec((tm, n), lambda i: (i, 0)),
        out_shape=jax.ShapeDtypeStruct((t, n), BF16),
        compiler_params=pltpu.CompilerParams(
            dimension_semantics=("parallel",), vmem_limit_bytes=VMEM_LIMIT),
        name=name,
    )(x2, w)


def _sb_kernel(q_ref, k_ref, v_ref, o_ref, *, tq):
    i = pl.program_id(2)
    pairs = q_ref.shape[1] // LANES
    lane = lax.broadcasted_iota(I32, (1, LANES), 1)
    r = lax.broadcasted_iota(I32, (tq, tq), 0)
    c = lax.broadcasted_iota(I32, (tq, tq), 1)
    tri = (r >= c).astype(BF16)
    causal = c < r
    nh = LANES // HEAD_DIM
    hmasks = [(lane >= h * HEAD_DIM) & (lane < (h + 1) * HEAD_DIM) for h in range(nh)]
    qs = []
    for p in range(pairs):
        q = q_ref[:, p * LANES:(p + 1) * LANES]
        qs.append(jnp.concatenate([jnp.where(hm, q, jnp.zeros_like(q)) for hm in hmasks], axis=0))
    causal2 = jnp.concatenate([causal] * nh, axis=0)
    tri2 = jnp.concatenate([tri, tri], axis=0)
    mp = nh * tq

    def blocks(jobs, carry, acc):
        chains = [(jb, p) for jb in range(len(jobs)) for p in range(pairs)]
        z, hl, suffix, ab, av, cin = {}, {}, {}, {}, {}, {}

        def rows(jb):
            return pl.ds(pl.multiple_of(jobs[jb][0] * tq, tq), tq)

        def scores(jb, p):
            zz = _nt_dot(qs[p], k_ref[rows(jb), p * LANES:(p + 1) * LANES])
            if jobs[jb][1]:
                zz = jnp.where(causal2, zz, SB_MASKED)
            z[jb, p] = zz

        def softplus_split(jb, p):
            sp = jnp.maximum(z[jb, p], 0.0) + jnp.log(1.0 + jnp.exp(-jnp.abs(z[jb, p])))
            hi = sp.astype(BF16)
            lo = (sp - hi.astype(F32)).astype(BF16)
            hl[jb, p] = jnp.concatenate([hi, lo], axis=1)

        def cumsum(jb, p):
            suffix[jb, p] = jnp.dot(hl[jb, p], tri2, preferred_element_type=F32)

        def weights(jb, p):
            cin[jb, p] = carry[p] if jb == 0 else cin[jb - 1, p] + step_sum(jb - 1, p)
            a = jnp.exp((z[jb, p] - cin[jb, p]) - suffix[jb, p])
            ab[jb, p] = a.astype(BF16)

        def values(jb, p):
            out = jnp.dot(ab[jb, p], v_ref[rows(jb), p * LANES:(p + 1) * LANES],
                          preferred_element_type=F32)
            live = jobs[jb][2]
            av[jb, p] = out if live is None else jnp.where(live, out, 0.0)

        def step_sum(jb, p):
            total = suffix[jb, p][:, 0:1]
            live = jobs[jb][2]
            return total if live is None else jnp.where(live, total, 0.0)

        stages = (scores, softplus_split, cumsum, weights, values)
        for t in range(len(chains) + len(stages) - 1):
            for s in reversed(range(len(stages))):
                if 0 <= t - s < len(chains):
                    stages[s](*chains[t - s])
        last = len(jobs) - 1
        new_acc = []
        for p in range(pairs):
            total = acc[p]
            for jb in range(len(jobs)):
                total = total + av[jb, p]
            new_acc.append(total)
        return [cin[last, p] + step_sum(last, p) for p in range(pairs)], new_acc

    def block(kb, carry, acc, diag):
        return blocks([(kb, diag, None)], carry, acc)

    carry, acc = blocks([(i, True, None), (jnp.maximum(i - 1, 0), False, i > 0)],
                        [jnp.zeros((mp, 1), F32)] * pairs,
                        [jnp.zeros((mp, LANES), F32)] * pairs)

    def cond(s):
        kb, carry, _ = s
        lowest = carry[0]
        for cp in carry[1:]:
            lowest = jnp.minimum(lowest, cp)
        return (kb >= 0) & (jnp.min(lowest) < SB_SKIP)

    def body(s):
        kb, carry, acc = s
        carry, acc = block(kb, carry, acc, False)
        return kb - 1, carry, acc

    _, _, acc = lax.while_loop(cond, body, (i - 2, carry, acc))
    for p in range(pairs):
        o_ref[:, p * LANES:(p + 1) * LANES] = jnp.where(
            hmasks[0], acc[p][:tq], acc[p][tq:]).astype(o_ref.dtype)


def _sb_attention(p, batch, seq, tq, pairs):
    t = batch * seq
    nq = seq // tq
    w = pairs * LANES
    ngrp = SB_HEADS * HEAD_DIM // w
    qc, kc, vc = C_QSB // w, C_KSB // w, C_VSB // w
    return pl.pallas_call(
        functools.partial(_sb_kernel, tq=tq),
        grid=(batch, ngrp, nq),
        in_specs=[pl.BlockSpec((tq, w), lambda b, h, i: (b * nq + i, qc + h)),
                  pl.BlockSpec((seq, w), lambda b, h, i: (b, kc + h)),
                  pl.BlockSpec((seq, w), lambda b, h, i: (b, vc + h))],
        out_specs=pl.BlockSpec((tq, w), lambda b, h, i: (b * nq + i, h)),
        out_shape=jax.ShapeDtypeStruct((t, SB_HEADS * HEAD_DIM), BF16),
        compiler_params=pltpu.CompilerParams(
            dimension_semantics=("parallel", "parallel", "arbitrary"),
            vmem_limit_bytes=VMEM_LIMIT),
        name="sb_attention",
    )(p, p, p)


def _swa_block(hp_ref, q_ref, rows, kprev, kcur, vprev, vcur, first, o_ref):
    blk = kcur.shape[0]
    sink = hp_ref[:, 2 * blk:]
    lane = lax.broadcasted_iota(I32, (1, LANES), 1)
    kvmasks = [(lane >= kv * HEAD_DIM) & (lane < (kv + 1) * HEAD_DIM) for kv in range(SWA_KV_HEADS)]
    groups = [q_ref[rows, g * LANES:(g + 1) * LANES] for g in range(SWA_GROUP)]
    qs = jnp.concatenate([jnp.where(kvmasks[kv], groups[g], jnp.zeros_like(groups[g]))
                          for kv in range(SWA_KV_HEADS) for g in range(SWA_GROUP)], axis=0)
    keys = jnp.concatenate([kprev[...], kcur[...]], axis=0)
    vals = jnp.concatenate([vprev[...], vcur[...]], axis=0)
    z = _nt_dot(qs, keys) + hp_ref[:, :2 * blk]
    if first is not None:
        col = lax.broadcasted_iota(I32, (1, 2 * blk), 1)
        z = jnp.where(first & (col < blk), jnp.float32(-jnp.inf), z)
    m = jnp.maximum(jnp.max(z, axis=1, keepdims=True), sink)
    p = jnp.exp(z - jnp.concatenate([m, m], axis=1))
    den = jnp.sum(p, axis=1, keepdims=True) + jnp.exp(sink - m)
    o = jnp.dot(p.astype(BF16), vals, preferred_element_type=F32) / den
    for g in range(SWA_GROUP):
        head0 = o[g * blk:(g + 1) * blk]
        head1 = o[(SWA_GROUP + g) * blk:(SWA_GROUP + g + 1) * blk]
        o_ref[rows, g * LANES:(g + 1) * LANES] = jnp.where(kvmasks[0], head0, head1).astype(o_ref.dtype)


def _swa_tables(sinks):
    w = SWA_WINDOW
    slopes = jnp.exp2(-8.0 * jnp.arange(1, SWA_HEADS + 1, dtype=F32) / SWA_HEADS)[:, None, None]
    r = jnp.arange(w)[:, None]
    c = jnp.arange(w)[None, :]
    dist = (r - c).astype(F32)[None]
    neg = jnp.float32(-jnp.inf)
    bias_c = jnp.where((c <= r)[None], -slopes * dist, neg)
    bias_p = jnp.where((c > r)[None], -slopes * (dist + w), neg)
    sink = jnp.broadcast_to(sinks.astype(F32)[:, None, None], (SWA_HEADS, w, w))
    return jnp.concatenate([bias_p, bias_c, sink], axis=2).reshape(SWA_HEADS * w, 3 * w)


def _mem_heads(q_ref, mk_ref, mv_ref, o_ref):
    scale = MEM_HEAD_DIM ** -0.5
    for h in range(MEM_HEADS):
        cols = slice(h * MEM_HEAD_DIM, (h + 1) * MEM_HEAD_DIM)
        z = _nt_dot(q_ref[:, cols], mk_ref[:, cols]) * scale
        m = jnp.max(z, axis=1, keepdims=True)
        p = jnp.exp(z - m)
        den = jnp.sum(p, axis=1, keepdims=True)
        o = jnp.dot(p.astype(BF16), mv_ref[:, cols], preferred_element_type=F32) / den
        o_ref[:, cols] = o.astype(o_ref.dtype)


def _merge_route_kernel(osb_ref, osw_ref, om_ref, g_ref, x_ref, wsb_ref, wsw_ref, wm_ref,
                        wout_ref, lng_ref, lnb_ref, wrh_ref, wrl_ref, rbias_ref,
                        x1_ref, x1p_ref, idx_ref, rank_ref, wgt_ref, cnt_ref, x1_prev, carry_ref):
    i = pl.program_id(0)

    @pl.when(i == 0)
    def _():
        x1_prev[...] = jnp.zeros_like(x1_prev)
        carry_ref[...] = jnp.zeros_like(carry_ref)

    d = x_ref.shape[1]
    st = {}

    def branch(b, o_ref, w_ref):
        def run():
            term = g_ref[:, b * d:(b + 1) * d].astype(F32) * jnp.dot(
                o_ref[...], w_ref[...], preferred_element_type=F32)
            st["merged"] = term if b == 0 else st["merged"] + term
        return run

    def out_proj():
        st["y"] = jnp.dot(st["merged"].astype(BF16), wout_ref[...], preferred_element_type=F32)

    idx, rank, wgt, count = _route(
        x1_prev[...], wrh_ref[...], wrl_ref[...], rbias_ref[...], carry_ref[...],
        side_work=(branch(0, osb_ref, wsb_ref), branch(1, osw_ref, wsw_ref),
                   branch(2, om_ref, wm_ref), out_proj))
    x1 = _layer_norm(DEEPNORM_ALPHA * x_ref[...] + st["y"], lng_ref[...], lnb_ref[...])
    x1_ref[...] = x1
    x1p_ref[...] = _pack_bf16_pair(x1[:, :d // 2], x1[:, d // 2:])
    idx_ref[...] = idx
    rank_ref[...] = rank
    wgt_ref[...] = wgt
    carry_ref[...] = carry_ref[...] + jnp.where(i > 0, count, 0.0)
    cnt_ref[...] = carry_ref[...]
    x1_prev[...] = x1


def _merge_route(o_sb, o_sw, o_m, p, x2, w_sb, w_sw, w_m, w_out, ln_g, ln_b, wr_hi, wr_lo,
                 bias_col, tm):
    t, d = x2.shape
    n = t // tm
    cur = lambda i: (jnp.minimum(i, n - 1), 0)
    row = lambda w: pl.BlockSpec((tm, w), cur)
    slot = pl.BlockSpec((TOP_K, tm), lambda i: (0, jnp.maximum(i - 1, 0)))
    return pl.pallas_call(
        _merge_route_kernel,
        grid=(n + 1,),
        in_specs=[row(o_sb.shape[1]), row(o_sw.shape[1]), row(o_m.shape[1]),
                  pl.BlockSpec((tm, N_BRANCH * d), cur),
                  row(d),
                  _resident(w_sb.shape), _resident(w_sw.shape), _resident(w_m.shape),
                  _resident(w_out.shape), _resident(ln_g.shape), _resident(ln_b.shape),
                  _resident(wr_hi.shape), _resident(wr_lo.shape), _resident(bias_col.shape)],
        out_specs=[row(d), row(d // 2), slot, slot, slot,
                   pl.BlockSpec((N_EXPERTS, 1), lambda i: (0, 0))],
        out_shape=[jax.ShapeDtypeStruct((t, d), F32), jax.ShapeDtypeStruct((t, d // 2), U32),
                   jax.ShapeDtypeStruct((TOP_K, t), I32),
                   jax.ShapeDtypeStruct((TOP_K, t), I32),
                   jax.ShapeDtypeStruct((TOP_K, t), F32),
                   jax.ShapeDtypeStruct((N_EXPERTS, 1), F32)],
        scratch_shapes=[pltpu.VMEM((tm, d), F32), pltpu.VMEM((N_EXPERTS, 1), F32)],
        compiler_params=pltpu.CompilerParams(
            dimension_semantics=("arbitrary",), vmem_limit_bytes=VMEM_LIMIT),
        name="merge_route",
    )(o_sb, o_sw, o_m, p, x2, w_sb, w_sw, w_m, w_out, ln_g, ln_b, wr_hi, wr_lo, bias_col)


def _route(x, wh, wl, bias, carry, side_work=()):
    tr = x.shape[0]
    xh = x.astype(BF16)
    xl = (x - xh.astype(F32)).astype(BF16)
    logits = _nt_dot(wh, xh) + _nt_dot(wh, xl) + _nt_dot(wl, xh)
    scores = _sigmoid(logits)
    biased = scores + bias
    neg = jnp.float32(-jnp.inf)

    sub = lax.broadcasted_iota(I32, (GROUP_SIZE, tr), 0)
    gscore = []
    for g in range(N_GROUPS):
        blk = biased[g * GROUP_SIZE:(g + 1) * GROUP_SIZE, :]
        m1 = jnp.max(blk, axis=0, keepdims=True)
        i1 = jnp.min(jnp.where(blk == m1, sub, GROUP_SIZE), axis=0, keepdims=True)
        m2 = jnp.max(jnp.where(sub == i1, neg, blk), axis=0, keepdims=True)
        gscore.append(m1 + m2)
    gs = jnp.concatenate(gscore, axis=0)

    giota = lax.broadcasted_iota(I32, (N_GROUPS, tr), 0)
    gsel = jnp.zeros((N_GROUPS, tr), F32)
    for _ in range(TOPK_GROUPS):
        m = jnp.max(gs, axis=0, keepdims=True)
        gi = jnp.min(jnp.where(gs == m, giota, N_GROUPS), axis=0, keepdims=True)
        hit = giota == gi
        gsel = jnp.where(hit, 1.0, gsel)
        gs = jnp.where(hit, neg, gs)

    masked = jnp.concatenate(
        [jnp.where(gsel[g:g + 1, :] > 0.0, biased[g * GROUP_SIZE:(g + 1) * GROUP_SIZE, :], neg)
         for g in range(N_GROUPS)], axis=0)

    eiota = lax.broadcasted_iota(I32, (N_EXPERTS, tr), 0)
    sel = jnp.zeros((N_EXPERTS, tr), F32)
    idx_rows, w_rows = [], []
    side_work = list(side_work)
    for k in range(TOP_K):
        if side_work and k % 2 == 0:
            side_work.pop(0)()
        m = jnp.max(masked, axis=0, keepdims=True)
        ei = jnp.min(jnp.where(masked == m, eiota, N_EXPERTS), axis=0, keepdims=True)
        hit = eiota == ei
        idx_rows.append(ei)
        w_rows.append(jnp.sum(jnp.where(hit, scores, 0.0), axis=0, keepdims=True))
        sel = jnp.where(hit, 1.0, sel)
        masked = jnp.where(hit, neg, masked)

    wsum = w_rows[0]
    for wk in w_rows[1:]:
        wsum = wsum + wk
    wgt = jnp.concatenate(w_rows, axis=0) / wsum * ROUTED_SCALE
    idx = jnp.concatenate(idx_rows, axis=0)

    a = lax.broadcasted_iota(I32, (tr, tr), 0)
    b = lax.broadcasted_iota(I32, (tr, tr), 1)
    before = (a < b).astype(BF16)
    rank = jnp.dot(sel.astype(BF16), before, preferred_element_type=F32) + carry
    rank_rows = [jnp.sum(jnp.where(eiota == ei, rank, 0.0), axis=0, keepdims=True)
                 for ei in idx_rows]
    rank = jnp.concatenate(rank_rows, axis=0).astype(I32)
    return idx, rank, wgt, jnp.sum(sel, axis=1, keepdims=True)


def _dest_kernel(idx_ref, rank_ref, start_ref, dest_ref):
    tr = idx_ref.shape[1]
    eiota = lax.broadcasted_iota(I32, (N_EXPERTS, tr), 0)
    rows = []
    for k in range(TOP_K):
        hit = eiota == idx_ref[k:k + 1, :]
        rows.append(jnp.sum(jnp.where(hit, start_ref[...], 0), axis=0, keepdims=True))
    dest_ref[...] = jnp.concatenate(rows, axis=0) + rank_ref[...]


def _dest(idx, rank, start_col, tr):
    t = idx.shape[1]
    slot = pl.BlockSpec((TOP_K, tr), lambda i: (0, i))
    return pl.pallas_call(
        _dest_kernel,
        grid=(t // tr,),
        in_specs=[slot, slot, _resident(start_col.shape)],
        out_specs=slot,
        out_shape=jax.ShapeDtypeStruct((TOP_K, t), I32),
        compiler_params=pltpu.CompilerParams(dimension_semantics=("parallel",)),
        name="slot_dest",
    )(idx, rank, start_col)


def _sc_worker_id():
    return lax.axis_index("s") * SC_CORES + lax.axis_index("c")


def _dispatch(x1p, dest, n_rows):
    t, w = x1p.shape
    per = t // SC_WORKERS
    win = min(SC_INDEX_WINDOW, per)
    mesh = plsc.VectorSubcoreMesh(core_axis_name="c", subcore_axis_name="s")

    @functools.partial(
        pl.kernel, mesh=mesh,
        out_type=jax.ShapeDtypeStruct((n_rows, w), x1p.dtype),
        scratch_types=[pltpu.VMEM((TOP_K, win), I32),
                       pltpu.VMEM((win, w), x1p.dtype),
                       pltpu.SemaphoreType.DMA],
        name="sc_dispatch",
    )
    def scatter_rows(x_hbm, dest_hbm, xs_hbm, idx_v, rows_v, sem):
        base = _sc_worker_id() * per

        @pl.loop(0, per // win)
        def _(j):
            t0 = pl.multiple_of(base + j * win, win)
            pltpu.sync_copy(dest_hbm.at[:, pl.ds(t0, win)], idx_v)
            pltpu.sync_copy(x_hbm.at[pl.ds(t0, win)], rows_v)
            copies = [pltpu.async_copy(rows_v, xs_hbm.at[idx_v.at[k]], sem) for k in range(TOP_K)]
            for c in copies:
                c.wait()

    return scatter_rows(x1p, dest)


def _gather_rows(table, idx):
    n = idx.shape[0]
    w = table.shape[1]
    per = n // SC_WORKERS
    chunk = min(SC_GATHER_ROWS, per // 2)
    assert n % SC_WORKERS == 0 and per % (2 * chunk) == 0, (n, chunk)
    mesh = plsc.VectorSubcoreMesh(core_axis_name="c", subcore_axis_name="s")

    @functools.partial(
        pl.kernel, mesh=mesh,
        out_type=jax.ShapeDtypeStruct((n, w), table.dtype),
        scratch_types=[pltpu.VMEM((per,), I32),
                       pltpu.VMEM((2, chunk, w), table.dtype),
                       pltpu.SemaphoreType.DMA((2,)),
                       pltpu.SemaphoreType.DMA((2,))],
        name="sc_gather",
    )
    def gather_rows(table_hbm, idx_hbm, out_hbm, idx_v, rows_v, gather_sem, put_sem):
        base = _sc_worker_id() * per
        nchunks = per // chunk
        pltpu.sync_copy(idx_hbm.at[pl.ds(base, per)], idx_v)

        def gather(j, b):
            off = pl.multiple_of(j * chunk, chunk)
            return pltpu.make_async_copy(table_hbm.at[idx_v.at[pl.ds(off, chunk)]],
                                         rows_v.at[b], gather_sem.at[b])

        def put(j, b):
            off = pl.multiple_of(j * chunk, chunk)
            return pltpu.make_async_copy(rows_v.at[b], out_hbm.at[pl.ds(base + off, chunk)],
                                         put_sem.at[b])

        gather(0, 0).start()

        @pl.loop(0, nchunks, step=2)
        def _(j):
            for b in (0, 1):
                jj = j + b

                @pl.when(jj + 1 < nchunks)
                def _():
                    @pl.when(jj >= 1)
                    def _():
                        put(jj - 1, 1 - b).wait()
                    gather(jj + 1, 1 - b).start()

                gather(jj, b).wait()
                put(jj, b).start()

        put(nchunks - 2, 0).wait()
        put(nchunks - 1, 1).wait()

    return gather_rows(table, idx)


def _expert_kernel(first_ref, nblk_ref, total_ref, wgu_ref, wd_ref, xs_hbm, ys_hbm,
                   wgu_s, wd_s, xbuf, ybuf, in_sem, out_sem):
    e = pl.program_id(0)
    total = total_ref[0]
    n_in, rows, half = xbuf.shape
    n_out = ybuf.shape[0]
    ahead = EXPERT_AHEAD

    def block_rows(g):
        return pl.ds(pl.multiple_of(g * rows, rows), rows)

    def load(g):
        slot = g % n_in
        return pltpu.make_async_copy(xs_hbm.at[block_rows(g), :], xbuf.at[slot], in_sem.at[slot])

    def store(g):
        slot = g % n_out
        return pltpu.make_async_copy(ybuf.at[slot], ys_hbm.at[block_rows(g), :], out_sem.at[slot])

    @pl.when(e == 0)
    def _():
        for g in range(ahead):
            @pl.when(g < total)
            def _(g=g):
                load(g).start()

    wgu_s[...] = wgu_ref[0].astype(BF16)
    wd_s[...] = wd_ref[0].astype(BF16)
    g0 = first_ref[e]
    nblk = nblk_ref[e]
    ff = wd_s.shape[0]

    def acquire(g):
        load(g).wait()

        @pl.when(g + ahead < total)
        def _():
            load(g + ahead).start()

        @pl.when(g >= n_out)
        def _():
            store(g - n_out).wait()

    def compute(gs):
        st = {}

        def unpack(k):
            x_lo, x_hi = _unpack_bf16_pair(xbuf[gs[k] % n_in])
            st[k, "x"] = (x_lo.astype(BF16), x_hi.astype(BF16))

        def up_proj(k):
            x_lo, x_hi = st[k, "x"]
            st[k, "h"] = (jnp.dot(x_lo, wgu_s[:half, :], preferred_element_type=F32)
                          + jnp.dot(x_hi, wgu_s[half:, :], preferred_element_type=F32))

        def activate(k):
            gate, up = st[k, "h"][:, :ff], st[k, "h"][:, ff:]
            st[k, "a"] = (gate * _sigmoid(gate) * up).astype(BF16)

        def down_proj(k):
            st[k, "y"] = jnp.dot(st[k, "a"], wd_s[...], preferred_element_type=F32)

        def pack(k):
            y = st[k, "y"]
            ybuf[gs[k] % n_out] = _pack_bf16_pair(y[:, :half], y[:, half:])

        stages = (unpack, up_proj, activate, down_proj, pack)
        for t in range(len(gs) + len(stages) - 1):
            for s in reversed(range(len(stages))):
                if 0 <= t - s < len(gs):
                    stages[s](t - s)

    def pair(jj, carry):
        g = g0 + 2 * jj
        acquire(g)
        acquire(g + 1)
        compute([g, g + 1])
        store(g).start()
        store(g + 1).start()
        return carry

    lax.fori_loop(0, nblk // 2, pair, 0)

    @pl.when(nblk % 2 == 1)
    def _():
        g = g0 + nblk - 1
        acquire(g)
        compute([g])
        store(g).start()

    @pl.when(e == pl.num_programs(0) - 1)
    def _():
        for back in range(n_out, 0, -1):
            @pl.when(total >= back)
            def _(back=back):
                store(total - back).wait()


def _experts(xs, first_blk, n_blk, total_blk, w_gu, w_down):
    n_rows, half = xs.shape
    d = 2 * half
    n_exp, _, ff2 = w_gu.shape
    ff = w_down.shape[1]
    grid_spec = pltpu.PrefetchScalarGridSpec(
        num_scalar_prefetch=3,
        grid=(n_exp,),
        in_specs=[pl.BlockSpec((1, d, ff2), lambda e, *_: (e, 0, 0)),
                  pl.BlockSpec((1, ff, d), lambda e, *_: (e, 0, 0)),
                  pl.BlockSpec(memory_space=pl.ANY)],
        out_specs=pl.BlockSpec(memory_space=pl.ANY),
        scratch_shapes=[pltpu.VMEM((d, ff2), BF16), pltpu.VMEM((ff, d), BF16),
                        pltpu.VMEM((EXPERT_IN_SLOTS, EXPERT_ROWS, half), xs.dtype),
                        pltpu.VMEM((EXPERT_OUT_SLOTS, EXPERT_ROWS, half), xs.dtype),
                        pltpu.SemaphoreType.DMA((EXPERT_IN_SLOTS,)),
                        pltpu.SemaphoreType.DMA((EXPERT_OUT_SLOTS,))],
    )
    return pl.pallas_call(
        _expert_kernel,
        grid_spec=grid_spec,
        out_shape=jax.ShapeDtypeStruct((n_rows, half), xs.dtype),
        compiler_params=pltpu.CompilerParams(
            dimension_semantics=("arbitrary",), vmem_limit_bytes=VMEM_LIMIT),
        name="experts",
    )(first_blk, n_blk, total_blk, w_gu, w_down, xs)


def _combine_kernel(x1_ref, w_ref, yg_ref, wsgu_ref, wsd_ref, lng_ref, lnb_ref, o_ref):
    tc = x1_ref.shape[0]
    x1 = x1_ref[...]
    ff = wsd_ref.shape[0]
    h = jnp.dot(x1.astype(BF16), wsgu_ref[...], preferred_element_type=F32)
    gate, up = h[:, :ff], h[:, ff:]
    act = gate * _sigmoid(gate) * up
    moe = jnp.dot(act.astype(BF16), wsd_ref[...], preferred_element_type=F32)

    half = yg_ref.shape[2]
    wt = jnp.transpose(w_ref[...])
    r_lo = jnp.zeros((tc, half), F32)
    r_hi = jnp.zeros((tc, half), F32)
    for k in range(TOP_K):
        y_lo, y_hi = _unpack_bf16_pair(yg_ref[k])
        w = wt[:, k:k + 1]
        r_lo = r_lo + w * y_lo
        r_hi = r_hi + w * y_hi
    moe = moe + jnp.concatenate([r_lo, r_hi], axis=1)
    o_ref[...] = _layer_norm(DEEPNORM_ALPHA * x1 + moe, lng_ref[...], lnb_ref[...])


def _combine(x1, wgt, yg, chunk, ws_gu, ws_down, ln_g, ln_b, tc):
    t, d = x1.shape
    steps = yg.shape[1] // tc
    tok = lambda i: (chunk * steps + i, 0)
    return pl.pallas_call(
        _combine_kernel,
        grid=(steps,),
        in_specs=[pl.BlockSpec((tc, d), tok),
                  pl.BlockSpec((TOP_K, tc), lambda i: (0, chunk * steps + i)),
                  pl.BlockSpec((TOP_K, tc, yg.shape[2]), lambda i: (0, i, 0)),
                  _resident(ws_gu.shape), _resident(ws_down.shape),
                  _resident(ln_g.shape), _resident(ln_b.shape)],
        out_specs=pl.BlockSpec((tc, d), tok),
        out_shape=jax.ShapeDtypeStruct((t, d), F32),
        input_output_aliases={0: 0},
        compiler_params=pltpu.CompilerParams(
            dimension_semantics=("arbitrary",), vmem_limit_bytes=VMEM_LIMIT),
        name="combine_ln2",
    )(x1, wgt, yg, ws_gu, ws_down, ln_g, ln_b)


def _fused_in_weights(w_in):
    d = w_in.shape[0]
    sizes = (SB_HEADS * HEAD_DIM,) * 3 + (SWA_HEADS * HEAD_DIM, SWA_KV_HEADS * HEAD_DIM,
                                          SWA_KV_HEADS * HEAD_DIM, MEM_HEADS * MEM_HEAD_DIM)
    parts, off = [], 0
    for s in sizes:
        parts.append(w_in[:, off:off + s])
        off += s
    q_sb, k_sb, v_sb, q_sw, k_sw, v_sw, q_m = parts
    gates = w_in[:, off:]
    scale = HEAD_DIM ** -0.5
    q_sw = (q_sw * scale).reshape(d, SWA_KV_HEADS, SWA_GROUP, HEAD_DIM)
    q_sw = jnp.swapaxes(q_sw, 1, 2).reshape(d, SWA_HEADS * HEAD_DIM)
    w_a =jnp.concatenate([q_sw, q_m, k_sw, v_sw], axis=1).astype(BF16)
    w_b = jnp.concatenate([gates, q_sb * scale, k_sb, v_sb], axis=1).astype(BF16)
    assert w_a.shape[1] == PA_COLS and w_b.shape[1] == PB_COLS
    return w_a, w_b


def _grouped_swa_out_weights(w_o_swa):
    d = w_o_swa.shape[1]
    w = w_o_swa.reshape(SWA_KV_HEADS, SWA_GROUP, HEAD_DIM, d)
    return jnp.swapaxes(w, 0, 1).reshape(SWA_HEADS * HEAD_DIM, d).astype(BF16)


def kernel(x, mem, w_in, b_gate, w_mem_kv, sinks, w_o_sb, w_o_swa, w_o_mem, w_out,
           ln1_g, ln1_b, w_router, router_bias, w_e_gu, w_e_down, w_s_gu, w_s_down,
           ln2_g, ln2_b):
    batch, seq, d = x.shape
    mem_len = mem.shape[1]
    t = batch * seq
    x2 = x.reshape(t, d)
    row_tile = min(512, t)

    w_a, w_b = _fused_in_weights(w_in)
    p_a = _matmul_bf16(x2, w_a, row_tile, "in_proj_attn")
    mkv = _matmul_bf16(mem.reshape(batch * mem_len, d), w_mem_kv.astype(BF16), mem_len,
                       "mem_kv_proj")
    p, o_sw, o_m = _proj_attn(x2, w_b, b_gate.reshape(1, -1), p_a, mkv, _swa_tables(sinks),
                              batch, seq, mem_len)
    o_sb = _sb_attention(p, batch, seq, min(256, seq), SB_PAIRS_PER_STEP)
    wr_t = w_router.T
    wr_hi = wr_t.astype(BF16)
    wr_lo = (wr_t - wr_hi.astype(F32)).astype(BF16)
    x1, x1p, idx, rank, wgt, cnt = _merge_route(
        o_sb, o_sw, o_m, p, x2, w_o_sb.astype(BF16), _grouped_swa_out_weights(w_o_swa),
        w_o_mem.astype(BF16), w_out.astype(BF16), ln1_g.reshape(1, d), ln1_b.reshape(1, d),
        wr_hi, wr_lo, router_bias.reshape(-1, 1).astype(F32), row_tile)

    out = _moe_ln(x1, x1p, idx, rank, wgt, cnt, w_e_gu, w_e_down, w_s_gu, w_s_down, ln2_g, ln2_b)
    return out.reshape(batch, seq, d)


def _moe_ln(x1, x1p, idx, rank, wgt, cnt, w_e_gu, w_e_down, w_s_gu, w_s_down, ln2_g, ln2_b):
    t, d = x1.shape
    counts = cnt[:, 0].astype(I32)
    padded = (counts + EXPERT_ROWS - 1) // EXPERT_ROWS * EXPERT_ROWS
    pad_end = jnp.cumsum(padded)
    pad_start = pad_end - padded
    n_blocks = t * TOP_K // EXPERT_ROWS + N_EXPERTS
    dest = _dest(idx, rank, pad_start.reshape(-1, 1), min(2048, t))

    xs = _dispatch(x1p, dest, n_blocks * EXPERT_ROWS)
    ys = _experts(xs, pad_start // EXPERT_ROWS, padded // EXPERT_ROWS,
                  pad_end[-1:] // EXPERT_ROWS, w_e_gu, w_e_down)
    tchunk = t // COMBINE_CHUNKS
    ws_gu, ws_down = w_s_gu.astype(BF16), w_s_down.astype(BF16)
    out = x1
    for c in range(COMBINE_CHUNKS):
        slots = dest[:, c * tchunk:(c + 1) * tchunk].reshape(-1)
        yg = _gather_rows(ys, slots).reshape(TOP_K, tchunk, -1)
        out = _combine(out, wgt, yg, c, ws_gu, ws_down, ln2_g.reshape(1, d), ln2_b.reshape(1, d),
                       min(256, tchunk))
    return out
```

```python
import functools

import jax
import jax.numpy as jnp
from jax import lax
from jax.experimental import pallas as pl
from jax.experimental.pallas import tpu as pltpu
from jax.experimental.pallas import tpu_sc as plsc

F32 = jnp.float32
BF16 = jnp.bfloat16
I32 = jnp.int32
U32 = jnp.uint32

HEAD_DIM = 64
SB_HEADS = 8
SWA_HEADS = 8
SWA_KV_HEADS = 2
SWA_GROUP = SWA_HEADS // SWA_KV_HEADS
SWA_WINDOW = 128
MEM_HEADS = 4
MEM_HEAD_DIM = 128
N_BRANCH = 3
N_EXPERTS = 256
TOP_K = 8
N_GROUPS = 8
GROUP_SIZE = N_EXPERTS // N_GROUPS
TOPK_GROUPS = 4
ROUTED_SCALE = 2.5
LN_EPS = 1e-5
DEPTH = 1
DEEPNORM_ALPHA = (2 * DEPTH) ** 0.25

LANES = 128
SC_CORES = 2
SC_SUBCORES = 16
SC_WORKERS = SC_CORES * SC_SUBCORES
SC_INDEX_WINDOW = 128
SC_GATHER_ROWS = 64
VMEM_LIMIT = 56 * 1024 * 1024

A_QSW = 0
A_QM = 512
A_KSW = 1024
A_VSW = 1152
PA_COLS = 1280
C_QSB = 3072
C_KSB = 3584
C_VSB = 4096
PB_COLS = 4608
PROJ_CHUNK = 256
PROJ_ROWS = 512

SB_SKIP = 110.0
SB_MASKED = -1e30
SB_PAIRS_PER_STEP = 4

EXPERT_ROWS = 512
EXPERT_AHEAD = 4
EXPERT_IN_SLOTS = EXPERT_AHEAD + 2
EXPERT_OUT_SLOTS = 4
COMBINE_CHUNKS = 8


def _nt_dot(a, b):
    return lax.dot_general(a, b, (((1,), (1,)), ((), ())), preferred_element_type=F32)


def _sigmoid(x):
    return 1.0 / (1.0 + jnp.exp(-x))


def _layer_norm(h, g, b):
    mu = jnp.mean(h, axis=-1, keepdims=True)
    d = h - mu
    var = jnp.mean(d * d, axis=-1, keepdims=True)
    return d * lax.rsqrt(var + LN_EPS) * g + b


def _pack_bf16_pair(a, b):
    a_bits = lax.bitcast_convert_type(a.astype(BF16).astype(F32), U32)
    b_bits = lax.bitcast_convert_type(b.astype(BF16).astype(F32), U32)
    return (a_bits >> 16) | b_bits


def _unpack_bf16_pair(w):
    a = lax.bitcast_convert_type(w << 16, F32)
    b = lax.bitcast_convert_type(w & jnp.uint32(0xFFFF0000), F32)
    return a, b


def _resident(shape):
    nd = len(shape)
    return pl.BlockSpec(shape, lambda *_: (0,) * nd, pipeline_mode=pl.Buffered(1))


def _proj_attn_kernel(x_ref, w_ref, b_ref, hp_ref, qsw_ref, k_refs, v_refs, qm_ref, mk_ref, mv_ref,
                      p_ref, osw_ref, om_ref, *, gate_cols):
    j = pl.program_id(1)
    xb = x_ref[...].astype(BF16)

    def proj_chunk(c):
        cols = slice(c * PROJ_CHUNK, (c + 1) * PROJ_CHUNK)
        acc = jnp.dot(xb, w_ref[:, cols], preferred_element_type=F32)
        if (c + 1) * PROJ_CHUNK <= gate_cols:
            acc = _sigmoid(acc + b_ref[:, cols])
        p_ref[:, cols] = acc.astype(p_ref.dtype)

    blk = SWA_WINDOW
    nblk = x_ref.shape[0] // blk

    def swa(hb):
        rows = slice(hb * blk, (hb + 1) * blk)
        first = (j == 0) if hb == 0 else None
        _swa_block(hp_ref, qsw_ref, rows, k_refs[hb], k_refs[hb + 1], v_refs[hb], v_refs[hb + 1],
                   first, osw_ref)

    side = [functools.partial(swa, hb) for hb in range(nblk)]
    side.append(functools.partial(_mem_heads, qm_ref, mk_ref, mv_ref, om_ref))
    nchunk = p_ref.shape[1] // PROJ_CHUNK
    per = -(-nchunk // len(side))
    for s, work in enumerate(side):
        for c in range(s * per, min((s + 1) * per, nchunk)):
            proj_chunk(c)
        work()


def _proj_attn(x2, w_b, b_gate, p_a, mkv, swa_tables, batch, seq, mem_len):
    t, d = x2.shape
    tm = PROJ_ROWS
    nt = seq // tm
    blk = SWA_WINDOW
    nblk = tm // blk
    nb = seq // blk
    qw = SWA_HEADS * HEAD_DIM
    mw = MEM_HEADS * MEM_HEAD_DIM
    tile = lambda w, col: pl.BlockSpec((tm, w), lambda b, j: (b * nt + j, col))
    kv = lambda col: [pl.BlockSpec((blk, LANES),
                                   functools.partial(lambda b, j, off, col: (
                                       b * nb + jnp.maximum(nblk * j + off, 0), col), off=off, col=col))
                      for off in range(-1, nblk)]
    return pl.pallas_call(
        functools.partial(_proj_attn_kernel, gate_cols=b_gate.shape[1]),
        grid=(batch, nt),
        in_specs=[tile(d, 0), _resident(w_b.shape), _resident(b_gate.shape),
                  _resident(swa_tables.shape),
                  tile(qw, A_QSW // qw),
                  kv(A_KSW // LANES), kv(A_VSW // LANES),
                  tile(mw, A_QM // mw),
                  pl.BlockSpec((mem_len, mw), lambda b, j: (b, 0)),
                  pl.BlockSpec((mem_len, mw), lambda b, j: (b, 1))],
        out_specs=[tile(w_b.shape[1], 0), tile(qw, 0), tile(mw, 0)],
        out_shape=[jax.ShapeDtypeStruct((t, w_b.shape[1]), BF16),
                   jax.ShapeDtypeStruct((t, qw), BF16),
                   jax.ShapeDtypeStruct((t, mw), BF16)],
        compiler_params=pltpu.CompilerParams(
            dimension_semantics=("parallel", "arbitrary"), vmem_limit_bytes=VMEM_LIMIT),
        name="in_proj_swa_mem",
    )(x2, w_b, b_gate, swa_tables, p_a, [p_a] * (nblk + 1), [p_a] * (nblk + 1), p_a, mkv, mkv)


def _mm_kernel(x_ref, w_ref, o_ref):
    o_ref[...] = jnp.dot(x_ref[...].astype(BF16), w_ref[...],
                         preferred_element_type=F32).astype(o_ref.dtype)


def _matmul_bf16(x2, w, tm, name):
    t, d = x2.shape
    n = w.shape[1]
    return pl.pallas_call(
        _mm_kernel,
        grid=(t // tm,),
        in_specs=[pl.BlockSpec((tm, d), lambda i: (i, 0)), _resident((d, n))],
        out_specs=pl.BlockSpec((tm, n), lambda i: (i, 0)),
        out_shape=jax.ShapeDtypeStruct((t, n), BF16),
        compiler_params=pltpu.CompilerParams(
            dimension_semantics=("parallel",), vmem_limit_bytes=VMEM_LIMIT),
        name=name,
    )(x2, w)


def _sb_kernel(q_ref, k_ref, v_ref, o_ref, *, tq):
    i = pl.program_id(2)
    pairs = q_ref.shape[1] // LANES
    lane = lax.broadcasted_iota(I32, (1, LANES), 1)
    r = lax.broadcasted_iota(I32, (tq, tq), 0)
    c = lax.broadcasted_iota(I32, (tq, tq), 1)
    tri = (r >= c).astype(BF16)
    causal = c < r
    nh = LANES // HEAD_DIM
    hmasks = [(lane >= h * HEAD_DIM) & (lane < (h + 1) * HEAD_DIM) for h in range(nh)]
    qs = []
    for p in range(pairs):
        q = q_ref[:, p * LANES:(p + 1) * LANES]
        qs.append(jnp.concatenate([jnp.where(hm, q, jnp.zeros_like(q)) for hm in hmasks], axis=0))
    causal2 = jnp.concatenate([causal] * nh, axis=0)
    tri2 = jnp.concatenate([tri, tri], axis=0)
    mp = nh * tq

    def blocks(jobs, carry, acc):
        chains = [(jb, p) for jb in range(len(jobs)) for p in range(pairs)]
        z, hl, suffix, ab, av, cin = {}, {}, {}, {}, {}, {}

        def rows(jb):
            return pl.ds(pl.multiple_of(jobs[jb][0] * tq, tq), tq)

        def scores(jb, p):
            zz = _nt_dot(qs[p], k_ref[rows(jb), p * LANES:(p + 1) * LANES])
            if jobs[jb][1]:
                zz = jnp.where(causal2, zz, SB_MASKED)
            z[jb, p] = zz

        def softplus_split(jb, p):
            sp = jnp.maximum(z[jb, p], 0.0) + jnp.log(1.0 + jnp.exp(-jnp.abs(z[jb, p])))
            hi = sp.astype(BF16)
            lo = (sp - hi.astype(F32)).astype(BF16)
            hl[jb, p] = jnp.concatenate([hi, lo], axis=1)

        def cumsum(jb, p):
            suffix[jb, p] = jnp.dot(hl[jb, p], tri2, preferred_element_type=F32)

        def weights(jb, p):
            cin[jb, p] = carry[p] if jb == 0 else cin[jb - 1, p] + step_sum(jb - 1, p)
            a = jnp.exp((z[jb, p] - cin[jb, p]) - suffix[jb, p])
            ab[jb, p] = a.astype(BF16)

        def values(jb, p):
            out = jnp.dot(ab[jb, p], v_ref[rows(jb), p * LANES:(p + 1) * LANES],
                          preferred_element_type=F32)
            live = jobs[jb][2]
            av[jb, p] = out if live is None else jnp.where(live, out, 0.0)

        def step_sum(jb, p):
            total = suffix[jb, p][:, 0:1]
            live = jobs[jb][2]
            return total if live is None else jnp.where(live, total, 0.0)

        stages = (scores, softplus_split, cumsum, weights, values)
        for t in range(len(chains) + len(stages) - 1):
            for s in reversed(range(len(stages))):
                if 0 <= t - s < len(chains):
                    stages[s](*chains[t - s])
        last = len(jobs) - 1
        new_acc = []
        for p in range(pairs):
            total = acc[p]
            for jb in range(len(jobs)):
                total = total + av[jb, p]
            new_acc.append(total)
        return [cin[last, p] + step_sum(last, p) for p in range(pairs)], new_acc

    def block(kb, carry, acc, diag):
        return blocks([(kb, diag, None)], carry, acc)

    carry, acc = blocks([(i, True, None), (jnp.maximum(i - 1, 0), False, i > 0)],
                        [jnp.zeros((mp, 1), F32)] * pairs,
                        [jnp.zeros((mp, LANES), F32)] * pairs)

    def cond(s):
        kb, carry, _ = s
        lowest = carry[0]
        for cp in carry[1:]:
            lowest = jnp.minimum(lowest, cp)
        return (kb >= 0) & (jnp.min(lowest) < SB_SKIP)

    def body(s):
        kb, carry, acc = s
        carry, acc = block(kb, carry, acc, False)
        return kb - 1, carry, acc

    _, _, acc = lax.while_loop(cond, body, (i - 2, carry, acc))
    for p in range(pairs):
        o_ref[:, p * LANES:(p + 1) * LANES] = jnp.where(
            hmasks[0], acc[p][:tq], acc[p][tq:]).astype(o_ref.dtype)


def _sb_attention(p, batch, seq, tq, pairs):
    t = batch * seq
    nq = seq // tq
    w = pairs * LANES
    ngrp = SB_HEADS * HEAD_DIM // w
    qc, kc, vc = C_QSB // w, C_KSB // w, C_VSB // w
    return pl.pallas_call(
        functools.partial(_sb_kernel, tq=tq),
        grid=(batch, ngrp, nq),
        in_specs=[pl.BlockSpec((tq, w), lambda b, h, i: (b * nq + i, qc + h)),
                  pl.BlockSpec((seq, w), lambda b, h, i: (b, kc + h)),
                  pl.BlockSpec((seq, w), lambda b, h, i: (b, vc + h))],
        out_specs=pl.BlockSpec((tq, w), lambda b, h, i: (b * nq + i, h)),
        out_shape=jax.ShapeDtypeStruct((t, SB_HEADS * HEAD_DIM), BF16),
        compiler_params=pltpu.CompilerParams(
            dimension_semantics=("parallel", "parallel", "arbitrary"),
            vmem_limit_bytes=VMEM_LIMIT),
        name="sb_attention",
    )(p, p, p)


def _swa_block(hp_ref, q_ref, rows, kprev, kcur, vprev, vcur, first, o_ref):
    blk = kcur.shape[0]
    sink = hp_ref[:, 2 * blk:]
    lane = lax.broadcasted_iota(I32, (1, LANES), 1)
    kvmasks = [(lane >= kv * HEAD_DIM) & (lane < (kv + 1) * HEAD_DIM) for kv in range(SWA_KV_HEADS)]
    groups = [q_ref[rows, g * LANES:(g + 1) * LANES] for g in range(SWA_GROUP)]
    qs = jnp.concatenate([jnp.where(kvmasks[kv], groups[g], jnp.zeros_like(groups[g]))
                          for kv in range(SWA_KV_HEADS) for g in range(SWA_GROUP)], axis=0)
    keys = jnp.concatenate([kprev[...], kcur[...]], axis=0)
    vals = jnp.concatenate([vprev[...], vcur[...]], axis=0)
    z = _nt_dot(qs, keys) + hp_ref[:, :2 * blk]
    if first is not None:
        col = lax.broadcasted_iota(I32, (1, 2 * blk), 1)
        z = jnp.where(first & (col < blk), jnp.float32(-jnp.inf), z)
    m = jnp.maximum(jnp.max(z, axis=1, keepdims=True), sink)
    p = jnp.exp(z - jnp.concatenate([m, m], axis=1))
    den = jnp.sum(p, axis=1, keepdims=True) + jnp.exp(sink - m)
    o = jnp.dot(p.astype(BF16), vals, preferred_element_type=F32) / den
    for g in range(SWA_GROUP):
        head0 = o[g * blk:(g + 1) * blk]
        head1 = o[(SWA_GROUP + g) * blk:(SWA_GROUP + g + 1) * blk]
        o_ref[rows, g * LANES:(g + 1) * LANES] = jnp.where(kvmasks[0], head0, head1).astype(o_ref.dtype)


def _swa_tables(sinks):
    w = SWA_WINDOW
    slopes = jnp.exp2(-8.0 * jnp.arange(1, SWA_HEADS + 1, dtype=F32) / SWA_HEADS)[:, None, None]
    r = jnp.arange(w)[:, None]
    c = jnp.arange(w)[None, :]
    dist = (r - c).astype(F32)[None]
    neg = jnp.float32(-jnp.inf)
    bias_c = jnp.where((c <= r)[None], -slopes * dist, neg)
    bias_p = jnp.where((c > r)[None], -slopes * (dist + w), neg)
    sink = jnp.broadcast_to(sinks.astype(F32)[:, None, None], (SWA_HEADS, w, w))
    return jnp.concatenate([bias_p, bias_c, sink], axis=2).reshape(SWA_HEADS * w, 3 * w)


def _mem_heads(q_ref, mk_ref, mv_ref, o_ref):
    scale = MEM_HEAD_DIM ** -0.5
    for h in range(MEM_HEADS):
        cols = slice(h * MEM_HEAD_DIM, (h + 1) * MEM_HEAD_DIM)
        z = _nt_dot(q_ref[:, cols], mk_ref[:, cols]) * scale
        m = jnp.max(z, axis=1, keepdims=True)
        p = jnp.exp(z - m)
        den = jnp.sum(p, axis=1, keepdims=True)
        o = jnp.dot(p.astype(BF16), mv_ref[:, cols], preferred_element_type=F32) / den
        o_ref[:, cols] = o.astype(o_ref.dtype)


def _merge_route_kernel(osb_ref, osw_ref, om_ref, g_ref, x_ref, wsb_ref, wsw_ref, wm_ref,
                        wout_ref, lng_ref, lnb_ref, wrh_ref, wrl_ref, rbias_ref,
                        x1_ref, x1p_ref, idx_ref, rank_ref, wgt_ref, cnt_ref, x1_prev, carry_ref):
    i = pl.program_id(0)

    @pl.when(i == 0)
    def _():
        x1_prev[...] = jnp.zeros_like(x1_prev)
        carry_ref[...] = jnp.zeros_like(carry_ref)

    d = x_ref.shape[1]
    st = {}

    def branch(b, o_ref, w_ref):
        def run():
            term = g_ref[:, b * d:(b + 1) * d].astype(F32) * jnp.dot(
                o_ref[...], w_ref[...], preferred_element_type=F32)
            st["merged"] = term if b == 0 else st["merged"] + term
        return run

    def out_proj():
        st["y"] = jnp.dot(st["merged"].astype(BF16), wout_ref[...], preferred_element_type=F32)

    idx, rank, wgt, count = _route(
        x1_prev[...], wrh_ref[...], wrl_ref[...], rbias_ref[...], carry_ref[...],
        side_work=(branch(0, osb_ref, wsb_ref), branch(1, osw_ref, wsw_ref),
                   branch(2, om_ref, wm_ref), out_proj))
    x1 = _layer_norm(DEEPNORM_ALPHA * x_ref[...] + st["y"], lng_ref[...], lnb_ref[...])
    x1_ref[...] = x1
    x1p_ref[...] = _pack_bf16_pair(x1[:, :d // 2], x1[:, d // 2:])
    idx_ref[...] = idx
    rank_ref[...] = rank
    wgt_ref[...] = wgt
    carry_ref[...] = carry_ref[...] + jnp.where(i > 0, count, 0.0)
    cnt_ref[...] = carry_ref[...]
    x1_prev[...] = x1


def _merge_route(o_sb, o_sw, o_m, p, x2, w_sb, w_sw, w_m, w_out, ln_g, ln_b, wr_hi, wr_lo,
                 bias_col, tm):
    t, d = x2.shape
    n = t // tm
    cur = lambda i: (jnp.minimum(i, n - 1), 0)
    row = lambda w: pl.BlockSpec((tm, w), cur)
    slot = pl.BlockSpec((TOP_K, tm), lambda i: (0, jnp.maximum(i - 1, 0)))
    return pl.pallas_call(
        _merge_route_kernel,
        grid=(n + 1,),
        in_specs=[row(o_sb.shape[1]), row(o_sw.shape[1]), row(o_m.shape[1]),
                  pl.BlockSpec((tm, N_BRANCH * d), cur),
                  row(d),
                  _resident(w_sb.shape), _resident(w_sw.shape), _resident(w_m.shape),
                  _resident(w_out.shape), _resident(ln_g.shape), _resident(ln_b.shape),
                  _resident(wr_hi.shape), _resident(wr_lo.shape), _resident(bias_col.shape)],
        out_specs=[row(d), row(d // 2), slot, slot, slot,
                   pl.BlockSpec((N_EXPERTS, 1), lambda i: (0, 0))],
        out_shape=[jax.ShapeDtypeStruct((t, d), F32), jax.ShapeDtypeStruct((t, d // 2), U32),
                   jax.ShapeDtypeStruct((TOP_K, t), I32),
                   jax.ShapeDtypeStruct((TOP_K, t), I32),
                   jax.ShapeDtypeStruct((TOP_K, t), F32),
                   jax.ShapeDtypeStruct((N_EXPERTS, 1), F32)],
        scratch_shapes=[pltpu.VMEM((tm, d), F32), pltpu.VMEM((N_EXPERTS, 1), F32)],
        compiler_params=pltpu.CompilerParams(
            dimension_semantics=("arbitrary",), vmem_limit_bytes=VMEM_LIMIT),
        name="merge_route",
    )(o_sb, o_sw, o_m, p, x2, w_sb, w_sw, w_m, w_out, ln_g, ln_b, wr_hi, wr_lo, bias_col)


def _route(x, wh, wl, bias, carry, side_work=()):
    tr = x.shape[0]
    xh = x.astype(BF16)
    xl = (x - xh.astype(F32)).astype(BF16)
    logits = _nt_dot(wh, xh) + _nt_dot(wh, xl) + _nt_dot(wl, xh)
    scores = _sigmoid(logits)
    biased = scores + bias
    neg = jnp.float32(-jnp.inf)

    sub = lax.broadcasted_iota(I32, (GROUP_SIZE, tr), 0)
    gscore = []
    for g in range(N_GROUPS):
        blk = biased[g * GROUP_SIZE:(g + 1) * GROUP_SIZE, :]
        m1 = jnp.max(blk, axis=0, keepdims=True)
        i1 = jnp.min(jnp.where(blk == m1, sub, GROUP_SIZE), axis=0, keepdims=True)
        m2 = jnp.max(jnp.where(sub == i1, neg, blk), axis=0, keepdims=True)
        gscore.append(m1 + m2)
    gs = jnp.concatenate(gscore, axis=0)

    giota = lax.broadcasted_iota(I32, (N_GROUPS, tr), 0)
    gsel = jnp.zeros((N_GROUPS, tr), F32)
    for _ in range(TOPK_GROUPS):
        m = jnp.max(gs, axis=0, keepdims=True)
        gi = jnp.min(jnp.where(gs == m, giota, N_GROUPS), axis=0, keepdims=True)
        hit = giota == gi
        gsel = jnp.where(hit, 1.0, gsel)
        gs = jnp.where(hit, neg, gs)

    masked = jnp.concatenate(
        [jnp.where(gsel[g:g + 1, :] > 0.0, biased[g * GROUP_SIZE:(g + 1) * GROUP_SIZE, :], neg)
         for g in range(N_GROUPS)], axis=0)

    eiota = lax.broadcasted_iota(I32, (N_EXPERTS, tr), 0)
    sel = jnp.zeros((N_EXPERTS, tr), F32)
    idx_rows, w_rows = [], []
    side_work = list(side_work)
    for k in range(TOP_K):
        if side_work and k % 2 == 0:
            side_work.pop(0)()
        m = jnp.max(masked, axis=0, keepdims=True)
        ei = jnp.min(jnp.where(masked == m, eiota, N_EXPERTS), axis=0, keepdims=True)
        hit = eiota == ei
        idx_rows.append(ei)
        w_rows.append(jnp.sum(jnp.where(hit, scores, 0.0), axis=0, keepdims=True))
        sel = jnp.where(hit, 1.0, sel)
        masked = jnp.where(hit, neg, masked)

    wsum = w_rows[0]
    for wk in w_rows[1:]:
        wsum = wsum + wk
    wgt = jnp.concatenate(w_rows, axis=0) / wsum * ROUTED_SCALE
    idx = jnp.concatenate(idx_rows, axis=0)

    a = lax.broadcasted_iota(I32, (tr, tr), 0)
    b = lax.broadcasted_iota(I32, (tr, tr), 1)
    before = (a < b).astype(BF16)
    rank = jnp.dot(sel.astype(BF16), before, preferred_element_type=F32) + carry
    rank_rows = [jnp.sum(jnp.where(eiota == ei, rank, 0.0), axis=0, keepdims=True)
                 for ei in idx_rows]
    rank = jnp.concatenate(rank_rows, axis=0).astype(I32)
    return idx, rank, wgt, jnp.sum(sel, axis=1, keepdims=True)


def _dest_kernel(idx_ref, rank_ref, start_ref, dest_ref):
    tr = idx_ref.shape[1]
    eiota = lax.broadcasted_iota(I32, (N_EXPERTS, tr), 0)
    rows = []
    for k in range(TOP_K):
        hit = eiota == idx_ref[k:k + 1, :]
        rows.append(jnp.sum(jnp.where(hit, start_ref[...], 0), axis=0, keepdims=True))
    dest_ref[...] = jnp.concatenate(rows, axis=0) + rank_ref[...]


def _dest(idx, rank, start_col, tr):
    t = idx.shape[1]
    slot = pl.BlockSpec((TOP_K, tr), lambda i: (0, i))
    return pl.pallas_call(
        _dest_kernel,
        grid=(t // tr,),
        in_specs=[slot, slot, _resident(start_col.shape)],
        out_specs=slot,
        out_shape=jax.ShapeDtypeStruct((TOP_K, t), I32),
        compiler_params=pltpu.CompilerParams(dimension_semantics=("parallel",)),
        name="slot_dest",
    )(idx, rank, start_col)


def _sc_worker_id():
    return lax.axis_index("s") * SC_CORES + lax.axis_index("c")


def _dispatch(x1p, dest, n_rows):
    t, w = x1p.shape
    per = t // SC_WORKERS
    win = min(SC_INDEX_WINDOW, per)
    mesh = plsc.VectorSubcoreMesh(core_axis_name="c", subcore_axis_name="s")

    @functools.partial(
        pl.kernel, mesh=mesh,
        out_type=jax.ShapeDtypeStruct((n_rows, w), x1p.dtype),
        scratch_types=[pltpu.VMEM((TOP_K, win), I32),
                       pltpu.VMEM((win, w), x1p.dtype),
                       pltpu.SemaphoreType.DMA],
        name="sc_dispatch",
    )
    def scatter_rows(x_hbm, dest_hbm, xs_hbm, idx_v, rows_v, sem):
        base = _sc_worker_id() * per

        @pl.loop(0, per // win)
        def _(j):
            t0 = pl.multiple_of(base + j * win, win)
            pltpu.sync_copy(dest_hbm.at[:, pl.ds(t0, win)], idx_v)
            pltpu.sync_copy(x_hbm.at[pl.ds(t0, win)], rows_v)
            copies = [pltpu.async_copy(rows_v, xs_hbm.at[idx_v.at[k]], sem) for k in range(TOP_K)]
            for c in copies:
                c.wait()

    return scatter_rows(x1p, dest)


def _gather_rows(table, idx):
    n = idx.shape[0]
    w = table.shape[1]
    per = n // SC_WORKERS
    chunk = min(SC_GATHER_ROWS, per // 2)
    assert n % SC_WORKERS == 0 and per % (2 * chunk) == 0, (n, chunk)
    mesh = plsc.VectorSubcoreMesh(core_axis_name="c", subcore_axis_name="s")

    @functools.partial(
        pl.kernel, mesh=mesh,
        out_type=jax.ShapeDtypeStruct((n, w), table.dtype),
        scratch_types=[pltpu.VMEM((per,), I32),
                       pltpu.VMEM((2, chunk, w), table.dtype),
                       pltpu.SemaphoreType.DMA((2,)),
                       pltpu.SemaphoreType.DMA((2,))],
        name="sc_gather",
    )
    def gather_rows(table_hbm, idx_hbm, out_hbm, idx_v, rows_v, gather_sem, put_sem):
        base = _sc_worker_id() * per
        nchunks = per // chunk
        pltpu.sync_copy(idx_hbm.at[pl.ds(base, per)], idx_v)

        def gather(j, b):
            off = pl.multiple_of(j * chunk, chunk)
            return pltpu.make_async_copy(table_hbm.at[idx_v.at[pl.ds(off, chunk)]],
                                         rows_v.at[b], gather_sem.at[b])

        def put(j, b):
            off = pl.multiple_of(j * chunk, chunk)
            return pltpu.make_async_copy(rows_v.at[b], out_hbm.at[pl.ds(base + off, chunk)],
                                         put_sem.at[b])

        gather(0, 0).start()

        @pl.loop(0, nchunks, step=2)
        def _(j):
            for b in (0, 1):
                jj = j + b

                @pl.when(jj + 1 < nchunks)
                def _():
                    @pl.when(jj >= 1)
                    def _():
                        put(jj - 1, 1 - b).wait()
                    gather(jj + 1, 1 - b).start()

                gather(jj, b).wait()
                put(jj, b).start()

        put(nchunks - 2, 0).wait()
        put(nchunks - 1, 1).wait()

    return gather_rows(table, idx)


def _expert_kernel(first_ref, nblk_ref, total_ref, wgu_ref, wd_ref, xs_hbm, ys_hbm,
                   wgu_s, wd_s, xbuf, ybuf, in_sem, out_sem):
    e = pl.program_id(0)
    total = total_ref[0]
    n_in, rows, half = xbuf.shape
    n_out = ybuf.shape[0]
    ahead = EXPERT_AHEAD

    def block_rows(g):
        return pl.ds(pl.multiple_of(g * rows, rows), rows)

    def load(g):
        slot = g % n_in
        return pltpu.make_async_copy(xs_hbm.at[block_rows(g), :], xbuf.at[slot], in_sem.at[slot])

    def store(g):
        slot = g % n_out
        return pltpu.make_async_copy(ybuf.at[slot], ys_hbm.at[block_rows(g), :], out_sem.at[slot])

    @pl.when(e == 0)
    def _():
        for g in range(ahead):
            @pl.when(g < total)
            def _(g=g):
                load(g).start()

    wgu_s[...] = wgu_ref[0].astype(BF16)
    wd_s[...] = wd_ref[0].astype(BF16)
    g0 = first_ref[e]
    nblk = nblk_ref[e]
    ff = wd_s.shape[0]

    def acquire(g):
        load(g).wait()

        @pl.when(g + ahead < total)
        def _():
            load(g + ahead).start()

        @pl.when(g >= n_out)
        def _():
            store(g - n_out).wait()

    def compute(gs):
        st = {}

        def unpack(k):
            x_lo, x_hi = _unpack_bf16_pair(xbuf[gs[k] % n_in])
            st[k, "x"] = (x_lo.astype(BF16), x_hi.astype(BF16))

        def up_proj(k):
            x_lo, x_hi = st[k, "x"]
            st[k, "h"] = (jnp.dot(x_lo, wgu_s[:half, :], preferred_element_type=F32)
                          + jnp.dot(x_hi, wgu_s[half:, :], preferred_element_type=F32))

        def activate(k):
            gate, up = st[k, "h"][:, :ff], st[k, "h"][:, ff:]
            st[k, "a"] = (gate * _sigmoid(gate) * up).astype(BF16)

        def down_proj(k):
            st[k, "y"] = jnp.dot(st[k, "a"], wd_s[...], preferred_element_type=F32)

        def pack(k):
            y = st[k, "y"]
            ybuf[gs[k] % n_out] = _pack_bf16_pair(y[:, :half], y[:, half:])

        stages = (unpack, up_proj, activate, down_proj, pack)
        for t in range(len(gs) + len(stages) - 1):
            for s in reversed(range(len(stages))):
                if 0 <= t - s < len(gs):
                    stages[s](t - s)

    def pair(jj, carry):
        g = g0 + 2 * jj
        acquire(g)
        acquire(g + 1)
        compute([g, g + 1])
        store(g).start()
        store(g + 1).start()
        return carry

    lax.fori_loop(0, nblk // 2, pair, 0)

    @pl.when(nblk % 2 == 1)
    def _():
        g = g0 + nblk - 1
        acquire(g)
        compute([g])
        store(g).start()

    @pl.when(e == pl.num_programs(0) - 1)
    def _():
        for back in range(n_out, 0, -1):
            @pl.when(total >= back)
            def _(back=back):
                store(total - back).wait()


def _experts(xs, first_blk, n_blk, total_blk, w_gu, w_down):
    n_rows, half = xs.shape
    d = 2 * half
    n_exp, _, ff2 = w_gu.shape
    ff = w_down.shape[1]
    grid_spec = pltpu.PrefetchScalarGridSpec(
        num_scalar_prefetch=3,
        grid=(n_exp,),
        in_specs=[pl.BlockSpec((1, d, ff2), lambda e, *_: (e, 0, 0)),
                  pl.BlockSpec((1, ff, d), lambda e, *_: (e, 0, 0)),
                  pl.BlockSpec(memory_space=pl.ANY)],
        out_specs=pl.BlockSpec(memory_space=pl.ANY),
        scratch_shapes=[pltpu.VMEM((d, ff2), BF16), pltpu.VMEM((ff, d), BF16),
                        pltpu.VMEM((EXPERT_IN_SLOTS, EXPERT_ROWS, half), xs.dtype),
                        pltpu.VMEM((EXPERT_OUT_SLOTS, EXPERT_ROWS, half), xs.dtype),
                        pltpu.SemaphoreType.DMA((EXPERT_IN_SLOTS,)),
                        pltpu.SemaphoreType.DMA((EXPERT_OUT_SLOTS,))],
    )
    return pl.pallas_call(
        _expert_kernel,
        grid_spec=grid_spec,
        out_shape=jax.ShapeDtypeStruct((n_rows, half), xs.dtype),
        compiler_params=pltpu.CompilerParams(
            dimension_semantics=("arbitrary",), vmem_limit_bytes=VMEM_LIMIT),
        name="experts",
    )(first_blk, n_blk, total_blk, w_gu, w_down, xs)


def _combine_kernel(x1_ref, w_ref, yg_ref, wsgu_ref, wsd_ref, lng_ref, lnb_ref, o_ref):
    tc = x1_ref.shape[0]
    x1 = x1_ref[...]
    ff = wsd_ref.shape[0]
    h = jnp.dot(x1.astype(BF16), wsgu_ref[...], preferred_element_type=F32)
    gate, up = h[:, :ff], h[:, ff:]
    act = gate * _sigmoid(gate) * up
    moe = jnp.dot(act.astype(BF16), wsd_ref[...], preferred_element_type=F32)

    half = yg_ref.shape[2]
    wt = jnp.transpose(w_ref[...])
    r_lo = jnp.zeros((tc, half), F32)
    r_hi = jnp.zeros((tc, half), F32)
    for k in range(TOP_K):
        y_lo, y_hi = _unpack_bf16_pair(yg_ref[k])
        w = wt[:, k:k + 1]
        r_lo = r_lo + w * y_lo
        r_hi = r_hi + w * y_hi
    moe = moe + jnp.concatenate([r_lo, r_hi], axis=1)
    o_ref[...] = _layer_norm(DEEPNORM_ALPHA * x1 + moe, lng_ref[...], lnb_ref[...])


def _combine(x1, wgt, yg, chunk, ws_gu, ws_down, ln_g, ln_b, tc):
    t, d = x1.shape
    steps = yg.shape[1] // tc
    tok = lambda i: (chunk * steps + i, 0)
    return pl.pallas_call(
        _combine_kernel,
        grid=(steps,),
        in_specs=[pl.BlockSpec((tc, d), tok),
                  pl.BlockSpec((TOP_K, tc), lambda i: (0, chunk * steps + i)),
                  pl.BlockSpec((TOP_K, tc, yg.shape[2]), lambda i: (0, i, 0)),
                  _resident(ws_gu.shape), _resident(ws_down.shape),
                  _resident(ln_g.shape), _resident(ln_b.shape)],
        out_specs=pl.BlockSpec((tc, d), tok),
        out_shape=jax.ShapeDtypeStruct((t, d), F32),
        input_output_aliases={0: 0},
        compiler_params=pltpu.CompilerParams(
            dimension_semantics=("arbitrary",), vmem_limit_bytes=VMEM_LIMIT),
        name="combine_ln2",
    )(x1, wgt, yg, ws_gu, ws_down, ln_g, ln_b)


def _fused_in_weights(w_in):
    d = w_in.shape[0]
    sizes = (SB_HEADS * HEAD_DIM,) * 3 + (SWA_HEADS * HEAD_DIM, SWA_KV_HEADS * HEAD_DIM,
                                          SWA_KV_HEADS * HEAD_DIM, MEM_HEADS * MEM_HEAD_DIM)
    parts, off = [], 0
    for s in sizes:
        parts.append(w_in[:, off:off + s])
        off += s
    q_sb, k_sb, v_sb, q_sw, k_sw, v_sw, q_m = parts
    gates = w_in[:, off:]
    scale = HEAD_DIM ** -0.5
    q_sw = (q_sw * scale).reshape(d, SWA_KV_HEADS, SWA_GROUP, HEAD_DIM)
    q_sw = jnp.swapaxes(q_sw, 1, 2).reshape(d, SWA_HEADS * HEAD_DIM)
    w_a =jnp.concatenate([q_sw, q_m, k_sw, v_sw], axis=1).astype(BF16)
    w_b = jnp.concatenate([gates, q_sb * scale, k_sb, v_sb], axis=1).astype(BF16)
    assert w_a.shape[1] == PA_COLS and w_b.shape[1] == PB_COLS
    return w_a, w_b


def _grouped_swa_out_weights(w_o_swa):
    d = w_o_swa.shape[1]
    w = w_o_swa.reshape(SWA_KV_HEADS, SWA_GROUP, HEAD_DIM, d)
    return jnp.swapaxes(w, 0, 1).reshape(SWA_HEADS * HEAD_DIM, d).astype(BF16)


def kernel(x, mem, w_in, b_gate, w_mem_kv, sinks, w_o_sb, w_o_swa, w_o_mem, w_out,
           ln1_g, ln1_b, w_router, router_bias, w_e_gu, w_e_down, w_s_gu, w_s_down,
           ln2_g, ln2_b):
    batch, seq, d = x.shape
    mem_len = mem.shape[1]
    t = batch * seq
    x2 = x.reshape(t, d)
    row_tile = min(512, t)

    w_a, w_b = _fused_in_weights(w_in)
    p_a = _matmul_bf16(x2, w_a, row_tile, "in_proj_attn")
    mkv = _matmul_bf16(mem.reshape(batch * mem_len, d), w_mem_kv.astype(BF16), mem_len,
                       "mem_kv_proj")
    p, o_sw, o_m = _proj_attn(x2, w_b, b_gate.reshape(1, -1), p_a, mkv, _swa_tables(sinks),
                              batch, seq, mem_len)
    o_sb = _sb_attention(p, batch, seq, min(256, seq), SB_PAIRS_PER_STEP)
    wr_t = w_router.T
    wr_hi = wr_t.astype(BF16)
    wr_lo = (wr_t - wr_hi.astype(F32)).astype(BF16)
    x1, x1p, idx, rank, wgt, cnt = _merge_route(
        o_sb, o_sw, o_m, p, x2, w_o_sb.astype(BF16), _grouped_swa_out_weights(w_o_swa),
        w_o_mem.astype(BF16), w_out.astype(BF16), ln1_g.reshape(1, d), ln1_b.reshape(1, d),
        wr_hi, wr_lo, router_bias.reshape(-1, 1).astype(F32), row_tile)

    out = _moe_ln(x1, x1p, idx, rank, wgt, cnt, w_e_gu, w_e_down, w_s_gu, w_s_down, ln2_g, ln2_b)
    return out.reshape(batch, seq, d)


def _moe_ln(x1, x1p, idx, rank, wgt, cnt, w_e_gu, w_e_down, w_s_gu, w_s_down, ln2_g, ln2_b):
    t, d = x1.shape
    counts = cnt[:, 0].astype(I32)
    padded = (counts + EXPERT_ROWS - 1) // EXPERT_ROWS * EXPERT_ROWS
    pad_end = jnp.cumsum(padded)
    pad_start = pad_end - padded
    n_blocks = t * TOP_K // EXPERT_ROWS + N_EXPERTS
    dest = _dest(idx, rank, pad_start.reshape(-1, 1), min(2048, t))

    xs = _dispatch(x1p, dest, n_blocks * EXPERT_ROWS)
    ys = _experts(xs, pad_start // EXPERT_ROWS, padded // EXPERT_ROWS,
                  pad_end[-1:] // EXPERT_ROWS, w_e_gu, w_e_down)
    tchunk = t // COMBINE_CHUNKS
    ws_gu, ws_down = w_s_gu.astype(BF16), w_s_down.astype(BF16)
    out = x1
    for c in range(COMBINE_CHUNKS):
        slots = dest[:, c * tchunk:(c + 1) * tchunk].reshape(-1)
        yg = _gather_rows(ys, slots).reshape(TOP_K, tchunk, -1)
        out = _combine(out, wgt, yg, c, ws_gu, ws_down, ln2_g.reshape(1, d), ln2_b.reshape(1, d),
                       min(512, tchunk))
    return out
```

```python
import functools

import jax
import jax.numpy as jnp
from jax import lax
from jax.experimental import pallas as pl
from jax.experimental.pallas import tpu as pltpu
from jax.experimental.pallas import tpu_sc as plsc

F32 = jnp.float32
BF16 = jnp.bfloat16
I32 = jnp.int32
U32 = jnp.uint32

HEAD_DIM = 64
SB_HEADS = 8
SWA_HEADS = 8
SWA_KV_HEADS = 2
SWA_GROUP = SWA_HEADS // SWA_KV_HEADS
SWA_WINDOW = 128
MEM_HEADS = 4
MEM_HEAD_DIM = 128
N_BRANCH = 3
N_EXPERTS = 256
TOP_K = 8
N_GROUPS = 8
GROUP_SIZE = N_EXPERTS // N_GROUPS
TOPK_GROUPS = 4
ROUTED_SCALE = 2.5
LN_EPS = 1e-5
DEPTH = 1
DEEPNORM_ALPHA = (2 * DEPTH) ** 0.25

LANES = 128
SC_CORES = 2
SC_SUBCORES = 16
SC_WORKERS = SC_CORES * SC_SUBCORES
SC_INDEX_WINDOW = 128
SC_GATHER_ROWS = 64
VMEM_LIMIT = 56 * 1024 * 1024

A_QSW = 0
A_QM = 512
A_KSW = 1024
A_VSW = 1152
PA_COLS = 1280
C_QSB = 3072
C_KSB = 3584
C_VSB = 4096
PB_COLS = 4608
PROJ_CHUNK = 256
PROJ_ROWS = 512

SB_SKIP = 110.0
SB_MASKED = -1e30
SB_PAIRS_PER_STEP = 4

EXPERT_ROWS = 512
EXPERT_AHEAD = 4
EXPERT_IN_SLOTS = EXPERT_AHEAD + 2
EXPERT_OUT_SLOTS = 4
EXPERT_PARTS = 4
COMBINE_CHUNKS = 8


def _nt_dot(a, b):
    return lax.dot_general(a, b, (((1,), (1,)), ((), ())), preferred_element_type=F32)


def _sigmoid(x):
    return 1.0 / (1.0 + jnp.exp(-x))


def _layer_norm(h, g, b):
    mu = jnp.mean(h, axis=-1, keepdims=True)
    d = h - mu
    var = jnp.mean(d * d, axis=-1, keepdims=True)
    return d * lax.rsqrt(var + LN_EPS) * g + b


def _pack_bf16_pair(a, b):
    a_bits = lax.bitcast_convert_type(a.astype(BF16).astype(F32), U32)
    b_bits = lax.bitcast_convert_type(b.astype(BF16).astype(F32), U32)
    return (a_bits >> 16) | b_bits


def _unpack_bf16_pair(w):
    a = lax.bitcast_convert_type(w << 16, F32)
    b = lax.bitcast_convert_type(w & jnp.uint32(0xFFFF0000), F32)
    return a, b


def _resident(shape):
    nd = len(shape)
    return pl.BlockSpec(shape, lambda *_: (0,) * nd, pipeline_mode=pl.Buffered(1))


def _proj_attn_kernel(x_ref, w_ref, b_ref, hp_ref, qsw_ref, k_refs, v_refs, qm_ref, mk_ref, mv_ref,
                      p_ref, osw_ref, om_ref, *, gate_cols):
    j = pl.program_id(1)
    xb = x_ref[...].astype(BF16)

    def proj_chunk(c):
        cols = slice(c * PROJ_CHUNK, (c + 1) * PROJ_CHUNK)
        acc = jnp.dot(xb, w_ref[:, cols], preferred_element_type=F32)
        if (c + 1) * PROJ_CHUNK <= gate_cols:
            acc = _sigmoid(acc + b_ref[:, cols])
        p_ref[:, cols] = acc.astype(p_ref.dtype)

    blk = SWA_WINDOW
    nblk = x_ref.shape[0] // blk

    def swa(hb):
        rows = slice(hb * blk, (hb + 1) * blk)
        first = (j == 0) if hb == 0 else None
        _swa_block(hp_ref, qsw_ref, rows, k_refs[hb], k_refs[hb + 1], v_refs[hb], v_refs[hb + 1],
                   first, osw_ref)

    side = [functools.partial(swa, hb) for hb in range(nblk)]
    side.append(functools.partial(_mem_heads, qm_ref, mk_ref, mv_ref, om_ref))
    nchunk = p_ref.shape[1] // PROJ_CHUNK
    per = -(-nchunk // len(side))
    for s, work in enumerate(side):
        for c in range(s * per, min((s + 1) * per, nchunk)):
            proj_chunk(c)
        work()


def _proj_attn(x2, w_b, b_gate, p_a, mkv, swa_tables, batch, seq, mem_len):
    t, d = x2.shape
    tm = PROJ_ROWS
    nt = seq // tm
    blk = SWA_WINDOW
    nblk = tm // blk
    nb = seq // blk
    qw = SWA_HEADS * HEAD_DIM
    mw = MEM_HEADS * MEM_HEAD_DIM
    tile = lambda w, col: pl.BlockSpec((tm, w), lambda b, j: (b * nt + j, col))
    kv = lambda col: [pl.BlockSpec((blk, LANES),
                                   functools.partial(lambda b, j, off, col: (
                                       b * nb + jnp.maximum(nblk * j + off, 0), col), off=off, col=col))
                      for off in range(-1, nblk)]
    return pl.pallas_call(
        functools.partial(_proj_attn_kernel, gate_cols=b_gate.shape[1]),
        grid=(batch, nt),
        in_specs=[tile(d, 0), _resident(w_b.shape), _resident(b_gate.shape),
                  _resident(swa_tables.shape),
                  tile(qw, A_QSW // qw),
                  kv(A_KSW // LANES), kv(A_VSW // LANES),
                  tile(mw, A_QM // mw),
                  pl.BlockSpec((mem_len, mw), lambda b, j: (b, 0)),
                  pl.BlockSpec((mem_len, mw), lambda b, j: (b, 1))],
        out_specs=[tile(w_b.shape[1], 0), tile(qw, 0), tile(mw, 0)],
        out_shape=[jax.ShapeDtypeStruct((t, w_b.shape[1]), BF16),
                   jax.ShapeDtypeStruct((t, qw), BF16),
                   jax.ShapeDtypeStruct((t, mw), BF16)],
        compiler_params=pltpu.CompilerParams(
            dimension_semantics=("parallel", "arbitrary"), vmem_limit_bytes=VMEM_LIMIT),
        name="in_proj_swa_mem",
    )(x2, w_b, b_gate, swa_tables, p_a, [p_a] * (nblk + 1), [p_a] * (nblk + 1), p_a, mkv, mkv)


def _mm_kernel(x_ref, w_ref, o_ref):
    o_ref[...] = jnp.dot(x_ref[...].astype(BF16), w_ref[...],
                         preferred_element_type=F32).astype(o_ref.dtype)


def _matmul_bf16(x2, w, tm, name):
    t, d = x2.shape
    n = w.shape[1]
    return pl.pallas_call(
        _mm_kernel,
        grid=(t // tm,),
        in_specs=[pl.BlockSpec((tm, d), lambda i: (i, 0)), _resident((d, n))],
        out_specs=pl.BlockSpec((tm, n), lambda i: (i, 0)),
        out_shape=jax.ShapeDtypeStruct((t, n), BF16),
        compiler_params=pltpu.CompilerParams(
            dimension_semantics=("parallel",), vmem_limit_bytes=VMEM_LIMIT),
        name=name,
    )(x2, w)


def _sb_kernel(q_ref, k_ref, v_ref, o_ref, *, tq):
    i = pl.program_id(2)
    pairs = q_ref.shape[1] // LANES
    lane = lax.broadcasted_iota(I32, (1, LANES), 1)
    r = lax.broadcasted_iota(I32, (tq, tq), 0)
    c = lax.broadcasted_iota(I32, (tq, tq), 1)
    tri = (r >= c).astype(BF16)
    causal = c < r
    nh = LANES // HEAD_DIM
    hmasks = [(lane >= h * HEAD_DIM) & (lane < (h + 1) * HEAD_DIM) for h in range(nh)]
    qs = []
    for p in range(pairs):
        q = q_ref[:, p * LANES:(p + 1) * LANES]
        qs.append(jnp.concatenate([jnp.where(hm, q, jnp.zeros_like(q)) for hm in hmasks], axis=0))
    causal2 = jnp.concatenate([causal] * nh, axis=0)
    tri2 = jnp.concatenate([tri, tri], axis=0)
    mp = nh * tq

    def blocks(jobs, carry, acc):
        chains = [(jb, p) for jb in range(len(jobs)) for p in range(pairs)]
        z, hl, suffix, ab, av, cin = {}, {}, {}, {}, {}, {}

        def rows(jb):
            return pl.ds(pl.multiple_of(jobs[jb][0] * tq, tq), tq)

        def scores(jb, p):
            zz = _nt_dot(qs[p], k_ref[rows(jb), p * LANES:(p + 1) * LANES])
            if jobs[jb][1]:
                zz = jnp.where(causal2, zz, SB_MASKED)
            z[jb, p] = zz

        def softplus_split(jb, p):
            sp = jnp.maximum(z[jb, p], 0.0) + jnp.log(1.0 + jnp.exp(-jnp.abs(z[jb, p])))
            hi = sp.astype(BF16)
            lo = (sp - hi.astype(F32)).astype(BF16)
            hl[jb, p] = jnp.concatenate([hi, lo], axis=1)

        def cumsum(jb, p):
            suffix[jb, p] = jnp.dot(hl[jb, p], tri2, preferred_element_type=F32)

        def weights(jb, p):
            cin[jb, p] = carry[p] if jb == 0 else cin[jb - 1, p] + step_sum(jb - 1, p)
            a = jnp.exp((z[jb, p] - cin[jb, p]) - suffix[jb, p])
            ab[jb, p] = a.astype(BF16)

        def values(jb, p):
            out = jnp.dot(ab[jb, p], v_ref[rows(jb), p * LANES:(p + 1) * LANES],
                          preferred_element_type=F32)
            live = jobs[jb][2]
            av[jb, p] = out if live is None else jnp.where(live, out, 0.0)

        def step_sum(jb, p):
            total = suffix[jb, p][:, 0:1]
            live = jobs[jb][2]
            return total if live is None else jnp.where(live, total, 0.0)

        stages = (scores, softplus_split, cumsum, weights, values)
        for t in range(len(chains) + len(stages) - 1):
            for s in reversed(range(len(stages))):
                if 0 <= t - s < len(chains):
                    stages[s](*chains[t - s])
        last = len(jobs) - 1
        new_acc = []
        for p in range(pairs):
            total = acc[p]
            for jb in range(len(jobs)):
                total = total + av[jb, p]
            new_acc.append(total)
        return [cin[last, p] + step_sum(last, p) for p in range(pairs)], new_acc

    def block(kb, carry, acc, diag):
        return blocks([(kb, diag, None)], carry, acc)

    carry, acc = blocks([(i, True, None), (jnp.maximum(i - 1, 0), False, i > 0)],
                        [jnp.zeros((mp, 1), F32)] * pairs,
                        [jnp.zeros((mp, LANES), F32)] * pairs)

    def cond(s):
        kb, carry, _ = s
        lowest = carry[0]
        for cp in carry[1:]:
            lowest = jnp.minimum(lowest, cp)
        return (kb >= 0) & (jnp.min(lowest) < SB_SKIP)

    def body(s):
        kb, carry, acc = s
        carry, acc = block(kb, carry, acc, False)
        return kb - 1, carry, acc

    _, _, acc = lax.while_loop(cond, body, (i - 2, carry, acc))
    for p in range(pairs):
        o_ref[:, p * LANES:(p + 1) * LANES] = jnp.where(
            hmasks[0], acc[p][:tq], acc[p][tq:]).astype(o_ref.dtype)


def _sb_attention(p, batch, seq, tq, pairs):
    t = batch * seq
    nq = seq // tq
    w = pairs * LANES
    ngrp = SB_HEADS * HEAD_DIM // w
    qc, kc, vc = C_QSB // w, C_KSB // w, C_VSB // w
    return pl.pallas_call(
        functools.partial(_sb_kernel, tq=tq),
        grid=(batch, ngrp, nq),
        in_specs=[pl.BlockSpec((tq, w), lambda b, h, i: (b * nq + i, qc + h)),
                  pl.BlockSpec((seq, w), lambda b, h, i: (b, kc + h)),
                  pl.BlockSpec((seq, w), lambda b, h, i: (b, vc + h))],
        out_specs=pl.BlockSpec((tq, w), lambda b, h, i: (b * nq + i, h)),
        out_shape=jax.ShapeDtypeStruct((t, SB_HEADS * HEAD_DIM), BF16),
        compiler_params=pltpu.CompilerParams(
            dimension_semantics=("parallel", "parallel", "arbitrary"),
            vmem_limit_bytes=VMEM_LIMIT),
        name="sb_attention",
    )(p, p, p)


def _swa_block(hp_ref, q_ref, rows, kprev, kcur, vprev, vcur, first, o_ref):
    blk = kcur.shape[0]
    sink = hp_ref[:, 2 * blk:]
    lane = lax.broadcasted_iota(I32, (1, LANES), 1)
    kvmasks = [(lane >= kv * HEAD_DIM) & (lane < (kv + 1) * HEAD_DIM) for kv in range(SWA_KV_HEADS)]
    groups = [q_ref[rows, g * LANES:(g + 1) * LANES] for g in range(SWA_GROUP)]
    qs = jnp.concatenate([jnp.where(kvmasks[kv], groups[g], jnp.zeros_like(groups[g]))
                          for kv in range(SWA_KV_HEADS) for g in range(SWA_GROUP)], axis=0)
    keys = jnp.concatenate([kprev[...], kcur[...]], axis=0)
    vals = jnp.concatenate([vprev[...], vcur[...]], axis=0)
    z = _nt_dot(qs, keys) + hp_ref[:, :2 * blk]
    if first is not None:
        col = lax.broadcasted_iota(I32, (1, 2 * blk), 1)
        z = jnp.where(first & (col < blk), jnp.float32(-jnp.inf), z)
    m = jnp.maximum(jnp.max(z, axis=1, keepdims=True), sink)
    p = jnp.exp(z - jnp.concatenate([m, m], axis=1))
    den = jnp.sum(p, axis=1, keepdims=True) + jnp.exp(sink - m)
    o = jnp.dot(p.astype(BF16), vals, preferred_element_type=F32) / den
    for g in range(SWA_GROUP):
        head0 = o[g * blk:(g + 1) * blk]
        head1 = o[(SWA_GROUP + g) * blk:(SWA_GROUP + g + 1) * blk]
        o_ref[rows, g * LANES:(g + 1) * LANES] = jnp.where(kvmasks[0], head0, head1).astype(o_ref.dtype)


def _swa_tables(sinks):
    w = SWA_WINDOW
    slopes = jnp.exp2(-8.0 * jnp.arange(1, SWA_HEADS + 1, dtype=F32) / SWA_HEADS)[:, None, None]
    r = jnp.arange(w)[:, None]
    c = jnp.arange(w)[None, :]
    dist = (r - c).astype(F32)[None]
    neg = jnp.float32(-jnp.inf)
    bias_c = jnp.where((c <= r)[None], -slopes * dist, neg)
    bias_p = jnp.where((c > r)[None], -slopes * (dist + w), neg)
    sink = jnp.broadcast_to(sinks.astype(F32)[:, None, None], (SWA_HEADS, w, w))
    return jnp.concatenate([bias_p, bias_c, sink], axis=2).reshape(SWA_HEADS * w, 3 * w)


def _mem_heads(q_ref, mk_ref, mv_ref, o_ref):
    scale = MEM_HEAD_DIM ** -0.5
    for h in range(MEM_HEADS):
        cols = slice(h * MEM_HEAD_DIM, (h + 1) * MEM_HEAD_DIM)
        z = _nt_dot(q_ref[:, cols], mk_ref[:, cols]) * scale
        m = jnp.max(z, axis=1, keepdims=True)
        p = jnp.exp(z - m)
        den = jnp.sum(p, axis=1, keepdims=True)
        o = jnp.dot(p.astype(BF16), mv_ref[:, cols], preferred_element_type=F32) / den
        o_ref[:, cols] = o.astype(o_ref.dtype)


def _merge_route_kernel(osb_ref, osw_ref, om_ref, g_ref, x_ref, wsb_ref, wsw_ref, wm_ref,
                        wout_ref, lng_ref, lnb_ref, wrh_ref, wrl_ref, rbias_ref,
                        x1_ref, x1p_ref, idx_ref, rank_ref, wgt_ref, cnt_ref, x1_prev, carry_ref):
    i = pl.program_id(0)

    @pl.when(i == 0)
    def _():
        x1_prev[...] = jnp.zeros_like(x1_prev)
        carry_ref[...] = jnp.zeros_like(carry_ref)

    d = x_ref.shape[1]
    st = {}

    def branch(b, o_ref, w_ref):
        def run():
            term = g_ref[:, b * d:(b + 1) * d].astype(F32) * jnp.dot(
                o_ref[...], w_ref[...], preferred_element_type=F32)
            st["merged"] = term if b == 0 else st["merged"] + term
        return run

    def out_proj():
        st["y"] = jnp.dot(st["merged"].astype(BF16), wout_ref[...], preferred_element_type=F32)

    idx, rank, wgt, count = _route(
        x1_prev[...], wrh_ref[...], wrl_ref[...], rbias_ref[...], carry_ref[...],
        side_work=(branch(0, osb_ref, wsb_ref), branch(1, osw_ref, wsw_ref),
                   branch(2, om_ref, wm_ref), out_proj))
    x1 = _layer_norm(DEEPNORM_ALPHA * x_ref[...] + st["y"], lng_ref[...], lnb_ref[...])
    x1_ref[...] = x1
    x1p_ref[...] = _pack_bf16_pair(x1[:, :d // 2], x1[:, d // 2:])
    idx_ref[...] = idx
    rank_ref[...] = rank
    wgt_ref[...] = wgt
    carry_ref[...] = carry_ref[...] + jnp.where(i > 0, count, 0.0)
    cnt_ref[...] = carry_ref[...]
    x1_prev[...] = x1


def _merge_route(o_sb, o_sw, o_m, p, x2, w_sb, w_sw, w_m, w_out, ln_g, ln_b, wr_hi, wr_lo,
                 bias_col, tm):
    t, d = x2.shape
    n = t // tm
    cur = lambda i: (jnp.minimum(i, n - 1), 0)
    row = lambda w: pl.BlockSpec((tm, w), cur)
    slot = pl.BlockSpec((TOP_K, tm), lambda i: (0, jnp.maximum(i - 1, 0)))
    return pl.pallas_call(
        _merge_route_kernel,
        grid=(n + 1,),
        in_specs=[row(o_sb.shape[1]), row(o_sw.shape[1]), row(o_m.shape[1]),
                  pl.BlockSpec((tm, N_BRANCH * d), cur),
                  row(d),
                  _resident(w_sb.shape), _resident(w_sw.shape), _resident(w_m.shape),
                  _resident(w_out.shape), _resident(ln_g.shape), _resident(ln_b.shape),
                  _resident(wr_hi.shape), _resident(wr_lo.shape), _resident(bias_col.shape)],
        out_specs=[row(d), row(d // 2), slot, slot, slot,
                   pl.BlockSpec((N_EXPERTS, 1), lambda i: (0, 0))],
        out_shape=[jax.ShapeDtypeStruct((t, d), F32), jax.ShapeDtypeStruct((t, d // 2), U32),
                   jax.ShapeDtypeStruct((TOP_K, t), I32),
                   jax.ShapeDtypeStruct((TOP_K, t), I32),
                   jax.ShapeDtypeStruct((TOP_K, t), F32),
                   jax.ShapeDtypeStruct((N_EXPERTS, 1), F32)],
        scratch_shapes=[pltpu.VMEM((tm, d), F32), pltpu.VMEM((N_EXPERTS, 1), F32)],
        compiler_params=pltpu.CompilerParams(
            dimension_semantics=("arbitrary",), vmem_limit_bytes=VMEM_LIMIT),
        name="merge_route",
    )(o_sb, o_sw, o_m, p, x2, w_sb, w_sw, w_m, w_out, ln_g, ln_b, wr_hi, wr_lo, bias_col)


def _route(x, wh, wl, bias, carry, side_work=()):
    tr = x.shape[0]
    xh = x.astype(BF16)
    xl = (x - xh.astype(F32)).astype(BF16)
    logits = _nt_dot(wh, xh) + _nt_dot(wh, xl) + _nt_dot(wl, xh)
    scores = _sigmoid(logits)
    biased = scores + bias
    neg = jnp.float32(-jnp.inf)

    sub = lax.broadcasted_iota(I32, (GROUP_SIZE, tr), 0)
    gscore = []
    for g in range(N_GROUPS):
        blk = biased[g * GROUP_SIZE:(g + 1) * GROUP_SIZE, :]
        m1 = jnp.max(blk, axis=0, keepdims=True)
        i1 = jnp.min(jnp.where(blk == m1, sub, GROUP_SIZE), axis=0, keepdims=True)
        m2 = jnp.max(jnp.where(sub == i1, neg, blk), axis=0, keepdims=True)
        gscore.append(m1 + m2)
    gs = jnp.concatenate(gscore, axis=0)

    giota = lax.broadcasted_iota(I32, (N_GROUPS, tr), 0)
    gsel = jnp.zeros((N_GROUPS, tr), F32)
    for _ in range(TOPK_GROUPS):
        m = jnp.max(gs, axis=0, keepdims=True)
        gi = jnp.min(jnp.where(gs == m, giota, N_GROUPS), axis=0, keepdims=True)
        hit = giota == gi
        gsel = jnp.where(hit, 1.0, gsel)
        gs = jnp.where(hit, neg, gs)

    masked = jnp.concatenate(
        [jnp.where(gsel[g:g + 1, :] > 0.0, biased[g * GROUP_SIZE:(g + 1) * GROUP_SIZE, :], neg)
         for g in range(N_GROUPS)], axis=0)

    eiota = lax.broadcasted_iota(I32, (N_EXPERTS, tr), 0)
    sel = jnp.zeros((N_EXPERTS, tr), F32)
    idx_rows, w_rows = [], []
    side_work = list(side_work)
    for k in range(TOP_K):
        if side_work and k % 2 == 0:
            side_work.pop(0)()
        m = jnp.max(masked, axis=0, keepdims=True)
        ei = jnp.min(jnp.where(masked == m, eiota, N_EXPERTS), axis=0, keepdims=True)
        hit = eiota == ei
        idx_rows.append(ei)
        w_rows.append(jnp.sum(jnp.where(hit, scores, 0.0), axis=0, keepdims=True))
        sel = jnp.where(hit, 1.0, sel)
        masked = jnp.where(hit, neg, masked)

    wsum = w_rows[0]
    for wk in w_rows[1:]:
        wsum = wsum + wk
    wgt = jnp.concatenate(w_rows, axis=0) / wsum * ROUTED_SCALE
    idx = jnp.concatenate(idx_rows, axis=0)

    a = lax.broadcasted_iota(I32, (tr, tr), 0)
    b = lax.broadcasted_iota(I32, (tr, tr), 1)
    before = (a < b).astype(BF16)
    rank = jnp.dot(sel.astype(BF16), before, preferred_element_type=F32) + carry
    rank_rows = [jnp.sum(jnp.where(eiota == ei, rank, 0.0), axis=0, keepdims=True)
                 for ei in idx_rows]
    rank = jnp.concatenate(rank_rows, axis=0).astype(I32)
    return idx, rank, wgt, jnp.sum(sel, axis=1, keepdims=True)


def _dest_kernel(idx_ref, rank_ref, start_ref, dest_ref):
    tr = idx_ref.shape[1]
    eiota = lax.broadcasted_iota(I32, (N_EXPERTS, tr), 0)
    rows = []
    for k in range(TOP_K):
        hit = eiota == idx_ref[k:k + 1, :]
        rows.append(jnp.sum(jnp.where(hit, start_ref[...], 0), axis=0, keepdims=True))
    dest_ref[...] = jnp.concatenate(rows, axis=0) + rank_ref[...]


def _dest(idx, rank, start_col, tr):
    t = idx.shape[1]
    slot = pl.BlockSpec((TOP_K, tr), lambda i: (0, i))
    return pl.pallas_call(
        _dest_kernel,
        grid=(t // tr,),
        in_specs=[slot, slot, _resident(start_col.shape)],
        out_specs=slot,
        out_shape=jax.ShapeDtypeStruct((TOP_K, t), I32),
        compiler_params=pltpu.CompilerParams(dimension_semantics=("parallel",)),
        name="slot_dest",
    )(idx, rank, start_col)


def _sc_worker_id():
    return lax.axis_index("s") * SC_CORES + lax.axis_index("c")


def _dispatch(x1p, dest, n_rows):
    t, w = x1p.shape
    per = t // SC_WORKERS
    win = min(SC_INDEX_WINDOW, per)
    mesh = plsc.VectorSubcoreMesh(core_axis_name="c", subcore_axis_name="s")

    @functools.partial(
        pl.kernel, mesh=mesh,
        out_type=jax.ShapeDtypeStruct((n_rows, w), x1p.dtype),
        scratch_types=[pltpu.VMEM((TOP_K, win), I32),
                       pltpu.VMEM((win, w), x1p.dtype),
                       pltpu.SemaphoreType.DMA],
        name="sc_dispatch",
    )
    def scatter_rows(x_hbm, dest_hbm, xs_hbm, idx_v, rows_v, sem):
        base = _sc_worker_id() * per

        @pl.loop(0, per // win)
        def _(j):
            t0 = pl.multiple_of(base + j * win, win)
            pltpu.sync_copy(dest_hbm.at[:, pl.ds(t0, win)], idx_v)
            pltpu.sync_copy(x_hbm.at[pl.ds(t0, win)], rows_v)
            copies = [pltpu.async_copy(rows_v, xs_hbm.at[idx_v.at[k]], sem) for k in range(TOP_K)]
            for c in copies:
                c.wait()

    return scatter_rows(x1p, dest)


def _gather_rows(table, idx):
    n = idx.shape[0]
    w = table.shape[1]
    per = n // SC_WORKERS
    chunk = min(SC_GATHER_ROWS, per // 2)
    assert n % SC_WORKERS == 0 and per % (2 * chunk) == 0, (n, chunk)
    mesh = plsc.VectorSubcoreMesh(core_axis_name="c", subcore_axis_name="s")

    @functools.partial(
        pl.kernel, mesh=mesh,
        out_type=jax.ShapeDtypeStruct((n, w), table.dtype),
        scratch_types=[pltpu.VMEM((per,), I32),
                       pltpu.VMEM((2, chunk, w), table.dtype),
                       pltpu.SemaphoreType.DMA((2,)),
                       pltpu.SemaphoreType.DMA((2,))],
        name="sc_gather",
    )
    def gather_rows(table_hbm, idx_hbm, out_hbm, idx_v, rows_v, gather_sem, put_sem):
        base = _sc_worker_id() * per
        nchunks = per // chunk
        pltpu.sync_copy(idx_hbm.at[pl.ds(base, per)], idx_v)

        def gather(j, b):
            off = pl.multiple_of(j * chunk, chunk)
            return pltpu.make_async_copy(table_hbm.at[idx_v.at[pl.ds(off, chunk)]],
                                         rows_v.at[b], gather_sem.at[b])

        def put(j, b):
            off = pl.multiple_of(j * chunk, chunk)
            return pltpu.make_async_copy(rows_v.at[b], out_hbm.at[pl.ds(base + off, chunk)],
                                         put_sem.at[b])

        gather(0, 0).start()

        @pl.loop(0, nchunks, step=2)
        def _(j):
            for b in (0, 1):
                jj = j + b

                @pl.when(jj + 1 < nchunks)
                def _():
                    @pl.when(jj >= 1)
                    def _():
                        put(jj - 1, 1 - b).wait()
                    gather(jj + 1, 1 - b).start()

                gather(jj, b).wait()
                put(jj, b).start()

        put(nchunks - 2, 0).wait()
        put(nchunks - 1, 1).wait()

    return gather_rows(table, idx)


def _expert_kernel(first_ref, nblk_ref, total_ref, parts_ref, wgu_ref, wd_ref, xs_hbm, ys_hbm,
                   wgu_s, wd_s, xbuf, ybuf, in_sem, out_sem):
    e = pl.program_id(0)
    total = total_ref[0]
    n_in, rows, half = xbuf.shape
    n_out = ybuf.shape[0]
    ahead = EXPERT_AHEAD

    part = rows // EXPERT_PARTS

    def part_rows(g, q):
        return pl.ds(pl.multiple_of(g * rows + q * part, part), part)

    def load_part(g, q):
        slot = g % n_in
        return pltpu.make_async_copy(xs_hbm.at[part_rows(g, q), :],
                                     xbuf.at[slot, pl.ds(q * part, part), :], in_sem.at[slot])

    def store_part(g, q):
        slot = g % n_out
        return pltpu.make_async_copy(ybuf.at[slot, pl.ds(q * part, part), :],
                                     ys_hbm.at[part_rows(g, q), :], out_sem.at[slot])

    class BlockCopy:
        def __init__(self, g, make):
            self.g, self.make = g, make

        def _each(self, op):
            for q in range(EXPERT_PARTS):
                @pl.when(q < parts_ref[self.g])
                def _(q=q):
                    op(self.make(self.g, q))

        def start(self):
            self._each(lambda c: c.start())

        def wait(self):
            self._each(lambda c: c.wait())

    def load(g):
        return BlockCopy(g, load_part)

    def store(g):
        return BlockCopy(g, store_part)

    @pl.when(e == 0)
    def _():
        xbuf[...] = jnp.zeros_like(xbuf)
        for g in range(ahead):
            @pl.when(g < total)
            def _(g=g):
                load(g).start()

    wgu_s[...] = wgu_ref[0].astype(BF16)
    wd_s[...] = wd_ref[0].astype(BF16)
    g0 = first_ref[e]
    nblk = nblk_ref[e]
    ff = wd_s.shape[0]

    def acquire(g):
        load(g).wait()

        @pl.when(g + ahead < total)
        def _():
            load(g + ahead).start()

        @pl.when(g >= n_out)
        def _():
            store(g - n_out).wait()

    def compute(gs):
        st = {}

        def unpack(k):
            x_lo, x_hi = _unpack_bf16_pair(xbuf[gs[k] % n_in])
            st[k, "x"] = (x_lo.astype(BF16), x_hi.astype(BF16))

        def up_proj(k):
            x_lo, x_hi = st[k, "x"]
            st[k, "h"] = (jnp.dot(x_lo, wgu_s[:half, :], preferred_element_type=F32)
                          + jnp.dot(x_hi, wgu_s[half:, :], preferred_element_type=F32))

        def activate(k):
            gate, up = st[k, "h"][:, :ff], st[k, "h"][:, ff:]
            st[k, "a"] = (gate * _sigmoid(gate) * up).astype(BF16)

        def down_proj(k):
            st[k, "y"] = jnp.dot(st[k, "a"], wd_s[...], preferred_element_type=F32)

        def pack(k):
            y = st[k, "y"]
            ybuf[gs[k] % n_out] = _pack_bf16_pair(y[:, :half], y[:, half:])

        stages = (unpack, up_proj, activate, down_proj, pack)
        for t in range(len(gs) + len(stages) - 1):
            for s in reversed(range(len(stages))):
                if 0 <= t - s < len(gs):
                    stages[s](t - s)

    def pair(jj, carry):
        g = g0 + 2 * jj
        acquire(g)
        acquire(g + 1)
        compute([g, g + 1])
        store(g).start()
        store(g + 1).start()
        return carry

    lax.fori_loop(0, nblk // 2, pair, 0)

    @pl.when(nblk % 2 == 1)
    def _():
        g = g0 + nblk - 1
        acquire(g)
        compute([g])
        store(g).start()

    @pl.when(e == pl.num_programs(0) - 1)
    def _():
        for back in range(n_out, 0, -1):
            @pl.when(total >= back)
            def _(back=back):
                store(total - back).wait()


def _experts(xs, first_blk, n_blk, total_blk, blk_parts, w_gu, w_down):
    n_rows, half = xs.shape
    d = 2 * half
    n_exp, _, ff2 = w_gu.shape
    ff = w_down.shape[1]
    grid_spec = pltpu.PrefetchScalarGridSpec(
        num_scalar_prefetch=4,
        grid=(n_exp,),
        in_specs=[pl.BlockSpec((1, d, ff2), lambda e, *_: (e, 0, 0)),
                  pl.BlockSpec((1, ff, d), lambda e, *_: (e, 0, 0)),
                  pl.BlockSpec(memory_space=pl.ANY)],
        out_specs=pl.BlockSpec(memory_space=pl.ANY),
        scratch_shapes=[pltpu.VMEM((d, ff2), BF16), pltpu.VMEM((ff, d), BF16),
                        pltpu.VMEM((EXPERT_IN_SLOTS, EXPERT_ROWS, half), xs.dtype),
                        pltpu.VMEM((EXPERT_OUT_SLOTS, EXPERT_ROWS, half), xs.dtype),
                        pltpu.SemaphoreType.DMA((EXPERT_IN_SLOTS,)),
                        pltpu.SemaphoreType.DMA((EXPERT_OUT_SLOTS,))],
    )
    return pl.pallas_call(
        _expert_kernel,
        grid_spec=grid_spec,
        out_shape=jax.ShapeDtypeStruct((n_rows, half), xs.dtype),
        compiler_params=pltpu.CompilerParams(
            dimension_semantics=("arbitrary",), vmem_limit_bytes=VMEM_LIMIT),
        name="experts",
    )(first_blk, n_blk, total_blk, blk_parts, w_gu, w_down, xs)


def _combine_kernel(x1_ref, w_ref, yg_ref, wsgu_ref, wsd_ref, lng_ref, lnb_ref, o_ref):
    tc = x1_ref.shape[0]
    x1 = x1_ref[...]
    ff = wsd_ref.shape[0]
    h = jnp.dot(x1.astype(BF16), wsgu_ref[...], preferred_element_type=F32)
    gate, up = h[:, :ff], h[:, ff:]
    act = gate * _sigmoid(gate) * up
    moe = jnp.dot(act.astype(BF16), wsd_ref[...], preferred_element_type=F32)

    half = yg_ref.shape[2]
    wt = jnp.transpose(w_ref[...])
    r_lo = jnp.zeros((tc, half), F32)
    r_hi = jnp.zeros((tc, half), F32)
    for k in range(TOP_K):
        y_lo, y_hi = _unpack_bf16_pair(yg_ref[k])
        w = wt[:, k:k + 1]
        r_lo = r_lo + w * y_lo
        r_hi = r_hi + w * y_hi
    moe = moe + jnp.concatenate([r_lo, r_hi], axis=1)
    o_ref[...] = _layer_norm(DEEPNORM_ALPHA * x1 + moe, lng_ref[...], lnb_ref[...])


def _combine(x1, wgt, yg, chunk, ws_gu, ws_down, ln_g, ln_b, tc):
    t, d = x1.shape
    steps = yg.shape[1] // tc
    tok = lambda i: (chunk * steps + i, 0)
    return pl.pallas_call(
        _combine_kernel,
        grid=(steps,),
        in_specs=[pl.BlockSpec((tc, d), tok),
                  pl.BlockSpec((TOP_K, tc), lambda i: (0, chunk * steps + i)),
                  pl.BlockSpec((TOP_K, tc, yg.shape[2]), lambda i: (0, i, 0)),
                  _resident(ws_gu.shape), _resident(ws_down.shape),
                  _resident(ln_g.shape), _resident(ln_b.shape)],
        out_specs=pl.BlockSpec((tc, d), tok),
        out_shape=jax.ShapeDtypeStruct((t, d), F32),
        input_output_aliases={0: 0},
        compiler_params=pltpu.CompilerParams(
            dimension_semantics=("arbitrary",), vmem_limit_bytes=VMEM_LIMIT),
        name="combine_ln2",
    )(x1, wgt, yg, ws_gu, ws_down, ln_g, ln_b)


def _fused_in_weights(w_in):
    d = w_in.shape[0]
    sizes = (SB_HEADS * HEAD_DIM,) * 3 + (SWA_HEADS * HEAD_DIM, SWA_KV_HEADS * HEAD_DIM,
                                          SWA_KV_HEADS * HEAD_DIM, MEM_HEADS * MEM_HEAD_DIM)
    parts, off = [], 0
    for s in sizes:
        parts.append(w_in[:, off:off + s])
        off += s
    q_sb, k_sb, v_sb, q_sw, k_sw, v_sw, q_m = parts
    gates = w_in[:, off:]
    scale = HEAD_DIM ** -0.5
    q_sw = (q_sw * scale).reshape(d, SWA_KV_HEADS, SWA_GROUP, HEAD_DIM)
    q_sw = jnp.swapaxes(q_sw, 1, 2).reshape(d, SWA_HEADS * HEAD_DIM)
    w_a =jnp.concatenate([q_sw, q_m, k_sw, v_sw], axis=1).astype(BF16)
    w_b = jnp.concatenate([gates, q_sb * scale, k_sb, v_sb], axis=1).astype(BF16)
    assert w_a.shape[1] == PA_COLS and w_b.shape[1] == PB_COLS
    return w_a, w_b


def _grouped_swa_out_weights(w_o_swa):
    d = w_o_swa.shape[1]
    w = w_o_swa.reshape(SWA_KV_HEADS, SWA_GROUP, HEAD_DIM, d)
    return jnp.swapaxes(w, 0, 1).reshape(SWA_HEADS * HEAD_DIM, d).astype(BF16)


def kernel(x, mem, w_in, b_gate, w_mem_kv, sinks, w_o_sb, w_o_swa, w_o_mem, w_out,
           ln1_g, ln1_b, w_router, router_bias, w_e_gu, w_e_down, w_s_gu, w_s_down,
           ln2_g, ln2_b):
    batch, seq, d = x.shape
    mem_len = mem.shape[1]
    t = batch * seq
    x2 = x.reshape(t, d)
    row_tile = min(512, t)

    w_a, w_b = _fused_in_weights(w_in)
    p_a = _matmul_bf16(x2, w_a, row_tile, "in_proj_attn")
    mkv = _matmul_bf16(mem.reshape(batch * mem_len, d), w_mem_kv.astype(BF16), mem_len,
                       "mem_kv_proj")
    p, o_sw, o_m = _proj_attn(x2, w_b, b_gate.reshape(1, -1), p_a, mkv, _swa_tables(sinks),
                              batch, seq, mem_len)
    o_sb = _sb_attention(p, batch, seq, min(256, seq), SB_PAIRS_PER_STEP)
    wr_t = w_router.T
    wr_hi = wr_t.astype(BF16)
    wr_lo = (wr_t - wr_hi.astype(F32)).astype(BF16)
    x1, x1p, idx, rank, wgt, cnt = _merge_route(
        o_sb, o_sw, o_m, p, x2, w_o_sb.astype(BF16), _grouped_swa_out_weights(w_o_swa),
        w_o_mem.astype(BF16), w_out.astype(BF16), ln1_g.reshape(1, d), ln1_b.reshape(1, d),
        wr_hi, wr_lo, router_bias.reshape(-1, 1).astype(F32), row_tile)

    out = _moe_ln(x1, x1p, idx, rank, wgt, cnt, w_e_gu, w_e_down, w_s_gu, w_s_down, ln2_g, ln2_b)
    return out.reshape(batch, seq, d)


def _moe_ln(x1, x1p, idx, rank, wgt, cnt, w_e_gu, w_e_down, w_s_gu, w_s_down, ln2_g, ln2_b):
    t, d = x1.shape
    counts = cnt[:, 0].astype(I32)
    padded = (counts + EXPERT_ROWS - 1) // EXPERT_ROWS * EXPERT_ROWS
    pad_end = jnp.cumsum(padded)
    pad_start = pad_end - padded
    n_blocks = t * TOP_K // EXPERT_ROWS + N_EXPERTS
    dest = _dest(idx, rank, pad_start.reshape(-1, 1), min(2048, t))

    xs = _dispatch(x1p, dest, n_blocks * EXPERT_ROWS)
    first_blk, n_blk = pad_start // EXPERT_ROWS, padded // EXPERT_ROWS
    blk = jnp.arange(n_blocks, dtype=I32)
    owner = jnp.minimum((pad_end[None, :] // EXPERT_ROWS <= blk[:, None]).sum(-1), N_EXPERTS - 1)
    tail_rows = counts[owner] - (n_blk[owner] - 1) * EXPERT_ROWS
    part_rows = EXPERT_ROWS // EXPERT_PARTS
    blk_parts = jnp.where(blk == first_blk[owner] + n_blk[owner] - 1,
                          (tail_rows + part_rows - 1) // part_rows, EXPERT_PARTS).astype(I32)
    ys = _experts(xs, first_blk, n_blk, pad_end[-1:] // EXPERT_ROWS, blk_parts, w_e_gu, w_e_down)
    tchunk = t // COMBINE_CHUNKS
    ws_gu, ws_down = w_s_gu.astype(BF16), w_s_down.astype(BF16)
    out = x1
    for c in range(COMBINE_CHUNKS):
        slots = dest[:, c * tchunk:(c + 1) * tchunk].reshape(-1)
        yg = _gather_rows(ys, slots).reshape(TOP_K, tchunk, -1)
        out = _combine(out, wgt, yg, c, ws_gu, ws_down, ln2_g.reshape(1, d), ln2_b.reshape(1, d),
                       min(512, tchunk))
    return out
```

```python
import functools

import jax
import jax.numpy as jnp
from jax import lax
from jax.experimental import pallas as pl
from jax.experimental.pallas import tpu as pltpu
from jax.experimental.pallas import tpu_sc as plsc

F32 = jnp.float32
BF16 = jnp.bfloat16
I32 = jnp.int32
U32 = jnp.uint32

HEAD_DIM = 64
SB_HEADS = 8
SWA_HEADS = 8
SWA_KV_HEADS = 2
SWA_GROUP = SWA_HEADS // SWA_KV_HEADS
SWA_WINDOW = 128
MEM_HEADS = 4
MEM_HEAD_DIM = 128
N_BRANCH = 3
N_EXPERTS = 256
TOP_K = 8
N_GROUPS = 8
GROUP_SIZE = N_EXPERTS // N_GROUPS
TOPK_GROUPS = 4
ROUTED_SCALE = 2.5
LN_EPS = 1e-5
DEPTH = 1
DEEPNORM_ALPHA = (2 * DEPTH) ** 0.25

LANES = 128
SC_CORES = 2
SC_SUBCORES = 16
SC_WORKERS = SC_CORES * SC_SUBCORES
SC_INDEX_WINDOW = 128
SC_GATHER_ROWS = 64
VMEM_LIMIT = 56 * 1024 * 1024

A_QSW = 0
A_QM = 512
A_KSW = 1024
A_VSW = 1152
PA_COLS = 1280
C_QSB = 3072
C_KSB = 3584
C_VSB = 4096
PB_COLS = 4608
PROJ_CHUNK = 256
PROJ_ROWS = 512

SB_SKIP = 110.0
SB_MASKED = -1e30
SB_PAIRS_PER_STEP = 4

EXPERT_ROWS = 512
EXPERT_AHEAD = 6
EXPERT_IN_SLOTS = EXPERT_AHEAD + 2
EXPERT_OUT_SLOTS = 4
COMBINE_CHUNKS = 8


def _nt_dot(a, b):
    return lax.dot_general(a, b, (((1,), (1,)), ((), ())), preferred_element_type=F32)


def _sigmoid(x):
    return 1.0 / (1.0 + jnp.exp(-x))


def _layer_norm(h, g, b):
    mu = jnp.mean(h, axis=-1, keepdims=True)
    d = h - mu
    var = jnp.mean(d * d, axis=-1, keepdims=True)
    return d * lax.rsqrt(var + LN_EPS) * g + b


def _pack_bf16_pair(a, b):
    a_bits = lax.bitcast_convert_type(a.astype(BF16).astype(F32), U32)
    b_bits = lax.bitcast_convert_type(b.astype(BF16).astype(F32), U32)
    return (a_bits >> 16) | b_bits


def _unpack_bf16_pair(w):
    a = lax.bitcast_convert_type(w << 16, F32)
    b = lax.bitcast_convert_type(w & jnp.uint32(0xFFFF0000), F32)
    return a, b


def _resident(shape):
    nd = len(shape)
    return pl.BlockSpec(shape, lambda *_: (0,) * nd, pipeline_mode=pl.Buffered(1))


def _proj_attn_kernel(x_ref, w_ref, b_ref, hp_ref, qsw_ref, k_refs, v_refs, qm_ref, mk_ref, mv_ref,
                      p_ref, osw_ref, om_ref, *, gate_cols):
    j = pl.program_id(1)
    xb = x_ref[...].astype(BF16)

    def proj_chunk(c):
        cols = slice(c * PROJ_CHUNK, (c + 1) * PROJ_CHUNK)
        acc = jnp.dot(xb, w_ref[:, cols], preferred_element_type=F32)
        if (c + 1) * PROJ_CHUNK <= gate_cols:
            acc = _sigmoid(acc + b_ref[:, cols])
        p_ref[:, cols] = acc.astype(p_ref.dtype)

    blk = SWA_WINDOW
    nblk = x_ref.shape[0] // blk

    def swa(hb):
        rows = slice(hb * blk, (hb + 1) * blk)
        first = (j == 0) if hb == 0 else None
        _swa_block(hp_ref, qsw_ref, rows, k_refs[hb], k_refs[hb + 1], v_refs[hb], v_refs[hb + 1],
                   first, osw_ref)

    side = [functools.partial(swa, hb) for hb in range(nblk)]
    side.append(functools.partial(_mem_heads, qm_ref, mk_ref, mv_ref, om_ref))
    nchunk = p_ref.shape[1] // PROJ_CHUNK
    per = -(-nchunk // len(side))
    for s, work in enumerate(side):
        for c in range(s * per, min((s + 1) * per, nchunk)):
            proj_chunk(c)
        work()


def _proj_attn(x2, w_b, b_gate, p_a, mkv, swa_tables, batch, seq, mem_len):
    t, d = x2.shape
    tm = PROJ_ROWS
    nt = seq // tm
    blk = SWA_WINDOW
    nblk = tm // blk
    nb = seq // blk
    qw = SWA_HEADS * HEAD_DIM
    mw = MEM_HEADS * MEM_HEAD_DIM
    tile = lambda w, col: pl.BlockSpec((tm, w), lambda b, j: (b * nt + j, col))
    kv = lambda col: [pl.BlockSpec((blk, LANES),
                                   functools.partial(lambda b, j, off, col: (
                                       b * nb + jnp.maximum(nblk * j + off, 0), col), off=off, col=col))
                      for off in range(-1, nblk)]
    return pl.pallas_call(
        functools.partial(_proj_attn_kernel, gate_cols=b_gate.shape[1]),
        grid=(batch, nt),
        in_specs=[tile(d, 0), _resident(w_b.shape), _resident(b_gate.shape),
                  _resident(swa_tables.shape),
                  tile(qw, A_QSW // qw),
                  kv(A_KSW // LANES), kv(A_VSW // LANES),
                  tile(mw, A_QM // mw),
                  pl.BlockSpec((mem_len, mw), lambda b, j: (b, 0)),
                  pl.BlockSpec((mem_len, mw), lambda b, j: (b, 1))],
        out_specs=[tile(w_b.shape[1], 0), tile(qw, 0), tile(mw, 0)],
        out_shape=[jax.ShapeDtypeStruct((t, w_b.shape[1]), BF16),
                   jax.ShapeDtypeStruct((t, qw), BF16),
                   jax.ShapeDtypeStruct((t, mw), BF16)],
        compiler_params=pltpu.CompilerParams(
            dimension_semantics=("parallel", "arbitrary"), vmem_limit_bytes=VMEM_LIMIT),
        name="in_proj_swa_mem",
    )(x2, w_b, b_gate, swa_tables, p_a, [p_a] * (nblk + 1), [p_a] * (nblk + 1), p_a, mkv, mkv)


def _mm_kernel(x_ref, w_ref, o_ref):
    o_ref[...] = jnp.dot(x_ref[...].astype(BF16), w_ref[...],
                         preferred_element_type=F32).astype(o_ref.dtype)


def _matmul_bf16(x2, w, tm, name):
    t, d = x2.shape
    n = w.shape[1]
    return pl.pallas_call(
        _mm_kernel,
        grid=(t // tm,),
        in_specs=[pl.BlockSpec((tm, d), lambda i: (i, 0)), _resident((d, n))],
        out_specs=pl.BlockSpec((tm, n), lambda i: (i, 0)),
        out_shape=jax.ShapeDtypeStruct((t, n), BF16),
        compiler_params=pltpu.CompilerParams(
            dimension_semantics=("parallel",), vmem_limit_bytes=VMEM_LIMIT),
        name=name,
    )(x2, w)


def _sb_kernel(q_ref, k_ref, v_ref, o_ref, *, tq):
    i = pl.program_id(2)
    pairs = q_ref.shape[1] // LANES
    lane = lax.broadcasted_iota(I32, (1, LANES), 1)
    r = lax.broadcasted_iota(I32, (tq, tq), 0)
    c = lax.broadcasted_iota(I32, (tq, tq), 1)
    tri = (r >= c).astype(BF16)
    causal = c < r
    nh = LANES // HEAD_DIM
    hmasks = [(lane >= h * HEAD_DIM) & (lane < (h + 1) * HEAD_DIM) for h in range(nh)]
    qs = []
    for p in range(pairs):
        q = q_ref[:, p * LANES:(p + 1) * LANES]
        qs.append(jnp.concatenate([jnp.where(hm, q, jnp.zeros_like(q)) for hm in hmasks], axis=0))
    causal2 = jnp.concatenate([causal] * nh, axis=0)
    tri2 = jnp.concatenate([tri, tri], axis=0)
    mp = nh * tq

    def blocks(jobs, carry, acc):
        chains = [(jb, p) for jb in range(len(jobs)) for p in range(pairs)]
        z, hl, suffix, ab, av, cin = {}, {}, {}, {}, {}, {}

        def rows(jb):
            return pl.ds(pl.multiple_of(jobs[jb][0] * tq, tq), tq)

        def scores(jb, p):
            zz = _nt_dot(qs[p], k_ref[rows(jb), p * LANES:(p + 1) * LANES])
            if jobs[jb][1]:
                zz = jnp.where(causal2, zz, SB_MASKED)
            z[jb, p] = zz

        def softplus_split(jb, p):
            sp = jnp.maximum(z[jb, p], 0.0) + jnp.log(1.0 + jnp.exp(-jnp.abs(z[jb, p])))
            hi = sp.astype(BF16)
            lo = (sp - hi.astype(F32)).astype(BF16)
            hl[jb, p] = jnp.concatenate([hi, lo], axis=1)

        def cumsum(jb, p):
            suffix[jb, p] = jnp.dot(hl[jb, p], tri2, preferred_element_type=F32)

        def weights(jb, p):
            cin[jb, p] = carry[p] if jb == 0 else cin[jb - 1, p] + step_sum(jb - 1, p)
            a = jnp.exp((z[jb, p] - cin[jb, p]) - suffix[jb, p])
            ab[jb, p] = a.astype(BF16)

        def values(jb, p):
            out = jnp.dot(ab[jb, p], v_ref[rows(jb), p * LANES:(p + 1) * LANES],
                          preferred_element_type=F32)
            live = jobs[jb][2]
            av[jb, p] = out if live is None else jnp.where(live, out, 0.0)

        def step_sum(jb, p):
            total = suffix[jb, p][:, 0:1]
            live = jobs[jb][2]
            return total if live is None else jnp.where(live, total, 0.0)

        stages = (scores, softplus_split, cumsum, weights, values)
        for t in range(len(chains) + len(stages) - 1):
            for s in reversed(range(len(stages))):
                if 0 <= t - s < len(chains):
                    stages[s](*chains[t - s])
        last = len(jobs) - 1
        new_acc = []
        for p in range(pairs):
            total = acc[p]
            for jb in range(len(jobs)):
                total = total + av[jb, p]
            new_acc.append(total)
        return [cin[last, p] + step_sum(last, p) for p in range(pairs)], new_acc

    def block(kb, carry, acc, diag):
        return blocks([(kb, diag, None)], carry, acc)

    carry, acc = blocks([(i, True, None), (jnp.maximum(i - 1, 0), False, i > 0)],
                        [jnp.zeros((mp, 1), F32)] * pairs,
                        [jnp.zeros((mp, LANES), F32)] * pairs)

    def cond(s):
        kb, carry, _ = s
        lowest = carry[0]
        for cp in carry[1:]:
            lowest = jnp.minimum(lowest, cp)
        return (kb >= 0) & (jnp.min(lowest) < SB_SKIP)

    def body(s):
        kb, carry, acc = s
        carry, acc = block(kb, carry, acc, False)
        return kb - 1, carry, acc

    _, _, acc = lax.while_loop(cond, body, (i - 2, carry, acc))
    for p in range(pairs):
        o_ref[:, p * LANES:(p + 1) * LANES] = jnp.where(
            hmasks[0], acc[p][:tq], acc[p][tq:]).astype(o_ref.dtype)


def _sb_attention(p, batch, seq, tq, pairs):
    t = batch * seq
    nq = seq // tq
    w = pairs * LANES
    ngrp = SB_HEADS * HEAD_DIM // w
    qc, kc, vc = C_QSB // w, C_KSB // w, C_VSB // w
    return pl.pallas_call(
        functools.partial(_sb_kernel, tq=tq),
        grid=(batch, ngrp, nq),
        in_specs=[pl.BlockSpec((tq, w), lambda b, h, i: (b * nq + i, qc + h)),
                  pl.BlockSpec((seq, w), lambda b, h, i: (b, kc + h)),
                  pl.BlockSpec((seq, w), lambda b, h, i: (b, vc + h))],
        out_specs=pl.BlockSpec((tq, w), lambda b, h, i: (b * nq + i, h)),
        out_shape=jax.ShapeDtypeStruct((t, SB_HEADS * HEAD_DIM), BF16),
        compiler_params=pltpu.CompilerParams(
            dimension_semantics=("parallel", "parallel", "arbitrary"),
            vmem_limit_bytes=VMEM_LIMIT),
        name="sb_attention",
    )(p, p, p)


def _swa_block(hp_ref, q_ref, rows, kprev, kcur, vprev, vcur, first, o_ref):
    blk = kcur.shape[0]
    sink = hp_ref[:, 2 * blk:]
    lane = lax.broadcasted_iota(I32, (1, LANES), 1)
    kvmasks = [(lane >= kv * HEAD_DIM) & (lane < (kv + 1) * HEAD_DIM) for kv in range(SWA_KV_HEADS)]
    groups = [q_ref[rows, g * LANES:(g + 1) * LANES] for g in range(SWA_GROUP)]
    qs = jnp.concatenate([jnp.where(kvmasks[kv], groups[g], jnp.zeros_like(groups[g]))
                          for kv in range(SWA_KV_HEADS) for g in range(SWA_GROUP)], axis=0)
    keys = jnp.concatenate([kprev[...], kcur[...]], axis=0)
    vals = jnp.concatenate([vprev[...], vcur[...]], axis=0)
    z = _nt_dot(qs, keys) + hp_ref[:, :2 * blk]
    if first is not None:
        col = lax.broadcasted_iota(I32, (1, 2 * blk), 1)
        z = jnp.where(first & (col < blk), jnp.float32(-jnp.inf), z)
    m = jnp.maximum(jnp.max(z, axis=1, keepdims=True), sink)
    p = jnp.exp(z - jnp.concatenate([m, m], axis=1))
    den = jnp.sum(p, axis=1, keepdims=True) + jnp.exp(sink - m)
    o = jnp.dot(p.astype(BF16), vals, preferred_element_type=F32) / den
    for g in range(SWA_GROUP):
        head0 = o[g * blk:(g + 1) * blk]
        head1 = o[(SWA_GROUP + g) * blk:(SWA_GROUP + g + 1) * blk]
        o_ref[rows, g * LANES:(g + 1) * LANES] = jnp.where(kvmasks[0], head0, head1).astype(o_ref.dtype)


def _swa_tables(sinks):
    w = SWA_WINDOW
    slopes = jnp.exp2(-8.0 * jnp.arange(1, SWA_HEADS + 1, dtype=F32) / SWA_HEADS)[:, None, None]
    r = jnp.arange(w)[:, None]
    c = jnp.arange(w)[None, :]
    dist = (r - c).astype(F32)[None]
    neg = jnp.float32(-jnp.inf)
    bias_c = jnp.where((c <= r)[None], -slopes * dist, neg)
    bias_p = jnp.where((c > r)[None], -slopes * (dist + w), neg)
    sink = jnp.broadcast_to(sinks.astype(F32)[:, None, None], (SWA_HEADS, w, w))
    return jnp.concatenate([bias_p, bias_c, sink], axis=2).reshape(SWA_HEADS * w, 3 * w)


def _mem_heads(q_ref, mk_ref, mv_ref, o_ref):
    scale = MEM_HEAD_DIM ** -0.5
    for h in range(MEM_HEADS):
        cols = slice(h * MEM_HEAD_DIM, (h + 1) * MEM_HEAD_DIM)
        z = _nt_dot(q_ref[:, cols], mk_ref[:, cols]) * scale
        m = jnp.max(z, axis=1, keepdims=True)
        p = jnp.exp(z - m)
        den = jnp.sum(p, axis=1, keepdims=True)
        o = jnp.dot(p.astype(BF16), mv_ref[:, cols], preferred_element_type=F32) / den
        o_ref[:, cols] = o.astype(o_ref.dtype)


def _merge_route_kernel(osb_ref, osw_ref, om_ref, g_ref, x_ref, wsb_ref, wsw_ref, wm_ref,
                        wout_ref, lng_ref, lnb_ref, wrh_ref, wrl_ref, rbias_ref,
                        x1_ref, x1p_ref, idx_ref, rank_ref, wgt_ref, cnt_ref, x1_prev, carry_ref):
    i = pl.program_id(0)

    @pl.when(i == 0)
    def _():
        x1_prev[...] = jnp.zeros_like(x1_prev)
        carry_ref[...] = jnp.zeros_like(carry_ref)

    d = x_ref.shape[1]
    st = {}

    def branch(b, o_ref, w_ref):
        def run():
            term = g_ref[:, b * d:(b + 1) * d].astype(F32) * jnp.dot(
                o_ref[...], w_ref[...], preferred_element_type=F32)
            st["merged"] = term if b == 0 else st["merged"] + term
        return run

    def out_proj():
        st["y"] = jnp.dot(st["merged"].astype(BF16), wout_ref[...], preferred_element_type=F32)

    idx, rank, wgt, count = _route(
        x1_prev[...], wrh_ref[...], wrl_ref[...], rbias_ref[...], carry_ref[...],
        side_work=(branch(0, osb_ref, wsb_ref), branch(1, osw_ref, wsw_ref),
                   branch(2, om_ref, wm_ref), out_proj))
    x1 = _layer_norm(DEEPNORM_ALPHA * x_ref[...] + st["y"], lng_ref[...], lnb_ref[...])
    x1_ref[...] = x1
    x1p_ref[...] = _pack_bf16_pair(x1[:, :d // 2], x1[:, d // 2:])
    idx_ref[...] = idx
    rank_ref[...] = rank
    wgt_ref[...] = wgt
    carry_ref[...] = carry_ref[...] + jnp.where(i > 0, count, 0.0)
    cnt_ref[...] = carry_ref[...]
    x1_prev[...] = x1


def _merge_route(o_sb, o_sw, o_m, p, x2, w_sb, w_sw, w_m, w_out, ln_g, ln_b, wr_hi, wr_lo,
                 bias_col, tm):
    t, d = x2.shape
    n = t // tm
    cur = lambda i: (jnp.minimum(i, n - 1), 0)
    row = lambda w: pl.BlockSpec((tm, w), cur)
    slot = pl.BlockSpec((TOP_K, tm), lambda i: (0, jnp.maximum(i - 1, 0)))
    return pl.pallas_call(
        _merge_route_kernel,
        grid=(n + 1,),
        in_specs=[row(o_sb.shape[1]), row(o_sw.shape[1]), row(o_m.shape[1]),
                  pl.BlockSpec((tm, N_BRANCH * d), cur),
                  row(d),
                  _resident(w_sb.shape), _resident(w_sw.shape), _resident(w_m.shape),
                  _resident(w_out.shape), _resident(ln_g.shape), _resident(ln_b.shape),
                  _resident(wr_hi.shape), _resident(wr_lo.shape), _resident(bias_col.shape)],
        out_specs=[row(d), row(d // 2), slot, slot, slot,
                   pl.BlockSpec((N_EXPERTS, 1), lambda i: (0, 0))],
        out_shape=[jax.ShapeDtypeStruct((t, d), F32), jax.ShapeDtypeStruct((t, d // 2), U32),
                   jax.ShapeDtypeStruct((TOP_K, t), I32),
                   jax.ShapeDtypeStruct((TOP_K, t), I32),
                   jax.ShapeDtypeStruct((TOP_K, t), F32),
                   jax.ShapeDtypeStruct((N_EXPERTS, 1), F32)],
        scratch_shapes=[pltpu.VMEM((tm, d), F32), pltpu.VMEM((N_EXPERTS, 1), F32)],
        compiler_params=pltpu.CompilerParams(
            dimension_semantics=("arbitrary",), vmem_limit_bytes=VMEM_LIMIT),
        name="merge_route",
    )(o_sb, o_sw, o_m, p, x2, w_sb, w_sw, w_m, w_out, ln_g, ln_b, wr_hi, wr_lo, bias_col)


def _route(x, wh, wl, bias, carry, side_work=()):
    tr = x.shape[0]
    xh = x.astype(BF16)
    xl = (x - xh.astype(F32)).astype(BF16)
    logits = _nt_dot(wh, xh) + _nt_dot(wh, xl) + _nt_dot(wl, xh)
    scores = _sigmoid(logits)
    biased = scores + bias
    neg = jnp.float32(-jnp.inf)

    sub = lax.broadcasted_iota(I32, (GROUP_SIZE, tr), 0)
    gscore = []
    for g in range(N_GROUPS):
        blk = biased[g * GROUP_SIZE:(g + 1) * GROUP_SIZE, :]
        m1 = jnp.max(blk, axis=0, keepdims=True)
        i1 = jnp.min(jnp.where(blk == m1, sub, GROUP_SIZE), axis=0, keepdims=True)
        m2 = jnp.max(jnp.where(sub == i1, neg, blk), axis=0, keepdims=True)
        gscore.append(m1 + m2)
    gs = jnp.concatenate(gscore, axis=0)

    giota = lax.broadcasted_iota(I32, (N_GROUPS, tr), 0)
    gsel = jnp.zeros((N_GROUPS, tr), F32)
    for _ in range(TOPK_GROUPS):
        m = jnp.max(gs, axis=0, keepdims=True)
        gi = jnp.min(jnp.where(gs == m, giota, N_GROUPS), axis=0, keepdims=True)
        hit = giota == gi
        gsel = jnp.where(hit, 1.0, gsel)
        gs = jnp.where(hit, neg, gs)

    masked = jnp.concatenate(
        [jnp.where(gsel[g:g + 1, :] > 0.0, biased[g * GROUP_SIZE:(g + 1) * GROUP_SIZE, :], neg)
         for g in range(N_GROUPS)], axis=0)

    eiota = lax.broadcasted_iota(I32, (N_EXPERTS, tr), 0)
    sel = jnp.zeros((N_EXPERTS, tr), F32)
    idx_rows, w_rows = [], []
    side_work = list(side_work)
    for k in range(TOP_K):
        if side_work and k % 2 == 0:
            side_work.pop(0)()
        m = jnp.max(masked, axis=0, keepdims=True)
        ei = jnp.min(jnp.where(masked == m, eiota, N_EXPERTS), axis=0, keepdims=True)
        hit = eiota == ei
        idx_rows.append(ei)
        w_rows.append(jnp.sum(jnp.where(hit, scores, 0.0), axis=0, keepdims=True))
        sel = jnp.where(hit, 1.0, sel)
        masked = jnp.where(hit, neg, masked)

    wsum = w_rows[0]
    for wk in w_rows[1:]:
        wsum = wsum + wk
    wgt = jnp.concatenate(w_rows, axis=0) / wsum * ROUTED_SCALE
    idx = jnp.concatenate(idx_rows, axis=0)

    a = lax.broadcasted_iota(I32, (tr, tr), 0)
    b = lax.broadcasted_iota(I32, (tr, tr), 1)
    before = (a < b).astype(BF16)
    rank = jnp.dot(sel.astype(BF16), before, preferred_element_type=F32) + carry
    rank_rows = [jnp.sum(jnp.where(eiota == ei, rank, 0.0), axis=0, keepdims=True)
                 for ei in idx_rows]
    rank = jnp.concatenate(rank_rows, axis=0).astype(I32)
    return idx, rank, wgt, jnp.sum(sel, axis=1, keepdims=True)


def _dest_kernel(idx_ref, rank_ref, start_ref, dest_ref):
    tr = idx_ref.shape[1]
    eiota = lax.broadcasted_iota(I32, (N_EXPERTS, tr), 0)
    rows = []
    for k in range(TOP_K):
        hit = eiota == idx_ref[k:k + 1, :]
        rows.append(jnp.sum(jnp.where(hit, start_ref[...], 0), axis=0, keepdims=True))
    dest_ref[...] = jnp.concatenate(rows, axis=0) + rank_ref[...]


def _dest(idx, rank, start_col, tr):
    t = idx.shape[1]
    slot = pl.BlockSpec((TOP_K, tr), lambda i: (0, i))
    return pl.pallas_call(
        _dest_kernel,
        grid=(t // tr,),
        in_specs=[slot, slot, _resident(start_col.shape)],
        out_specs=slot,
        out_shape=jax.ShapeDtypeStruct((TOP_K, t), I32),
        compiler_params=pltpu.CompilerParams(dimension_semantics=("parallel",)),
        name="slot_dest",
    )(idx, rank, start_col)


def _sc_worker_id():
    return lax.axis_index("s") * SC_CORES + lax.axis_index("c")


def _dispatch(x1p, dest, n_rows):
    t, w = x1p.shape
    per = t // SC_WORKERS
    win = min(SC_INDEX_WINDOW, per)
    mesh = plsc.VectorSubcoreMesh(core_axis_name="c", subcore_axis_name="s")

    @functools.partial(
        pl.kernel, mesh=mesh,
        out_type=jax.ShapeDtypeStruct((n_rows, w), x1p.dtype),
        scratch_types=[pltpu.VMEM((TOP_K, win), I32),
                       pltpu.VMEM((win, w), x1p.dtype),
                       pltpu.SemaphoreType.DMA],
        name="sc_dispatch",
    )
    def scatter_rows(x_hbm, dest_hbm, xs_hbm, idx_v, rows_v, sem):
        base = _sc_worker_id() * per

        @pl.loop(0, per // win)
        def _(j):
            t0 = pl.multiple_of(base + j * win, win)
            pltpu.sync_copy(dest_hbm.at[:, pl.ds(t0, win)], idx_v)
            pltpu.sync_copy(x_hbm.at[pl.ds(t0, win)], rows_v)
            copies = [pltpu.async_copy(rows_v, xs_hbm.at[idx_v.at[k]], sem) for k in range(TOP_K)]
            for c in copies:
                c.wait()

    return scatter_rows(x1p, dest)


def _gather_rows(table, idx):
    n = idx.shape[0]
    w = table.shape[1]
    per = n // SC_WORKERS
    chunk = min(SC_GATHER_ROWS, per // 2)
    assert n % SC_WORKERS == 0 and per % (2 * chunk) == 0, (n, chunk)
    mesh = plsc.VectorSubcoreMesh(core_axis_name="c", subcore_axis_name="s")

    @functools.partial(
        pl.kernel, mesh=mesh,
        out_type=jax.ShapeDtypeStruct((n, w), table.dtype),
        scratch_types=[pltpu.VMEM((per,), I32),
                       pltpu.VMEM((2, chunk, w), table.dtype),
                       pltpu.SemaphoreType.DMA((2,)),
                       pltpu.SemaphoreType.DMA((2,))],
        name="sc_gather",
    )
    def gather_rows(table_hbm, idx_hbm, out_hbm, idx_v, rows_v, gather_sem, put_sem):
        base = _sc_worker_id() * per
        nchunks = per // chunk
        pltpu.sync_copy(idx_hbm.at[pl.ds(base, per)], idx_v)

        def gather(j, b):
            off = pl.multiple_of(j * chunk, chunk)
            return pltpu.make_async_copy(table_hbm.at[idx_v.at[pl.ds(off, chunk)]],
                                         rows_v.at[b], gather_sem.at[b])

        def put(j, b):
            off = pl.multiple_of(j * chunk, chunk)
            return pltpu.make_async_copy(rows_v.at[b], out_hbm.at[pl.ds(base + off, chunk)],
                                         put_sem.at[b])

        gather(0, 0).start()

        @pl.loop(0, nchunks, step=2)
        def _(j):
            for b in (0, 1):
                jj = j + b

                @pl.when(jj + 1 < nchunks)
                def _():
                    @pl.when(jj >= 1)
                    def _():
                        put(jj - 1, 1 - b).wait()
                    gather(jj + 1, 1 - b).start()

                gather(jj, b).wait()
                put(jj, b).start()

        put(nchunks - 2, 0).wait()
        put(nchunks - 1, 1).wait()

    return gather_rows(table, idx)


def _expert_kernel(first_ref, nblk_ref, total_ref, wgu_ref, wd_ref, xs_hbm, ys_hbm,
                   wgu_s, wd_s, xbuf, ybuf, in_sem, out_sem):
    e = pl.program_id(0)
    total = total_ref[0]
    n_in, rows, half = xbuf.shape
    n_out = ybuf.shape[0]
    ahead = EXPERT_AHEAD

    def block_rows(g):
        return pl.ds(pl.multiple_of(g * rows, rows), rows)

    def load(g):
        slot = g % n_in
        return pltpu.make_async_copy(xs_hbm.at[block_rows(g), :], xbuf.at[slot], in_sem.at[slot])

    def store(g):
        slot = g % n_out
        return pltpu.make_async_copy(ybuf.at[slot], ys_hbm.at[block_rows(g), :], out_sem.at[slot])

    @pl.when(e == 0)
    def _():
        for g in range(ahead):
            @pl.when(g < total)
            def _(g=g):
                load(g).start()

    wgu_s[...] = wgu_ref[0].astype(BF16)
    wd_s[...] = wd_ref[0].astype(BF16)
    g0 = first_ref[e]
    nblk = nblk_ref[e]
    ff = wd_s.shape[0]

    def acquire(g):
        load(g).wait()

        @pl.when(g + ahead < total)
        def _():
            load(g + ahead).start()

        @pl.when(g >= n_out)
        def _():
            store(g - n_out).wait()

    def compute(gs):
        st = {}

        def unpack(k):
            x_lo, x_hi = _unpack_bf16_pair(xbuf[gs[k] % n_in])
            st[k, "x"] = (x_lo.astype(BF16), x_hi.astype(BF16))

        def up_proj(k):
            x_lo, x_hi = st[k, "x"]
            st[k, "h"] = (jnp.dot(x_lo, wgu_s[:half, :], preferred_element_type=F32)
                          + jnp.dot(x_hi, wgu_s[half:, :], preferred_element_type=F32))

        def activate(k):
            gate, up = st[k, "h"][:, :ff], st[k, "h"][:, ff:]
            st[k, "a"] = (gate * _sigmoid(gate) * up).astype(BF16)

        def down_proj(k):
            st[k, "y"] = jnp.dot(st[k, "a"], wd_s[...], preferred_element_type=F32)

        def pack(k):
            y = st[k, "y"]
            ybuf[gs[k] % n_out] = _pack_bf16_pair(y[:, :half], y[:, half:])

        stages = (unpack, up_proj, activate, down_proj, pack)
        for t in range(len(gs) + len(stages) - 1):
            for s in reversed(range(len(stages))):
                if 0 <= t - s < len(gs):
                    stages[s](t - s)

    def pair(jj, carry):
        g = g0 + 2 * jj
        acquire(g)
        acquire(g + 1)
        compute([g, g + 1])
        store(g).start()
        store(g + 1).start()
        return carry

    lax.fori_loop(0, nblk // 2, pair, 0)

    @pl.when(nblk % 2 == 1)
    def _():
        g = g0 + nblk - 1
        acquire(g)
        compute([g])
        store(g).start()

    @pl.when(e == pl.num_programs(0) - 1)
    def _():
        for back in range(n_out, 0, -1):
            @pl.when(total >= back)
            def _(back=back):
                store(total - back).wait()


def _experts(xs, first_blk, n_blk, total_blk, w_gu, w_down):
    n_rows, half = xs.shape
    d = 2 * half
    n_exp, _, ff2 = w_gu.shape
    ff = w_down.shape[1]
    grid_spec = pltpu.PrefetchScalarGridSpec(
        num_scalar_prefetch=3,
        grid=(n_exp,),
        in_specs=[pl.BlockSpec((1, d, ff2), lambda e, *_: (e, 0, 0)),
                  pl.BlockSpec((1, ff, d), lambda e, *_: (e, 0, 0)),
                  pl.BlockSpec(memory_space=pl.ANY)],
        out_specs=pl.BlockSpec(memory_space=pl.ANY),
        scratch_shapes=[pltpu.VMEM((d, ff2), BF16), pltpu.VMEM((ff, d), BF16),
                        pltpu.VMEM((EXPERT_IN_SLOTS, EXPERT_ROWS, half), xs.dtype),
                        pltpu.VMEM((EXPERT_OUT_SLOTS, EXPERT_ROWS, half), xs.dtype),
                        pltpu.SemaphoreType.DMA((EXPERT_IN_SLOTS,)),
                        pltpu.SemaphoreType.DMA((EXPERT_OUT_SLOTS,))],
    )
    return pl.pallas_call(
        _expert_kernel,
        grid_spec=grid_spec,
        out_shape=jax.ShapeDtypeStruct((n_rows, half), xs.dtype),
        compiler_params=pltpu.CompilerParams(
            dimension_semantics=("arbitrary",), vmem_limit_bytes=VMEM_LIMIT),
        name="experts",
    )(first_blk, n_blk, total_blk, w_gu, w_down, xs)


def _combine_kernel(x1_ref, w_ref, yg_ref, wsgu_ref, wsd_ref, lng_ref, lnb_ref, o_ref):
    tc = x1_ref.shape[0]
    x1 = x1_ref[...]
    ff = wsd_ref.shape[0]
    h = jnp.dot(x1.astype(BF16), wsgu_ref[...], preferred_element_type=F32)
    gate, up = h[:, :ff], h[:, ff:]
    act = gate * _sigmoid(gate) * up
    moe = jnp.dot(act.astype(BF16), wsd_ref[...], preferred_element_type=F32)

    half = yg_ref.shape[2]
    wt = jnp.transpose(w_ref[...])
    r_lo = jnp.zeros((tc, half), F32)
    r_hi = jnp.zeros((tc, half), F32)
    for k in range(TOP_K):
        y_lo, y_hi = _unpack_bf16_pair(yg_ref[k])
        w = wt[:, k:k + 1]
        r_lo = r_lo + w * y_lo
        r_hi = r_hi + w * y_hi
    moe = moe + jnp.concatenate([r_lo, r_hi], axis=1)
    o_ref[...] = _layer_norm(DEEPNORM_ALPHA * x1 + moe, lng_ref[...], lnb_ref[...])


def _combine(x1, wgt, yg, chunk, ws_gu, ws_down, ln_g, ln_b, tc):
    t, d = x1.shape
    steps = yg.shape[1] // tc
    tok = lambda i: (chunk * steps + i, 0)
    return pl.pallas_call(
        _combine_kernel,
        grid=(steps,),
        in_specs=[pl.BlockSpec((tc, d), tok),
                  pl.BlockSpec((TOP_K, tc), lambda i: (0, chunk * steps + i)),
                  pl.BlockSpec((TOP_K, tc, yg.shape[2]), lambda i: (0, i, 0)),
                  _resident(ws_gu.shape), _resident(ws_down.shape),
                  _resident(ln_g.shape), _resident(ln_b.shape)],
        out_specs=pl.BlockSpec((tc, d), tok),
        out_shape=jax.ShapeDtypeStruct((t, d), F32),
        input_output_aliases={0: 0},
        compiler_params=pltpu.CompilerParams(
            dimension_semantics=("arbitrary",), vmem_limit_bytes=VMEM_LIMIT),
        name="combine_ln2",
    )(x1, wgt, yg, ws_gu, ws_down, ln_g, ln_b)


def _fused_in_weights(w_in):
    d = w_in.shape[0]
    sizes = (SB_HEADS * HEAD_DIM,) * 3 + (SWA_HEADS * HEAD_DIM, SWA_KV_HEADS * HEAD_DIM,
                                          SWA_KV_HEADS * HEAD_DIM, MEM_HEADS * MEM_HEAD_DIM)
    parts, off = [], 0
    for s in sizes:
        parts.append(w_in[:, off:off + s])
        off += s
    q_sb, k_sb, v_sb, q_sw, k_sw, v_sw, q_m = parts
    gates = w_in[:, off:]
    scale = HEAD_DIM ** -0.5
    q_sw = (q_sw * scale).reshape(d, SWA_KV_HEADS, SWA_GROUP, HEAD_DIM)
    q_sw = jnp.swapaxes(q_sw, 1, 2).reshape(d, SWA_HEADS * HEAD_DIM)
    w_a =jnp.concatenate([q_sw, q_m, k_sw, v_sw], axis=1).astype(BF16)
    w_b = jnp.concatenate([gates, q_sb * scale, k_sb, v_sb], axis=1).astype(BF16)
    assert w_a.shape[1] == PA_COLS and w_b.shape[1] == PB_COLS
    return w_a, w_b


def _grouped_swa_out_weights(w_o_swa):
    d = w_o_swa.shape[1]
    w = w_o_swa.reshape(SWA_KV_HEADS, SWA_GROUP, HEAD_DIM, d)
    return jnp.swapaxes(w, 0, 1).reshape(SWA_HEADS * HEAD_DIM, d).astype(BF16)


def kernel(x, mem, w_in, b_gate, w_mem_kv, sinks, w_o_sb, w_o_swa, w_o_mem, w_out,
           ln1_g, ln1_b, w_router, router_bias, w_e_gu, w_e_down, w_s_gu, w_s_down,
           ln2_g, ln2_b):
    batch, seq, d = x.shape
    mem_len = mem.shape[1]
    t = batch * seq
    x2 = x.reshape(t, d)
    row_tile = min(512, t)

    w_a, w_b = _fused_in_weights(w_in)
    p_a = _matmul_bf16(x2, w_a, row_tile, "in_proj_attn")
    mkv = _matmul_bf16(mem.reshape(batch * mem_len, d), w_mem_kv.astype(BF16), mem_len,
                       "mem_kv_proj")
    p, o_sw, o_m = _proj_attn(x2, w_b, b_gate.reshape(1, -1), p_a, mkv, _swa_tables(sinks),
                              batch, seq, mem_len)
    o_sb = _sb_attention(p, batch, seq, min(256, seq), SB_PAIRS_PER_STEP)
    wr_t = w_router.T
    wr_hi = wr_t.astype(BF16)
    wr_lo = (wr_t - wr_hi.astype(F32)).astype(BF16)
    x1, x1p, idx, rank, wgt, cnt = _merge_route(
        o_sb, o_sw, o_m, p, x2, w_o_sb.astype(BF16), _grouped_swa_out_weights(w_o_swa),
        w_o_mem.astype(BF16), w_out.astype(BF16), ln1_g.reshape(1, d), ln1_b.reshape(1, d),
        wr_hi, wr_lo, router_bias.reshape(-1, 1).astype(F32), row_tile)

    out = _moe_ln(x1, x1p, idx, rank, wgt, cnt, w_e_gu, w_e_down, w_s_gu, w_s_down, ln2_g, ln2_b)
    return out.reshape(batch, seq, d)


def _moe_ln(x1, x1p, idx, rank, wgt, cnt, w_e_gu, w_e_down, w_s_gu, w_s_down, ln2_g, ln2_b):
    t, d = x1.shape
    counts = cnt[:, 0].astype(I32)
    padded = (counts + EXPERT_ROWS - 1) // EXPERT_ROWS * EXPERT_ROWS
    pad_end = jnp.cumsum(padded)
    pad_start = pad_end - padded
    n_blocks = t * TOP_K // EXPERT_ROWS + N_EXPERTS
    dest = _dest(idx, rank, pad_start.reshape(-1, 1), min(2048, t))

    xs = _dispatch(x1p, dest, n_blocks * EXPERT_ROWS)
    ys = _experts(xs, pad_start // EXPERT_ROWS, padded // EXPERT_ROWS,
                  pad_end[-1:] // EXPERT_ROWS, w_e_gu, w_e_down)
    tchunk = t // COMBINE_CHUNKS
    ws_gu, ws_down = w_s_gu.astype(BF16), w_s_down.astype(BF16)
    out = x1
    for c in range(COMBINE_CHUNKS):
        slots = dest[:, c * tchunk:(c + 1) * tchunk].reshape(-1)
        yg = _gather_rows(ys, slots).reshape(TOP_K, tchunk, -1)
        out = _combine(out, wgt, yg, c, ws_gu, ws_down, ln2_g.reshape(1, d), ln2_b.reshape(1, d),
                       min(512, tchunk))
    return out
```

```python
import functools

import jax
import jax.numpy as jnp
from jax import lax
from jax.experimental import pallas as pl
from jax.experimental.pallas import tpu as pltpu
from jax.experimental.pallas import tpu_sc as plsc

F32 = jnp.float32
BF16 = jnp.bfloat16
I32 = jnp.int32
U32 = jnp.uint32

HEAD_DIM = 64
SB_HEADS = 8
SWA_HEADS = 8
SWA_KV_HEADS = 2
SWA_GROUP = SWA_HEADS // SWA_KV_HEADS
SWA_WINDOW = 128
MEM_HEADS = 4
MEM_HEAD_DIM = 128
N_BRANCH = 3
N_EXPERTS = 256
TOP_K = 8
N_GROUPS = 8
GROUP_SIZE = N_EXPERTS // N_GROUPS
TOPK_GROUPS = 4
ROUTED_SCALE = 2.5
LN_EPS = 1e-5
DEPTH = 1
DEEPNORM_ALPHA = (2 * DEPTH) ** 0.25

LANES = 128
SC_CORES = 2
SC_SUBCORES = 16
SC_WORKERS = SC_CORES * SC_SUBCORES
SC_INDEX_WINDOW = 128
SC_GATHER_ROWS = 64
VMEM_LIMIT = 56 * 1024 * 1024

A_QSW = 0
A_QM = 512
A_KSW = 1024
A_VSW = 1152
PA_COLS = 1280
C_QSB = 3072
C_KSB = 3584
C_VSB = 4096
PB_COLS = 4608
PROJ_CHUNK = 256
PROJ_ROWS = 512

SB_SKIP = 110.0
SB_MASKED = -1e30
SB_PAIRS_PER_STEP = 4

EXPERT_ROWS = 512
EXPERT_AHEAD = 8
EXPERT_IN_SLOTS = EXPERT_AHEAD + 2
EXPERT_OUT_SLOTS = 4
COMBINE_CHUNKS = 8


def _nt_dot(a, b):
    return lax.dot_general(a, b, (((1,), (1,)), ((), ())), preferred_element_type=F32)


def _sigmoid(x):
    return 1.0 / (1.0 + jnp.exp(-x))


def _layer_norm(h, g, b):
    mu = jnp.mean(h, axis=-1, keepdims=True)
    d = h - mu
    var = jnp.mean(d * d, axis=-1, keepdims=True)
    return d * lax.rsqrt(var + LN_EPS) * g + b


def _pack_bf16_pair(a, b):
    a_bits = lax.bitcast_convert_type(a.astype(BF16).astype(F32), U32)
    b_bits = lax.bitcast_convert_type(b.astype(BF16).astype(F32), U32)
    return (a_bits >> 16) | b_bits


def _unpack_bf16_pair(w):
    a = lax.bitcast_convert_type(w << 16, F32)
    b = lax.bitcast_convert_type(w & jnp.uint32(0xFFFF0000), F32)
    return a, b


def _resident(shape):
    nd = len(shape)
    return pl.BlockSpec(shape, lambda *_: (0,) * nd, pipeline_mode=pl.Buffered(1))


def _proj_attn_kernel(x_ref, w_ref, b_ref, hp_ref, qsw_ref, k_refs, v_refs, qm_ref, mk_ref, mv_ref,
                      p_ref, osw_ref, om_ref, *, gate_cols):
    j = pl.program_id(1)
    xb = x_ref[...].astype(BF16)

    def proj_chunk(c):
        cols = slice(c * PROJ_CHUNK, (c + 1) * PROJ_CHUNK)
        acc = jnp.dot(xb, w_ref[:, cols], preferred_element_type=F32)
        if (c + 1) * PROJ_CHUNK <= gate_cols:
            acc = _sigmoid(acc + b_ref[:, cols])
        p_ref[:, cols] = acc.astype(p_ref.dtype)

    blk = SWA_WINDOW
    nblk = x_ref.shape[0] // blk

    def swa(hb):
        rows = slice(hb * blk, (hb + 1) * blk)
        first = (j == 0) if hb == 0 else None
        _swa_block(hp_ref, qsw_ref, rows, k_refs[hb], k_refs[hb + 1], v_refs[hb], v_refs[hb + 1],
                   first, osw_ref)

    side = [functools.partial(swa, hb) for hb in range(nblk)]
    side.append(functools.partial(_mem_heads, qm_ref, mk_ref, mv_ref, om_ref))
    nchunk = p_ref.shape[1] // PROJ_CHUNK
    per = -(-nchunk // len(side))
    for s, work in enumerate(side):
        for c in range(s * per, min((s + 1) * per, nchunk)):
            proj_chunk(c)
        work()


def _proj_attn(x2, w_b, b_gate, p_a, mkv, swa_tables, batch, seq, mem_len):
    t, d = x2.shape
    tm = PROJ_ROWS
    nt = seq // tm
    blk = SWA_WINDOW
    nblk = tm // blk
    nb = seq // blk
    qw = SWA_HEADS * HEAD_DIM
    mw = MEM_HEADS * MEM_HEAD_DIM
    tile = lambda w, col: pl.BlockSpec((tm, w), lambda b, j: (b * nt + j, col))
    kv = lambda col: [pl.BlockSpec((blk, LANES),
                                   functools.partial(lambda b, j, off, col: (
                                       b * nb + jnp.maximum(nblk * j + off, 0), col), off=off, col=col))
                      for off in range(-1, nblk)]
    return pl.pallas_call(
        functools.partial(_proj_attn_kernel, gate_cols=b_gate.shape[1]),
        grid=(batch, nt),
        in_specs=[tile(d, 0), _resident(w_b.shape), _resident(b_gate.shape),
                  _resident(swa_tables.shape),
                  tile(qw, A_QSW // qw),
                  kv(A_KSW // LANES), kv(A_VSW // LANES),
                  tile(mw, A_QM // mw),
                  pl.BlockSpec((mem_len, mw), lambda b, j: (b, 0)),
                  pl.BlockSpec((mem_len, mw), lambda b, j: (b, 1))],
        out_specs=[tile(w_b.shape[1], 0), tile(qw, 0), tile(mw, 0)],
        out_shape=[jax.ShapeDtypeStruct((t, w_b.shape[1]), BF16),
                   jax.ShapeDtypeStruct((t, qw), BF16),
                   jax.ShapeDtypeStruct((t, mw), BF16)],
        compiler_params=pltpu.CompilerParams(
            dimension_semantics=("parallel", "arbitrary"), vmem_limit_bytes=VMEM_LIMIT),
        name="in_proj_swa_mem",
    )(x2, w_b, b_gate, swa_tables, p_a, [p_a] * (nblk + 1), [p_a] * (nblk + 1), p_a, mkv, mkv)


def _mm_kernel(x_ref, w_ref, o_ref):
    o_ref[...] = jnp.dot(x_ref[...].astype(BF16), w_ref[...],
                         preferred_element_type=F32).astype(o_ref.dtype)


def _matmul_bf16(x2, w, tm, name):
    t, d = x2.shape
    n = w.shape[1]
    return pl.pallas_call(
        _mm_kernel,
        grid=(t // tm,),
        in_specs=[pl.BlockSpec((tm, d), lambda i: (i, 0)), _resident((d, n))],
        out_specs=pl.BlockSpec((tm, n), lambda i: (i, 0)),
        out_shape=jax.ShapeDtypeStruct((t, n), BF16),
        compiler_params=pltpu.CompilerParams(
            dimension_semantics=("parallel",), vmem_limit_bytes=VMEM_LIMIT),
        name=name,
    )(x2, w)


def _sb_kernel(q_ref, k_ref, v_ref, o_ref, *, tq):
    i = pl.program_id(2)
    pairs = q_ref.shape[1] // LANES
    lane = lax.broadcasted_iota(I32, (1, LANES), 1)
    r = lax.broadcasted_iota(I32, (tq, tq), 0)
    c = lax.broadcasted_iota(I32, (tq, tq), 1)
    tri = (r >= c).astype(BF16)
    causal = c < r
    nh = LANES // HEAD_DIM
    hmasks = [(lane >= h * HEAD_DIM) & (lane < (h + 1) * HEAD_DIM) for h in range(nh)]
    qs = []
    for p in range(pairs):
        q = q_ref[:, p * LANES:(p + 1) * LANES]
        qs.append(jnp.concatenate([jnp.where(hm, q, jnp.zeros_like(q)) for hm in hmasks], axis=0))
    causal2 = jnp.concatenate([causal] * nh, axis=0)
    tri2 = jnp.concatenate([tri, tri], axis=0)
    mp = nh * tq

    def blocks(jobs, carry, acc):
        chains = [(jb, p) for jb in range(len(jobs)) for p in range(pairs)]
        z, hl, suffix, ab, av, cin = {}, {}, {}, {}, {}, {}

        def rows(jb):
            return pl.ds(pl.multiple_of(jobs[jb][0] * tq, tq), tq)

        def scores(jb, p):
            zz = _nt_dot(qs[p], k_ref[rows(jb), p * LANES:(p + 1) * LANES])
            if jobs[jb][1]:
                zz = jnp.where(causal2, zz, SB_MASKED)
            z[jb, p] = zz

        def softplus_split(jb, p):
            sp = jnp.maximum(z[jb, p], 0.0) + jnp.log(1.0 + jnp.exp(-jnp.abs(z[jb, p])))
            hi = sp.astype(BF16)
            lo = (sp - hi.astype(F32)).astype(BF16)
            hl[jb, p] = jnp.concatenate([hi, lo], axis=1)

        def cumsum(jb, p):
            suffix[jb, p] = jnp.dot(hl[jb, p], tri2, preferred_element_type=F32)

        def weights(jb, p):
            cin[jb, p] = carry[p] if jb == 0 else cin[jb - 1, p] + step_sum(jb - 1, p)
            a = jnp.exp((z[jb, p] - cin[jb, p]) - suffix[jb, p])
            ab[jb, p] = a.astype(BF16)

        def values(jb, p):
            out = jnp.dot(ab[jb, p], v_ref[rows(jb), p * LANES:(p + 1) * LANES],
                          preferred_element_type=F32)
            live = jobs[jb][2]
            av[jb, p] = out if live is None else jnp.where(live, out, 0.0)

        def step_sum(jb, p):
            total = suffix[jb, p][:, 0:1]
            live = jobs[jb][2]
            return total if live is None else jnp.where(live, total, 0.0)

        stages = (scores, softplus_split, cumsum, weights, values)
        for t in range(len(chains) + len(stages) - 1):
            for s in reversed(range(len(stages))):
                if 0 <= t - s < len(chains):
                    stages[s](*chains[t - s])
        last = len(jobs) - 1
        new_acc = []
        for p in range(pairs):
            total = acc[p]
            for jb in range(len(jobs)):
                total = total + av[jb, p]
            new_acc.append(total)
        return [cin[last, p] + step_sum(last, p) for p in range(pairs)], new_acc

    def block(kb, carry, acc, diag):
        return blocks([(kb, diag, None)], carry, acc)

    carry, acc = blocks([(i, True, None), (jnp.maximum(i - 1, 0), False, i > 0)],
                        [jnp.zeros((mp, 1), F32)] * pairs,
                        [jnp.zeros((mp, LANES), F32)] * pairs)

    def cond(s):
        kb, carry, _ = s
        lowest = carry[0]
        for cp in carry[1:]:
            lowest = jnp.minimum(lowest, cp)
        return (kb >= 0) & (jnp.min(lowest) < SB_SKIP)

    def body(s):
        kb, carry, acc = s
        carry, acc = block(kb, carry, acc, False)
        return kb - 1, carry, acc

    _, _, acc = lax.while_loop(cond, body, (i - 2, carry, acc))
    for p in range(pairs):
        o_ref[:, p * LANES:(p + 1) * LANES] = jnp.where(
            hmasks[0], acc[p][:tq], acc[p][tq:]).astype(o_ref.dtype)


def _sb_attention(p, batch, seq, tq, pairs):
    t = batch * seq
    nq = seq // tq
    w = pairs * LANES
    ngrp = SB_HEADS * HEAD_DIM // w
    qc, kc, vc = C_QSB // w, C_KSB // w, C_VSB // w
    return pl.pallas_call(
        functools.partial(_sb_kernel, tq=tq),
        grid=(batch, ngrp, nq),
        in_specs=[pl.BlockSpec((tq, w), lambda b, h, i: (b * nq + i, qc + h)),
                  pl.BlockSpec((seq, w), lambda b, h, i: (b, kc + h)),
                  pl.BlockSpec((seq, w), lambda b, h, i: (b, vc + h))],
        out_specs=pl.BlockSpec((tq, w), lambda b, h, i: (b * nq + i, h)),
        out_shape=jax.ShapeDtypeStruct((t, SB_HEADS * HEAD_DIM), BF16),
        compiler_params=pltpu.CompilerParams(
            dimension_semantics=("parallel", "parallel", "arbitrary"),
            vmem_limit_bytes=VMEM_LIMIT),
        name="sb_attention",
    )(p, p, p)


def _swa_block(hp_ref, q_ref, rows, kprev, kcur, vprev, vcur, first, o_ref):
    blk = kcur.shape[0]
    sink = hp_ref[:, 2 * blk:]
    lane = lax.broadcasted_iota(I32, (1, LANES), 1)
    kvmasks = [(lane >= kv * HEAD_DIM) & (lane < (kv + 1) * HEAD_DIM) for kv in range(SWA_KV_HEADS)]
    groups = [q_ref[rows, g * LANES:(g + 1) * LANES] for g in range(SWA_GROUP)]
    qs = jnp.concatenate([jnp.where(kvmasks[kv], groups[g], jnp.zeros_like(groups[g]))
                          for kv in range(SWA_KV_HEADS) for g in range(SWA_GROUP)], axis=0)
    keys = jnp.concatenate([kprev[...], kcur[...]], axis=0)
    vals = jnp.concatenate([vprev[...], vcur[...]], axis=0)
    z = _nt_dot(qs, keys) + hp_ref[:, :2 * blk]
    if first is not None:
        col = lax.broadcasted_iota(I32, (1, 2 * blk), 1)
        z = jnp.where(first & (col < blk), jnp.float32(-jnp.inf), z)
    m = jnp.maximum(jnp.max(z, axis=1, keepdims=True), sink)
    p = jnp.exp(z - jnp.concatenate([m, m], axis=1))
    den = jnp.sum(p, axis=1, keepdims=True) + jnp.exp(sink - m)
    o = jnp.dot(p.astype(BF16), vals, preferred_element_type=F32) / den
    for g in range(SWA_GROUP):
        head0 = o[g * blk:(g + 1) * blk]
        head1 = o[(SWA_GROUP + g) * blk:(SWA_GROUP + g + 1) * blk]
        o_ref[rows, g * LANES:(g + 1) * LANES] = jnp.where(kvmasks[0], head0, head1).astype(o_ref.dtype)


def _swa_tables(sinks):
    w = SWA_WINDOW
    slopes = jnp.exp2(-8.0 * jnp.arange(1, SWA_HEADS + 1, dtype=F32) / SWA_HEADS)[:, None, None]
    r = jnp.arange(w)[:, None]
    c = jnp.arange(w)[None, :]
    dist = (r - c).astype(F32)[None]
    neg = jnp.float32(-jnp.inf)
    bias_c = jnp.where((c <= r)[None], -slopes * dist, neg)
    bias_p = jnp.where((c > r)[None], -slopes * (dist + w), neg)
    sink = jnp.broadcast_to(sinks.astype(F32)[:, None, None], (SWA_HEADS, w, w))
    return jnp.concatenate([bias_p, bias_c, sink], axis=2).reshape(SWA_HEADS * w, 3 * w)


def _mem_heads(q_ref, mk_ref, mv_ref, o_ref):
    scale = MEM_HEAD_DIM ** -0.5
    for h in range(MEM_HEADS):
        cols = slice(h * MEM_HEAD_DIM, (h + 1) * MEM_HEAD_DIM)
        z = _nt_dot(q_ref[:, cols], mk_ref[:, cols]) * scale
        m = jnp.max(z, axis=1, keepdims=True)
        p = jnp.exp(z - m)
        den = jnp.sum(p, axis=1, keepdims=True)
        o = jnp.dot(p.astype(BF16), mv_ref[:, cols], preferred_element_type=F32) / den
        o_ref[:, cols] = o.astype(o_ref.dtype)


def _merge_route_kernel(osb_ref, osw_ref, om_ref, g_ref, x_ref, wsb_ref, wsw_ref, wm_ref,
                        wout_ref, lng_ref, lnb_ref, wrh_ref, wrl_ref, rbias_ref,
                        x1_ref, x1p_ref, idx_ref, rank_ref, wgt_ref, cnt_ref, x1_prev, carry_ref):
    i = pl.program_id(0)

    @pl.when(i == 0)
    def _():
        x1_prev[...] = jnp.zeros_like(x1_prev)
        carry_ref[...] = jnp.zeros_like(carry_ref)

    d = x_ref.shape[1]
    st = {}

    def branch(b, o_ref, w_ref):
        def run():
            term = g_ref[:, b * d:(b + 1) * d].astype(F32) * jnp.dot(
                o_ref[...], w_ref[...], preferred_element_type=F32)
            st["merged"] = term if b == 0 else st["merged"] + term
        return run

    def out_proj():
        st["y"] = jnp.dot(st["merged"].astype(BF16), wout_ref[...], preferred_element_type=F32)

    idx, rank, wgt, count = _route(
        x1_prev[...], wrh_ref[...], wrl_ref[...], rbias_ref[...], carry_ref[...],
        side_work=(branch(0, osb_ref, wsb_ref), branch(1, osw_ref, wsw_ref),
                   branch(2, om_ref, wm_ref), out_proj))
    x1 = _layer_norm(DEEPNORM_ALPHA * x_ref[...] + st["y"], lng_ref[...], lnb_ref[...])
    x1_ref[...] = x1
    x1p_ref[...] = _pack_bf16_pair(x1[:, :d // 2], x1[:, d // 2:])
    idx_ref[...] = idx
    rank_ref[...] = rank
    wgt_ref[...] = wgt
    carry_ref[...] = carry_ref[...] + jnp.where(i > 0, count, 0.0)
    cnt_ref[...] = carry_ref[...]
    x1_prev[...] = x1


def _merge_route(o_sb, o_sw, o_m, p, x2, w_sb, w_sw, w_m, w_out, ln_g, ln_b, wr_hi, wr_lo,
                 bias_col, tm):
    t, d = x2.shape
    n = t // tm
    cur = lambda i: (jnp.minimum(i, n - 1), 0)
    row = lambda w: pl.BlockSpec((tm, w), cur)
    slot = pl.BlockSpec((TOP_K, tm), lambda i: (0, jnp.maximum(i - 1, 0)))
    return pl.pallas_call(
        _merge_route_kernel,
        grid=(n + 1,),
        in_specs=[row(o_sb.shape[1]), row(o_sw.shape[1]), row(o_m.shape[1]),
                  pl.BlockSpec((tm, N_BRANCH * d), cur),
                  row(d),
                  _resident(w_sb.shape), _resident(w_sw.shape), _resident(w_m.shape),
                  _resident(w_out.shape), _resident(ln_g.shape), _resident(ln_b.shape),
                  _resident(wr_hi.shape), _resident(wr_lo.shape), _resident(bias_col.shape)],
        out_specs=[row(d), row(d // 2), slot, slot, slot,
                   pl.BlockSpec((N_EXPERTS, 1), lambda i: (0, 0))],
        out_shape=[jax.ShapeDtypeStruct((t, d), F32), jax.ShapeDtypeStruct((t, d // 2), U32),
                   jax.ShapeDtypeStruct((TOP_K, t), I32),
                   jax.ShapeDtypeStruct((TOP_K, t), I32),
                   jax.ShapeDtypeStruct((TOP_K, t), F32),
                   jax.ShapeDtypeStruct((N_EXPERTS, 1), F32)],
        scratch_shapes=[pltpu.VMEM((tm, d), F32), pltpu.VMEM((N_EXPERTS, 1), F32)],
        compiler_params=pltpu.CompilerParams(
            dimension_semantics=("arbitrary",), vmem_limit_bytes=VMEM_LIMIT),
        name="merge_route",
    )(o_sb, o_sw, o_m, p, x2, w_sb, w_sw, w_m, w_out, ln_g, ln_b, wr_hi, wr_lo, bias_col)


def _route(x, wh, wl, bias, carry, side_work=()):
    tr = x.shape[0]
    xh = x.astype(BF16)
    xl = (x - xh.astype(F32)).astype(BF16)
    logits = _nt_dot(wh, xh) + _nt_dot(wh, xl) + _nt_dot(wl, xh)
    scores = _sigmoid(logits)
    biased = scores + bias
    neg = jnp.float32(-jnp.inf)

    sub = lax.broadcasted_iota(I32, (GROUP_SIZE, tr), 0)
    gscore = []
    for g in range(N_GROUPS):
        blk = biased[g * GROUP_SIZE:(g + 1) * GROUP_SIZE, :]
        m1 = jnp.max(blk, axis=0, keepdims=True)
        i1 = jnp.min(jnp.where(blk == m1, sub, GROUP_SIZE), axis=0, keepdims=True)
        m2 = jnp.max(jnp.where(sub == i1, neg, blk), axis=0, keepdims=True)
        gscore.append(m1 + m2)
    gs = jnp.concatenate(gscore, axis=0)

    giota = lax.broadcasted_iota(I32, (N_GROUPS, tr), 0)
    gsel = jnp.zeros((N_GROUPS, tr), F32)
    for _ in range(TOPK_GROUPS):
        m = jnp.max(gs, axis=0, keepdims=True)
        gi = jnp.min(jnp.where(gs == m, giota, N_GROUPS), axis=0, keepdims=True)
        hit = giota == gi
        gsel = jnp.where(hit, 1.0, gsel)
        gs = jnp.where(hit, neg, gs)

    masked = jnp.concatenate(
        [jnp.where(gsel[g:g + 1, :] > 0.0, biased[g * GROUP_SIZE:(g + 1) * GROUP_SIZE, :], neg)
         for g in range(N_GROUPS)], axis=0)

    eiota = lax.broadcasted_iota(I32, (N_EXPERTS, tr), 0)
    sel = jnp.zeros((N_EXPERTS, tr), F32)
    idx_rows, w_rows = [], []
    side_work = list(side_work)
    for k in range(TOP_K):
        if side_work and k % 2 == 0:
            side_work.pop(0)()
        m = jnp.max(masked, axis=0, keepdims=True)
        ei = jnp.min(jnp.where(masked == m, eiota, N_EXPERTS), axis=0, keepdims=True)
        hit = eiota == ei
        idx_rows.append(ei)
        w_rows.append(jnp.sum(jnp.where(hit, scores, 0.0), axis=0, keepdims=True))
        sel = jnp.where(hit, 1.0, sel)
        masked = jnp.where(hit, neg, masked)

    wsum = w_rows[0]
    for wk in w_rows[1:]:
        wsum = wsum + wk
    wgt = jnp.concatenate(w_rows, axis=0) / wsum * ROUTED_SCALE
    idx = jnp.concatenate(idx_rows, axis=0)

    a = lax.broadcasted_iota(I32, (tr, tr), 0)
    b = lax.broadcasted_iota(I32, (tr, tr), 1)
    before = (a < b).astype(BF16)
    rank = jnp.dot(sel.astype(BF16), before, preferred_element_type=F32) + carry
    rank_rows = [jnp.sum(jnp.where(eiota == ei, rank, 0.0), axis=0, keepdims=True)
                 for ei in idx_rows]
    rank = jnp.concatenate(rank_rows, axis=0).astype(I32)
    return idx, rank, wgt, jnp.sum(sel, axis=1, keepdims=True)


def _dest_kernel(idx_ref, rank_ref, start_ref, dest_ref):
    tr = idx_ref.shape[1]
    eiota = lax.broadcasted_iota(I32, (N_EXPERTS, tr), 0)
    rows = []
    for k in range(TOP_K):
        hit = eiota == idx_ref[k:k + 1, :]
        rows.append(jnp.sum(jnp.where(hit, start_ref[...], 0), axis=0, keepdims=True))
    dest_ref[...] = jnp.concatenate(rows, axis=0) + rank_ref[...]


def _dest(idx, rank, start_col, tr):
    t = idx.shape[1]
    slot = pl.BlockSpec((TOP_K, tr), lambda i: (0, i))
    return pl.pallas_call(
        _dest_kernel,
        grid=(t // tr,),
        in_specs=[slot, slot, _resident(start_col.shape)],
        out_specs=slot,
        out_shape=jax.ShapeDtypeStruct((TOP_K, t), I32),
        compiler_params=pltpu.CompilerParams(dimension_semantics=("parallel",)),
        name="slot_dest",
    )(idx, rank, start_col)


def _sc_worker_id():
    return lax.axis_index("s") * SC_CORES + lax.axis_index("c")


def _dispatch(x1p, dest, n_rows):
    t, w = x1p.shape
    per = t // SC_WORKERS
    win = min(SC_INDEX_WINDOW, per)
    mesh = plsc.VectorSubcoreMesh(core_axis_name="c", subcore_axis_name="s")

    @functools.partial(
        pl.kernel, mesh=mesh,
        out_type=jax.ShapeDtypeStruct((n_rows, w), x1p.dtype),
        scratch_types=[pltpu.VMEM((TOP_K, win), I32),
                       pltpu.VMEM((win, w), x1p.dtype),
                       pltpu.SemaphoreType.DMA],
        name="sc_dispatch",
    )
    def scatter_rows(x_hbm, dest_hbm, xs_hbm, idx_v, rows_v, sem):
        base = _sc_worker_id() * per

        @pl.loop(0, per // win)
        def _(j):
            t0 = pl.multiple_of(base + j * win, win)
            pltpu.sync_copy(dest_hbm.at[:, pl.ds(t0, win)], idx_v)
            pltpu.sync_copy(x_hbm.at[pl.ds(t0, win)], rows_v)
            copies = [pltpu.async_copy(rows_v, xs_hbm.at[idx_v.at[k]], sem) for k in range(TOP_K)]
            for c in copies:
                c.wait()

    return scatter_rows(x1p, dest)


def _gather_rows(table, idx):
    n = idx.shape[0]
    w = table.shape[1]
    per = n // SC_WORKERS
    chunk = min(SC_GATHER_ROWS, per // 2)
    assert n % SC_WORKERS == 0 and per % (2 * chunk) == 0, (n, chunk)
    mesh = plsc.VectorSubcoreMesh(core_axis_name="c", subcore_axis_name="s")

    @functools.partial(
        pl.kernel, mesh=mesh,
        out_type=jax.ShapeDtypeStruct((n, w), table.dtype),
        scratch_types=[pltpu.VMEM((per,), I32),
                       pltpu.VMEM((2, chunk, w), table.dtype),
                       pltpu.SemaphoreType.DMA((2,)),
                       pltpu.SemaphoreType.DMA((2,))],
        name="sc_gather",
    )
    def gather_rows(table_hbm, idx_hbm, out_hbm, idx_v, rows_v, gather_sem, put_sem):
        base = _sc_worker_id() * per
        nchunks = per // chunk
        pltpu.sync_copy(idx_hbm.at[pl.ds(base, per)], idx_v)

        def gather(j, b):
            off = pl.multiple_of(j * chunk, chunk)
            return pltpu.make_async_copy(table_hbm.at[idx_v.at[pl.ds(off, chunk)]],
                                         rows_v.at[b], gather_sem.at[b])

        def put(j, b):
            off = pl.multiple_of(j * chunk, chunk)
            return pltpu.make_async_copy(rows_v.at[b], out_hbm.at[pl.ds(base + off, chunk)],
                                         put_sem.at[b])

        gather(0, 0).start()

        @pl.loop(0, nchunks, step=2)
        def _(j):
            for b in (0, 1):
                jj = j + b

                @pl.when(jj + 1 < nchunks)
                def _():
                    @pl.when(jj >= 1)
                    def _():
                        put(jj - 1, 1 - b).wait()
                    gather(jj + 1, 1 - b).start()

                gather(jj, b).wait()
                put(jj, b).start()

        put(nchunks - 2, 0).wait()
        put(nchunks - 1, 1).wait()

    return gather_rows(table, idx)


def _expert_kernel(first_ref, nblk_ref, total_ref, wgu_ref, wd_ref, xs_hbm, ys_hbm,
                   wgu_s, wd_s, xbuf, ybuf, in_sem, out_sem):
    e = pl.program_id(0)
    total = total_ref[0]
    n_in, rows, half = xbuf.shape
    n_out = ybuf.shape[0]
    ahead = EXPERT_AHEAD

    def block_rows(g):
        return pl.ds(pl.multiple_of(g * rows, rows), rows)

    def load(g):
        slot = g % n_in
        return pltpu.make_async_copy(xs_hbm.at[block_rows(g), :], xbuf.at[slot], in_sem.at[slot])

    def store(g):
        slot = g % n_out
        return pltpu.make_async_copy(ybuf.at[slot], ys_hbm.at[block_rows(g), :], out_sem.at[slot])

    @pl.when(e == 0)
    def _():
        for g in range(ahead):
            @pl.when(g < total)
            def _(g=g):
                load(g).start()

    wgu_s[...] = wgu_ref[0].astype(BF16)
    wd_s[...] = wd_ref[0].astype(BF16)
    g0 = first_ref[e]
    nblk = nblk_ref[e]
    ff = wd_s.shape[0]

    def acquire(g):
        load(g).wait()

        @pl.when(g + ahead < total)
        def _():
            load(g + ahead).start()

        @pl.when(g >= n_out)
        def _():
            store(g - n_out).wait()

    def compute(gs):
        st = {}

        def unpack(k):
            x_lo, x_hi = _unpack_bf16_pair(xbuf[gs[k] % n_in])
            st[k, "x"] = (x_lo.astype(BF16), x_hi.astype(BF16))

        def up_proj(k):
            x_lo, x_hi = st[k, "x"]
            st[k, "h"] = (jnp.dot(x_lo, wgu_s[:half, :], preferred_element_type=F32)
                          + jnp.dot(x_hi, wgu_s[half:, :], preferred_element_type=F32))

        def activate(k):
            gate, up = st[k, "h"][:, :ff], st[k, "h"][:, ff:]
            st[k, "a"] = (gate * _sigmoid(gate) * up).astype(BF16)

        def down_proj(k):
            st[k, "y"] = jnp.dot(st[k, "a"], wd_s[...], preferred_element_type=F32)

        def pack(k):
            y = st[k, "y"]
            ybuf[gs[k] % n_out] = _pack_bf16_pair(y[:, :half], y[:, half:])

        stages = (unpack, up_proj, activate, down_proj, pack)
        for t in range(len(gs) + len(stages) - 1):
            for s in reversed(range(len(stages))):
                if 0 <= t - s < len(gs):
                    stages[s](t - s)

    def pair(jj, carry):
        g = g0 + 2 * jj
        acquire(g)
        acquire(g + 1)
        compute([g, g + 1])
        store(g).start()
        store(g + 1).start()
        return carry

    lax.fori_loop(0, nblk // 2, pair, 0)

    @pl.when(nblk % 2 == 1)
    def _():
        g = g0 + nblk - 1
        acquire(g)
        compute([g])
        store(g).start()

    @pl.when(e == pl.num_programs(0) - 1)
    def _():
        for back in range(n_out, 0, -1):
            @pl.when(total >= back)
            def _(back=back):
                store(total - back).wait()


def _experts(xs, first_blk, n_blk, total_blk, w_gu, w_down):
    n_rows, half = xs.shape
    d = 2 * half
    n_exp, _, ff2 = w_gu.shape
    ff = w_down.shape[1]
    grid_spec = pltpu.PrefetchScalarGridSpec(
        num_scalar_prefetch=3,
        grid=(n_exp,),
        in_specs=[pl.BlockSpec((1, d, ff2), lambda e, *_: (e, 0, 0)),
                  pl.BlockSpec((1, ff, d), lambda e, *_: (e, 0, 0)),
                  pl.BlockSpec(memory_space=pl.ANY)],
        out_specs=pl.BlockSpec(memory_space=pl.ANY),
        scratch_shapes=[pltpu.VMEM((d, ff2), BF16), pltpu.VMEM((ff, d), BF16),
                        pltpu.VMEM((EXPERT_IN_SLOTS, EXPERT_ROWS, half), xs.dtype),
                        pltpu.VMEM((EXPERT_OUT_SLOTS, EXPERT_ROWS, half), xs.dtype),
                        pltpu.SemaphoreType.DMA((EXPERT_IN_SLOTS,)),
                        pltpu.SemaphoreType.DMA((EXPERT_OUT_SLOTS,))],
    )
    return pl.pallas_call(
        _expert_kernel,
        grid_spec=grid_spec,
        out_shape=jax.ShapeDtypeStruct((n_rows, half), xs.dtype),
        compiler_params=pltpu.CompilerParams(
            dimension_semantics=("arbitrary",), vmem_limit_bytes=VMEM_LIMIT),
        name="experts",
    )(first_blk, n_blk, total_blk, w_gu, w_down, xs)


def _combine_kernel(x1_ref, w_ref, yg_ref, wsgu_ref, wsd_ref, lng_ref, lnb_ref, o_ref):
    tc = x1_ref.shape[0]
    x1 = x1_ref[...]
    ff = wsd_ref.shape[0]
    h = jnp.dot(x1.astype(BF16), wsgu_ref[...], preferred_element_type=F32)
    gate, up = h[:, :ff], h[:, ff:]
    act = gate * _sigmoid(gate) * up
    moe = jnp.dot(act.astype(BF16), wsd_ref[...], preferred_element_type=F32)

    half = yg_ref.shape[2]
    wt = jnp.transpose(w_ref[...])
    r_lo = jnp.zeros((tc, half), F32)
    r_hi = jnp.zeros((tc, half), F32)
    for k in range(TOP_K):
        y_lo, y_hi = _unpack_bf16_pair(yg_ref[k])
        w = wt[:, k:k + 1]
        r_lo = r_lo + w * y_lo
        r_hi = r_hi + w * y_hi
    moe = moe + jnp.concatenate([r_lo, r_hi], axis=1)
    o_ref[...] = _layer_norm(DEEPNORM_ALPHA * x1 + moe, lng_ref[...], lnb_ref[...])


def _combine(x1, wgt, yg, chunk, ws_gu, ws_down, ln_g, ln_b, tc):
    t, d = x1.shape
    steps = yg.shape[1] // tc
    tok = lambda i: (chunk * steps + i, 0)
    return pl.pallas_call(
        _combine_kernel,
        grid=(steps,),
        in_specs=[pl.BlockSpec((tc, d), tok),
                  pl.BlockSpec((TOP_K, tc), lambda i: (0, chunk * steps + i)),
                  pl.BlockSpec((TOP_K, tc, yg.shape[2]), lambda i: (0, i, 0)),
                  _resident(ws_gu.shape), _resident(ws_down.shape),
                  _resident(ln_g.shape), _resident(ln_b.shape)],
        out_specs=pl.BlockSpec((tc, d), tok),
        out_shape=jax.ShapeDtypeStruct((t, d), F32),
        input_output_aliases={0: 0},
        compiler_params=pltpu.CompilerParams(
            dimension_semantics=("arbitrary",), vmem_limit_bytes=VMEM_LIMIT),
        name="combine_ln2",
    )(x1, wgt, yg, ws_gu, ws_down, ln_g, ln_b)


def _fused_in_weights(w_in):
    d = w_in.shape[0]
    sizes = (SB_HEADS * HEAD_DIM,) * 3 + (SWA_HEADS * HEAD_DIM, SWA_KV_HEADS * HEAD_DIM,
                                          SWA_KV_HEADS * HEAD_DIM, MEM_HEADS * MEM_HEAD_DIM)
    parts, off = [], 0
    for s in sizes:
        parts.append(w_in[:, off:off + s])
        off += s
    q_sb, k_sb, v_sb, q_sw, k_sw, v_sw, q_m = parts
    gates = w_in[:, off:]
    scale = HEAD_DIM ** -0.5
    q_sw = (q_sw * scale).reshape(d, SWA_KV_HEADS, SWA_GROUP, HEAD_DIM)
    q_sw = jnp.swapaxes(q_sw, 1, 2).reshape(d, SWA_HEADS * HEAD_DIM)
    w_a =jnp.concatenate([q_sw, q_m, k_sw, v_sw], axis=1).astype(BF16)
    w_b = jnp.concatenate([gates, q_sb * scale, k_sb, v_sb], axis=1).astype(BF16)
    assert w_a.shape[1] == PA_COLS and w_b.shape[1] == PB_COLS
    return w_a, w_b


def _grouped_swa_out_weights(w_o_swa):
    d = w_o_swa.shape[1]
    w = w_o_swa.reshape(SWA_KV_HEADS, SWA_GROUP, HEAD_DIM, d)
    return jnp.swapaxes(w, 0, 1).reshape(SWA_HEADS * HEAD_DIM, d).astype(BF16)


def kernel(x, mem, w_in, b_gate, w_mem_kv, sinks, w_o_sb, w_o_swa, w_o_mem, w_out,
           ln1_g, ln1_b, w_router, router_bias, w_e_gu, w_e_down, w_s_gu, w_s_down,
           ln2_g, ln2_b):
    batch, seq, d = x.shape
    mem_len = mem.shape[1]
    t = batch * seq
    x2 = x.reshape(t, d)
    row_tile = min(512, t)

    w_a, w_b = _fused_in_weights(w_in)
    p_a = _matmul_bf16(x2, w_a, row_tile, "in_proj_attn")
    mkv = _matmul_bf16(mem.reshape(batch * mem_len, d), w_mem_kv.astype(BF16), mem_len,
                       "mem_kv_proj")
    p, o_sw, o_m = _proj_attn(x2, w_b, b_gate.reshape(1, -1), p_a, mkv, _swa_tables(sinks),
                              batch, seq, mem_len)
    o_sb = _sb_attention(p, batch, seq, min(256, seq), SB_PAIRS_PER_STEP)
    wr_t = w_router.T
    wr_hi = wr_t.astype(BF16)
    wr_lo = (wr_t - wr_hi.astype(F32)).astype(BF16)
    x1, x1p, idx, rank, wgt, cnt = _merge_route(
        o_sb, o_sw, o_m, p, x2, w_o_sb.astype(BF16), _grouped_swa_out_weights(w_o_swa),
        w_o_mem.astype(BF16), w_out.astype(BF16), ln1_g.reshape(1, d), ln1_b.reshape(1, d),
        wr_hi, wr_lo, router_bias.reshape(-1, 1).astype(F32), row_tile)

    out = _moe_ln(x1, x1p, idx, rank, wgt, cnt, w_e_gu, w_e_down, w_s_gu, w_s_down, ln2_g, ln2_b)
    return out.reshape(batch, seq, d)


def _moe_ln(x1, x1p, idx, rank, wgt, cnt, w_e_gu, w_e_down, w_s_gu, w_s_down, ln2_g, ln2_b):
    t, d = x1.shape
    counts = cnt[:, 0].astype(I32)
    padded = (counts + EXPERT_ROWS - 1) // EXPERT_ROWS * EXPERT_ROWS
    pad_end = jnp.cumsum(padded)
    pad_start = pad_end - padded
    n_blocks = t * TOP_K // EXPERT_ROWS + N_EXPERTS
    dest = _dest(idx, rank, pad_start.reshape(-1, 1), min(2048, t))

    xs = _dispatch(x1p, dest, n_blocks * EXPERT_ROWS)
    ys = _experts(xs, pad_start // EXPERT_ROWS, padded // EXPERT_ROWS,
                  pad_end[-1:] // EXPERT_ROWS, w_e_gu, w_e_down)
    tchunk = t // COMBINE_CHUNKS
    ws_gu, ws_down = w_s_gu.astype(BF16), w_s_down.astype(BF16)
    out = x1
    for c in range(COMBINE_CHUNKS):
        slots = dest[:, c * tchunk:(c + 1) * tchunk].reshape(-1)
        yg = _gather_rows(ys, slots).reshape(TOP_K, tchunk, -1)
        out = _combine(out, wgt, yg, c, ws_gu, ws_down, ln2_g.reshape(1, d), ln2_b.reshape(1, d),
                       min(512, tchunk))
    return out
```
